```python
import math
import jax, jax.numpy as jnp
from jax import lax
import numpy as np

D_MODEL = 1024
BATCH = 8
SEQ = 2048
DEPTH = 1
DEC_BATCH = 128
DEC_SEQ = 1
PAST_LEN = 16384
PAGE_SIZE = 128

CHUNK = 128
A_HEADS = 8
A_WIDTH = D_MODEL
A_HEAD_DIM = A_WIDTH // A_HEADS
POOL_WINDOWS = (2, 4, 8, 16)
POOL_GROUPS = len(POOL_WINDOWS)
B_WIDTH = D_MODEL // 2
B_GROUP_DIM = B_WIDTH // POOL_GROUPS
B_OUT_GROUP = D_MODEL // POOL_GROUPS
W_MAX = max(POOL_WINDOWS)
POOL_STATE = W_MAX - 1
PROJ_COLS = 2 * A_WIDTH + B_WIDTH + 2 * D_MODEL
N_EXPERT_GROUPS = 4
EXPERTS_PER_GROUP = 8
N_EXPERTS = N_EXPERT_GROUPS * EXPERTS_PER_GROUP
TOP_K_IN_GROUP = 2
D_EXPERT = 128
N_MOD = 6
EPS = 1e-6

kernel_name = "hybrid_chunkgmlp_pool_hmoe_step"


def rms_norm(x, g):
    xf = x.astype(jnp.float32)
    y = xf * lax.rsqrt(jnp.mean(xf * xf, axis=-1, keepdims=True) + EPS)
    return (y * g.astype(jnp.float32)).astype(x.dtype)


def layer_norm(x, g, b):
    xf = x.astype(jnp.float32)
    mu = jnp.mean(xf, axis=-1, keepdims=True)
    xc = xf - mu
    var = jnp.mean(xc * xc, axis=-1, keepdims=True)
    y = xc * lax.rsqrt(var + EPS) * g.astype(jnp.float32) + b.astype(jnp.float32)
    return y.astype(x.dtype)


def chunk_mix(v, w_s, b_s):
    bsz, t, _ = v.shape
    l = min(t, CHUNK)
    n = t // l
    vr = v.reshape(bsz, n, l, A_HEADS, A_HEAD_DIM)
    w = jnp.tril(w_s[:, :l, :l])
    out = jnp.einsum("hij,bnjhc->bnihc", w, vr)
    out = out + b_s[:, :l].T[None, None, :, :, None]
    return out.reshape(bsz, t, A_WIDTH)


def pool_mix(p, prefix, pos0, w_pool, pool_scale):
    bsz, t, c = p.shape
    n_prev = prefix.shape[1]
    xb = jnp.concatenate([prefix, p], axis=1)
    xf = xb.astype(jnp.float32)
    xpad = jnp.pad(xf, ((0, 0), (W_MAX, 0), (0, 0)))
    cs0 = jnp.concatenate([jnp.zeros((bsz, 1, c), jnp.float32),
                           jnp.cumsum(xpad, axis=1)], axis=1)
    start = W_MAX + n_prev + 1
    pos = pos0 + jnp.arange(t, dtype=jnp.int32)
    pf = p.astype(jnp.float32)
    groups = []
    for gi, w in enumerate(POOL_WINDOWS):
        sl = slice(gi * B_GROUP_DIM, (gi + 1) * B_GROUP_DIM)
        wsum = cs0[:, start:start + t, sl] - cs0[:, start - w:start - w + t, sl]
        cnt = jnp.minimum(pos + 1, w).astype(jnp.float32)[None, :, None]
        groups.append(wsum / cnt - pf[:, :, sl])
    d = jnp.stack(groups, axis=2)
    y = jnp.einsum("btgc,gco->btgo", d, w_pool.astype(jnp.float32))
    y = y.reshape(bsz, t, D_MODEL) * pool_scale.astype(jnp.float32)
    new_state = xb[:, -POOL_STATE:]
    return y.astype(p.dtype), new_state


def token_mixers(h, prefix, pos0, w_in, ln_v_g, ln_v_b, w_spatial, b_spatial,
                 w_pool, pool_scale, w_out):
    z = jnp.einsum("btd,dk->btk", h, w_in)
    u = z[..., :A_WIDTH]
    v = z[..., A_WIDTH:2 * A_WIDTH]
    p = z[..., 2 * A_WIDTH:2 * A_WIDTH + B_WIDTH]
    ga = z[..., 2 * A_WIDTH + B_WIDTH:2 * A_WIDTH + B_WIDTH + D_MODEL]
    gb = z[..., 2 * A_WIDTH + B_WIDTH + D_MODEL:]
    u = jax.nn.gelu(u)
    v = layer_norm(jax.nn.gelu(v), ln_v_g, ln_v_b)
    y_a = u * chunk_mix(v, w_spatial, b_spatial)
    y_b, new_pool = pool_mix(p, prefix, pos0, w_pool, pool_scale)
    merged = jax.nn.sigmoid(ga) * y_a + jax.nn.sigmoid(gb) * y_b
    return jnp.einsum("btd,de->bte", merged, w_out), new_pool, v


def hier_moe(h, w_router_grp, b_router_grp, w_router_exp, b_router_exp,
             w_exp_gate, w_exp_up, w_exp_down):
    bsz, t, _ = h.shape
    grp_logits = jnp.einsum("btd,dg->btg", h, w_router_grp).astype(jnp.float32) \
        + b_router_grp.astype(jnp.float32)
    grp_prob = jax.nn.softmax(grp_logits, axis=-1)
    g_idx = jnp.argmax(grp_logits, axis=-1)
    p_g = jnp.take_along_axis(grp_prob, g_idx[..., None], axis=-1)
    exp_logits = jnp.einsum("btd,de->bte", h, w_router_exp).astype(jnp.float32) \
        + b_router_exp.astype(jnp.float32)
    exp_logits = exp_logits.reshape(bsz, t, N_EXPERT_GROUPS, EXPERTS_PER_GROUP)
    sel = jnp.take_along_axis(exp_logits, g_idx[..., None, None], axis=2)[..., 0, :]
    top_v, top_i = lax.top_k(sel, TOP_K_IN_GROUP)
    top_w = jax.nn.softmax(top_v, axis=-1) * p_g
    expert_id = g_idx[..., None] * EXPERTS_PER_GROUP + top_i
    combine = jnp.sum(jax.nn.one_hot(expert_id, N_EXPERTS, dtype=jnp.float32)
                      * top_w[..., None], axis=2)
    gate = jnp.einsum("btd,edf->btef", h, w_exp_gate)
    up = jnp.einsum("btd,edf->btef", h, w_exp_up)
    act = jax.nn.silu(gate) * up * combine[..., None].astype(h.dtype)
    return jnp.einsum("btef,efd->btd", act, w_exp_down)


def trunk_layer(x, c, prefix, pos0, w_ada, b_ada, g_pre_mix, g_post_mix, g_pre_ffn,
                g_post_ffn, w_in, ln_v_g, ln_v_b, w_spatial, b_spatial, w_pool,
                pool_scale, w_out, w_router_grp, b_router_grp, w_router_exp,
                b_router_exp, w_exp_gate, w_exp_up, w_exp_down):
    mod = jnp.einsum("bd,de->be", jax.nn.silu(c), w_ada) + b_ada
    sh1, sc1, gt1, sh2, sc2, gt2 = [m[:, None, :] for m in jnp.split(mod, N_MOD, axis=-1)]
    h = rms_norm(x, g_pre_mix) * (1 + sc1) + sh1
    y, new_pool, v = token_mixers(h, prefix, pos0, w_in, ln_v_g, ln_v_b, w_spatial,
                                  b_spatial, w_pool, pool_scale, w_out)
    x = x + gt1 * rms_norm(y, g_post_mix)
    h2 = rms_norm(x, g_pre_ffn) * (1 + sc2) + sh2
    f = hier_moe(h2, w_router_grp, b_router_grp, w_router_exp, b_router_exp,
                 w_exp_gate, w_exp_up, w_exp_down)
    x = x + gt2 * rms_norm(f, g_post_ffn)
    return x, new_pool, v


def setup_inputs(seed: int = 0) -> dict:
    key = jax.random.key(seed)
    ks = jax.random.split(key, 32)
    f32 = jnp.float32
    nrm = lambda k, shape, s: jax.random.normal(k, shape, f32) * s
    gain = lambda k, shape: 1.0 + 0.1 * jax.random.normal(k, shape, f32)
    L = DEPTH
    return {
        "x_prompt": nrm(ks[0], (BATCH, SEQ, D_MODEL), 1.0),
        "x_sample": nrm(ks[1], (DEC_BATCH, DEC_SEQ, D_MODEL), 1.0),
        "c_prompt": nrm(ks[2], (BATCH, D_MODEL), 1.0),
        "c_sample": nrm(ks[3], (DEC_BATCH, D_MODEL), 1.0),
        "state_pool": nrm(ks[4], (L, DEC_BATCH, POOL_STATE, B_WIDTH), 1.0),
        "w_ada": nrm(ks[5], (L, D_MODEL, N_MOD * D_MODEL), 0.3 * D_MODEL ** -0.5),
        "b_ada": nrm(ks[6], (L, N_MOD * D_MODEL), 0.02),
        "g_pre_mix": gain(ks[7], (L, D_MODEL)),
        "g_post_mix": gain(ks[8], (L, D_MODEL)),
        "g_pre_ffn": gain(ks[9], (L, D_MODEL)),
        "g_post_ffn": gain(ks[10], (L, D_MODEL)),
        "w_in": nrm(ks[11], (L, D_MODEL, PROJ_COLS), D_MODEL ** -0.5),
        "ln_v_g": gain(ks[12], (L, A_WIDTH)),
        "ln_v_b": nrm(ks[13], (L, A_WIDTH), 0.02),
        "w_spatial": nrm(ks[14], (L, A_HEADS, CHUNK, CHUNK), CHUNK ** -0.5),
        "b_spatial": gain(ks[15], (L, A_HEADS, CHUNK)),
        "w_pool": nrm(ks[16], (L, POOL_GROUPS, B_GROUP_DIM, B_OUT_GROUP), B_GROUP_DIM ** -0.5),
        "pool_scale": gain(ks[17], (L, D_MODEL)),
        "w_out": nrm(ks[18], (L, D_MODEL, D_MODEL), D_MODEL ** -0.5),
        "w_router_grp": nrm(ks[19], (L, D_MODEL, N_EXPERT_GROUPS), D_MODEL ** -0.5),
        "b_router_grp": nrm(ks[20], (L, N_EXPERT_GROUPS), 0.01),
        "w_router_exp": nrm(ks[21], (L, D_MODEL, N_EXPERTS), D_MODEL ** -0.5),
        "b_router_exp": nrm(ks[22], (L, N_EXPERTS), 0.01),
        "w_exp_gate": nrm(ks[23], (L, N_EXPERTS, D_MODEL, D_EXPERT), D_MODEL ** -0.5),
        "w_exp_up": nrm(ks[24], (L, N_EXPERTS, D_MODEL, D_EXPERT), D_MODEL ** -0.5),
        "w_exp_down": nrm(ks[25], (L, N_EXPERTS, D_EXPERT, D_MODEL), D_EXPERT ** -0.5),
    }


def reference(x_prompt, x_sample, c_prompt, c_sample, state_pool, w_ada, b_ada,
              g_pre_mix, g_post_mix, g_pre_ffn, g_post_ffn, w_in, ln_v_g, ln_v_b,
              w_spatial, b_spatial, w_pool, pool_scale, w_out, w_router_grp,
              b_router_grp, w_router_exp, b_router_exp, w_exp_gate, w_exp_up,
              w_exp_down):
    xp = x_prompt
    xs = x_sample
    pool_p, pool_s, v_s = [], [], []
    for l in range(DEPTH):
        w = (w_ada[l], b_ada[l], g_pre_mix[l], g_post_mix[l], g_pre_ffn[l],
             g_post_ffn[l], w_in[l], ln_v_g[l], ln_v_b[l], w_spatial[l], b_spatial[l],
             w_pool[l], pool_scale[l], w_out[l], w_router_grp[l], b_router_grp[l],
             w_router_exp[l], b_router_exp[l], w_exp_gate[l], w_exp_up[l],
             w_exp_down[l])
        empty = jnp.zeros((xp.shape[0], 0, B_WIDTH), xp.dtype)
        xp, np_state, _ = trunk_layer(xp, c_prompt, empty, 0, *w)
        xs, ns_state, vs = trunk_layer(xs, c_sample, state_pool[l], PAST_LEN, *w)
        pool_p.append(np_state)
        pool_s.append(ns_state)
        v_s.append(vs)
    state_pool_prompt = jnp.stack(pool_p, axis=0)
    state_pool_sample = jnp.stack(pool_s, axis=0)
    chunk_v_sample = jnp.stack(v_s, axis=0)
    return (xp, xs, state_pool_prompt, state_pool_sample, chunk_v_sample)
```

```python
import functools

import jax
import jax.numpy as jnp
from jax import lax
from jax.experimental import pallas as pl
from jax.experimental.pallas import tpu as pltpu

D = 1024
CHUNK = 128
HEADS = 8
HEAD_DIM = 128
WINDOWS = (2, 4, 8, 16)
PW = 512
PG = 128
PO = 256
W_MAX = 16
N_GROUPS = 4
EPG = 8
N_EXPERTS = 32
D_EXPERT = 128
EPS = 1e-6
LANES = 128
GROUP_LANE0 = 32
GIDX_LANE = 96

TL = 256
TM = 256
VMEM_LIMIT = 56 * 1024 * 1024

bf16 = jnp.bfloat16
f32 = jnp.float32


def _rms(x, g):
    ms = jnp.mean(x * x, axis=-1, keepdims=True)
    return x * lax.rsqrt(ms + EPS) * g


def _dot(a, b):
    return jnp.dot(a, b, preferred_element_type=f32)


def _mod_kernel(c_ref, w_ref, b_ref, o_ref):
    c = c_ref[...]
    a = (c * jax.nn.sigmoid(c)).astype(bf16)
    o_ref[0] = _dot(a, w_ref[...].astype(bf16)) + b_ref[0]


def _mod_call(c_all, w_ada, b_ada):
    n = c_all.shape[0]
    return pl.pallas_call(
        _mod_kernel,
        grid=(6,),
        in_specs=[
            pl.BlockSpec((n, D), lambda j: (0, 0)),
            pl.BlockSpec((D, D), lambda j: (0, j)),
            pl.BlockSpec((1, 1, D), lambda j: (j, 0, 0)),
        ],
        out_specs=pl.BlockSpec((1, n, D), lambda j: (j, 0, 0)),
        out_shape=jax.ShapeDtypeStruct((6, n, D), f32),
        compiler_params=pltpu.CompilerParams(
            dimension_semantics=("arbitrary",), vmem_limit_bytes=VMEM_LIMIT),
    )(c_all, w_ada, b_ada.reshape(6, 1, D))


def _route(logits):
    t = logits.shape[0]
    lane = lax.broadcasted_iota(jnp.int32, (t, LANES), 1)
    lane_f = lane.astype(f32)
    neg = -jnp.inf
    big = 1e9
    gmask = (lane >= GROUP_LANE0) & (lane < GROUP_LANE0 + N_GROUPS)
    gl = jnp.where(gmask, logits, neg)
    gmax = jnp.max(gl, axis=-1, keepdims=True)
    g_idx = jnp.min(jnp.where(gl == gmax, lane_f - GROUP_LANE0, big), axis=-1, keepdims=True)
    sumexp = jnp.sum(jnp.where(gmask, jnp.exp(gl - gmax), 0.0), axis=-1, keepdims=True)
    p_g = 1.0 / sumexp
    lane_grp = (lane >> 3).astype(f32)
    emask = (lane < N_EXPERTS) & (lane_grp == g_idx)
    el = jnp.where(emask, logits, neg)
    m1 = jnp.max(el, axis=-1, keepdims=True)
    i1 = jnp.min(jnp.where(el == m1, lane_f, big), axis=-1, keepdims=True)
    el2 = jnp.where(lane_f == i1, neg, el)
    m2 = jnp.max(el2, axis=-1, keepdims=True)
    i2 = jnp.min(jnp.where(el2 == m2, lane_f, big), axis=-1, keepdims=True)
    e = jnp.exp(m2 - m1)
    w1 = p_g / (1.0 + e)
    w2 = w1 * e
    route = jnp.where(lane_f == i1, w1, 0.0) + jnp.where(lane_f == i2, w2, 0.0)
    return jnp.where(lane == GIDX_LANE, g_idx, route)


def _mixer_tail(x, u, ga, gb, mix, y_b, mods, vec_ref, w_out_ref, w_r_ref, b_r_ref):
    sh1, sc1, gt1, sh2, sc2, gt2 = mods
    y_a = u * mix
    merged = jax.nn.sigmoid(ga) * y_a + jax.nn.sigmoid(gb) * y_b
    y = _dot(merged.astype(bf16), w_out_ref[...])
    x1 = x + gt1 * _rms(y, vec_ref[1:2])
    h2 = _rms(x1, vec_ref[2:3]) * (1.0 + sc2) + sh2
    logits = jnp.dot(h2, w_r_ref[...], preferred_element_type=f32,
                     precision=lax.Precision.HIGHEST) + b_r_ref[...]
    return x1, h2, _route(logits)


def _front(x, mods, vec_ref, w_in_ref):
    sh1, sc1 = mods[0], mods[1]
    h = _rms(x, vec_ref[0:1]) * (1.0 + sc1) + sh1
    hb = h.astype(bf16)
    u = jax.nn.gelu(_dot(hb, w_in_ref[:, 0:D]))
    v = jax.nn.gelu(_dot(hb, w_in_ref[:, D:2 * D]))
    mu = jnp.mean(v, axis=-1, keepdims=True)
    vc = v - mu
    var = jnp.mean(vc * vc, axis=-1, keepdims=True)
    v = vc * lax.rsqrt(var + EPS) * vec_ref[4:5] + vec_ref[5:6]
    p = _dot(hb, w_in_ref[:, 2 * D:2 * D + PW])
    ga = _dot(hb, w_in_ref[:, 2 * D + PW:3 * D + PW])
    gb = _dot(hb, w_in_ref[:, 3 * D + PW:4 * D + PW])
    return u, v, p, ga, gb


def _pool_out(d_groups, vec_ref, w_pool_ref):
    parts = [_dot(d.astype(bf16), w_pool_ref[gi]) for gi, d in enumerate(d_groups)]
    return jnp.concatenate(parts, axis=1) * vec_ref[6:7]


def _stage1_prompt_kernel(x_ref, mod_ref, vec_ref, w_in_ref, w_sp_ref, bias_ref, w_pool_ref,
                          w_out_ref, w_r_ref, b_r_ref,
                          x1_ref, h2_ref, route_ref, plast_ref, pbuf):
    s = pl.program_id(1)

    @pl.when(s == 0)
    def _():
        pbuf[...] = jnp.zeros_like(pbuf)

    x = x_ref[0]
    mods = [mod_ref[i, 0] for i in range(6)]
    u, v, p, ga, gb = _front(x, mods, vec_ref, w_in_ref)

    vb = v.astype(bf16)
    row = lax.broadcasted_iota(jnp.int32, (CHUNK, CHUNK), 0)
    col = lax.broadcasted_iota(jnp.int32, (CHUNK, CHUNK), 1)
    w_tril = [jnp.where(row >= col, w_sp_ref[hd], 0.0).astype(bf16) for hd in range(HEADS)]
    bias = bias_ref[...]
    chunks = []
    for c in range(TL // CHUNK):
        heads = [_dot(w_tril[hd], vb[c * CHUNK:(c + 1) * CHUNK, hd * HEAD_DIM:(hd + 1) * HEAD_DIM])
                 for hd in range(HEADS)]
        chunks.append(jnp.concatenate(heads, axis=1) + bias)
    mix = jnp.concatenate(chunks, axis=0)

    ext = jnp.concatenate([pbuf[...], p], axis=0)
    pos = s * TL + lax.broadcasted_iota(jnp.int32, (TL, PG), 0)
    d_groups = []
    for gi, w in enumerate(WINDOWS):
        acc = ext[:, gi * PG:(gi + 1) * PG]
        k = 1
        while k < w:
            acc = acc + pltpu.roll(acc, k, 0)
            k *= 2
        cnt = jnp.minimum(pos + 1, w).astype(f32)
        d_groups.append(acc[W_MAX:] / cnt - p[:, gi * PG:(gi + 1) * PG])
    pbuf[...] = p[TL - W_MAX:]
    plast_ref[0] = p[TL - W_MAX:]
    y_b = _pool_out(d_groups, vec_ref, w_pool_ref)

    x1, h2, route = _mixer_tail(x, u, ga, gb, mix, y_b, mods, vec_ref, w_out_ref, w_r_ref, b_r_ref)
    x1_ref[0] = x1
    h2_ref[0] = h2.astype(bf16)
    route_ref[0] = route


def _stage1_sample_kernel(x_ref, mod_ref, vec_ref, w_in_ref, state_ref, w_pool_ref,
                          w_out_ref, w_r_ref, b_r_ref,
                          x1_ref, h2_ref, route_ref, p_ref, v_ref):
    x = x_ref[...]
    mods = [mod_ref[i] for i in range(6)]
    u, v, p, ga, gb = _front(x, mods, vec_ref, w_in_ref)
    v_ref[...] = v
    p_ref[...] = p
    mix = v * vec_ref[8:9] + vec_ref[9:10]
    d_groups = []
    for gi, w in enumerate(WINDOWS):
        sl = slice(gi * PG, (gi + 1) * PG)
        acc = p[:, sl]
        for r in range(W_MAX - w, W_MAX - 1):
            acc = acc + state_ref[r][:, sl]
        d_groups.append(acc / float(w) - p[:, sl])
    y_b = _pool_out(d_groups, vec_ref, w_pool_ref)
    x1, h2, route = _mixer_tail(x, u, ga, gb, mix, y_b, mods, vec_ref, w_out_ref, w_r_ref, b_r_ref)
    x1_ref[...] = x1
    h2_ref[...] = h2.astype(bf16)
    route_ref[...] = route


def _const_spec(shape):
    nd = len(shape)
    return pl.BlockSpec(shape, lambda *_: (0,) * nd, pipeline_mode=pl.Buffered(1))


def _stage1_prompt_call(x, mod_p, vecs, w_in_b, w_sp, bias_full, w_pool_b, w_out_b, w_r, b_r):
    b, s, _ = x.shape
    ns = s // TL
    tile = lambda bi, si: (bi, si, 0)
    return pl.pallas_call(
        _stage1_prompt_kernel,
        grid=(b, ns),
        in_specs=[
            pl.BlockSpec((1, TL, D), tile),
            pl.BlockSpec((6, 1, 1, D), lambda bi, si: (0, bi, 0, 0)),
            _const_spec(vecs.shape),
            _const_spec(w_in_b.shape),
            _const_spec(w_sp.shape),
            _const_spec(bias_full.shape),
            _const_spec(w_pool_b.shape),
            _const_spec(w_out_b.shape),
            _const_spec(w_r.shape),
            _const_spec(b_r.shape),
        ],
        out_specs=[
            pl.BlockSpec((1, TL, D), tile),
            pl.BlockSpec((1, TL, D), tile),
            pl.BlockSpec((1, TL, LANES), tile),
            pl.BlockSpec((1, W_MAX, PW), lambda bi, si: (bi, 0, 0)),
        ],
        out_shape=[
            jax.ShapeDtypeStruct((b, s, D), f32),
            jax.ShapeDtypeStruct((b, s, D), bf16),
            jax.ShapeDtypeStruct((b, s, LANES), f32),
            jax.ShapeDtypeStruct((b, W_MAX, PW), f32),
        ],
        scratch_shapes=[pltpu.VMEM((W_MAX, PW), f32)],
        compiler_params=pltpu.CompilerParams(
            dimension_semantics=("arbitrary", "arbitrary"), vmem_limit_bytes=VMEM_LIMIT),
    )(x, mod_p, vecs, w_in_b, w_sp, bias_full, w_pool_b, w_out_b, w_r, b_r)


def _stage1_sample_call(x, mod_s, vecs, w_in_b, state_t, w_pool_b, w_out_b, w_r, b_r):
    n = x.shape[0]
    return pl.pallas_call(
        _stage1_sample_kernel,
        out_shape=[
            jax.ShapeDtypeStruct((n, D), f32),
            jax.ShapeDtypeStruct((n, D), bf16),
            jax.ShapeDtypeStruct((n, LANES), f32),
            jax.ShapeDtypeStruct((n, PW), f32),
            jax.ShapeDtypeStruct((n, D), f32),
        ],
        compiler_params=pltpu.CompilerParams(vmem_limit_bytes=VMEM_LIMIT),
    )(x, mod_s, vecs, w_in_b, state_t, w_pool_b, w_out_b, w_r, b_r)


def _moe_kernel(x1_ref, h2_ref, route_ref, gt2_ref, g_ref, wgu_ref, wd_ref, ex_ref, o_ref):
    h2 = h2_ref[...]
    r = route_ref[...]
    r_hi = r.astype(bf16)
    r1 = r - r_hi.astype(f32)
    r_mid = r1.astype(bf16)
    r_lo = (r1 - r_mid.astype(f32)).astype(bf16)
    f = jnp.zeros((h2.shape[0], D), f32)
    for g in range(N_GROUPS):
        gu = _dot(h2, wgu_ref[g])
        gate = gu[:, :EPG * D_EXPERT]
        up = gu[:, EPG * D_EXPERT:]
        ex = ex_ref[g]
        cw = _dot(r_hi, ex) + _dot(r_mid, ex) + _dot(r_lo, ex)
        act = gate * jax.nn.sigmoid(gate) * up * cw
        f = f + _dot(act.astype(bf16), wd_ref[g])
    o_ref[...] = x1_ref[...] + gt2_ref[...].reshape(-1, D) * _rms(f, g_ref[...])


def _moe_call(x1, h2, route, gt2, gt2_spec, g_post, wgu, wd, ex):
    n = x1.shape[0]
    tile = lambda i: (i, 0)
    return pl.pallas_call(
        _moe_kernel,
        grid=(n // TM,),
        in_specs=[
            pl.BlockSpec((TM, D), tile),
            pl.BlockSpec((TM, D), tile),
            pl.BlockSpec((TM, LANES), tile),
            gt2_spec,
            _const_spec(g_post.shape),
            _const_spec(wgu.shape),
            _const_spec(wd.shape),
            _const_spec(ex.shape),
        ],
        out_specs=pl.BlockSpec((TM, D), tile),
        out_shape=jax.ShapeDtypeStruct((n, D), f32),
        compiler_params=pltpu.CompilerParams(
            dimension_semantics=("arbitrary",), vmem_limit_bytes=VMEM_LIMIT),
    )(x1, h2, route, gt2, g_post, wgu, wd, ex)


def kernel(x_prompt, x_sample, c_prompt, c_sample, state_pool, w_ada, b_ada, g_pre_mix, g_post_mix, g_pre_ffn, g_post_ffn, w_in, ln_v_g, ln_v_b, w_spatial, b_spatial, w_pool, pool_scale, w_out, w_router_grp, b_router_grp, w_router_exp, b_router_exp, w_exp_gate, w_exp_up, w_exp_down):
    depth = w_in.shape[0]
    assert depth == 1
    b, s, _ = x_prompt.shape
    n_s = x_sample.shape[0]
    l = 0

    c_all = jnp.concatenate([c_prompt, c_sample], axis=0)
    mod = _mod_call(c_all, w_ada[l], b_ada[l])
    mod_p = mod[:, :b].reshape(6, b, 1, D)
    mod_s = mod[:, b:]

    ws, bs = w_spatial[l], b_spatial[l]
    zeros = jnp.zeros((D,), f32)
    vecs = jnp.stack([
        g_pre_mix[l], g_post_mix[l], g_pre_ffn[l], g_post_ffn[l], ln_v_g[l], ln_v_b[l],
        pool_scale[l], zeros,
        jnp.repeat(ws[:, 0, 0], HEAD_DIM), jnp.repeat(bs[:, 0], HEAD_DIM),
        zeros, zeros, zeros, zeros, zeros, zeros])
    bias_full = jnp.repeat(bs.T, HEAD_DIM, axis=1)
    w_in_b = w_in[l].astype(bf16)
    w_out_b = w_out[l].astype(bf16)
    w_pool_b = w_pool[l].astype(bf16)
    w_r = jnp.zeros((D, LANES), f32)
    w_r = w_r.at[:, :N_EXPERTS].set(w_router_exp[l])
    w_r = w_r.at[:, GROUP_LANE0:GROUP_LANE0 + N_GROUPS].set(w_router_grp[l])
    b_r = jnp.zeros((1, LANES), f32)
    b_r = b_r.at[0, :N_EXPERTS].set(b_router_exp[l])
    b_r = b_r.at[0, GROUP_LANE0:GROUP_LANE0 + N_GROUPS].set(b_router_grp[l])

    x1_p, h2_p, route_p, plast = _stage1_prompt_call(
        x_prompt, mod_p, vecs, w_in_b, ws, bias_full, w_pool_b, w_out_b, w_r, b_r)
    state_t = jnp.transpose(state_pool[l], (1, 0, 2))
    x1_s, h2_s, route_s, p_s, v_s = _stage1_sample_call(
        x_sample.reshape(n_s, D), mod_s, vecs, w_in_b, state_t, w_pool_b, w_out_b, w_r, b_r)

    def grouped(w):
        return w.reshape(N_GROUPS, EPG, D, D_EXPERT).transpose(0, 2, 1, 3).reshape(
            N_GROUPS, D, EPG * D_EXPERT)
    wgu = jnp.concatenate([grouped(w_exp_gate[l]), grouped(w_exp_up[l])], axis=2).astype(bf16)
    wd = w_exp_down[l].reshape(N_GROUPS, EPG * D_EXPERT, D).astype(bf16)
    lane_id = jnp.arange(LANES)[None, :, None]
    col_e = (jnp.arange(EPG * D_EXPERT) // D_EXPERT)[None, None, :]
    ex = (lane_id == jnp.arange(N_GROUPS)[:, None, None] * EPG + col_e).astype(bf16)

    g_post = g_post_ffn[l].reshape(1, D)
    y_p = _moe_call(
        x1_p.reshape(b * s, D), h2_p.reshape(b * s, D), route_p.reshape(b * s, LANES),
        mod_p[5], pl.BlockSpec((1, 1, D), lambda i: ((i * TM) // s, 0, 0)),
        g_post, wgu, wd, ex)
    y_s = _moe_call_sample(x1_s, h2_s, route_s, mod_s[5], g_post, wgu, wd, ex)

    state_pool_prompt = plast[:, 1:][None]
    state_pool_sample = jnp.concatenate([state_pool[l][:, 1:], p_s[:, None, :]], axis=1)[None]
    chunk_v_sample = v_s.reshape(1, n_s, 1, D)
    return (y_p.reshape(b, s, D), y_s.reshape(n_s, 1, D), state_pool_prompt,
            state_pool_sample, chunk_v_sample)


def _moe_call_sample(x1, h2, route, gt2, g_post, wgu, wd, ex):
    n = x1.shape[0]
    return pl.pallas_call(
        _moe_kernel,
        out_shape=jax.ShapeDtypeStruct((n, D), f32),
        compiler_params=pltpu.CompilerParams(vmem_limit_bytes=VMEM_LIMIT),
    )(x1, h2, route, gt2, g_post, wgu, wd, ex)
```

```python
import functools

import jax
import jax.numpy as jnp
from jax import lax
from jax.experimental import pallas as pl
from jax.experimental.pallas import tpu as pltpu

D = 1024
CHUNK = 128
HEADS = 8
HEAD_DIM = 128
WINDOWS = (2, 4, 8, 16)
PW = 512
PG = 128
W_MAX = 16
N_GROUPS = 4
EPG = 8
N_EXPERTS = 32
D_EXPERT = 128
EPS = 1e-6
LANES = 128
GROUP_LANE0 = 32
GIDX_LANE = 96

TL = 256
MOE_WINDOW = 1024
ROW_ALIGN = 16
RB = 128
VMEM_LIMIT = 56 * 1024 * 1024

bf16 = jnp.bfloat16
f32 = jnp.float32


def _rms(x, g):
    ms = jnp.mean(x * x, axis=-1, keepdims=True)
    return x * lax.rsqrt(ms + EPS) * g


def _dot(a, b):
    return jnp.dot(a, b, preferred_element_type=f32)


def _mod_kernel(c_ref, w_ref, b_ref, o_ref):
    c = c_ref[...]
    a = (c * jax.nn.sigmoid(c)).astype(bf16)
    o_ref[0] = _dot(a, w_ref[...].astype(bf16)) + b_ref[0]


def _mod_call(c_all, w_ada, b_ada):
    n = c_all.shape[0]
    return pl.pallas_call(
        _mod_kernel,
        grid=(6,),
        in_specs=[
            pl.BlockSpec((n, D), lambda j: (0, 0)),
            pl.BlockSpec((D, D), lambda j: (0, j)),
            pl.BlockSpec((1, 1, D), lambda j: (j, 0, 0)),
        ],
        out_specs=pl.BlockSpec((1, n, D), lambda j: (j, 0, 0)),
        out_shape=jax.ShapeDtypeStruct((6, n, D), f32),
        compiler_params=pltpu.CompilerParams(
            dimension_semantics=("arbitrary",), vmem_limit_bytes=VMEM_LIMIT),
    )(c_all, w_ada, b_ada.reshape(6, 1, D))


def _route(logits):
    t = logits.shape[0]
    lane = lax.broadcasted_iota(jnp.int32, (t, LANES), 1)
    lane_f = lane.astype(f32)
    neg = -jnp.inf
    big = 1e9
    gmask = (lane >= GROUP_LANE0) & (lane < GROUP_LANE0 + N_GROUPS)
    gl = jnp.where(gmask, logits, neg)
    gmax = jnp.max(gl, axis=-1, keepdims=True)
    g_idx = jnp.min(jnp.where(gl == gmax, lane_f - GROUP_LANE0, big), axis=-1, keepdims=True)
    sumexp = jnp.sum(jnp.where(gmask, jnp.exp(gl - gmax), 0.0), axis=-1, keepdims=True)
    p_g = 1.0 / sumexp
    lane_grp = (lane >> 3).astype(f32)
    emask = (lane < N_EXPERTS) & (lane_grp == g_idx)
    el = jnp.where(emask, logits, neg)
    m1 = jnp.max(el, axis=-1, keepdims=True)
    i1 = jnp.min(jnp.where(el == m1, lane_f, big), axis=-1, keepdims=True)
    el2 = jnp.where(lane_f == i1, neg, el)
    m2 = jnp.max(el2, axis=-1, keepdims=True)
    i2 = jnp.min(jnp.where(el2 == m2, lane_f, big), axis=-1, keepdims=True)
    e = jnp.exp(m2 - m1)
    w1 = p_g / (1.0 + e)
    w2 = w1 * e

    def split3(w):
        hi = w.astype(bf16).astype(f32)
        mid = (w - hi).astype(bf16).astype(f32)
        lo = w - hi - mid
        return hi, mid, lo

    r3 = jnp.where(lane == GIDX_LANE, g_idx, 0.0)
    for idx, w in ((i1, w1), (i2, w2)):
        for part, wp in enumerate(split3(w)):
            r3 = r3 + jnp.where(lane_f == idx + float(part * N_EXPERTS), wp, 0.0)
    counts = jnp.sum(jnp.where(lane_f == g_idx, 1.0, 0.0), axis=0, keepdims=True)
    return r3.astype(bf16), jnp.broadcast_to(counts, (8, LANES))


def _mixer_tail(x, u, ga, gb, mix, y_b, mods, vec_ref, w_out_ref, w_r_ref, b_r_ref):
    sh1, sc1, gt1, sh2, sc2, gt2 = mods
    y_a = u * mix
    merged = jax.nn.sigmoid(ga) * y_a + jax.nn.sigmoid(gb) * y_b
    y = _dot(merged.astype(bf16), w_out_ref[...])
    x1 = x + gt1 * _rms(y, vec_ref[1:2])
    h2 = _rms(x1, vec_ref[2:3]) * (1.0 + sc2) + sh2
    h2_hi = h2.astype(bf16)
    h2_lo = (h2 - h2_hi.astype(f32)).astype(bf16)
    r = _dot(h2_hi, w_r_ref[...]) + _dot(h2_lo, w_r_ref[...])
    logits = r[:, :LANES] + r[:, LANES:] + b_r_ref[...]
    r3, counts = _route(logits)
    return x1, h2_hi, r3, counts


def _front(x, mods, vec_ref, w_in_ref):
    sh1, sc1 = mods[0], mods[1]
    h = _rms(x, vec_ref[0:1]) * (1.0 + sc1) + sh1
    hb = h.astype(bf16)
    u = jax.nn.gelu(_dot(hb, w_in_ref[:, 0:D]))
    v = jax.nn.gelu(_dot(hb, w_in_ref[:, D:2 * D]))
    mu = jnp.mean(v, axis=-1, keepdims=True)
    vc = v - mu
    var = jnp.mean(vc * vc, axis=-1, keepdims=True)
    v = vc * lax.rsqrt(var + EPS) * vec_ref[4:5] + vec_ref[5:6]
    p = _dot(hb, w_in_ref[:, 2 * D:2 * D + PW])
    ga = _dot(hb, w_in_ref[:, 2 * D + PW:3 * D + PW])
    gb = _dot(hb, w_in_ref[:, 3 * D + PW:4 * D + PW])
    return u, v, p, ga, gb


def _pool_out(d_groups, vec_ref, w_pool_ref):
    parts = [_dot(d.astype(bf16), w_pool_ref[gi]) for gi, d in enumerate(d_groups)]
    return jnp.concatenate(parts, axis=1) * vec_ref[6:7]


def _stage1_prompt_kernel(x_ref, mod_ref, vec_ref, w_in_ref, w_sp_ref, bias_ref, w_pool_ref,
                          w_out_ref, w_r_ref, b_r_ref,
                          x1_ref, h2_ref, r3_ref, cnt_ref, plast_ref, pbuf):
    s = pl.program_id(1)

    @pl.when(s == 0)
    def _():
        pbuf[...] = jnp.zeros_like(pbuf)

    x = x_ref[0]
    mods = [mod_ref[i, 0] for i in range(6)]
    u, v, p, ga, gb = _front(x, mods, vec_ref, w_in_ref)

    vb = v.astype(bf16)
    row = lax.broadcasted_iota(jnp.int32, (CHUNK, CHUNK), 0)
    col = lax.broadcasted_iota(jnp.int32, (CHUNK, CHUNK), 1)
    w_tril = [jnp.where(row >= col, w_sp_ref[hd], 0.0).astype(bf16) for hd in range(HEADS)]
    bias = bias_ref[...]
    chunks = []
    for c in range(TL // CHUNK):
        heads = [_dot(w_tril[hd], vb[c * CHUNK:(c + 1) * CHUNK, hd * HEAD_DIM:(hd + 1) * HEAD_DIM])
                 for hd in range(HEADS)]
        chunks.append(jnp.concatenate(heads, axis=1) + bias)
    mix = jnp.concatenate(chunks, axis=0)

    ext = jnp.concatenate([pbuf[...], p], axis=0)
    pos = s * TL + lax.broadcasted_iota(jnp.int32, (TL, PG), 0)
    d_groups = []
    for gi, w in enumerate(WINDOWS):
        acc = ext[:, gi * PG:(gi + 1) * PG]
        k = 1
        while k < w:
            acc = acc + pltpu.roll(acc, k, 0)
            k *= 2
        cnt = jnp.minimum(pos + 1, w).astype(f32)
        d_groups.append(acc[W_MAX:] / cnt - p[:, gi * PG:(gi + 1) * PG])
    pbuf[...] = p[TL - W_MAX:]
    plast_ref[0] = p[TL - W_MAX:]
    y_b = _pool_out(d_groups, vec_ref, w_pool_ref)

    x1, h2b, r3, counts = _mixer_tail(x, u, ga, gb, mix, y_b, mods, vec_ref, w_out_ref,
                                      w_r_ref, b_r_ref)
    x1_ref[0] = x1
    h2_ref[0] = h2b
    r3_ref[0] = r3
    cnt_ref[0] = counts


def _stage1_sample_kernel(x_ref, mod_ref, vec_ref, w_in_ref, state_ref, w_pool_ref,
                          w_out_ref, w_r_ref, b_r_ref,
                          x1_ref, h2_ref, r3_ref, cnt_ref, p_ref, v_ref):
    x = x_ref[...]
    mods = [mod_ref[i] for i in range(6)]
    u, v, p, ga, gb = _front(x, mods, vec_ref, w_in_ref)
    v_ref[...] = v
    p_ref[...] = p
    mix = v * vec_ref[8:9] + vec_ref[9:10]
    d_groups = []
    for gi, w in enumerate(WINDOWS):
        sl = slice(gi * PG, (gi + 1) * PG)
        acc = p[:, sl]
        for r in range(W_MAX - w, W_MAX - 1):
            acc = acc + state_ref[r][:, sl]
        d_groups.append(acc / float(w) - p[:, sl])
    y_b = _pool_out(d_groups, vec_ref, w_pool_ref)
    x1, h2b, r3, counts = _mixer_tail(x, u, ga, gb, mix, y_b, mods, vec_ref, w_out_ref,
                                      w_r_ref, b_r_ref)
    x1_ref[...] = x1
    h2_ref[...] = h2b
    r3_ref[...] = r3
    cnt_ref[...] = counts


def _const_spec(shape):
    nd = len(shape)
    return pl.BlockSpec(shape, lambda *_: (0,) * nd, pipeline_mode=pl.Buffered(1))


def _stage1_prompt_call(x, mod_p, vecs, w_in_b, w_sp, bias_full, w_pool_b, w_out_b, w_r, b_r):
    b, s, _ = x.shape
    ns = s // TL
    tile = lambda bi, si: (bi, si, 0)
    return pl.pallas_call(
        _stage1_prompt_kernel,
        grid=(b, ns),
        in_specs=[
            pl.BlockSpec((1, TL, D), tile),
            pl.BlockSpec((6, 1, 1, D), lambda bi, si: (0, bi, 0, 0)),
            _const_spec(vecs.shape),
            _const_spec(w_in_b.shape),
            _const_spec(w_sp.shape),
            _const_spec(bias_full.shape),
            _const_spec(w_pool_b.shape),
            _const_spec(w_out_b.shape),
            _const_spec(w_r.shape),
            _const_spec(b_r.shape),
        ],
        out_specs=[
            pl.BlockSpec((1, TL, D), tile),
            pl.BlockSpec((1, TL, D), tile),
            pl.BlockSpec((1, TL, LANES), tile),
            pl.BlockSpec((1, 8, LANES), lambda bi, si: (bi * ns + si, 0, 0)),
            pl.BlockSpec((1, W_MAX, PW), lambda bi, si: (bi, 0, 0)),
        ],
        out_shape=[
            jax.ShapeDtypeStruct((b, s, D), f32),
            jax.ShapeDtypeStruct((b, s, D), bf16),
            jax.ShapeDtypeStruct((b, s, LANES), bf16),
            jax.ShapeDtypeStruct((b * ns, 8, LANES), f32),
            jax.ShapeDtypeStruct((b, W_MAX, PW), f32),
        ],
        scratch_shapes=[pltpu.VMEM((W_MAX, PW), f32)],
        compiler_params=pltpu.CompilerParams(
            dimension_semantics=("arbitrary", "arbitrary"), vmem_limit_bytes=VMEM_LIMIT),
    )(x, mod_p, vecs, w_in_b, w_sp, bias_full, w_pool_b, w_out_b, w_r, b_r)


def _stage1_sample_call(x, mod_s, vecs, w_in_b, state_t, w_pool_b, w_out_b, w_r, b_r):
    n = x.shape[0]
    return pl.pallas_call(
        _stage1_sample_kernel,
        out_shape=[
            jax.ShapeDtypeStruct((n, D), f32),
            jax.ShapeDtypeStruct((n, D), bf16),
            jax.ShapeDtypeStruct((n, LANES), bf16),
            jax.ShapeDtypeStruct((8, LANES), f32),
            jax.ShapeDtypeStruct((n, PW), f32),
            jax.ShapeDtypeStruct((n, D), f32),
        ],
        compiler_params=pltpu.CompilerParams(vmem_limit_bytes=VMEM_LIMIT),
    )(x, mod_s, vecs, w_in_b, state_t, w_pool_b, w_out_b, w_r, b_r)


def _moe_buffer_rows(ts, nt):
    worst = ts * nt + nt * N_GROUPS * (ROW_ALIGN - 1) + N_GROUPS * (RB - 1)
    return -(-worst // RB) * RB


def _moe_kernel(cnt_ref, h2_ref, r3_ref, x1_ref, gt2_ref, g_ref, wgu_ref, wd_ref, ex_ref, o_ref,
                hsbuf, c3buf, ysbuf, hs_tile, c3_tile, ys_tile, pt_buf, *, ts, nt):
    w = pl.program_id(0)
    i = pl.program_id(1)
    rt = ts + LANES

    def run_len(tile, g):
        c = cnt_ref[(w * nt + tile) * N_GROUPS + g]
        return ((c + (ROW_ALIGN - 1)) // ROW_ALIGN) * ROW_ALIGN

    lens = [[run_len(t, g) for g in range(N_GROUPS)] for t in range(nt)]
    region = [sum(lens[t][g] for t in range(nt)) for g in range(N_GROUPS)]
    region = [((r + (RB - 1)) // RB) * RB for r in region]
    base = [sum(region[:g]) for g in range(N_GROUPS)]

    def copy_rows(src, src0, dst, dst0, nrows):
        def body(j, _):
            s0 = pl.multiple_of(src0 + j * ROW_ALIGN, ROW_ALIGN)
            d0 = pl.multiple_of(dst0 + j * ROW_ALIGN, ROW_ALIGN)
            dst[pl.ds(d0, ROW_ALIGN), :] = src[pl.ds(s0, ROW_ALIGN), :]
            return 0
        lax.fori_loop(0, nrows // ROW_ALIGN, body, 0)

    @pl.when((w == 0) & (i == 0))
    def _():
        hsbuf[...] = jnp.zeros_like(hsbuf)
        c3buf[...] = jnp.zeros_like(c3buf)
        ys_tile[...] = jnp.zeros_like(ys_tile)

    @pl.when(i == 0)
    def _sort_and_run_experts():
        lane = lax.broadcasted_iota(jnp.int32, (ts, LANES), 1)
        lane_f = lane.astype(f32)
        r_i = lax.broadcasted_iota(jnp.int32, (ts, ts), 0)
        c_i = lax.broadcasted_iota(jnp.int32, (ts, ts), 1)
        ltri = jnp.where(r_i > c_i, 1.0, 0.0).astype(bf16)
        lane8 = lax.broadcasted_iota(jnp.int32, (8, LANES), 1)
        sel = jnp.where(lane8 == 0, float(ROW_ALIGN), jnp.where(lane8 == 1, 1.0, 0.0)).astype(bf16)
        rt_lane = lax.broadcasted_iota(jnp.int32, (ts, rt), 1).astype(f32)
        rt_sub = lax.broadcasted_iota(jnp.int32, (rt, ts), 0).astype(f32)

        offs = list(base)
        for t in range(nt):
            h2t = h2_ref[t * ts:(t + 1) * ts, :]
            r3t = r3_ref[t * ts:(t + 1) * ts, :]
            gid = jnp.sum(jnp.where(lane == GIDX_LANE, r3t.astype(f32), 0.0), axis=-1, keepdims=True)
            onehot = jnp.where(lane_f == gid, 1.0, 0.0)
            rank = _dot(ltri, onehot.astype(bf16))
            seg = jnp.zeros((1, LANES), f32)
            start = 0
            for g in range(N_GROUPS):
                seg = seg + jnp.where(lane[0:1] == g, jnp.asarray(start, jnp.int32).astype(f32), 0.0)
                start = start + lens[t][g]
            pos = jnp.sum(onehot * (rank + seg), axis=-1, keepdims=True)
            pt = jnp.where(rt_lane == pos, 1.0, 0.0).astype(bf16)
            pt_buf[t] = pt
            q = jnp.floor(pos * (1.0 / ROW_ALIGN))
            digits = jnp.where(lane == 0, q, jnp.where(lane == 1, pos - q * ROW_ALIGN, 0.0))
            pos_row = lax.dot_general(sel, digits.astype(bf16), (((1,), (1,)), ((), ())),
                                      preferred_element_type=f32)
            p_mat = jnp.where(rt_sub == pos_row[0:1], 1.0, 0.0).astype(bf16)
            hs_tile[...] = _dot(p_mat, h2t).astype(bf16)
            c3_tile[...] = _dot(p_mat, r3t).astype(bf16)
            start = 0
            for g in range(N_GROUPS):
                copy_rows(hs_tile, start, hsbuf, offs[g], lens[t][g])
                copy_rows(c3_tile, start, c3buf, offs[g], lens[t][g])
                start = start + lens[t][g]
                offs[g] = offs[g] + lens[t][g]

        for g in range(N_GROUPS):
            def block(b, _, g=g):
                r0 = pl.multiple_of(base[g] + b * RB, RB)
                rows = hsbuf[pl.ds(r0, RB), :]
                gu = _dot(rows, wgu_ref[g])
                gate = gu[:, :EPG * D_EXPERT]
                up = gu[:, EPG * D_EXPERT:]
                cw = _dot(c3buf[pl.ds(r0, RB), :], ex_ref[g])
                act = gate * jax.nn.sigmoid(gate) * up * cw
                ysbuf[pl.ds(r0, RB), :] = _dot(act.astype(bf16), wd_ref[g]).astype(bf16)
                return 0
            lax.fori_loop(0, region[g] // RB, block, 0)

    start = 0
    for g in range(N_GROUPS):
        ln = run_len(i, g)
        off = base[g]
        for t in range(nt):
            off = off + jnp.where(t < i, lens[t][g], 0)
        copy_rows(ysbuf, off, ys_tile, start, ln)
        start = start + ln
    f = _dot(pt_buf[i], ys_tile[...])
    o_ref[...] = x1_ref[...] + gt2_ref[...].reshape(-1, D) * _rms(f, g_ref[...])


def _moe_call(cnt, x1, h2, r3, gt2, gt2_spec, g_post, wgu, wd, ex, ts, nt):
    n = x1.shape[0]
    win = ts * nt
    rbuf = _moe_buffer_rows(ts, nt)
    rt = ts + LANES
    grid_spec = pltpu.PrefetchScalarGridSpec(
        num_scalar_prefetch=1,
        grid=(n // win, nt),
        in_specs=[
            pl.BlockSpec((win, D), lambda w, i, c: (w, 0)),
            pl.BlockSpec((win, LANES), lambda w, i, c: (w, 0)),
            pl.BlockSpec((ts, D), lambda w, i, c: (w * nt + i, 0)),
            gt2_spec,
            _const_spec(g_post.shape),
            _const_spec(wgu.shape),
            _const_spec(wd.shape),
            _const_spec(ex.shape),
        ],
        out_specs=pl.BlockSpec((ts, D), lambda w, i, c: (w * nt + i, 0)),
        scratch_shapes=[
            pltpu.VMEM((rbuf, D), bf16),
            pltpu.VMEM((rbuf, LANES), bf16),
            pltpu.VMEM((rbuf, D), bf16),
            pltpu.VMEM((rt, D), bf16),
            pltpu.VMEM((rt, LANES), bf16),
            pltpu.VMEM((rt, D), bf16),
            pltpu.VMEM((nt, ts, rt), bf16),
        ],
    )
    return pl.pallas_call(
        functools.partial(_moe_kernel, ts=ts, nt=nt),
        grid_spec=grid_spec,
        out_shape=jax.ShapeDtypeStruct((n, D), f32),
        compiler_params=pltpu.CompilerParams(
            dimension_semantics=("arbitrary", "arbitrary"), vmem_limit_bytes=VMEM_LIMIT),
    )(cnt, h2, r3, x1, gt2, g_post, wgu, wd, ex)


def _count_table(cnt):
    return cnt[:, 0, :N_GROUPS].astype(jnp.int32).reshape(-1)


def kernel(x_prompt, x_sample, c_prompt, c_sample, state_pool, w_ada, b_ada, g_pre_mix, g_post_mix, g_pre_ffn, g_post_ffn, w_in, ln_v_g, ln_v_b, w_spatial, b_spatial, w_pool, pool_scale, w_out, w_router_grp, b_router_grp, w_router_exp, b_router_exp, w_exp_gate, w_exp_up, w_exp_down):
    depth = w_in.shape[0]
    assert depth == 1
    b, s, _ = x_prompt.shape
    n_s = x_sample.shape[0]
    l = 0

    c_all = jnp.concatenate([c_prompt, c_sample], axis=0)
    mod = _mod_call(c_all, w_ada[l], b_ada[l])
    mod_p = mod[:, :b].reshape(6, b, 1, D)
    mod_s = mod[:, b:]

    ws, bs = w_spatial[l], b_spatial[l]
    zeros = jnp.zeros((D,), f32)
    vecs = jnp.stack([
        g_pre_mix[l], g_post_mix[l], g_pre_ffn[l], g_post_ffn[l], ln_v_g[l], ln_v_b[l],
        pool_scale[l], zeros,
        jnp.repeat(ws[:, 0, 0], HEAD_DIM), jnp.repeat(bs[:, 0], HEAD_DIM),
        zeros, zeros, zeros, zeros, zeros, zeros])
    bias_full = jnp.repeat(bs.T, HEAD_DIM, axis=1)
    w_in_b = w_in[l].astype(bf16)
    w_out_b = w_out[l].astype(bf16)
    w_pool_b = w_pool[l].astype(bf16)
    w_r = jnp.zeros((D, LANES), f32)
    w_r = w_r.at[:, :N_EXPERTS].set(w_router_exp[l])
    w_r = w_r.at[:, GROUP_LANE0:GROUP_LANE0 + N_GROUPS].set(w_router_grp[l])
    w_r_hi = w_r.astype(bf16)
    w_r_lo = (w_r - w_r_hi.astype(f32)).astype(bf16)
    w_r2 = jnp.concatenate([w_r_hi, w_r_lo], axis=1)
    b_r = jnp.zeros((1, LANES), f32)
    b_r = b_r.at[0, :N_EXPERTS].set(b_router_exp[l])
    b_r = b_r.at[0, GROUP_LANE0:GROUP_LANE0 + N_GROUPS].set(b_router_grp[l])

    x1_p, h2_p, r3_p, cnt_p, plast = _stage1_prompt_call(
        x_prompt, mod_p, vecs, w_in_b, ws, bias_full, w_pool_b, w_out_b, w_r2, b_r)
    state_t = jnp.transpose(state_pool[l], (1, 0, 2))
    x1_s, h2_s, r3_s, cnt_s, p_s, v_s = _stage1_sample_call(
        x_sample.reshape(n_s, D), mod_s, vecs, w_in_b, state_t, w_pool_b, w_out_b, w_r2, b_r)

    def grouped(w):
        return w.reshape(N_GROUPS, EPG, D, D_EXPERT).transpose(0, 2, 1, 3).reshape(
            N_GROUPS, D, EPG * D_EXPERT)
    wgu = jnp.concatenate([grouped(w_exp_gate[l]), grouped(w_exp_up[l])], axis=2).astype(bf16)
    wd = w_exp_down[l].reshape(N_GROUPS, EPG * D_EXPERT, D).astype(bf16)
    lane_e = (jnp.arange(LANES) % N_EXPERTS)[None, :, None]
    lane_ok = (jnp.arange(LANES) < 3 * N_EXPERTS)[None, :, None]
    col_e = (jnp.arange(EPG * D_EXPERT) // D_EXPERT)[None, None, :]
    ex = ((lane_e == jnp.arange(N_GROUPS)[:, None, None] * EPG + col_e) & lane_ok).astype(bf16)

    g_post = g_post_ffn[l].reshape(1, D)
    nt = MOE_WINDOW // TL
    y_p = _moe_call(
        _count_table(cnt_p), x1_p.reshape(b * s, D), h2_p.reshape(b * s, D),
        r3_p.reshape(b * s, LANES), mod_p[5],
        pl.BlockSpec((1, 1, D), lambda w, i, c: (((w * nt + i) * TL) // s, 0, 0)),
        g_post, wgu, wd, ex, TL, nt)
    y_s = _moe_call(
        _count_table(cnt_s[None]), x1_s, h2_s, r3_s, mod_s[5],
        pl.BlockSpec((n_s, D), lambda w, i, c: (0, 0)),
        g_post, wgu, wd, ex, n_s, 1)

    state_pool_prompt = plast[:, 1:][None]
    state_pool_sample = jnp.concatenate([state_pool[l][:, 1:], p_s[:, None, :]], axis=1)[None]
    chunk_v_sample = v_s.reshape(1, n_s, 1, D)
    return (y_p.reshape(b, s, D), y_s.reshape(n_s, 1, D), state_pool_prompt,
            state_pool_sample, chunk_v_sample)
```

```python
import functools

import jax
import jax.numpy as jnp
from jax import lax
from jax.experimental import pallas as pl
from jax.experimental.pallas import tpu as pltpu

D = 1024
CHUNK = 128
HEADS = 8
HEAD_DIM = 128
WINDOWS = (2, 4, 8, 16)
PW = 512
PG = 128
W_MAX = 16
N_GROUPS = 4
EPG = 8
N_EXPERTS = 32
D_EXPERT = 128
EPS = 1e-6
LANES = 128
GROUP_LANE0 = 32
GIDX_LANE = 96

TL = 256
MOE_WINDOW = 1024
ROW_ALIGN = 16
RB = 128
VMEM_LIMIT = 56 * 1024 * 1024

bf16 = jnp.bfloat16
f32 = jnp.float32


def _rms(x, g):
    ms = jnp.mean(x * x, axis=-1, keepdims=True)
    return x * lax.rsqrt(ms + EPS) * g


def _dot(a, b):
    return jnp.dot(a, b, preferred_element_type=f32)


_GELU_C = 2.0 * 0.7978845608028654


def _gelu(x):
    t = x * ((-_GELU_C) + (-_GELU_C * 0.044715) * (x * x))
    return x / (1.0 + jnp.exp(t))


def _cast_kernel(w_ref, o_ref):
    o_ref[...] = w_ref[...].astype(bf16)


def _cast_call(w, block_cols):
    rows, cols = w.shape
    spec = pl.BlockSpec((rows, block_cols), lambda j: (0, j))
    return pl.pallas_call(
        _cast_kernel, grid=(cols // block_cols,), in_specs=[spec], out_specs=spec,
        out_shape=jax.ShapeDtypeStruct((rows, cols), bf16),
        compiler_params=pltpu.CompilerParams(dimension_semantics=("arbitrary",)),
    )(w)


def _expert_cast_kernel(g_ref, u_ref, d_ref, wg_ref, wu_ref, wd_ref):
    wg_ref[...] = g_ref[0].astype(bf16)
    wu_ref[...] = u_ref[0].astype(bf16)
    wd_ref[...] = d_ref[0].astype(bf16)


def _expert_cast_call(w_gate, w_up, w_down):
    col_block = pl.BlockSpec((D, D_EXPERT), lambda e: (e // EPG, e % EPG))
    return pl.pallas_call(
        _expert_cast_kernel,
        grid=(N_EXPERTS,),
        in_specs=[
            pl.BlockSpec((1, D, D_EXPERT), lambda e: (e, 0, 0)),
            pl.BlockSpec((1, D, D_EXPERT), lambda e: (e, 0, 0)),
            pl.BlockSpec((1, D_EXPERT, D), lambda e: (e, 0, 0)),
        ],
        out_specs=[col_block, col_block, pl.BlockSpec((D_EXPERT, D), lambda e: (e, 0))],
        out_shape=[
            jax.ShapeDtypeStruct((N_GROUPS * D, EPG * D_EXPERT), bf16),
            jax.ShapeDtypeStruct((N_GROUPS * D, EPG * D_EXPERT), bf16),
            jax.ShapeDtypeStruct((N_EXPERTS * D_EXPERT, D), bf16),
        ],
        compiler_params=pltpu.CompilerParams(dimension_semantics=("arbitrary",)),
    )(w_gate, w_up, w_down)


def _mod_kernel(c_ref, w_ref, b_ref, op_ref, os_ref):
    c = c_ref[...]
    a = (c * jax.nn.sigmoid(c)).astype(bf16)
    m = _dot(a, w_ref[...].astype(bf16)) + b_ref[0]
    nb = op_ref.shape[1]
    op_ref[0] = m[:nb]
    os_ref[0] = m[nb:]


def _mod_call(c_all, w_ada, b_ada, nb):
    n = c_all.shape[0]
    return pl.pallas_call(
        _mod_kernel,
        grid=(6,),
        in_specs=[
            pl.BlockSpec((n, D), lambda j: (0, 0)),
            pl.BlockSpec((D, D), lambda j: (0, j)),
            pl.BlockSpec((1, 1, D), lambda j: (j, 0, 0)),
        ],
        out_specs=[pl.BlockSpec((1, nb, D), lambda j: (j, 0, 0)),
                   pl.BlockSpec((1, n - nb, D), lambda j: (j, 0, 0))],
        out_shape=[jax.ShapeDtypeStruct((6, nb, D), f32),
                   jax.ShapeDtypeStruct((6, n - nb, D), f32)],
        compiler_params=pltpu.CompilerParams(
            dimension_semantics=("arbitrary",), vmem_limit_bytes=VMEM_LIMIT),
    )(c_all, w_ada, b_ada.reshape(6, 1, D))


def _route(logits):
    t = logits.shape[0]
    lane = lax.broadcasted_iota(jnp.int32, (t, LANES), 1)
    lane_f = lane.astype(f32)
    neg = -jnp.inf
    big = 1e9
    gmask = (lane >= GROUP_LANE0) & (lane < GROUP_LANE0 + N_GROUPS)
    gl = jnp.where(gmask, logits, neg)
    gmax = jnp.max(gl, axis=-1, keepdims=True)
    g_idx = jnp.min(jnp.where(gl == gmax, lane_f - GROUP_LANE0, big), axis=-1, keepdims=True)
    sumexp = jnp.sum(jnp.where(gmask, jnp.exp(gl - gmax), 0.0), axis=-1, keepdims=True)
    p_g = 1.0 / sumexp
    lane_grp = (lane >> 3).astype(f32)
    emask = (lane < N_EXPERTS) & (lane_grp == g_idx)
    el = jnp.where(emask, logits, neg)
    m1 = jnp.max(el, axis=-1, keepdims=True)
    i1 = jnp.min(jnp.where(el == m1, lane_f, big), axis=-1, keepdims=True)
    el2 = jnp.where(lane_f == i1, neg, el)
    m2 = jnp.max(el2, axis=-1, keepdims=True)
    i2 = jnp.min(jnp.where(el2 == m2, lane_f, big), axis=-1, keepdims=True)
    e = jnp.exp(m2 - m1)
    w1 = p_g / (1.0 + e)
    w2 = w1 * e

    def split3(w):
        hi = w.astype(bf16).astype(f32)
        mid = (w - hi).astype(bf16).astype(f32)
        lo = w - hi - mid
        return hi, mid, lo

    r3 = jnp.where(lane == GIDX_LANE, g_idx, 0.0)
    for idx, w in ((i1, w1), (i2, w2)):
        for part, wp in enumerate(split3(w)):
            r3 = r3 + jnp.where(lane_f == idx + float(part * N_EXPERTS), wp, 0.0)
    counts = jnp.sum(jnp.where(lane_f == g_idx, 1.0, 0.0), axis=0, keepdims=True)
    return r3.astype(bf16), jnp.broadcast_to(counts, (8, LANES))


def _mixer_tail(x, u, ga, gb, mix, y_b, mods, vec_ref, w_out_ref, w_r_ref, b_r_ref):
    sh1, sc1, gt1, sh2, sc2, gt2 = mods
    y_a = u * mix
    merged = jax.nn.sigmoid(ga) * y_a + jax.nn.sigmoid(gb) * y_b
    y = _dot(merged.astype(bf16), w_out_ref[...])
    x1 = x + gt1 * _rms(y, vec_ref[1:2])
    h2 = _rms(x1, vec_ref[2:3] * (1.0 + sc2)) + sh2
    h2_hi = h2.astype(bf16)
    h2_lo = (h2 - h2_hi.astype(f32)).astype(bf16)
    r = _dot(h2_hi, w_r_ref[...]) + _dot(h2_lo, w_r_ref[...])
    logits = r[:, :LANES] + r[:, LANES:] + b_r_ref[...]
    r3, counts = _route(logits)
    return x1, h2_hi, r3, counts


def _front(x, mods, vec_ref, w_in_ref):
    sh1, sc1 = mods[0], mods[1]
    h = _rms(x, vec_ref[0:1] * (1.0 + sc1)) + sh1
    hb = h.astype(bf16)
    u = _gelu(_dot(hb, w_in_ref[:, 0:D]))
    v = _gelu(_dot(hb, w_in_ref[:, D:2 * D]))
    mu = jnp.mean(v, axis=-1, keepdims=True)
    vc = v - mu
    var = jnp.mean(vc * vc, axis=-1, keepdims=True)
    v = vc * lax.rsqrt(var + EPS) * vec_ref[4:5] + vec_ref[5:6]
    p = _dot(hb, w_in_ref[:, 2 * D:2 * D + PW])
    ga = _dot(hb, w_in_ref[:, 2 * D + PW:3 * D + PW])
    gb = _dot(hb, w_in_ref[:, 3 * D + PW:4 * D + PW])
    return u, v, p, ga, gb


def _pool_out(d_groups, vec_ref, w_pool_ref):
    parts = [_dot(d.astype(bf16), w_pool_ref[gi].astype(bf16)) for gi, d in enumerate(d_groups)]
    return jnp.concatenate(parts, axis=1) * vec_ref[6:7]


def _stage1_prompt_kernel(x_ref, mod_ref, vec_ref, w_in_ref, w_sp_ref, bias_ref, w_pool_ref,
                          w_out_ref, w_r_ref, b_r_ref,
                          x1_ref, h2_ref, r3_ref, cnt_ref, plast_ref, pbuf):
    s = pl.program_id(1)

    @pl.when(s == 0)
    def _():
        pbuf[...] = jnp.zeros_like(pbuf)

    x = x_ref[0]
    bi = pl.program_id(0)
    mods = [mod_ref[i, pl.ds(bi, 1), :] for i in range(6)]
    u, v, p, ga, gb = _front(x, mods, vec_ref, w_in_ref)

    vb = v.astype(bf16)
    row = lax.broadcasted_iota(jnp.int32, (CHUNK, CHUNK), 0)
    col = lax.broadcasted_iota(jnp.int32, (CHUNK, CHUNK), 1)
    w_tril = [jnp.where(row >= col, w_sp_ref[hd], 0.0).astype(bf16) for hd in range(HEADS)]
    bias = bias_ref[...]
    chunks = []
    for c in range(TL // CHUNK):
        heads = [_dot(w_tril[hd], vb[c * CHUNK:(c + 1) * CHUNK, hd * HEAD_DIM:(hd + 1) * HEAD_DIM])
                 for hd in range(HEADS)]
        chunks.append(jnp.concatenate(heads, axis=1) + bias)
    mix = jnp.concatenate(chunks, axis=0)

    ext = jnp.concatenate([pbuf[...], p], axis=0)
    pos = s * TL + lax.broadcasted_iota(jnp.int32, (TL, PG), 0)
    d_groups = []
    for gi, w in enumerate(WINDOWS):
        acc = ext[:, gi * PG:(gi + 1) * PG]
        k = 1
        while k < w:
            acc = acc + pltpu.roll(acc, k, 0)
            k *= 2
        cnt = jnp.minimum(pos + 1, w).astype(f32)
        d_groups.append(acc[W_MAX:] / cnt - p[:, gi * PG:(gi + 1) * PG])
    pbuf[...] = p[TL - W_MAX:]
    plast_ref[0] = p[TL - W_MAX:]
    y_b = _pool_out(d_groups, vec_ref, w_pool_ref)

    x1, h2b, r3, counts = _mixer_tail(x, u, ga, gb, mix, y_b, mods, vec_ref, w_out_ref,
                                      w_r_ref, b_r_ref)
    x1_ref[0] = x1
    h2_ref[0] = h2b
    r3_ref[0] = r3
    cnt_ref[0] = counts


def _stage1_sample_kernel(x_ref, mod_ref, vec_ref, w_in_ref, state_ref, w_pool_ref,
                          w_out_ref, w_r_ref, b_r_ref,
                          x1_ref, h2_ref, r3_ref, cnt_ref, p_ref, v_ref):
    x = x_ref[...]
    mods = [mod_ref[i] for i in range(6)]
    u, v, p, ga, gb = _front(x, mods, vec_ref, w_in_ref)
    v_ref[...] = v
    p_ref[...] = p
    mix = v * vec_ref[8:9] + vec_ref[9:10]
    d_groups = []
    for gi, w in enumerate(WINDOWS):
        sl = slice(gi * PG, (gi + 1) * PG)
        acc = p[:, sl]
        for r in range(W_MAX - w, W_MAX - 1):
            acc = acc + state_ref[r][:, sl]
        d_groups.append(acc / float(w) - p[:, sl])
    y_b = _pool_out(d_groups, vec_ref, w_pool_ref)
    x1, h2b, r3, counts = _mixer_tail(x, u, ga, gb, mix, y_b, mods, vec_ref, w_out_ref,
                                      w_r_ref, b_r_ref)
    x1_ref[...] = x1
    h2_ref[...] = h2b
    r3_ref[...] = r3
    cnt_ref[...] = counts


def _const_spec(shape):
    nd = len(shape)
    return pl.BlockSpec(shape, lambda *_: (0,) * nd, pipeline_mode=pl.Buffered(1))


def _stage1_prompt_call(x, mod_p, vecs, w_in_b, w_sp, bias_full, w_pool_b, w_out_b, w_r, b_r):
    b, s, _ = x.shape
    ns = s // TL
    tile = lambda bi, si: (bi, si, 0)
    return pl.pallas_call(
        _stage1_prompt_kernel,
        grid=(b, ns),
        in_specs=[
            pl.BlockSpec((1, TL, D), tile),
            _const_spec(mod_p.shape),
            _const_spec(vecs.shape),
            _const_spec(w_in_b.shape),
            _const_spec(w_sp.shape),
            _const_spec(bias_full.shape),
            _const_spec(w_pool_b.shape),
            _const_spec(w_out_b.shape),
            _const_spec(w_r.shape),
            _const_spec(b_r.shape),
        ],
        out_specs=[
            pl.BlockSpec((1, TL, D), tile),
            pl.BlockSpec((1, TL, D), tile),
            pl.BlockSpec((1, TL, LANES), tile),
            pl.BlockSpec((1, 8, LANES), lambda bi, si: (bi * ns + si, 0, 0)),
            pl.BlockSpec((1, W_MAX, PW), lambda bi, si: (bi, 0, 0)),
        ],
        out_shape=[
            jax.ShapeDtypeStruct((b, s, D), f32),
            jax.ShapeDtypeStruct((b, s, D), bf16),
            jax.ShapeDtypeStruct((b, s, LANES), bf16),
            jax.ShapeDtypeStruct((b * ns, 8, LANES), f32),
            jax.ShapeDtypeStruct((b, W_MAX, PW), f32),
        ],
        scratch_shapes=[pltpu.VMEM((W_MAX, PW), f32)],
        compiler_params=pltpu.CompilerParams(
            dimension_semantics=("arbitrary", "arbitrary"), vmem_limit_bytes=VMEM_LIMIT),
    )(x, mod_p, vecs, w_in_b, w_sp, bias_full, w_pool_b, w_out_b, w_r, b_r)


def _stage1_sample_call(x, mod_s, vecs, w_in_b, state_t, w_pool_b, w_out_b, w_r, b_r):
    n = x.shape[0]
    return pl.pallas_call(
        _stage1_sample_kernel,
        out_shape=[
            jax.ShapeDtypeStruct((n, D), f32),
            jax.ShapeDtypeStruct((n, D), bf16),
            jax.ShapeDtypeStruct((n, LANES), bf16),
            jax.ShapeDtypeStruct((8, LANES), f32),
            jax.ShapeDtypeStruct((n, PW), f32),
            jax.ShapeDtypeStruct((n, D), f32),
        ],
        compiler_params=pltpu.CompilerParams(vmem_limit_bytes=VMEM_LIMIT),
    )(x, mod_s, vecs, w_in_b, state_t, w_pool_b, w_out_b, w_r, b_r)


def _moe_buffer_rows(ts, nt):
    worst = ts * nt + nt * N_GROUPS * (ROW_ALIGN - 1) + N_GROUPS * (RB - 1)
    return -(-worst // RB) * RB


def _moe_kernel(cnt_ref, h2_ref, r3_ref, x1_ref, gt2_ref, g_ref, wg_ref, wu_ref, wd_ref, ex_ref,
                o_ref,
                hsbuf, c3buf, ysbuf, hs_tile, c3_tile, ys_tile, pt_buf, *, ts, nt, tpr):
    w = pl.program_id(0)
    i = pl.program_id(1)
    rt = ts + LANES

    def run_len(tile, g):
        c = cnt_ref[(w * nt + tile) * N_GROUPS + g]
        return ((c + (ROW_ALIGN - 1)) // ROW_ALIGN) * ROW_ALIGN

    lens = [[run_len(t, g) for g in range(N_GROUPS)] for t in range(nt)]
    region = [sum(lens[t][g] for t in range(nt)) for g in range(N_GROUPS)]
    region = [((r + (RB - 1)) // RB) * RB for r in region]
    base = [sum(region[:g]) for g in range(N_GROUPS)]

    def copy_rows(src, src0, dst, dst0, nrows):
        def body(j, _):
            s0 = pl.multiple_of(src0 + j * ROW_ALIGN, ROW_ALIGN)
            d0 = pl.multiple_of(dst0 + j * ROW_ALIGN, ROW_ALIGN)
            dst[pl.ds(d0, ROW_ALIGN), :] = src[pl.ds(s0, ROW_ALIGN), :]
            return 0
        lax.fori_loop(0, nrows // ROW_ALIGN, body, 0)

    @pl.when((w == 0) & (i == 0))
    def _():
        hsbuf[...] = jnp.zeros_like(hsbuf)
        c3buf[...] = jnp.zeros_like(c3buf)
        ys_tile[...] = jnp.zeros_like(ys_tile)

    @pl.when(i == 0)
    def _sort_and_run_experts():
        lane = lax.broadcasted_iota(jnp.int32, (ts, LANES), 1)
        lane_f = lane.astype(f32)
        r_i = lax.broadcasted_iota(jnp.int32, (ts, ts), 0)
        c_i = lax.broadcasted_iota(jnp.int32, (ts, ts), 1)
        ltri = jnp.where(r_i > c_i, 1.0, 0.0).astype(bf16)
        lane8 = lax.broadcasted_iota(jnp.int32, (8, LANES), 1)
        sel = jnp.where(lane8 == 0, float(ROW_ALIGN), jnp.where(lane8 == 1, 1.0, 0.0)).astype(bf16)
        rt_lane = lax.broadcasted_iota(jnp.int32, (ts, rt), 1).astype(f32)
        rt_sub = lax.broadcasted_iota(jnp.int32, (rt, ts), 0).astype(f32)

        offs = list(base)
        for t in range(nt):
            h2t = h2_ref[t * ts:(t + 1) * ts, :]
            r3t = r3_ref[t * ts:(t + 1) * ts, :]
            gid = jnp.sum(jnp.where(lane == GIDX_LANE, r3t.astype(f32), 0.0), axis=-1, keepdims=True)
            onehot = jnp.where(lane_f == gid, 1.0, 0.0)
            rank = _dot(ltri, onehot.astype(bf16))
            seg = jnp.zeros((1, LANES), f32)
            start = 0
            for g in range(N_GROUPS):
                seg = seg + jnp.where(lane[0:1] == g, jnp.asarray(start, jnp.int32).astype(f32), 0.0)
                start = start + lens[t][g]
            pos = jnp.sum(onehot * (rank + seg), axis=-1, keepdims=True)
            pt = jnp.where(rt_lane == pos, 1.0, 0.0).astype(bf16)
            pt_buf[t] = pt
            q = jnp.floor(pos * (1.0 / ROW_ALIGN))
            digits = jnp.where(lane == 0, q, jnp.where(lane == 1, pos - q * ROW_ALIGN, 0.0))
            pos_row = lax.dot_general(sel, digits.astype(bf16), (((1,), (1,)), ((), ())),
                                      preferred_element_type=f32)
            p_mat = jnp.where(rt_sub == pos_row[0:1], 1.0, 0.0).astype(bf16)
            hs_tile[...] = _dot(p_mat, h2t).astype(bf16)
            c3_tile[...] = _dot(p_mat, r3t).astype(bf16)
            start = 0
            for g in range(N_GROUPS):
                copy_rows(hs_tile, start, hsbuf, offs[g], lens[t][g])
                copy_rows(c3_tile, start, c3buf, offs[g], lens[t][g])
                start = start + lens[t][g]
                offs[g] = offs[g] + lens[t][g]

        for g in range(N_GROUPS):
            def block(b, _, g=g):
                r0 = pl.multiple_of(base[g] + b * RB, RB)
                rows = hsbuf[pl.ds(r0, RB), :]
                gate = _dot(rows, wg_ref[g * D:(g + 1) * D, :])
                up = _dot(rows, wu_ref[g * D:(g + 1) * D, :])
                cw = _dot(c3buf[pl.ds(r0, RB), :], ex_ref[g * LANES:(g + 1) * LANES, :])
                act = gate * jax.nn.sigmoid(gate) * up * cw
                ysbuf[pl.ds(r0, RB), :] = _dot(
                    act.astype(bf16), wd_ref[g * D:(g + 1) * D, :]).astype(bf16)
                return 0
            lax.fori_loop(0, region[g] // RB, block, 0)

    start = 0
    for g in range(N_GROUPS):
        ln = run_len(i, g)
        off = base[g]
        for t in range(nt):
            off = off + jnp.where(t < i, lens[t][g], 0)
        copy_rows(ysbuf, off, ys_tile, start, ln)
        start = start + ln
    f = _dot(pt_buf[i], ys_tile[...])
    tok0 = (w * nt + i) * ts
    gt2 = gt2_ref[pl.ds(tok0, ts), :] if tpr == 1 else gt2_ref[pl.ds(tok0 // tpr, 1), :]
    o_ref[...] = x1_ref[...] + gt2 * _rms(f, g_ref[...])


def _moe_call(cnt, x1, h2, r3, gt2, g_post, wg, wu, wd, ex, ts, nt, tpr):
    n = x1.shape[0]
    win = ts * nt
    rbuf = _moe_buffer_rows(ts, nt)
    rt = ts + LANES
    grid_spec = pltpu.PrefetchScalarGridSpec(
        num_scalar_prefetch=1,
        grid=(n // win, nt),
        in_specs=[
            pl.BlockSpec((win, D), lambda w, i, c: (w, 0)),
            pl.BlockSpec((win, LANES), lambda w, i, c: (w, 0)),
            pl.BlockSpec((ts, D), lambda w, i, c: (w * nt + i, 0)),
            _const_spec(gt2.shape),
            _const_spec(g_post.shape),
            _const_spec(wg.shape),
            _const_spec(wu.shape),
            _const_spec(wd.shape),
            _const_spec(ex.shape),
        ],
        out_specs=pl.BlockSpec((ts, D), lambda w, i, c: (w * nt + i, 0)),
        scratch_shapes=[
            pltpu.VMEM((rbuf, D), bf16),
            pltpu.VMEM((rbuf, LANES), bf16),
            pltpu.VMEM((rbuf, D), bf16),
            pltpu.VMEM((rt, D), bf16),
            pltpu.VMEM((rt, LANES), bf16),
            pltpu.VMEM((rt, D), bf16),
            pltpu.VMEM((nt, ts, rt), bf16),
        ],
    )
    return pl.pallas_call(
        functools.partial(_moe_kernel, ts=ts, nt=nt, tpr=tpr),
        grid_spec=grid_spec,
        out_shape=jax.ShapeDtypeStruct((n, D), f32),
        compiler_params=pltpu.CompilerParams(
            dimension_semantics=("arbitrary", "arbitrary"), vmem_limit_bytes=VMEM_LIMIT),
    )(cnt, h2, r3, x1, gt2, g_post, wg, wu, wd, ex)


def _count_table(cnt):
    return cnt[:, 0, :N_GROUPS].astype(jnp.int32).reshape(-1)


def kernel(x_prompt, x_sample, c_prompt, c_sample, state_pool, w_ada, b_ada, g_pre_mix, g_post_mix, g_pre_ffn, g_post_ffn, w_in, ln_v_g, ln_v_b, w_spatial, b_spatial, w_pool, pool_scale, w_out, w_router_grp, b_router_grp, w_router_exp, b_router_exp, w_exp_gate, w_exp_up, w_exp_down):
    depth = w_in.shape[0]
    assert depth == 1
    b, s, _ = x_prompt.shape
    n_s = x_sample.shape[0]
    l = 0

    c_all = jnp.concatenate([c_prompt, c_sample], axis=0)
    mod_p, mod_s = _mod_call(c_all, w_ada[l], b_ada[l], b)

    ws, bs = w_spatial[l], b_spatial[l]
    zeros = jnp.zeros((D,), f32)
    vecs = jnp.stack([
        g_pre_mix[l], g_post_mix[l], g_pre_ffn[l], g_post_ffn[l], ln_v_g[l], ln_v_b[l],
        pool_scale[l], zeros,
        jnp.repeat(ws[:, 0, 0], HEAD_DIM), jnp.repeat(bs[:, 0], HEAD_DIM),
        zeros, zeros, zeros, zeros, zeros, zeros])
    bias_full = jnp.repeat(bs.T, HEAD_DIM, axis=1)
    w_in_b = _cast_call(w_in[l], 512)
    w_out_b = _cast_call(w_out[l], D)
    pad = LANES - N_EXPERTS - N_GROUPS
    w_r = jnp.concatenate([w_router_exp[l], w_router_grp[l], jnp.zeros((D, pad), f32)], axis=1)
    w_r_hi = w_r.astype(bf16)
    w_r_lo = (w_r - w_r_hi.astype(f32)).astype(bf16)
    w_r2 = jnp.concatenate([w_r_hi, w_r_lo], axis=1)
    b_r = jnp.concatenate([b_router_exp[l], b_router_grp[l], jnp.zeros((pad,), f32)])[None]

    x1_p, h2_p, r3_p, cnt_p, plast = _stage1_prompt_call(
        x_prompt, mod_p, vecs, w_in_b, ws, bias_full, w_pool[l], w_out_b, w_r2, b_r)
    state_t = jnp.transpose(state_pool[l], (1, 0, 2))
    x1_s, h2_s, r3_s, cnt_s, p_s, v_s = _stage1_sample_call(
        x_sample.reshape(n_s, D), mod_s, vecs, w_in_b, state_t, w_pool[l], w_out_b, w_r2, b_r)

    wg, wu, wd = _expert_cast_call(w_exp_gate[l], w_exp_up[l], w_exp_down[l])
    lane_e = (jnp.arange(LANES) % N_EXPERTS)[None, :, None]
    lane_ok = (jnp.arange(LANES) < 3 * N_EXPERTS)[None, :, None]
    col_e = (jnp.arange(EPG * D_EXPERT) // D_EXPERT)[None, None, :]
    ex = (lane_e == jnp.arange(N_GROUPS)[:, None, None] * EPG + col_e) & lane_ok
    ex = ex.astype(bf16).reshape(N_GROUPS * LANES, EPG * D_EXPERT)

    g_post = g_post_ffn[l].reshape(1, D)
    y_p = _moe_call(
        _count_table(cnt_p), x1_p.reshape(b * s, D), h2_p.reshape(b * s, D),
        r3_p.reshape(b * s, LANES), mod_p[5], g_post, wg, wu, wd, ex, TL, MOE_WINDOW // TL, s)
    y_s = _moe_call(
        _count_table(cnt_s[None]), x1_s, h2_s, r3_s, mod_s[5], g_post, wg, wu, wd, ex, n_s, 1, 1)

    state_pool_prompt = plast[:, 1:][None]
    state_pool_sample = jnp.concatenate([state_pool[l][:, 1:], p_s[:, None, :]], axis=1)[None]
    chunk_v_sample = v_s.reshape(1, n_s, 1, D)
    return (y_p.reshape(b, s, D), y_s.reshape(n_s, 1, D), state_pool_prompt,
            state_pool_sample, chunk_v_sample)
```

```python
import functools

import jax
import jax.numpy as jnp
from jax import lax
from jax.experimental import pallas as pl
from jax.experimental.pallas import tpu as pltpu

D = 1024
CHUNK = 128
HEADS = 8
HEAD_DIM = 128
WINDOWS = (2, 4, 8, 16)
PW = 512
PG = 128
W_MAX = 16
N_GROUPS = 4
EPG = 8
N_EXPERTS = 32
D_EXPERT = 128
EPS = 1e-6
LANES = 128
GROUP_LANE0 = 32
GIDX_LANE = 96

TL = 256
MOE_WINDOW = 1024
ROW_ALIGN = 16
RB = 128
VMEM_LIMIT = 56 * 1024 * 1024

bf16 = jnp.bfloat16
f32 = jnp.float32


def _rms(x, g):
    ms = jnp.mean(x * x, axis=-1, keepdims=True)
    return x * lax.rsqrt(ms + EPS) * g


def _dot(a, b):
    return jnp.dot(a, b, preferred_element_type=f32)


_GELU_C = 2.0 * 0.7978845608028654


def _gelu(x):
    t = x * ((-_GELU_C) + (-_GELU_C * 0.044715) * (x * x))
    return x / (1.0 + jnp.exp(t))


def _cast_kernel(w_ref, o_ref):
    o_ref[...] = w_ref[...].astype(bf16)


def _cast_call(w, block_cols):
    rows, cols = w.shape
    spec = pl.BlockSpec((rows, block_cols), lambda j: (0, j))
    return pl.pallas_call(
        _cast_kernel, grid=(cols // block_cols,), in_specs=[spec], out_specs=spec,
        out_shape=jax.ShapeDtypeStruct((rows, cols), bf16),
        compiler_params=pltpu.CompilerParams(dimension_semantics=("arbitrary",)),
    )(w)


def _expert_cast_kernel(g_ref, u_ref, d_ref, wg_ref, wu_ref, wd_ref):
    wg_ref[...] = g_ref[0].astype(bf16)
    wu_ref[...] = u_ref[0].astype(bf16)
    wd_ref[...] = d_ref[0].astype(bf16)


def _expert_cast_call(w_gate, w_up, w_down):
    col_block = pl.BlockSpec((D, D_EXPERT), lambda e: (e // EPG, e % EPG))
    return pl.pallas_call(
        _expert_cast_kernel,
        grid=(N_EXPERTS,),
        in_specs=[
            pl.BlockSpec((1, D, D_EXPERT), lambda e: (e, 0, 0)),
            pl.BlockSpec((1, D, D_EXPERT), lambda e: (e, 0, 0)),
            pl.BlockSpec((1, D_EXPERT, D), lambda e: (e, 0, 0)),
        ],
        out_specs=[col_block, col_block, pl.BlockSpec((D_EXPERT, D), lambda e: (e, 0))],
        out_shape=[
            jax.ShapeDtypeStruct((N_GROUPS * D, EPG * D_EXPERT), bf16),
            jax.ShapeDtypeStruct((N_GROUPS * D, EPG * D_EXPERT), bf16),
            jax.ShapeDtypeStruct((N_EXPERTS * D_EXPERT, D), bf16),
        ],
        compiler_params=pltpu.CompilerParams(dimension_semantics=("arbitrary",)),
    )(w_gate, w_up, w_down)


def _mod_kernel(c_ref, w_ref, b_ref, op_ref, os_ref):
    c = c_ref[...]
    a = (c * jax.nn.sigmoid(c)).astype(bf16)
    m = _dot(a, w_ref[...].astype(bf16)) + b_ref[0]
    nb = op_ref.shape[1]
    op_ref[0] = m[:nb]
    os_ref[0] = m[nb:]


def _mod_call(c_all, w_ada, b_ada, nb):
    n = c_all.shape[0]
    return pl.pallas_call(
        _mod_kernel,
        grid=(6,),
        in_specs=[
            pl.BlockSpec((n, D), lambda j: (0, 0)),
            pl.BlockSpec((D, D), lambda j: (0, j)),
            pl.BlockSpec((1, 1, D), lambda j: (j, 0, 0)),
        ],
        out_specs=[pl.BlockSpec((1, nb, D), lambda j: (j, 0, 0)),
                   pl.BlockSpec((1, n - nb, D), lambda j: (j, 0, 0))],
        out_shape=[jax.ShapeDtypeStruct((6, nb, D), f32),
                   jax.ShapeDtypeStruct((6, n - nb, D), f32)],
        compiler_params=pltpu.CompilerParams(
            dimension_semantics=("arbitrary",), vmem_limit_bytes=VMEM_LIMIT),
    )(c_all, w_ada, b_ada.reshape(6, 1, D))


def _route(logits):
    t = logits.shape[0]
    lane = lax.broadcasted_iota(jnp.int32, (t, LANES), 1)
    lane_f = lane.astype(f32)
    neg = -jnp.inf
    big = 1e9
    gmask = (lane >= GROUP_LANE0) & (lane < GROUP_LANE0 + N_GROUPS)
    gl = jnp.where(gmask, logits, neg)
    gmax = jnp.max(gl, axis=-1, keepdims=True)
    g_idx = jnp.min(jnp.where(gl == gmax, lane_f - GROUP_LANE0, big), axis=-1, keepdims=True)
    sumexp = jnp.sum(jnp.where(gmask, jnp.exp(gl - gmax), 0.0), axis=-1, keepdims=True)
    p_g = 1.0 / sumexp
    lane_grp = (lane >> 3).astype(f32)
    emask = (lane < N_EXPERTS) & (lane_grp == g_idx)
    el = jnp.where(emask, logits, neg)
    m1 = jnp.max(el, axis=-1, keepdims=True)
    i1 = jnp.min(jnp.where(el == m1, lane_f, big), axis=-1, keepdims=True)
    el2 = jnp.where(lane_f == i1, neg, el)
    m2 = jnp.max(el2, axis=-1, keepdims=True)
    i2 = jnp.min(jnp.where(el2 == m2, lane_f, big), axis=-1, keepdims=True)
    e = jnp.exp(m2 - m1)
    w1 = p_g / (1.0 + e)
    w2 = w1 * e

    def split3(w):
        hi = w.astype(bf16).astype(f32)
        mid = (w - hi).astype(bf16).astype(f32)
        lo = w - hi - mid
        return hi, mid, lo

    r3 = jnp.where(lane == GIDX_LANE, g_idx, 0.0)
    for idx, w in ((i1, w1), (i2, w2)):
        for part, wp in enumerate(split3(w)):
            r3 = r3 + jnp.where(lane_f == idx + float(part * N_EXPERTS), wp, 0.0)
    counts = jnp.sum(jnp.where(lane_f == g_idx, 1.0, 0.0), axis=0, keepdims=True)
    return r3.astype(bf16), jnp.broadcast_to(counts, (8, LANES))


def _merge_project(u, ga, gb, mix, y_b, w_out_ref):
    y_a = u * mix
    merged = jax.nn.sigmoid(ga) * y_a + jax.nn.sigmoid(gb) * y_b
    return _dot(merged.astype(bf16), w_out_ref[...])


def _residual_route(x, y, mods, vec_ref, w_r_ref, b_r_ref):
    sh1, sc1, gt1, sh2, sc2, gt2 = mods
    x1 = x + gt1 * _rms(y, vec_ref[1:2])
    h2 = _rms(x1, vec_ref[2:3] * (1.0 + sc2)) + sh2
    h2_hi = h2.astype(bf16)
    h2_lo = (h2 - h2_hi.astype(f32)).astype(bf16)
    r = _dot(h2_hi, w_r_ref[...]) + _dot(h2_lo, w_r_ref[...])
    logits = r[:, :LANES] + r[:, LANES:] + b_r_ref[...]
    r3, counts = _route(logits)
    return x1, h2_hi, r3, counts


def _in_proj(x, mods, vec_ref, w_in_ref, after_first_dot=lambda: None):
    sh1, sc1 = mods[0], mods[1]
    h = _rms(x, vec_ref[0:1] * (1.0 + sc1)) + sh1
    hb = h.astype(bf16)
    zu = _dot(hb, w_in_ref[:, 0:D])
    after_first_dot()
    zv = _dot(hb, w_in_ref[:, D:2 * D])
    p = _dot(hb, w_in_ref[:, 2 * D:2 * D + PW])
    ga = _dot(hb, w_in_ref[:, 2 * D + PW:3 * D + PW])
    gb = _dot(hb, w_in_ref[:, 3 * D + PW:4 * D + PW])
    return zu, zv, p, ga, gb


def _activate(zu, zv, vec_ref):
    u = _gelu(zu)
    v = _gelu(zv)
    mu = jnp.mean(v, axis=-1, keepdims=True)
    vc = v - mu
    var = jnp.mean(vc * vc, axis=-1, keepdims=True)
    v = vc * lax.rsqrt(var + EPS) * vec_ref[4:5] + vec_ref[5:6]
    return u, v


def _pool_out(d_groups, vec_ref, w_pool_ref):
    parts = [_dot(d.astype(bf16), w_pool_ref[gi].astype(bf16)) for gi, d in enumerate(d_groups)]
    return jnp.concatenate(parts, axis=1) * vec_ref[6:7]


def _stage1_prompt_kernel(x_ref, xprev_ref, mod_ref, vec_ref, w_in_ref, w_sp_ref, bias_ref,
                          w_pool_ref, w_out_ref, w_r_ref, b_r_ref,
                          x1_ref, h2_ref, r3_ref, cnt_ref, plast_ref, pbuf, ybuf, *, tiles_per_seq):
    t = pl.program_id(0)
    n_tiles = pl.num_programs(0) - 1

    @pl.when(t == 0)
    def _():
        ybuf[...] = jnp.zeros_like(ybuf)
        pbuf[...] = jnp.zeros_like(pbuf)

    t_cur = jnp.minimum(t, n_tiles - 1)
    bi = t_cur // tiles_per_seq
    s = t_cur % tiles_per_seq
    x = x_ref[...]
    mods = [mod_ref[i, pl.ds(bi, 1), :] for i in range(6)]

    def prev_tile_second_half():
        t_prev = jnp.maximum(t - 1, 0)
        b_prev = t_prev // tiles_per_seq
        mods_prev = [mod_ref[i, pl.ds(b_prev, 1), :] for i in range(6)]
        x1, h2b, r3, counts = _residual_route(xprev_ref[...], ybuf[...], mods_prev, vec_ref,
                                              w_r_ref, b_r_ref)
        x1_ref[...] = x1
        h2_ref[...] = h2b
        r3_ref[...] = r3
        cnt_ref[0] = counts

    zu, zv, p, ga, gb = _in_proj(x, mods, vec_ref, w_in_ref, prev_tile_second_half)

    u, v = _activate(zu, zv, vec_ref)

    vb = v.astype(bf16)
    row = lax.broadcasted_iota(jnp.int32, (CHUNK, CHUNK), 0)
    col = lax.broadcasted_iota(jnp.int32, (CHUNK, CHUNK), 1)
    w_tril = [jnp.where(row >= col, w_sp_ref[hd], 0.0).astype(bf16) for hd in range(HEADS)]
    bias = bias_ref[...]
    chunks = []
    for c in range(TL // CHUNK):
        heads = [_dot(w_tril[hd], vb[c * CHUNK:(c + 1) * CHUNK, hd * HEAD_DIM:(hd + 1) * HEAD_DIM])
                 for hd in range(HEADS)]
        chunks.append(jnp.concatenate(heads, axis=1) + bias)
    mix = jnp.concatenate(chunks, axis=0)

    carry = jnp.where(s == 0, 0.0, pbuf[...])
    ext = jnp.concatenate([carry, p], axis=0)
    pos = s * TL + lax.broadcasted_iota(jnp.int32, (TL, PG), 0)
    d_groups = []
    for gi, w in enumerate(WINDOWS):
        acc = ext[:, gi * PG:(gi + 1) * PG]
        k = 1
        while k < w:
            acc = acc + pltpu.roll(acc, k, 0)
            k *= 2
        cnt = jnp.minimum(pos + 1, w).astype(f32)
        d_groups.append(acc[W_MAX:] / cnt - p[:, gi * PG:(gi + 1) * PG])
    pbuf[...] = p[TL - W_MAX:]
    plast_ref[0] = p[TL - W_MAX:]
    y_b = _pool_out(d_groups, vec_ref, w_pool_ref)
    ybuf[...] = _merge_project(u, ga, gb, mix, y_b, w_out_ref)


def _stage1_sample_kernel(x_ref, mod_ref, vec_ref, w_in_ref, state_ref, w_pool_ref,
                          w_out_ref, w_r_ref, b_r_ref,
                          x1_ref, h2_ref, r3_ref, cnt_ref, p_ref, v_ref):
    x = x_ref[...]
    mods = [mod_ref[i] for i in range(6)]
    zu, zv, p, ga, gb = _in_proj(x, mods, vec_ref, w_in_ref)
    u, v = _activate(zu, zv, vec_ref)
    v_ref[...] = v
    p_ref[...] = p
    mix = v * vec_ref[8:9] + vec_ref[9:10]
    d_groups = []
    for gi, w in enumerate(WINDOWS):
        sl = slice(gi * PG, (gi + 1) * PG)
        acc = p[:, sl]
        for r in range(W_MAX - w, W_MAX - 1):
            acc = acc + state_ref[r][:, sl]
        d_groups.append(acc / float(w) - p[:, sl])
    y_b = _pool_out(d_groups, vec_ref, w_pool_ref)
    y = _merge_project(u, ga, gb, mix, y_b, w_out_ref)
    x1, h2b, r3, counts = _residual_route(x, y, mods, vec_ref, w_r_ref, b_r_ref)
    x1_ref[...] = x1
    h2_ref[...] = h2b
    r3_ref[...] = r3
    cnt_ref[...] = counts


def _const_spec(shape):
    nd = len(shape)
    return pl.BlockSpec(shape, lambda *_: (0,) * nd, pipeline_mode=pl.Buffered(1))


def _stage1_prompt_call(x, mod_p, vecs, w_in_b, w_sp, bias_full, w_pool_b, w_out_b, w_r, b_r):
    b, s, _ = x.shape
    ns = s // TL
    nt = b * ns
    x2 = x.reshape(b * s, D)
    cur = lambda t: (jnp.minimum(t, nt - 1), 0)
    prev = lambda t: (jnp.maximum(t - 1, 0), 0)
    return pl.pallas_call(
        functools.partial(_stage1_prompt_kernel, tiles_per_seq=ns),
        grid=(nt + 1,),
        in_specs=[
            pl.BlockSpec((TL, D), cur),
            pl.BlockSpec((TL, D), prev),
            _const_spec(mod_p.shape),
            _const_spec(vecs.shape),
            _const_spec(w_in_b.shape),
            _const_spec(w_sp.shape),
            _const_spec(bias_full.shape),
            _const_spec(w_pool_b.shape),
            _const_spec(w_out_b.shape),
            _const_spec(w_r.shape),
            _const_spec(b_r.shape),
        ],
        out_specs=[
            pl.BlockSpec((TL, D), prev),
            pl.BlockSpec((TL, D), prev),
            pl.BlockSpec((TL, LANES), prev),
            pl.BlockSpec((1, 8, LANES), lambda t: (jnp.maximum(t - 1, 0), 0, 0)),
            pl.BlockSpec((1, W_MAX, PW), lambda t: (jnp.minimum(t, nt - 1) // ns, 0, 0)),
        ],
        out_shape=[
            jax.ShapeDtypeStruct((b * s, D), f32),
            jax.ShapeDtypeStruct((b * s, D), bf16),
            jax.ShapeDtypeStruct((b * s, LANES), bf16),
            jax.ShapeDtypeStruct((nt, 8, LANES), f32),
            jax.ShapeDtypeStruct((b, W_MAX, PW), f32),
        ],
        scratch_shapes=[pltpu.VMEM((W_MAX, PW), f32), pltpu.VMEM((TL, D), f32)],
        compiler_params=pltpu.CompilerParams(
            dimension_semantics=("arbitrary",), vmem_limit_bytes=VMEM_LIMIT),
    )(x2, x2, mod_p, vecs, w_in_b, w_sp, bias_full, w_pool_b, w_out_b, w_r, b_r)


def _stage1_sample_call(x, mod_s, vecs, w_in_b, state_t, w_pool_b, w_out_b, w_r, b_r):
    n = x.shape[0]
    return pl.pallas_call(
        _stage1_sample_kernel,
        out_shape=[
            jax.ShapeDtypeStruct((n, D), f32),
            jax.ShapeDtypeStruct((n, D), bf16),
            jax.ShapeDtypeStruct((n, LANES), bf16),
            jax.ShapeDtypeStruct((8, LANES), f32),
            jax.ShapeDtypeStruct((n, PW), f32),
            jax.ShapeDtypeStruct((n, D), f32),
        ],
        compiler_params=pltpu.CompilerParams(vmem_limit_bytes=VMEM_LIMIT),
    )(x, mod_s, vecs, w_in_b, state_t, w_pool_b, w_out_b, w_r, b_r)


def _moe_buffer_rows(ts, nt):
    worst = ts * nt + nt * N_GROUPS * (ROW_ALIGN - 1) + N_GROUPS * (RB - 1)
    return -(-worst // RB) * RB


def _moe_kernel(cnt_ref, h2_ref, r3_ref, x1_ref, gt2_ref, g_ref, wg_ref, wu_ref, wd_ref, ex_ref,
                o_ref,
                hsbuf, c3buf, ysbuf, hs_tile, c3_tile, ys_tile, pt_buf, *, ts, nt, tpr):
    w = pl.program_id(0)
    i = pl.program_id(1)
    rt = ts + LANES

    def run_len(tile, g):
        c = cnt_ref[(w * nt + tile) * N_GROUPS + g]
        return ((c + (ROW_ALIGN - 1)) // ROW_ALIGN) * ROW_ALIGN

    lens = [[run_len(t, g) for g in range(N_GROUPS)] for t in range(nt)]
    region = [sum(lens[t][g] for t in range(nt)) for g in range(N_GROUPS)]
    region = [((r + (RB - 1)) // RB) * RB for r in region]
    base = [sum(region[:g]) for g in range(N_GROUPS)]

    def copy_rows(src, src0, dst, dst0, nrows):
        def body(j, _):
            s0 = pl.multiple_of(src0 + j * ROW_ALIGN, ROW_ALIGN)
            d0 = pl.multiple_of(dst0 + j * ROW_ALIGN, ROW_ALIGN)
            dst[pl.ds(d0, ROW_ALIGN), :] = src[pl.ds(s0, ROW_ALIGN), :]
            return 0
        lax.fori_loop(0, nrows // ROW_ALIGN, body, 0)

    @pl.when((w == 0) & (i == 0))
    def _():
        hsbuf[...] = jnp.zeros_like(hsbuf)
        c3buf[...] = jnp.zeros_like(c3buf)
        ys_tile[...] = jnp.zeros_like(ys_tile)

    @pl.when(i == 0)
    def _sort_and_run_experts():
        lane = lax.broadcasted_iota(jnp.int32, (ts, LANES), 1)
        lane_f = lane.astype(f32)
        r_i = lax.broadcasted_iota(jnp.int32, (ts, ts), 0)
        c_i = lax.broadcasted_iota(jnp.int32, (ts, ts), 1)
        ltri = jnp.where(r_i > c_i, 1.0, 0.0).astype(bf16)
        lane8 = lax.broadcasted_iota(jnp.int32, (8, LANES), 1)
        sel = jnp.where(lane8 == 0, float(ROW_ALIGN), jnp.where(lane8 == 1, 1.0, 0.0)).astype(bf16)
        rt_lane = lax.broadcasted_iota(jnp.int32, (ts, rt), 1).astype(f32)
        rt_sub = lax.broadcasted_iota(jnp.int32, (rt, ts), 0).astype(f32)

        offs = list(base)
        for t in range(nt):
            h2t = h2_ref[t * ts:(t + 1) * ts, :]
            r3t = r3_ref[t * ts:(t + 1) * ts, :]
            gid = jnp.sum(jnp.where(lane == GIDX_LANE, r3t.astype(f32), 0.0), axis=-1, keepdims=True)
            onehot = jnp.where(lane_f == gid, 1.0, 0.0)
            rank = _dot(ltri, onehot.astype(bf16))
            seg = jnp.zeros((1, LANES), f32)
            start = 0
            for g in range(N_GROUPS):
                seg = seg + jnp.where(lane[0:1] == g, jnp.asarray(start, jnp.int32).astype(f32), 0.0)
                start = start + lens[t][g]
            pos = jnp.sum(onehot * (rank + seg), axis=-1, keepdims=True)
            pt = jnp.where(rt_lane == pos, 1.0, 0.0).astype(bf16)
            pt_buf[t] = pt
            q = jnp.floor(pos * (1.0 / ROW_ALIGN))
            digits = jnp.where(lane == 0, q, jnp.where(lane == 1, pos - q * ROW_ALIGN, 0.0))
            pos_row = lax.dot_general(sel, digits.astype(bf16), (((1,), (1,)), ((), ())),
                                      preferred_element_type=f32)
            p_mat = jnp.where(rt_sub == pos_row[0:1], 1.0, 0.0).astype(bf16)
            hs_tile[...] = _dot(p_mat, h2t).astype(bf16)
            c3_tile[...] = _dot(p_mat, r3t).astype(bf16)
            start = 0
            for g in range(N_GROUPS):
                copy_rows(hs_tile, start, hsbuf, offs[g], lens[t][g])
                copy_rows(c3_tile, start, c3buf, offs[g], lens[t][g])
                start = start + lens[t][g]
                offs[g] = offs[g] + lens[t][g]

        for g in range(N_GROUPS):
            def block(b, _, g=g):
                r0 = pl.multiple_of(base[g] + b * RB, RB)
                rows = hsbuf[pl.ds(r0, RB), :]
                gate = _dot(rows, wg_ref[g * D:(g + 1) * D, :])
                up = _dot(rows, wu_ref[g * D:(g + 1) * D, :])
                cw = _dot(c3buf[pl.ds(r0, RB), :], ex_ref[g * LANES:(g + 1) * LANES, :])
                act = gate * jax.nn.sigmoid(gate) * up * cw
                ysbuf[pl.ds(r0, RB), :] = _dot(
                    act.astype(bf16), wd_ref[g * D:(g + 1) * D, :]).astype(bf16)
                return 0
            lax.fori_loop(0, region[g] // RB, block, 0)

    start = 0
    for g in range(N_GROUPS):
        ln = run_len(i, g)
        off = base[g]
        for t in range(nt):
            off = off + jnp.where(t < i, lens[t][g], 0)
        copy_rows(ysbuf, off, ys_tile, start, ln)
        start = start + ln
    f = _dot(pt_buf[i], ys_tile[...])
    tok0 = (w * nt + i) * ts
    gt2 = gt2_ref[pl.ds(tok0, ts), :] if tpr == 1 else gt2_ref[pl.ds(tok0 // tpr, 1), :]
    o_ref[...] = x1_ref[...] + gt2 * _rms(f, g_ref[...])


def _moe_call(cnt, x1, h2, r3, gt2, g_post, wg, wu, wd, ex, ts, nt, tpr):
    n = x1.shape[0]
    win = ts * nt
    rbuf = _moe_buffer_rows(ts, nt)
    rt = ts + LANES
    grid_spec = pltpu.PrefetchScalarGridSpec(
        num_scalar_prefetch=1,
        grid=(n // win, nt),
        in_specs=[
            pl.BlockSpec((win, D), lambda w, i, c: (w, 0)),
            pl.BlockSpec((win, LANES), lambda w, i, c: (w, 0)),
            pl.BlockSpec((ts, D), lambda w, i, c: (w * nt + i, 0)),
            _const_spec(gt2.shape),
            _const_spec(g_post.shape),
            _const_spec(wg.shape),
            _const_spec(wu.shape),
            _const_spec(wd.shape),
            _const_spec(ex.shape),
        ],
        out_specs=pl.BlockSpec((ts, D), lambda w, i, c: (w * nt + i, 0)),
        scratch_shapes=[
            pltpu.VMEM((rbuf, D), bf16),
            pltpu.VMEM((rbuf, LANES), bf16),
            pltpu.VMEM((rbuf, D), bf16),
            pltpu.VMEM((rt, D), bf16),
            pltpu.VMEM((rt, LANES), bf16),
            pltpu.VMEM((rt, D), bf16),
            pltpu.VMEM((nt, ts, rt), bf16),
        ],
    )
    return pl.pallas_call(
        functools.partial(_moe_kernel, ts=ts, nt=nt, tpr=tpr),
        grid_spec=grid_spec,
        out_shape=jax.ShapeDtypeStruct((n, D), f32),
        compiler_params=pltpu.CompilerParams(
            dimension_semantics=("arbitrary", "arbitrary"), vmem_limit_bytes=VMEM_LIMIT),
    )(cnt, h2, r3, x1, gt2, g_post, wg, wu, wd, ex)


def _count_table(cnt):
    return cnt[:, 0, :N_GROUPS].astype(jnp.int32).reshape(-1)


def kernel(x_prompt, x_sample, c_prompt, c_sample, state_pool, w_ada, b_ada, g_pre_mix, g_post_mix, g_pre_ffn, g_post_ffn, w_in, ln_v_g, ln_v_b, w_spatial, b_spatial, w_pool, pool_scale, w_out, w_router_grp, b_router_grp, w_router_exp, b_router_exp, w_exp_gate, w_exp_up, w_exp_down):
    depth = w_in.shape[0]
    assert depth == 1
    b, s, _ = x_prompt.shape
    n_s = x_sample.shape[0]
    l = 0

    c_all = jnp.concatenate([c_prompt, c_sample], axis=0)
    mod_p, mod_s = _mod_call(c_all, w_ada[l], b_ada[l], b)

    ws, bs = w_spatial[l], b_spatial[l]
    zeros = jnp.zeros((D,), f32)
    vecs = jnp.stack([
        g_pre_mix[l], g_post_mix[l], g_pre_ffn[l], g_post_ffn[l], ln_v_g[l], ln_v_b[l],
        pool_scale[l], zeros,
        jnp.repeat(ws[:, 0, 0], HEAD_DIM), jnp.repeat(bs[:, 0], HEAD_DIM),
        zeros, zeros, zeros, zeros, zeros, zeros])
    bias_full = jnp.repeat(bs.T, HEAD_DIM, axis=1)
    w_in_b = _cast_call(w_in[l], 512)
    w_out_b = _cast_call(w_out[l], D)
    pad = LANES - N_EXPERTS - N_GROUPS
    w_r = jnp.concatenate([w_router_exp[l], w_router_grp[l], jnp.zeros((D, pad), f32)], axis=1)
    w_r_hi = w_r.astype(bf16)
    w_r_lo = (w_r - w_r_hi.astype(f32)).astype(bf16)
    w_r2 = jnp.concatenate([w_r_hi, w_r_lo], axis=1)
    b_r = jnp.concatenate([b_router_exp[l], b_router_grp[l], jnp.zeros((pad,), f32)])[None]

    x1_p, h2_p, r3_p, cnt_p, plast = _stage1_prompt_call(
        x_prompt, mod_p, vecs, w_in_b, ws, bias_full, w_pool[l], w_out_b, w_r2, b_r)
    state_t = jnp.transpose(state_pool[l], (1, 0, 2))
    x1_s, h2_s, r3_s, cnt_s, p_s, v_s = _stage1_sample_call(
        x_sample.reshape(n_s, D), mod_s, vecs, w_in_b, state_t, w_pool[l], w_out_b, w_r2, b_r)

    wg, wu, wd = _expert_cast_call(w_exp_gate[l], w_exp_up[l], w_exp_down[l])
    lane_e = (jnp.arange(LANES) % N_EXPERTS)[None, :, None]
    lane_ok = (jnp.arange(LANES) < 3 * N_EXPERTS)[None, :, None]
    col_e = (jnp.arange(EPG * D_EXPERT) // D_EXPERT)[None, None, :]
    ex = (lane_e == jnp.arange(N_GROUPS)[:, None, None] * EPG + col_e) & lane_ok
    ex = ex.astype(bf16).reshape(N_GROUPS * LANES, EPG * D_EXPERT)

    g_post = g_post_ffn[l].reshape(1, D)
    y_p = _moe_call(
        _count_table(cnt_p), x1_p.reshape(b * s, D), h2_p.reshape(b * s, D),
        r3_p.reshape(b * s, LANES), mod_p[5], g_post, wg, wu, wd, ex, TL, MOE_WINDOW // TL, s)
    y_s = _moe_call(
        _count_table(cnt_s[None]), x1_s, h2_s, r3_s, mod_s[5], g_post, wg, wu, wd, ex, n_s, 1, 1)

    state_pool_prompt = plast[:, 1:][None]
    state_pool_sample = jnp.concatenate([state_pool[l][:, 1:], p_s[:, None, :]], axis=1)[None]
    chunk_v_sample = v_s.reshape(1, n_s, 1, D)
    return (y_p.reshape(b, s, D), y_s.reshape(n_s, 1, D), state_pool_prompt,
            state_pool_sample, chunk_v_sample)
```

```python
import functools

import jax
import jax.numpy as jnp
from jax import lax
from jax.experimental import pallas as pl
from jax.experimental.pallas import tpu as pltpu

D = 1024
CHUNK = 128
HEADS = 8
HEAD_DIM = 128
WINDOWS = (2, 4, 8, 16)
PW = 512
PG = 128
W_MAX = 16
N_GROUPS = 4
EPG = 8
N_EXPERTS = 32
D_EXPERT = 128
EPS = 1e-6
LANES = 128
GROUP_LANE0 = 32
GIDX_LANE = 96

TL = 256
MOE_WINDOW = 1024
ROW_ALIGN = 16
RB = 128
VMEM_LIMIT = 56 * 1024 * 1024

bf16 = jnp.bfloat16
f32 = jnp.float32


def _rms(x, g):
    ms = jnp.mean(x * x, axis=-1, keepdims=True)
    return x * lax.rsqrt(ms + EPS) * g


def _dot(a, b):
    return jnp.dot(a, b, preferred_element_type=f32)


_GELU_C = 2.0 * 0.7978845608028654


def _gelu(x):
    t = x * ((-_GELU_C) + (-_GELU_C * 0.044715) * (x * x))
    return x / (1.0 + jnp.exp(t))


def _cast_kernel(w_ref, o_ref):
    o_ref[...] = w_ref[...].astype(bf16)


def _cast_call(w, block_cols):
    rows, cols = w.shape
    spec = pl.BlockSpec((rows, block_cols), lambda j: (0, j))
    return pl.pallas_call(
        _cast_kernel, grid=(cols // block_cols,), in_specs=[spec], out_specs=spec,
        out_shape=jax.ShapeDtypeStruct((rows, cols), bf16),
        compiler_params=pltpu.CompilerParams(dimension_semantics=("arbitrary",)),
    )(w)


def _expert_cast_specs(step_to_expert):
    e = step_to_expert
    col_block = pl.BlockSpec((D, D_EXPERT), lambda t: (e(t) // EPG, e(t) % EPG))
    in_specs = [
        pl.BlockSpec((1, D, D_EXPERT), lambda t: (e(t), 0, 0)),
        pl.BlockSpec((1, D, D_EXPERT), lambda t: (e(t), 0, 0)),
        pl.BlockSpec((1, D_EXPERT, D), lambda t: (e(t), 0, 0)),
    ]
    out_specs = [col_block, col_block, pl.BlockSpec((D_EXPERT, D), lambda t: (e(t), 0))]
    out_shapes = [
        jax.ShapeDtypeStruct((N_GROUPS * D, EPG * D_EXPERT), bf16),
        jax.ShapeDtypeStruct((N_GROUPS * D, EPG * D_EXPERT), bf16),
        jax.ShapeDtypeStruct((N_EXPERTS * D_EXPERT, D), bf16),
    ]
    return in_specs, out_specs, out_shapes


def _mod_kernel(c_ref, w_ref, b_ref, op_ref, os_ref):
    c = c_ref[...]
    a = (c * jax.nn.sigmoid(c)).astype(bf16)
    m = _dot(a, w_ref[...].astype(bf16)) + b_ref[0]
    nb = op_ref.shape[1]
    op_ref[0] = m[:nb]
    os_ref[0] = m[nb:]


def _mod_call(c_all, w_ada, b_ada, nb):
    n = c_all.shape[0]
    return pl.pallas_call(
        _mod_kernel,
        grid=(6,),
        in_specs=[
            pl.BlockSpec((n, D), lambda j: (0, 0)),
            pl.BlockSpec((D, D), lambda j: (0, j)),
            pl.BlockSpec((1, 1, D), lambda j: (j, 0, 0)),
        ],
        out_specs=[pl.BlockSpec((1, nb, D), lambda j: (j, 0, 0)),
                   pl.BlockSpec((1, n - nb, D), lambda j: (j, 0, 0))],
        out_shape=[jax.ShapeDtypeStruct((6, nb, D), f32),
                   jax.ShapeDtypeStruct((6, n - nb, D), f32)],
        compiler_params=pltpu.CompilerParams(
            dimension_semantics=("arbitrary",), vmem_limit_bytes=VMEM_LIMIT),
    )(c_all, w_ada, b_ada.reshape(6, 1, D))


def _route(logits):
    t = logits.shape[0]
    lane = lax.broadcasted_iota(jnp.int32, (t, LANES), 1)
    lane_f = lane.astype(f32)
    neg = -jnp.inf
    big = 1e9
    gmask = (lane >= GROUP_LANE0) & (lane < GROUP_LANE0 + N_GROUPS)
    gl = jnp.where(gmask, logits, neg)
    gmax = jnp.max(gl, axis=-1, keepdims=True)
    g_idx = jnp.min(jnp.where(gl == gmax, lane_f - GROUP_LANE0, big), axis=-1, keepdims=True)
    sumexp = jnp.sum(jnp.where(gmask, jnp.exp(gl - gmax), 0.0), axis=-1, keepdims=True)
    p_g = 1.0 / sumexp
    lane_grp = (lane >> 3).astype(f32)
    emask = (lane < N_EXPERTS) & (lane_grp == g_idx)
    el = jnp.where(emask, logits, neg)
    m1 = jnp.max(el, axis=-1, keepdims=True)
    i1 = jnp.min(jnp.where(el == m1, lane_f, big), axis=-1, keepdims=True)
    el2 = jnp.where(lane_f == i1, neg, el)
    m2 = jnp.max(el2, axis=-1, keepdims=True)
    i2 = jnp.min(jnp.where(el2 == m2, lane_f, big), axis=-1, keepdims=True)
    e = jnp.exp(m2 - m1)
    w1 = p_g / (1.0 + e)
    w2 = w1 * e

    def split3(w):
        hi = w.astype(bf16).astype(f32)
        mid = (w - hi).astype(bf16).astype(f32)
        lo = w - hi - mid
        return hi, mid, lo

    r3 = jnp.where(lane == GIDX_LANE, g_idx, 0.0)
    for idx, w in ((i1, w1), (i2, w2)):
        for part, wp in enumerate(split3(w)):
            r3 = r3 + jnp.where(lane_f == idx + float(part * N_EXPERTS), wp, 0.0)
    counts = jnp.sum(jnp.where(lane_f == g_idx, 1.0, 0.0), axis=0, keepdims=True)
    return r3.astype(bf16), jnp.broadcast_to(counts, (8, LANES))


def _merge_project(u, ga, gb, mix, y_b, w_out_ref):
    y_a = u * mix
    merged = jax.nn.sigmoid(ga) * y_a + jax.nn.sigmoid(gb) * y_b
    return _dot(merged.astype(bf16), w_out_ref[...])


def _residual_route(x, y, mods, vec_ref, w_r_ref, b_r_ref):
    sh1, sc1, gt1, sh2, sc2, gt2 = mods
    x1 = x + gt1 * _rms(y, vec_ref[1:2])
    h2 = _rms(x1, vec_ref[2:3] * (1.0 + sc2)) + sh2
    h2_hi = h2.astype(bf16)
    h2_lo = (h2 - h2_hi.astype(f32)).astype(bf16)
    r = _dot(h2_hi, w_r_ref[...]) + _dot(h2_lo, w_r_ref[...])
    logits = r[:, :LANES] + r[:, LANES:] + b_r_ref[...]
    r3, counts = _route(logits)
    return x1, h2_hi, r3, counts


def _in_proj(x, mods, vec_ref, w_in_ref, after_first_dot=lambda: None):
    sh1, sc1 = mods[0], mods[1]
    h = _rms(x, vec_ref[0:1] * (1.0 + sc1)) + sh1
    hb = h.astype(bf16)
    zu = _dot(hb, w_in_ref[:, 0:D])
    after_first_dot()
    zv = _dot(hb, w_in_ref[:, D:2 * D])
    p = _dot(hb, w_in_ref[:, 2 * D:2 * D + PW])
    ga = _dot(hb, w_in_ref[:, 2 * D + PW:3 * D + PW])
    gb = _dot(hb, w_in_ref[:, 3 * D + PW:4 * D + PW])
    return zu, zv, p, ga, gb


def _activate(zu, zv, vec_ref):
    u = _gelu(zu)
    v = _gelu(zv)
    mu = jnp.mean(v, axis=-1, keepdims=True)
    vc = v - mu
    var = jnp.mean(vc * vc, axis=-1, keepdims=True)
    v = vc * lax.rsqrt(var + EPS) * vec_ref[4:5] + vec_ref[5:6]
    return u, v


def _pool_out(d_groups, vec_ref, w_pool_ref):
    parts = [_dot(d.astype(bf16), w_pool_ref[gi].astype(bf16)) for gi, d in enumerate(d_groups)]
    return jnp.concatenate(parts, axis=1) * vec_ref[6:7]


def _stage1_prompt_kernel(x_ref, xprev_ref, mod_ref, vec_ref, w_in_ref, w_sp_ref, bias_ref,
                          w_pool_ref, w_out_ref, w_r_ref, b_r_ref, eg_ref, eu_ref, ed_ref,
                          x1_ref, h2_ref, r3_ref, cnt_ref, plast_ref, wg_ref, wu_ref, wd_ref,
                          pbuf, ybuf, *, tiles_per_seq):
    t = pl.program_id(0)
    n_tiles = pl.num_programs(0) - 1

    @pl.when(t == 0)
    def _():
        ybuf[...] = jnp.zeros_like(ybuf)
        pbuf[...] = jnp.zeros_like(pbuf)

    t_cur = jnp.minimum(t, n_tiles - 1)
    bi = t_cur // tiles_per_seq
    s = t_cur % tiles_per_seq
    x = x_ref[...]
    mods = [mod_ref[i, pl.ds(bi, 1), :] for i in range(6)]

    def prev_tile_second_half():
        wg_ref[...] = eg_ref[0].astype(bf16)
        wu_ref[...] = eu_ref[0].astype(bf16)
        wd_ref[...] = ed_ref[0].astype(bf16)
        t_prev = jnp.maximum(t - 1, 0)
        b_prev = t_prev // tiles_per_seq
        mods_prev = [mod_ref[i, pl.ds(b_prev, 1), :] for i in range(6)]
        x1, h2b, r3, counts = _residual_route(xprev_ref[...], ybuf[...], mods_prev, vec_ref,
                                              w_r_ref, b_r_ref)
        x1_ref[...] = x1
        h2_ref[...] = h2b
        r3_ref[...] = r3
        cnt_ref[0] = counts

    zu, zv, p, ga, gb = _in_proj(x, mods, vec_ref, w_in_ref, prev_tile_second_half)

    u, v = _activate(zu, zv, vec_ref)

    vb = v.astype(bf16)
    row = lax.broadcasted_iota(jnp.int32, (CHUNK, CHUNK), 0)
    col = lax.broadcasted_iota(jnp.int32, (CHUNK, CHUNK), 1)
    w_tril = [jnp.where(row >= col, w_sp_ref[hd], 0.0).astype(bf16) for hd in range(HEADS)]
    bias = bias_ref[...]
    chunks = []
    for c in range(TL // CHUNK):
        heads = [_dot(w_tril[hd], vb[c * CHUNK:(c + 1) * CHUNK, hd * HEAD_DIM:(hd + 1) * HEAD_DIM])
                 for hd in range(HEADS)]
        chunks.append(jnp.concatenate(heads, axis=1) + bias)
    mix = jnp.concatenate(chunks, axis=0)

    carry = jnp.where(s == 0, 0.0, pbuf[...])
    ext = jnp.concatenate([carry, p], axis=0)
    pos = s * TL + lax.broadcasted_iota(jnp.int32, (TL, PG), 0)
    d_groups = []
    for gi, w in enumerate(WINDOWS):
        acc = ext[:, gi * PG:(gi + 1) * PG]
        k = 1
        while k < w:
            acc = acc + pltpu.roll(acc, k, 0)
            k *= 2
        cnt = jnp.minimum(pos + 1, w).astype(f32)
        d_groups.append(acc[W_MAX:] / cnt - p[:, gi * PG:(gi + 1) * PG])
    pbuf[...] = p[TL - W_MAX:]
    plast_ref[0] = p[TL - W_MAX:]
    y_b = _pool_out(d_groups, vec_ref, w_pool_ref)
    ybuf[...] = _merge_project(u, ga, gb, mix, y_b, w_out_ref)


def _stage1_sample_kernel(x_ref, mod_ref, vec_ref, w_in_ref, state_ref, w_pool_ref,
                          w_out_ref, w_r_ref, b_r_ref,
                          x1_ref, h2_ref, r3_ref, cnt_ref, p_ref, v_ref):
    x = x_ref[...]
    mods = [mod_ref[i] for i in range(6)]
    zu, zv, p, ga, gb = _in_proj(x, mods, vec_ref, w_in_ref)
    u, v = _activate(zu, zv, vec_ref)
    v_ref[...] = v
    p_ref[...] = p
    mix = v * vec_ref[8:9] + vec_ref[9:10]
    d_groups = []
    for gi, w in enumerate(WINDOWS):
        sl = slice(gi * PG, (gi + 1) * PG)
        acc = p[:, sl]
        for r in range(W_MAX - w, W_MAX - 1):
            acc = acc + state_ref[r][:, sl]
        d_groups.append(acc / float(w) - p[:, sl])
    y_b = _pool_out(d_groups, vec_ref, w_pool_ref)
    y = _merge_project(u, ga, gb, mix, y_b, w_out_ref)
    x1, h2b, r3, counts = _residual_route(x, y, mods, vec_ref, w_r_ref, b_r_ref)
    x1_ref[...] = x1
    h2_ref[...] = h2b
    r3_ref[...] = r3
    cnt_ref[...] = counts


def _const_spec(shape):
    nd = len(shape)
    return pl.BlockSpec(shape, lambda *_: (0,) * nd, pipeline_mode=pl.Buffered(1))


def _stage1_prompt_call(x, mod_p, vecs, w_in_b, w_sp, bias_full, w_pool_b, w_out_b, w_r, b_r,
                        w_gate, w_up, w_down):
    b, s, _ = x.shape
    ns = s // TL
    nt = b * ns
    assert nt + 1 >= N_EXPERTS
    x2 = x.reshape(b * s, D)
    cur = lambda t: (jnp.minimum(t, nt - 1), 0)
    prev = lambda t: (jnp.maximum(t - 1, 0), 0)
    e_in, e_out, e_shapes = _expert_cast_specs(lambda t: jnp.minimum(t, N_EXPERTS - 1))
    return pl.pallas_call(
        functools.partial(_stage1_prompt_kernel, tiles_per_seq=ns),
        grid=(nt + 1,),
        in_specs=[
            pl.BlockSpec((TL, D), cur),
            pl.BlockSpec((TL, D), prev),
            _const_spec(mod_p.shape),
            _const_spec(vecs.shape),
            _const_spec(w_in_b.shape),
            _const_spec(w_sp.shape),
            _const_spec(bias_full.shape),
            _const_spec(w_pool_b.shape),
            _const_spec(w_out_b.shape),
            _const_spec(w_r.shape),
            _const_spec(b_r.shape),
        ] + e_in,
        out_specs=[
            pl.BlockSpec((TL, D), prev),
            pl.BlockSpec((TL, D), prev),
            pl.BlockSpec((TL, LANES), prev),
            pl.BlockSpec((1, 8, LANES), lambda t: (jnp.maximum(t - 1, 0), 0, 0)),
            pl.BlockSpec((1, W_MAX, PW), lambda t: (jnp.minimum(t, nt - 1) // ns, 0, 0)),
        ] + e_out,
        out_shape=[
            jax.ShapeDtypeStruct((b * s, D), f32),
            jax.ShapeDtypeStruct((b * s, D), bf16),
            jax.ShapeDtypeStruct((b * s, LANES), bf16),
            jax.ShapeDtypeStruct((nt, 8, LANES), f32),
            jax.ShapeDtypeStruct((b, W_MAX, PW), f32),
        ] + e_shapes,
        scratch_shapes=[pltpu.VMEM((W_MAX, PW), f32), pltpu.VMEM((TL, D), f32)],
        compiler_params=pltpu.CompilerParams(
            dimension_semantics=("arbitrary",), vmem_limit_bytes=VMEM_LIMIT),
    )(x2, x2, mod_p, vecs, w_in_b, w_sp, bias_full, w_pool_b, w_out_b, w_r, b_r,
      w_gate, w_up, w_down)


def _stage1_sample_call(x, mod_s, vecs, w_in_b, state_t, w_pool_b, w_out_b, w_r, b_r):
    n = x.shape[0]
    return pl.pallas_call(
        _stage1_sample_kernel,
        out_shape=[
            jax.ShapeDtypeStruct((n, D), f32),
            jax.ShapeDtypeStruct((n, D), bf16),
            jax.ShapeDtypeStruct((n, LANES), bf16),
            jax.ShapeDtypeStruct((8, LANES), f32),
            jax.ShapeDtypeStruct((n, PW), f32),
            jax.ShapeDtypeStruct((n, D), f32),
        ],
        compiler_params=pltpu.CompilerParams(vmem_limit_bytes=VMEM_LIMIT),
    )(x, mod_s, vecs, w_in_b, state_t, w_pool_b, w_out_b, w_r, b_r)


def _moe_buffer_rows(ts, nt):
    worst = ts * nt + nt * N_GROUPS * (ROW_ALIGN - 1) + N_GROUPS * (RB - 1)
    return -(-worst // RB) * RB


def _moe_kernel(cnt_ref, h2_ref, r3_ref, x1_ref, gt2_ref, g_ref, wg_ref, wu_ref, wd_ref, ex_ref,
                o_ref,
                hsbuf, c3buf, ysbuf, hs_tile, c3_tile, ys_tile, pt_buf, pm_buf, tab,
                *, ts, nt, tpr):
    w = pl.program_id(0)
    i = pl.program_id(1)
    rt = ts + LANES

    def copy_rows(src, src0, dst, dst0, nrows):
        def body(j, _):
            s0 = pl.multiple_of(src0 + j * ROW_ALIGN, ROW_ALIGN)
            d0 = pl.multiple_of(dst0 + j * ROW_ALIGN, ROW_ALIGN)
            dst[pl.ds(d0, ROW_ALIGN), :] = src[pl.ds(s0, ROW_ALIGN), :]
            return 0
        lax.fori_loop(0, nrows // ROW_ALIGN, body, 0)

    @pl.when((w == 0) & (i == 0))
    def _():
        hsbuf[...] = jnp.zeros_like(hsbuf)
        c3buf[...] = jnp.zeros_like(c3buf)
        ys_tile[...] = jnp.zeros_like(ys_tile)

    @pl.when(i == 0)
    def _sort_and_run_experts():
        def run_len(tile, g):
            c = cnt_ref[(w * nt + tile) * N_GROUPS + g]
            return ((c + (ROW_ALIGN - 1)) // ROW_ALIGN) * ROW_ALIGN

        lens = [[run_len(t, g) for g in range(N_GROUPS)] for t in range(nt)]
        region = [sum(lens[t][g] for t in range(nt)) for g in range(N_GROUPS)]
        region = [((r + (RB - 1)) // RB) * RB for r in region]
        base = [sum(region[:g]) for g in range(N_GROUPS)]
        offs = list(base)
        woff = []
        for t in range(nt):
            woff.append(list(offs))
            for g in range(N_GROUPS):
                tab[t * 2 * N_GROUPS + g] = lens[t][g]
                tab[t * 2 * N_GROUPS + N_GROUPS + g] = offs[g]
                offs[g] = offs[g] + lens[t][g]

        lane = lax.broadcasted_iota(jnp.int32, (ts, LANES), 1)
        lane_f = lane.astype(f32)
        r_i = lax.broadcasted_iota(jnp.int32, (ts, ts), 0)
        c_i = lax.broadcasted_iota(jnp.int32, (ts, ts), 1)
        ltri = jnp.where(r_i > c_i, 1.0, 0.0).astype(bf16)
        lane8 = lax.broadcasted_iota(jnp.int32, (8, LANES), 1)
        sel = jnp.where(lane8 == 0, float(ROW_ALIGN), jnp.where(lane8 == 1, 1.0, 0.0)).astype(bf16)
        rt_lane = lax.broadcasted_iota(jnp.int32, (ts, rt), 1).astype(f32)
        rt_sub = lax.broadcasted_iota(jnp.int32, (rt, ts), 0).astype(f32)

        tiles = range(nt)
        r3s = [r3_ref[t * ts:(t + 1) * ts, :] for t in tiles]
        gids = [jnp.sum(jnp.where(lane == GIDX_LANE, r3s[t].astype(f32), 0.0), axis=-1,
                        keepdims=True) for t in tiles]
        onehots = [jnp.where(lane_f == gids[t], 1.0, 0.0) for t in tiles]
        ranks = [_dot(ltri, onehots[t].astype(bf16)) for t in tiles]
        poss = []
        for t in tiles:
            seg = jnp.zeros((1, LANES), f32)
            start = 0
            for g in range(N_GROUPS):
                seg = seg + jnp.where(lane[0:1] == g, jnp.asarray(start, jnp.int32).astype(f32), 0.0)
                start = start + lens[t][g]
            poss.append(jnp.sum(onehots[t] * (ranks[t] + seg), axis=-1, keepdims=True))
        for t in tiles:
            pt_buf[t] = jnp.where(rt_lane == poss[t], 1.0, 0.0).astype(bf16)
        pos_rows = []
        for t in tiles:
            q = jnp.floor(poss[t] * (1.0 / ROW_ALIGN))
            digits = jnp.where(lane == 0, q, jnp.where(lane == 1, poss[t] - q * ROW_ALIGN, 0.0))
            pos_rows.append(lax.dot_general(sel, digits.astype(bf16), (((1,), (1,)), ((), ())),
                                            preferred_element_type=f32))
        for t in tiles:
            pm_buf[t] = jnp.where(rt_sub == pos_rows[t][0:1], 1.0, 0.0).astype(bf16)

        for t in tiles:
            p_mat = pm_buf[t]
            hs_tile[...] = _dot(p_mat, h2_ref[t * ts:(t + 1) * ts, :]).astype(bf16)
            c3_tile[...] = _dot(p_mat, r3_ref[t * ts:(t + 1) * ts, :]).astype(bf16)
            start = 0
            for g in range(N_GROUPS):
                copy_rows(hs_tile, start, hsbuf, woff[t][g], lens[t][g])
                copy_rows(c3_tile, start, c3buf, woff[t][g], lens[t][g])
                start = start + lens[t][g]

        def expert_inputs(g, r0):
            rows = hsbuf[pl.ds(r0, RB), :]
            gate = _dot(rows, wg_ref[g * D:(g + 1) * D, :])
            up = _dot(rows, wu_ref[g * D:(g + 1) * D, :])
            return gate, up

        def expert_outputs(g, r0, gate, up):
            cw = _dot(c3buf[pl.ds(r0, RB), :], ex_ref[g * LANES:(g + 1) * LANES, :])
            act = gate * jax.nn.sigmoid(gate) * up * cw
            ysbuf[pl.ds(r0, RB), :] = _dot(
                act.astype(bf16), wd_ref[g * D:(g + 1) * D, :]).astype(bf16)

        for g in range(N_GROUPS):
            n_blocks = region[g] // RB

            def block_pair(b, _, g=g):
                r0 = pl.multiple_of(base[g] + 2 * b * RB, RB)
                r1 = pl.multiple_of(r0 + RB, RB)
                in0 = expert_inputs(g, r0)
                in1 = expert_inputs(g, r1)
                expert_outputs(g, r0, *in0)
                expert_outputs(g, r1, *in1)
                return 0
            lax.fori_loop(0, n_blocks // 2, block_pair, 0)

            @pl.when(n_blocks % 2 == 1)
            def _(g=g, n_blocks=n_blocks):
                r0 = pl.multiple_of(base[g] + (n_blocks - 1) * RB, RB)
                expert_outputs(g, r0, *expert_inputs(g, r0))

    start = 0
    for g in range(N_GROUPS):
        ln = tab[i * 2 * N_GROUPS + g]
        copy_rows(ysbuf, tab[i * 2 * N_GROUPS + N_GROUPS + g], ys_tile, start, ln)
        start = start + ln
    f = _dot(pt_buf[i], ys_tile[...])
    tok0 = (w * nt + i) * ts
    gt2 = gt2_ref[pl.ds(tok0, ts), :] if tpr == 1 else gt2_ref[pl.ds(tok0 // tpr, 1), :]
    o_ref[...] = x1_ref[...] + gt2 * _rms(f, g_ref[...])


def _moe_call(cnt, x1, h2, r3, gt2, g_post, wg, wu, wd, ex, ts, nt, tpr):
    n = x1.shape[0]
    win = ts * nt
    rbuf = _moe_buffer_rows(ts, nt)
    rt = ts + LANES
    grid_spec = pltpu.PrefetchScalarGridSpec(
        num_scalar_prefetch=1,
        grid=(n // win, nt),
        in_specs=[
            pl.BlockSpec((win, D), lambda w, i, c: (w, 0)),
            pl.BlockSpec((win, LANES), lambda w, i, c: (w, 0)),
            pl.BlockSpec((ts, D), lambda w, i, c: (w * nt + i, 0)),
            _const_spec(gt2.shape),
            _const_spec(g_post.shape),
            _const_spec(wg.shape),
            _const_spec(wu.shape),
            _const_spec(wd.shape),
            _const_spec(ex.shape),
        ],
        out_specs=pl.BlockSpec((ts, D), lambda w, i, c: (w * nt + i, 0)),
        scratch_shapes=[
            pltpu.VMEM((rbuf, D), bf16),
            pltpu.VMEM((rbuf, LANES), bf16),
            pltpu.VMEM((rbuf, D), bf16),
            pltpu.VMEM((rt, D), bf16),
            pltpu.VMEM((rt, LANES), bf16),
            pltpu.VMEM((rt, D), bf16),
            pltpu.VMEM((nt, ts, rt), bf16),
            pltpu.VMEM((nt, rt, ts), bf16),
            pltpu.SMEM((nt * 2 * N_GROUPS,), jnp.int32),
        ],
    )
    return pl.pallas_call(
        functools.partial(_moe_kernel, ts=ts, nt=nt, tpr=tpr),
        grid_spec=grid_spec,
        out_shape=jax.ShapeDtypeStruct((n, D), f32),
        compiler_params=pltpu.CompilerParams(
            dimension_semantics=("arbitrary", "arbitrary"), vmem_limit_bytes=VMEM_LIMIT),
    )(cnt, h2, r3, x1, gt2, g_post, wg, wu, wd, ex)


def _count_table(cnt):
    return cnt[:, 0, :N_GROUPS].astype(jnp.int32).reshape(-1)


def kernel(x_prompt, x_sample, c_prompt, c_sample, state_pool, w_ada, b_ada, g_pre_mix, g_post_mix, g_pre_ffn, g_post_ffn, w_in, ln_v_g, ln_v_b, w_spatial, b_spatial, w_pool, pool_scale, w_out, w_router_grp, b_router_grp, w_router_exp, b_router_exp, w_exp_gate, w_exp_up, w_exp_down):
    depth = w_in.shape[0]
    assert depth == 1
    b, s, _ = x_prompt.shape
    n_s = x_sample.shape[0]
    l = 0

    c_all = jnp.concatenate([c_prompt, c_sample], axis=0)
    mod_p, mod_s = _mod_call(c_all, w_ada[l], b_ada[l], b)

    ws, bs = w_spatial[l], b_spatial[l]
    zeros = jnp.zeros((D,), f32)
    vecs = jnp.stack([
        g_pre_mix[l], g_post_mix[l], g_pre_ffn[l], g_post_ffn[l], ln_v_g[l], ln_v_b[l],
        pool_scale[l], zeros,
        jnp.repeat(ws[:, 0, 0], HEAD_DIM), jnp.repeat(bs[:, 0], HEAD_DIM),
        zeros, zeros, zeros, zeros, zeros, zeros])
    bias_full = jnp.repeat(bs.T, HEAD_DIM, axis=1)
    w_in_b = _cast_call(w_in[l], 512)
    w_out_b = _cast_call(w_out[l], D)
    pad = LANES - N_EXPERTS - N_GROUPS
    w_r = jnp.concatenate([w_router_exp[l], w_router_grp[l], jnp.zeros((D, pad), f32)], axis=1)
    w_r_hi = w_r.astype(bf16)
    w_r_lo = (w_r - w_r_hi.astype(f32)).astype(bf16)
    w_r2 = jnp.concatenate([w_r_hi, w_r_lo], axis=1)
    b_r = jnp.concatenate([b_router_exp[l], b_router_grp[l], jnp.zeros((pad,), f32)])[None]

    x1_p, h2_p, r3_p, cnt_p, plast, wg, wu, wd = _stage1_prompt_call(
        x_prompt, mod_p, vecs, w_in_b, ws, bias_full, w_pool[l], w_out_b, w_r2, b_r,
        w_exp_gate[l], w_exp_up[l], w_exp_down[l])
    state_t = jnp.transpose(state_pool[l], (1, 0, 2))
    x1_s, h2_s, r3_s, cnt_s, p_s, v_s = _stage1_sample_call(
        x_sample.reshape(n_s, D), mod_s, vecs, w_in_b, state_t, w_pool[l], w_out_b, w_r2, b_r)

    lane_e = (jnp.arange(LANES) % N_EXPERTS)[None, :, None]
    lane_ok = (jnp.arange(LANES) < 3 * N_EXPERTS)[None, :, None]
    col_e = (jnp.arange(EPG * D_EXPERT) // D_EXPERT)[None, None, :]
    ex = (lane_e == jnp.arange(N_GROUPS)[:, None, None] * EPG + col_e) & lane_ok
    ex = ex.astype(bf16).reshape(N_GROUPS * LANES, EPG * D_EXPERT)

    g_post = g_post_ffn[l].reshape(1, D)
    y_p = _moe_call(
        _count_table(cnt_p), x1_p.reshape(b * s, D), h2_p.reshape(b * s, D),
        r3_p.reshape(b * s, LANES), mod_p[5], g_post, wg, wu, wd, ex, TL, MOE_WINDOW // TL, s)
    y_s = _moe_call(
        _count_table(cnt_s[None]), x1_s, h2_s, r3_s, mod_s[5], g_post, wg, wu, wd, ex, n_s, 1, 1)

    state_pool_prompt = plast[:, 1:][None]
    state_pool_sample = jnp.concatenate([state_pool[l][:, 1:], p_s[:, None, :]], axis=1)[None]
    chunk_v_sample = v_s.reshape(1, n_s, 1, D)
    return (y_p.reshape(b, s, D), y_s.reshape(n_s, 1, D), state_pool_prompt,
            state_pool_sample, chunk_v_sample)
```

```python
import functools

import jax
import jax.numpy as jnp
from jax import lax
from jax.experimental import pallas as pl
from jax.experimental.pallas import tpu as pltpu

D = 1024
CHUNK = 128
HEADS = 8
HEAD_DIM = 128
WINDOWS = (2, 4, 8, 16)
PW = 512
PG = 128
W_MAX = 16
N_GROUPS = 4
EPG = 8
N_EXPERTS = 32
D_EXPERT = 128
EPS = 1e-6
LANES = 128
GROUP_LANE0 = 32
GIDX_LANE = 96

TL = 256
MOE_WINDOW = 2048
ROW_ALIGN = 16
RB = 128
VMEM_LIMIT = 56 * 1024 * 1024

bf16 = jnp.bfloat16
f32 = jnp.float32


def _rms(x, g):
    ms = jnp.mean(x * x, axis=-1, keepdims=True)
    return x * lax.rsqrt(ms + EPS) * g


def _dot(a, b):
    return jnp.dot(a, b, preferred_element_type=f32)


_GELU_C = 2.0 * 0.7978845608028654


def _gelu(x):
    t = x * ((-_GELU_C) + (-_GELU_C * 0.044715) * (x * x))
    return x / (1.0 + jnp.exp(t))


def _cast_kernel(w_ref, o_ref):
    o_ref[...] = w_ref[...].astype(bf16)


def _cast_call(w, block_cols):
    rows, cols = w.shape
    spec = pl.BlockSpec((rows, block_cols), lambda j: (0, j))
    return pl.pallas_call(
        _cast_kernel, grid=(cols // block_cols,), in_specs=[spec], out_specs=spec,
        out_shape=jax.ShapeDtypeStruct((rows, cols), bf16),
        compiler_params=pltpu.CompilerParams(dimension_semantics=("arbitrary",)),
    )(w)


def _expert_cast_specs(step_to_expert):
    e = step_to_expert
    col_block = pl.BlockSpec((D, D_EXPERT), lambda t: (e(t) // EPG, e(t) % EPG))
    in_specs = [
        pl.BlockSpec((1, D, D_EXPERT), lambda t: (e(t), 0, 0)),
        pl.BlockSpec((1, D, D_EXPERT), lambda t: (e(t), 0, 0)),
        pl.BlockSpec((1, D_EXPERT, D), lambda t: (e(t), 0, 0)),
    ]
    out_specs = [col_block, col_block, pl.BlockSpec((D_EXPERT, D), lambda t: (e(t), 0))]
    out_shapes = [
        jax.ShapeDtypeStruct((N_GROUPS * D, EPG * D_EXPERT), bf16),
        jax.ShapeDtypeStruct((N_GROUPS * D, EPG * D_EXPERT), bf16),
        jax.ShapeDtypeStruct((N_EXPERTS * D_EXPERT, D), bf16),
    ]
    return in_specs, out_specs, out_shapes


def _mod_kernel(c_ref, w_ref, b_ref, op_ref, os_ref):
    c = c_ref[...]
    a = (c * jax.nn.sigmoid(c)).astype(bf16)
    m = _dot(a, w_ref[...].astype(bf16)) + b_ref[0]
    nb = op_ref.shape[1]
    op_ref[0] = m[:nb]
    os_ref[0] = m[nb:]


def _mod_call(c_all, w_ada, b_ada, nb):
    n = c_all.shape[0]
    return pl.pallas_call(
        _mod_kernel,
        grid=(6,),
        in_specs=[
            pl.BlockSpec((n, D), lambda j: (0, 0)),
            pl.BlockSpec((D, D), lambda j: (0, j)),
            pl.BlockSpec((1, 1, D), lambda j: (j, 0, 0)),
        ],
        out_specs=[pl.BlockSpec((1, nb, D), lambda j: (j, 0, 0)),
                   pl.BlockSpec((1, n - nb, D), lambda j: (j, 0, 0))],
        out_shape=[jax.ShapeDtypeStruct((6, nb, D), f32),
                   jax.ShapeDtypeStruct((6, n - nb, D), f32)],
        compiler_params=pltpu.CompilerParams(
            dimension_semantics=("arbitrary",), vmem_limit_bytes=VMEM_LIMIT),
    )(c_all, w_ada, b_ada.reshape(6, 1, D))


def _route(logits):
    t = logits.shape[0]
    lane = lax.broadcasted_iota(jnp.int32, (t, LANES), 1)
    lane_f = lane.astype(f32)
    neg = -jnp.inf
    big = 1e9
    gmask = (lane >= GROUP_LANE0) & (lane < GROUP_LANE0 + N_GROUPS)
    gl = jnp.where(gmask, logits, neg)
    gmax = jnp.max(gl, axis=-1, keepdims=True)
    g_idx = jnp.min(jnp.where(gl == gmax, lane_f - GROUP_LANE0, big), axis=-1, keepdims=True)
    sumexp = jnp.sum(jnp.where(gmask, jnp.exp(gl - gmax), 0.0), axis=-1, keepdims=True)
    p_g = 1.0 / sumexp
    lane_grp = (lane >> 3).astype(f32)
    emask = (lane < N_EXPERTS) & (lane_grp == g_idx)
    el = jnp.where(emask, logits, neg)
    m1 = jnp.max(el, axis=-1, keepdims=True)
    i1 = jnp.min(jnp.where(el == m1, lane_f, big), axis=-1, keepdims=True)
    el2 = jnp.where(lane_f == i1, neg, el)
    m2 = jnp.max(el2, axis=-1, keepdims=True)
    i2 = jnp.min(jnp.where(el2 == m2, lane_f, big), axis=-1, keepdims=True)
    e = jnp.exp(m2 - m1)
    w1 = p_g / (1.0 + e)
    w2 = w1 * e

    def split3(w):
        hi = w.astype(bf16).astype(f32)
        mid = (w - hi).astype(bf16).astype(f32)
        lo = w - hi - mid
        return hi, mid, lo

    r3 = jnp.where(lane == GIDX_LANE, g_idx, 0.0)
    for idx, w in ((i1, w1), (i2, w2)):
        for part, wp in enumerate(split3(w)):
            r3 = r3 + jnp.where(lane_f == idx + float(part * N_EXPERTS), wp, 0.0)
    counts = jnp.sum(jnp.where(lane_f == g_idx, 1.0, 0.0), axis=0, keepdims=True)
    return r3.astype(bf16), jnp.broadcast_to(counts, (8, LANES))


def _merge_project(u, ga, gb, mix, y_b, w_out_ref):
    y_a = u * mix
    merged = jax.nn.sigmoid(ga) * y_a + jax.nn.sigmoid(gb) * y_b
    return _dot(merged.astype(bf16), w_out_ref[...])


def _residual_route(x, y, mods, vec_ref, w_r_ref, b_r_ref):
    sh1, sc1, gt1, sh2, sc2, gt2 = mods
    x1 = x + gt1 * _rms(y, vec_ref[1:2])
    h2 = _rms(x1, vec_ref[2:3] * (1.0 + sc2)) + sh2
    h2_hi = h2.astype(bf16)
    h2_lo = (h2 - h2_hi.astype(f32)).astype(bf16)
    r = _dot(h2_hi, w_r_ref[...]) + _dot(h2_lo, w_r_ref[...])
    logits = r[:, :LANES] + r[:, LANES:] + b_r_ref[...]
    r3, counts = _route(logits)
    return x1, h2_hi, r3, counts


def _in_proj(x, mods, vec_ref, w_in_ref, after_first_dot=lambda: None):
    sh1, sc1 = mods[0], mods[1]
    h = _rms(x, vec_ref[0:1] * (1.0 + sc1)) + sh1
    hb = h.astype(bf16)
    zu = _dot(hb, w_in_ref[:, 0:D])
    after_first_dot()
    zv = _dot(hb, w_in_ref[:, D:2 * D])
    p = _dot(hb, w_in_ref[:, 2 * D:2 * D + PW])
    ga = _dot(hb, w_in_ref[:, 2 * D + PW:3 * D + PW])
    gb = _dot(hb, w_in_ref[:, 3 * D + PW:4 * D + PW])
    return zu, zv, p, ga, gb


def _activate(zu, zv, vec_ref):
    u = _gelu(zu)
    v = _gelu(zv)
    mu = jnp.mean(v, axis=-1, keepdims=True)
    vc = v - mu
    var = jnp.mean(vc * vc, axis=-1, keepdims=True)
    v = vc * lax.rsqrt(var + EPS) * vec_ref[4:5] + vec_ref[5:6]
    return u, v


def _pool_out(d_groups, vec_ref, w_pool_ref):
    parts = [_dot(d.astype(bf16), w_pool_ref[gi].astype(bf16)) for gi, d in enumerate(d_groups)]
    return jnp.concatenate(parts, axis=1) * vec_ref[6:7]


def _stage1_prompt_kernel(x_ref, xprev_ref, mod_ref, vec_ref, w_in_ref, w_sp_ref, bias_ref,
                          w_pool_ref, w_out_ref, w_r_ref, b_r_ref, eg_ref, eu_ref, ed_ref,
                          x1_ref, h2_ref, r3_ref, cnt_ref, plast_ref, wg_ref, wu_ref, wd_ref,
                          pbuf, ybuf, *, tiles_per_seq):
    t = pl.program_id(0)
    n_tiles = pl.num_programs(0) - 1

    @pl.when(t == 0)
    def _():
        ybuf[...] = jnp.zeros_like(ybuf)
        pbuf[...] = jnp.zeros_like(pbuf)

    t_cur = jnp.minimum(t, n_tiles - 1)
    bi = t_cur // tiles_per_seq
    s = t_cur % tiles_per_seq
    x = x_ref[...]
    mods = [mod_ref[i, pl.ds(bi, 1), :] for i in range(6)]

    def prev_tile_second_half():
        wg_ref[...] = eg_ref[0].astype(bf16)
        wu_ref[...] = eu_ref[0].astype(bf16)
        wd_ref[...] = ed_ref[0].astype(bf16)
        t_prev = jnp.maximum(t - 1, 0)
        b_prev = t_prev // tiles_per_seq
        mods_prev = [mod_ref[i, pl.ds(b_prev, 1), :] for i in range(6)]
        x1, h2b, r3, counts = _residual_route(xprev_ref[...], ybuf[...], mods_prev, vec_ref,
                                              w_r_ref, b_r_ref)
        x1_ref[...] = x1
        h2_ref[...] = h2b
        r3_ref[...] = r3
        cnt_ref[0] = counts

    zu, zv, p, ga, gb = _in_proj(x, mods, vec_ref, w_in_ref, prev_tile_second_half)

    u, v = _activate(zu, zv, vec_ref)

    vb = v.astype(bf16)
    row = lax.broadcasted_iota(jnp.int32, (CHUNK, CHUNK), 0)
    col = lax.broadcasted_iota(jnp.int32, (CHUNK, CHUNK), 1)
    w_tril = [jnp.where(row >= col, w_sp_ref[hd], 0.0).astype(bf16) for hd in range(HEADS)]
    bias = bias_ref[...]
    chunks = []
    for c in range(TL // CHUNK):
        heads = [_dot(w_tril[hd], vb[c * CHUNK:(c + 1) * CHUNK, hd * HEAD_DIM:(hd + 1) * HEAD_DIM])
                 for hd in range(HEADS)]
        chunks.append(jnp.concatenate(heads, axis=1) + bias)
    mix = jnp.concatenate(chunks, axis=0)

    carry = jnp.where(s == 0, 0.0, pbuf[...])
    ext = jnp.concatenate([carry, p], axis=0)
    pos = s * TL + lax.broadcasted_iota(jnp.int32, (TL, PG), 0)
    d_groups = []
    for gi, w in enumerate(WINDOWS):
        acc = ext[:, gi * PG:(gi + 1) * PG]
        k = 1
        while k < w:
            acc = acc + pltpu.roll(acc, k, 0)
            k *= 2
        cnt = jnp.minimum(pos + 1, w).astype(f32)
        d_groups.append(acc[W_MAX:] / cnt - p[:, gi * PG:(gi + 1) * PG])
    pbuf[...] = p[TL - W_MAX:]
    plast_ref[0] = p[TL - W_MAX:]
    y_b = _pool_out(d_groups, vec_ref, w_pool_ref)
    ybuf[...] = _merge_project(u, ga, gb, mix, y_b, w_out_ref)


def _stage1_sample_kernel(x_ref, mod_ref, vec_ref, w_in_ref, state_ref, w_pool_ref,
                          w_out_ref, w_r_ref, b_r_ref,
                          x1_ref, h2_ref, r3_ref, cnt_ref, p_ref, v_ref):
    x = x_ref[...]
    mods = [mod_ref[i] for i in range(6)]
    zu, zv, p, ga, gb = _in_proj(x, mods, vec_ref, w_in_ref)
    u, v = _activate(zu, zv, vec_ref)
    v_ref[...] = v
    p_ref[...] = p
    mix = v * vec_ref[8:9] + vec_ref[9:10]
    d_groups = []
    for gi, w in enumerate(WINDOWS):
        sl = slice(gi * PG, (gi + 1) * PG)
        acc = p[:, sl]
        for r in range(W_MAX - w, W_MAX - 1):
            acc = acc + state_ref[r][:, sl]
        d_groups.append(acc / float(w) - p[:, sl])
    y_b = _pool_out(d_groups, vec_ref, w_pool_ref)
    y = _merge_project(u, ga, gb, mix, y_b, w_out_ref)
    x1, h2b, r3, counts = _residual_route(x, y, mods, vec_ref, w_r_ref, b_r_ref)
    x1_ref[...] = x1
    h2_ref[...] = h2b
    r3_ref[...] = r3
    cnt_ref[...] = counts


def _const_spec(shape):
    nd = len(shape)
    return pl.BlockSpec(shape, lambda *_: (0,) * nd, pipeline_mode=pl.Buffered(1))


def _stage1_prompt_call(x, mod_p, vecs, w_in_b, w_sp, bias_full, w_pool_b, w_out_b, w_r, b_r,
                        w_gate, w_up, w_down):
    b, s, _ = x.shape
    ns = s // TL
    nt = b * ns
    assert nt + 1 >= N_EXPERTS
    x2 = x.reshape(b * s, D)
    cur = lambda t: (jnp.minimum(t, nt - 1), 0)
    prev = lambda t: (jnp.maximum(t - 1, 0), 0)
    e_in, e_out, e_shapes = _expert_cast_specs(lambda t: jnp.minimum(t, N_EXPERTS - 1))
    return pl.pallas_call(
        functools.partial(_stage1_prompt_kernel, tiles_per_seq=ns),
        grid=(nt + 1,),
        in_specs=[
            pl.BlockSpec((TL, D), cur),
            pl.BlockSpec((TL, D), prev),
            _const_spec(mod_p.shape),
            _const_spec(vecs.shape),
            _const_spec(w_in_b.shape),
            _const_spec(w_sp.shape),
            _const_spec(bias_full.shape),
            _const_spec(w_pool_b.shape),
            _const_spec(w_out_b.shape),
            _const_spec(w_r.shape),
            _const_spec(b_r.shape),
        ] + e_in,
        out_specs=[
            pl.BlockSpec((TL, D), prev),
            pl.BlockSpec((TL, D), prev),
            pl.BlockSpec((TL, LANES), prev),
            pl.BlockSpec((1, 8, LANES), lambda t: (jnp.maximum(t - 1, 0), 0, 0)),
            pl.BlockSpec((1, W_MAX, PW), lambda t: (jnp.minimum(t, nt - 1) // ns, 0, 0)),
        ] + e_out,
        out_shape=[
            jax.ShapeDtypeStruct((b * s, D), f32),
            jax.ShapeDtypeStruct((b * s, D), bf16),
            jax.ShapeDtypeStruct((b * s, LANES), bf16),
            jax.ShapeDtypeStruct((nt, 8, LANES), f32),
            jax.ShapeDtypeStruct((b, W_MAX, PW), f32),
        ] + e_shapes,
        scratch_shapes=[pltpu.VMEM((W_MAX, PW), f32), pltpu.VMEM((TL, D), f32)],
        compiler_params=pltpu.CompilerParams(
            dimension_semantics=("arbitrary",), vmem_limit_bytes=VMEM_LIMIT),
    )(x2, x2, mod_p, vecs, w_in_b, w_sp, bias_full, w_pool_b, w_out_b, w_r, b_r,
      w_gate, w_up, w_down)


def _stage1_sample_call(x, mod_s, vecs, w_in_b, state_t, w_pool_b, w_out_b, w_r, b_r):
    n = x.shape[0]
    return pl.pallas_call(
        _stage1_sample_kernel,
        out_shape=[
            jax.ShapeDtypeStruct((n, D), f32),
            jax.ShapeDtypeStruct((n, D), bf16),
            jax.ShapeDtypeStruct((n, LANES), bf16),
            jax.ShapeDtypeStruct((8, LANES), f32),
            jax.ShapeDtypeStruct((n, PW), f32),
            jax.ShapeDtypeStruct((n, D), f32),
        ],
        compiler_params=pltpu.CompilerParams(vmem_limit_bytes=VMEM_LIMIT),
    )(x, mod_s, vecs, w_in_b, state_t, w_pool_b, w_out_b, w_r, b_r)


def _moe_buffer_rows(ts, nt):
    worst = ts * nt + nt * N_GROUPS * (ROW_ALIGN - 1) + N_GROUPS * (RB - 1)
    return -(-worst // RB) * RB


def _moe_kernel(cnt_ref, h2_ref, r3_ref, x1_ref, gt2_ref, g_ref, wg_ref, wu_ref, wd_ref, ex_ref,
                o_ref,
                hsbuf, c3buf, hs_tile, c3_tile, ys_tile, pt_buf, pm_buf, tab,
                *, ts, nt, tpr):
    w = pl.program_id(0)
    i = pl.program_id(1)
    rt = ts + LANES

    def copy_rows(src, src0, dst, dst0, nrows):
        def body(j, _):
            s0 = pl.multiple_of(src0 + j * ROW_ALIGN, ROW_ALIGN)
            d0 = pl.multiple_of(dst0 + j * ROW_ALIGN, ROW_ALIGN)
            dst[pl.ds(d0, ROW_ALIGN), :] = src[pl.ds(s0, ROW_ALIGN), :]
            return 0
        lax.fori_loop(0, nrows // ROW_ALIGN, body, 0)

    @pl.when((w == 0) & (i == 0))
    def _():
        hsbuf[...] = jnp.zeros_like(hsbuf)
        c3buf[...] = jnp.zeros_like(c3buf)
        ys_tile[...] = jnp.zeros_like(ys_tile)

    @pl.when(i == 0)
    def _sort_and_run_experts():
        def run_len(tile, g):
            c = cnt_ref[(w * nt + tile) * N_GROUPS + g]
            return ((c + (ROW_ALIGN - 1)) // ROW_ALIGN) * ROW_ALIGN

        lens = [[run_len(t, g) for g in range(N_GROUPS)] for t in range(nt)]
        region = [sum(lens[t][g] for t in range(nt)) for g in range(N_GROUPS)]
        region = [((r + (RB - 1)) // RB) * RB for r in region]
        base = [sum(region[:g]) for g in range(N_GROUPS)]
        offs = list(base)
        woff = []
        for t in range(nt):
            woff.append(list(offs))
            for g in range(N_GROUPS):
                tab[t * 2 * N_GROUPS + g] = lens[t][g]
                tab[t * 2 * N_GROUPS + N_GROUPS + g] = offs[g]
                offs[g] = offs[g] + lens[t][g]

        lane = lax.broadcasted_iota(jnp.int32, (ts, LANES), 1)
        lane_f = lane.astype(f32)
        r_i = lax.broadcasted_iota(jnp.int32, (ts, ts), 0)
        c_i = lax.broadcasted_iota(jnp.int32, (ts, ts), 1)
        ltri = jnp.where(r_i > c_i, 1.0, 0.0).astype(bf16)
        lane8 = lax.broadcasted_iota(jnp.int32, (8, LANES), 1)
        sel = jnp.where(lane8 == 0, float(ROW_ALIGN), jnp.where(lane8 == 1, 1.0, 0.0)).astype(bf16)
        rt_lane = lax.broadcasted_iota(jnp.int32, (ts, rt), 1).astype(f32)
        rt_sub = lax.broadcasted_iota(jnp.int32, (rt, ts), 0).astype(f32)

        tiles = range(nt)
        r3s = [r3_ref[t * ts:(t + 1) * ts, :] for t in tiles]
        gids = [jnp.sum(jnp.where(lane == GIDX_LANE, r3s[t].astype(f32), 0.0), axis=-1,
                        keepdims=True) for t in tiles]
        onehots = [jnp.where(lane_f == gids[t], 1.0, 0.0) for t in tiles]
        ranks = [_dot(ltri, onehots[t].astype(bf16)) for t in tiles]
        poss = []
        for t in tiles:
            seg = jnp.zeros((1, LANES), f32)
            start = 0
            for g in range(N_GROUPS):
                seg = seg + jnp.where(lane[0:1] == g, jnp.asarray(start, jnp.int32).astype(f32), 0.0)
                start = start + lens[t][g]
            poss.append(jnp.sum(onehots[t] * (ranks[t] + seg), axis=-1, keepdims=True))
        for t in tiles:
            pt_buf[t] = jnp.where(rt_lane == poss[t], 1.0, 0.0).astype(bf16)
        pos_rows = []
        for t in tiles:
            q = jnp.floor(poss[t] * (1.0 / ROW_ALIGN))
            digits = jnp.where(lane == 0, q, jnp.where(lane == 1, poss[t] - q * ROW_ALIGN, 0.0))
            pos_rows.append(lax.dot_general(sel, digits.astype(bf16), (((1,), (1,)), ((), ())),
                                            preferred_element_type=f32))
        for t in tiles:
            pm_buf[t] = jnp.where(rt_sub == pos_rows[t][0:1], 1.0, 0.0).astype(bf16)

        for t in tiles:
            p_mat = pm_buf[t]
            hs_tile[...] = _dot(p_mat, h2_ref[t * ts:(t + 1) * ts, :]).astype(bf16)
            c3_tile[...] = _dot(p_mat, r3_ref[t * ts:(t + 1) * ts, :]).astype(bf16)
            start = 0
            for g in range(N_GROUPS):
                copy_rows(hs_tile, start, hsbuf, woff[t][g], lens[t][g])
                copy_rows(c3_tile, start, c3buf, woff[t][g], lens[t][g])
                start = start + lens[t][g]

        def expert_inputs(g, r0):
            rows = hsbuf[pl.ds(r0, RB), :]
            gate = _dot(rows, wg_ref[g * D:(g + 1) * D, :])
            up = _dot(rows, wu_ref[g * D:(g + 1) * D, :])
            return gate, up

        def expert_outputs(g, r0, gate, up):
            cw = _dot(c3buf[pl.ds(r0, RB), :], ex_ref[g * LANES:(g + 1) * LANES, :])
            act = gate * jax.nn.sigmoid(gate) * up * cw
            hsbuf[pl.ds(r0, RB), :] = _dot(
                act.astype(bf16), wd_ref[g * D:(g + 1) * D, :]).astype(bf16)

        for g in range(N_GROUPS):
            n_blocks = region[g] // RB

            def block_pair(b, _, g=g):
                r0 = pl.multiple_of(base[g] + 2 * b * RB, RB)
                r1 = pl.multiple_of(r0 + RB, RB)
                in0 = expert_inputs(g, r0)
                in1 = expert_inputs(g, r1)
                expert_outputs(g, r0, *in0)
                expert_outputs(g, r1, *in1)
                return 0
            lax.fori_loop(0, n_blocks // 2, block_pair, 0)

            @pl.when(n_blocks % 2 == 1)
            def _(g=g, n_blocks=n_blocks):
                r0 = pl.multiple_of(base[g] + (n_blocks - 1) * RB, RB)
                expert_outputs(g, r0, *expert_inputs(g, r0))

    start = 0
    for g in range(N_GROUPS):
        ln = tab[i * 2 * N_GROUPS + g]
        copy_rows(hsbuf, tab[i * 2 * N_GROUPS + N_GROUPS + g], ys_tile, start, ln)
        start = start + ln
    f = _dot(pt_buf[i], ys_tile[...])
    tok0 = (w * nt + i) * ts
    gt2 = gt2_ref[pl.ds(tok0, ts), :] if tpr == 1 else gt2_ref[pl.ds(tok0 // tpr, 1), :]
    o_ref[...] = x1_ref[...] + gt2 * _rms(f, g_ref[...])


def _moe_call(cnt, x1, h2, r3, gt2, g_post, wg, wu, wd, ex, ts, nt, tpr):
    n = x1.shape[0]
    win = ts * nt
    rbuf = _moe_buffer_rows(ts, nt)
    rt = ts + LANES
    grid_spec = pltpu.PrefetchScalarGridSpec(
        num_scalar_prefetch=1,
        grid=(n // win, nt),
        in_specs=[
            pl.BlockSpec((win, D), lambda w, i, c: (w, 0), pipeline_mode=pl.Buffered(1)),
            pl.BlockSpec((win, LANES), lambda w, i, c: (w, 0)),
            pl.BlockSpec((ts, D), lambda w, i, c: (w * nt + i, 0)),
            _const_spec(gt2.shape),
            _const_spec(g_post.shape),
            _const_spec(wg.shape),
            _const_spec(wu.shape),
            _const_spec(wd.shape),
            _const_spec(ex.shape),
        ],
        out_specs=pl.BlockSpec((ts, D), lambda w, i, c: (w * nt + i, 0)),
        scratch_shapes=[
            pltpu.VMEM((rbuf, D), bf16),
            pltpu.VMEM((rbuf, LANES), bf16),
            pltpu.VMEM((rt, D), bf16),
            pltpu.VMEM((rt, LANES), bf16),
            pltpu.VMEM((rt, D), bf16),
            pltpu.VMEM((nt, ts, rt), bf16),
            pltpu.VMEM((nt, rt, ts), bf16),
            pltpu.SMEM((nt * 2 * N_GROUPS,), jnp.int32),
        ],
    )
    return pl.pallas_call(
        functools.partial(_moe_kernel, ts=ts, nt=nt, tpr=tpr),
        grid_spec=grid_spec,
        out_shape=jax.ShapeDtypeStruct((n, D), f32),
        compiler_params=pltpu.CompilerParams(
            dimension_semantics=("arbitrary", "arbitrary"), vmem_limit_bytes=VMEM_LIMIT),
    )(cnt, h2, r3, x1, gt2, g_post, wg, wu, wd, ex)


def _count_table(cnt):
    return cnt[:, 0, :N_GROUPS].astype(jnp.int32).reshape(-1)


def kernel(x_prompt, x_sample, c_prompt, c_sample, state_pool, w_ada, b_ada, g_pre_mix, g_post_mix, g_pre_ffn, g_post_ffn, w_in, ln_v_g, ln_v_b, w_spatial, b_spatial, w_pool, pool_scale, w_out, w_router_grp, b_router_grp, w_router_exp, b_router_exp, w_exp_gate, w_exp_up, w_exp_down):
    depth = w_in.shape[0]
    assert depth == 1
    b, s, _ = x_prompt.shape
    n_s = x_sample.shape[0]
    l = 0

    c_all = jnp.concatenate([c_prompt, c_sample], axis=0)
    mod_p, mod_s = _mod_call(c_all, w_ada[l], b_ada[l], b)

    ws, bs = w_spatial[l], b_spatial[l]
    zeros = jnp.zeros((D,), f32)
    vecs = jnp.stack([
        g_pre_mix[l], g_post_mix[l], g_pre_ffn[l], g_post_ffn[l], ln_v_g[l], ln_v_b[l],
        pool_scale[l], zeros,
        jnp.repeat(ws[:, 0, 0], HEAD_DIM), jnp.repeat(bs[:, 0], HEAD_DIM),
        zeros, zeros, zeros, zeros, zeros, zeros])
    bias_full = jnp.repeat(bs.T, HEAD_DIM, axis=1)
    w_in_b = _cast_call(w_in[l], 512)
    w_out_b = _cast_call(w_out[l], D)
    pad = LANES - N_EXPERTS - N_GROUPS
    w_r = jnp.concatenate([w_router_exp[l], w_router_grp[l], jnp.zeros((D, pad), f32)], axis=1)
    w_r_hi = w_r.astype(bf16)
    w_r_lo = (w_r - w_r_hi.astype(f32)).astype(bf16)
    w_r2 = jnp.concatenate([w_r_hi, w_r_lo], axis=1)
    b_r = jnp.concatenate([b_router_exp[l], b_router_grp[l], jnp.zeros((pad,), f32)])[None]

    x1_p, h2_p, r3_p, cnt_p, plast, wg, wu, wd = _stage1_prompt_call(
        x_prompt, mod_p, vecs, w_in_b, ws, bias_full, w_pool[l], w_out_b, w_r2, b_r,
        w_exp_gate[l], w_exp_up[l], w_exp_down[l])
    state_t = jnp.transpose(state_pool[l], (1, 0, 2))
    x1_s, h2_s, r3_s, cnt_s, p_s, v_s = _stage1_sample_call(
        x_sample.reshape(n_s, D), mod_s, vecs, w_in_b, state_t, w_pool[l], w_out_b, w_r2, b_r)

    lane_e = (jnp.arange(LANES) % N_EXPERTS)[None, :, None]
    lane_ok = (jnp.arange(LANES) < 3 * N_EXPERTS)[None, :, None]
    col_e = (jnp.arange(EPG * D_EXPERT) // D_EXPERT)[None, None, :]
    ex = (lane_e == jnp.arange(N_GROUPS)[:, None, None] * EPG + col_e) & lane_ok
    ex = ex.astype(bf16).reshape(N_GROUPS * LANES, EPG * D_EXPERT)

    g_post = g_post_ffn[l].reshape(1, D)
    y_p = _moe_call(
        _count_table(cnt_p), x1_p.reshape(b * s, D), h2_p.reshape(b * s, D),
        r3_p.reshape(b * s, LANES), mod_p[5], g_post, wg, wu, wd, ex, TL, MOE_WINDOW // TL, s)
    y_s = _moe_call(
        _count_table(cnt_s[None]), x1_s, h2_s, r3_s, mod_s[5], g_post, wg, wu, wd, ex, n_s, 1, 1)

    state_pool_prompt = plast[:, 1:][None]
    state_pool_sample = jnp.concatenate([state_pool[l][:, 1:], p_s[:, None, :]], axis=1)[None]
    chunk_v_sample = v_s.reshape(1, n_s, 1, D)
    return (y_p.reshape(b, s, D), y_s.reshape(n_s, 1, D), state_pool_prompt,
            state_pool_sample, chunk_v_sample)
```

```python
import functools

import jax
import jax.numpy as jnp
from jax import lax
from jax.experimental import pallas as pl
from jax.experimental.pallas import tpu as pltpu

D = 1024
CHUNK = 128
HEADS = 8
HEAD_DIM = 128
WINDOWS = (2, 4, 8, 16)
PW = 512
PG = 128
W_MAX = 16
N_GROUPS = 4
EPG = 8
N_EXPERTS = 32
D_EXPERT = 128
EPS = 1e-6
LANES = 128
GROUP_LANE0 = 32
GIDX_LANE = 96

TL = 256
MOE_WINDOW = 2048
ROW_ALIGN = 16
RB = 128
VMEM_LIMIT = 56 * 1024 * 1024

bf16 = jnp.bfloat16
f32 = jnp.float32


def _rms(x, g):
    ms = jnp.mean(x * x, axis=-1, keepdims=True)
    return x * lax.rsqrt(ms + EPS) * g


def _dot(a, b):
    return jnp.dot(a, b, preferred_element_type=f32)


_GELU_C = 2.0 * 0.7978845608028654


def _gelu(x):
    t = x * ((-_GELU_C) + (-_GELU_C * 0.044715) * (x * x))
    return x / (1.0 + jnp.exp(t))


def _cast_kernel(w_ref, o_ref):
    o_ref[...] = w_ref[...].astype(bf16)


def _cast_call(w, block_cols):
    rows, cols = w.shape
    spec = pl.BlockSpec((rows, block_cols), lambda j: (0, j))
    return pl.pallas_call(
        _cast_kernel, grid=(cols // block_cols,), in_specs=[spec], out_specs=spec,
        out_shape=jax.ShapeDtypeStruct((rows, cols), bf16),
        compiler_params=pltpu.CompilerParams(dimension_semantics=("arbitrary",)),
    )(w)


def _expert_cast_specs(step_to_expert):
    e = step_to_expert
    col_block = pl.BlockSpec((D, D_EXPERT), lambda t: (e(t) // EPG, e(t) % EPG))
    in_specs = [
        pl.BlockSpec((1, D, D_EXPERT), lambda t: (e(t), 0, 0)),
        pl.BlockSpec((1, D, D_EXPERT), lambda t: (e(t), 0, 0)),
        pl.BlockSpec((1, D_EXPERT, D), lambda t: (e(t), 0, 0)),
    ]
    out_specs = [col_block, col_block, pl.BlockSpec((D_EXPERT, D), lambda t: (e(t), 0))]
    out_shapes = [
        jax.ShapeDtypeStruct((N_GROUPS * D, EPG * D_EXPERT), bf16),
        jax.ShapeDtypeStruct((N_GROUPS * D, EPG * D_EXPERT), bf16),
        jax.ShapeDtypeStruct((N_EXPERTS * D_EXPERT, D), bf16),
    ]
    return in_specs, out_specs, out_shapes


def _mod_kernel(c_ref, w_ref, b_ref, op_ref, os_ref):
    c = c_ref[...]
    a = (c * jax.nn.sigmoid(c)).astype(bf16)
    m = _dot(a, w_ref[...].astype(bf16)) + b_ref[0]
    nb = op_ref.shape[1]
    op_ref[0] = m[:nb]
    os_ref[0] = m[nb:]


def _mod_call(c_all, w_ada, b_ada, nb):
    n = c_all.shape[0]
    return pl.pallas_call(
        _mod_kernel,
        grid=(6,),
        in_specs=[
            pl.BlockSpec((n, D), lambda j: (0, 0)),
            pl.BlockSpec((D, D), lambda j: (0, j)),
            pl.BlockSpec((1, 1, D), lambda j: (j, 0, 0)),
        ],
        out_specs=[pl.BlockSpec((1, nb, D), lambda j: (j, 0, 0)),
                   pl.BlockSpec((1, n - nb, D), lambda j: (j, 0, 0))],
        out_shape=[jax.ShapeDtypeStruct((6, nb, D), f32),
                   jax.ShapeDtypeStruct((6, n - nb, D), f32)],
        compiler_params=pltpu.CompilerParams(
            dimension_semantics=("arbitrary",), vmem_limit_bytes=VMEM_LIMIT),
    )(c_all, w_ada, b_ada.reshape(6, 1, D))


def _route(logits):
    t = logits.shape[0]
    lane = lax.broadcasted_iota(jnp.int32, (t, LANES), 1)
    lane_f = lane.astype(f32)
    neg = -jnp.inf
    big = 1e9
    gmask = (lane >= GROUP_LANE0) & (lane < GROUP_LANE0 + N_GROUPS)
    gl = jnp.where(gmask, logits, neg)
    gmax = jnp.max(gl, axis=-1, keepdims=True)
    g_idx = jnp.min(jnp.where(gl == gmax, lane_f - GROUP_LANE0, big), axis=-1, keepdims=True)
    sumexp = jnp.sum(jnp.where(gmask, jnp.exp(gl - gmax), 0.0), axis=-1, keepdims=True)
    p_g = 1.0 / sumexp
    lane_grp = (lane >> 3).astype(f32)
    emask = (lane < N_EXPERTS) & (lane_grp == g_idx)
    el = jnp.where(emask, logits, neg)
    m1 = jnp.max(el, axis=-1, keepdims=True)
    i1 = jnp.min(jnp.where(el == m1, lane_f, big), axis=-1, keepdims=True)
    el2 = jnp.where(lane_f == i1, neg, el)
    m2 = jnp.max(el2, axis=-1, keepdims=True)
    i2 = jnp.min(jnp.where(el2 == m2, lane_f, big), axis=-1, keepdims=True)
    e = jnp.exp(m2 - m1)
    w1 = p_g / (1.0 + e)
    w2 = w1 * e

    def split3(w):
        hi = w.astype(bf16).astype(f32)
        mid = (w - hi).astype(bf16).astype(f32)
        lo = w - hi - mid
        return hi, mid, lo

    r3 = jnp.where(lane == GIDX_LANE, g_idx, 0.0)
    for idx, w in ((i1, w1), (i2, w2)):
        for part, wp in enumerate(split3(w)):
            r3 = r3 + jnp.where(lane_f == idx + float(part * N_EXPERTS), wp, 0.0)
    counts = jnp.sum(jnp.where(lane_f == g_idx, 1.0, 0.0), axis=0, keepdims=True)
    return r3.astype(bf16), jnp.broadcast_to(counts, (8, LANES))


def _merge_project(u, ga, gb, mix, y_b, w_out_ref):
    y_a = u * mix
    merged = jax.nn.sigmoid(ga) * y_a + jax.nn.sigmoid(gb) * y_b
    return _dot(merged.astype(bf16), w_out_ref[...])


def _residual_route(x, y, mods, vec_ref, w_r_ref, b_r_ref):
    sh1, sc1, gt1, sh2, sc2, gt2 = mods
    x1 = x + gt1 * _rms(y, vec_ref[1:2])
    h2 = _rms(x1, vec_ref[2:3] * (1.0 + sc2)) + sh2
    h2_hi = h2.astype(bf16)
    h2_lo = (h2 - h2_hi.astype(f32)).astype(bf16)
    r = _dot(h2_hi, w_r_ref[...]) + _dot(h2_lo, w_r_ref[...])
    logits = r[:, :LANES] + r[:, LANES:] + b_r_ref[...]
    r3, counts = _route(logits)
    return x1, h2_hi, r3, counts


def _in_proj(x, mods, vec_ref, w_in_ref, after_first_dot=lambda: None):
    sh1, sc1 = mods[0], mods[1]
    h = _rms(x, vec_ref[0:1] * (1.0 + sc1)) + sh1
    hb = h.astype(bf16)
    zu = _dot(hb, w_in_ref[:, 0:D])
    after_first_dot()
    zv = _dot(hb, w_in_ref[:, D:2 * D])
    p = _dot(hb, w_in_ref[:, 2 * D:2 * D + PW])
    ga = _dot(hb, w_in_ref[:, 2 * D + PW:3 * D + PW])
    gb = _dot(hb, w_in_ref[:, 3 * D + PW:4 * D + PW])
    return zu, zv, p, ga, gb


def _activate(zu, zv, vec_ref):
    u = _gelu(zu)
    v = _gelu(zv)
    mu = jnp.mean(v, axis=-1, keepdims=True)
    vc = v - mu
    var = jnp.mean(vc * vc, axis=-1, keepdims=True)
    v = vc * lax.rsqrt(var + EPS) * vec_ref[4:5] + vec_ref[5:6]
    return u, v


def _pool_out(d_groups, vec_ref, w_pool_ref):
    parts = [_dot(d.astype(bf16), w_pool_ref[gi].astype(bf16)) for gi, d in enumerate(d_groups)]
    return jnp.concatenate(parts, axis=1) * vec_ref[6:7]


def _stage1_prompt_kernel(x_ref, xprev_ref, mod_ref, vec_ref, w_in_ref, w_sp_ref, bias_ref,
                          w_pool_ref, w_out_ref, w_r_ref, b_r_ref, eg_ref, eu_ref, ed_ref,
                          x1_ref, h2_ref, r3_ref, cnt_ref, plast_ref, wg_ref, wu_ref, wd_ref,
                          pbuf, ybuf, *, tiles_per_seq):
    t = pl.program_id(0)
    n_tiles = pl.num_programs(0) - 1

    @pl.when(t == 0)
    def _():
        ybuf[...] = jnp.zeros_like(ybuf)
        pbuf[...] = jnp.zeros_like(pbuf)

    t_cur = jnp.minimum(t, n_tiles - 1)
    bi = t_cur // tiles_per_seq
    s = t_cur % tiles_per_seq
    x = x_ref[...]
    mods = [mod_ref[i, pl.ds(bi, 1), :] for i in range(6)]

    def prev_tile_second_half():
        wg_ref[...] = eg_ref[0].astype(bf16)
        wu_ref[...] = eu_ref[0].astype(bf16)
        wd_ref[...] = ed_ref[0].astype(bf16)
        t_prev = jnp.maximum(t - 1, 0)
        b_prev = t_prev // tiles_per_seq
        mods_prev = [mod_ref[i, pl.ds(b_prev, 1), :] for i in range(6)]
        x1, h2b, r3, counts = _residual_route(xprev_ref[...], ybuf[...], mods_prev, vec_ref,
                                              w_r_ref, b_r_ref)
        x1_ref[...] = x1
        h2_ref[...] = h2b
        r3_ref[...] = r3
        cnt_ref[0] = counts

    zu, zv, p, ga, gb = _in_proj(x, mods, vec_ref, w_in_ref, prev_tile_second_half)

    u, v = _activate(zu, zv, vec_ref)

    vb = v.astype(bf16)
    row = lax.broadcasted_iota(jnp.int32, (CHUNK, CHUNK), 0)
    col = lax.broadcasted_iota(jnp.int32, (CHUNK, CHUNK), 1)
    w_tril = [jnp.where(row >= col, w_sp_ref[hd], 0.0).astype(bf16) for hd in range(HEADS)]
    bias = bias_ref[...]
    chunks = []
    for c in range(TL // CHUNK):
        heads = [_dot(w_tril[hd], vb[c * CHUNK:(c + 1) * CHUNK, hd * HEAD_DIM:(hd + 1) * HEAD_DIM])
                 for hd in range(HEADS)]
        chunks.append(jnp.concatenate(heads, axis=1) + bias)
    mix = jnp.concatenate(chunks, axis=0)

    carry = jnp.where(s == 0, 0.0, pbuf[...])
    ext = jnp.concatenate([carry, p], axis=0)
    pos = s * TL + lax.broadcasted_iota(jnp.int32, (TL, PG), 0)
    d_groups = []
    for gi, w in enumerate(WINDOWS):
        acc = ext[:, gi * PG:(gi + 1) * PG]
        k = 1
        while k < w:
            acc = acc + pltpu.roll(acc, k, 0)
            k *= 2
        cnt = jnp.minimum(pos + 1, w).astype(f32)
        d_groups.append(acc[W_MAX:] / cnt - p[:, gi * PG:(gi + 1) * PG])
    pbuf[...] = p[TL - W_MAX:]
    plast_ref[0] = p[TL - W_MAX:]
    y_b = _pool_out(d_groups, vec_ref, w_pool_ref)
    ybuf[...] = _merge_project(u, ga, gb, mix, y_b, w_out_ref)


def _stage1_sample_kernel(x_ref, mod_ref, vec_ref, w_in_ref, state_ref, w_pool_ref,
                          w_out_ref, w_r_ref, b_r_ref,
                          x1_ref, h2_ref, r3_ref, cnt_ref, p_ref, v_ref):
    x = x_ref[...]
    mods = [mod_ref[i] for i in range(6)]
    zu, zv, p, ga, gb = _in_proj(x, mods, vec_ref, w_in_ref)
    u, v = _activate(zu, zv, vec_ref)
    v_ref[...] = v
    p_ref[...] = p
    mix = v * vec_ref[8:9] + vec_ref[9:10]
    d_groups = []
    for gi, w in enumerate(WINDOWS):
        sl = slice(gi * PG, (gi + 1) * PG)
        acc = p[:, sl]
        for r in range(W_MAX - w, W_MAX - 1):
            acc = acc + state_ref[r][:, sl]
        d_groups.append(acc / float(w) - p[:, sl])
    y_b = _pool_out(d_groups, vec_ref, w_pool_ref)
    y = _merge_project(u, ga, gb, mix, y_b, w_out_ref)
    x1, h2b, r3, counts = _residual_route(x, y, mods, vec_ref, w_r_ref, b_r_ref)
    x1_ref[...] = x1
    h2_ref[...] = h2b
    r3_ref[...] = r3
    cnt_ref[...] = counts


def _const_spec(shape):
    nd = len(shape)
    return pl.BlockSpec(shape, lambda *_: (0,) * nd, pipeline_mode=pl.Buffered(1))


def _stage1_prompt_call(x, mod_p, vecs, w_in_b, w_sp, bias_full, w_pool_b, w_out_b, w_r, b_r,
                        w_gate, w_up, w_down):
    b, s, _ = x.shape
    ns = s // TL
    nt = b * ns
    assert nt + 1 >= N_EXPERTS
    x2 = x.reshape(b * s, D)
    cur = lambda t: (jnp.minimum(t, nt - 1), 0)
    prev = lambda t: (jnp.maximum(t - 1, 0), 0)
    e_in, e_out, e_shapes = _expert_cast_specs(lambda t: jnp.minimum(t, N_EXPERTS - 1))
    return pl.pallas_call(
        functools.partial(_stage1_prompt_kernel, tiles_per_seq=ns),
        grid=(nt + 1,),
        in_specs=[
            pl.BlockSpec((TL, D), cur),
            pl.BlockSpec((TL, D), prev),
            _const_spec(mod_p.shape),
            _const_spec(vecs.shape),
            _const_spec(w_in_b.shape),
            _const_spec(w_sp.shape),
            _const_spec(bias_full.shape),
            _const_spec(w_pool_b.shape),
            _const_spec(w_out_b.shape),
            _const_spec(w_r.shape),
            _const_spec(b_r.shape),
        ] + e_in,
        out_specs=[
            pl.BlockSpec((TL, D), prev),
            pl.BlockSpec((TL, D), prev),
            pl.BlockSpec((TL, LANES), prev),
            pl.BlockSpec((1, 8, LANES), lambda t: (jnp.maximum(t - 1, 0), 0, 0)),
            pl.BlockSpec((1, W_MAX, PW), lambda t: (jnp.minimum(t, nt - 1) // ns, 0, 0)),
        ] + e_out,
        out_shape=[
            jax.ShapeDtypeStruct((b * s, D), f32),
            jax.ShapeDtypeStruct((b * s, D), bf16),
            jax.ShapeDtypeStruct((b * s, LANES), bf16),
            jax.ShapeDtypeStruct((nt, 8, LANES), f32),
            jax.ShapeDtypeStruct((b, W_MAX, PW), f32),
        ] + e_shapes,
        scratch_shapes=[pltpu.VMEM((W_MAX, PW), f32), pltpu.VMEM((TL, D), f32)],
        compiler_params=pltpu.CompilerParams(
            dimension_semantics=("arbitrary",), vmem_limit_bytes=VMEM_LIMIT),
    )(x2, x2, mod_p, vecs, w_in_b, w_sp, bias_full, w_pool_b, w_out_b, w_r, b_r,
      w_gate, w_up, w_down)


def _stage1_sample_call(x, mod_s, vecs, w_in_b, state_t, w_pool_b, w_out_b, w_r, b_r):
    n = x.shape[0]
    return pl.pallas_call(
        _stage1_sample_kernel,
        out_shape=[
            jax.ShapeDtypeStruct((n, D), f32),
            jax.ShapeDtypeStruct((n, D), bf16),
            jax.ShapeDtypeStruct((n, LANES), bf16),
            jax.ShapeDtypeStruct((8, LANES), f32),
            jax.ShapeDtypeStruct((n, PW), f32),
            jax.ShapeDtypeStruct((n, D), f32),
        ],
        compiler_params=pltpu.CompilerParams(vmem_limit_bytes=VMEM_LIMIT),
    )(x, mod_s, vecs, w_in_b, state_t, w_pool_b, w_out_b, w_r, b_r)


def _moe_buffer_rows(ts, nt):
    worst = ts * nt + nt * N_GROUPS * (ROW_ALIGN - 1) + N_GROUPS * (RB - 1)
    return -(-worst // RB) * RB


def _moe_kernel(cnt_ref, h2_ref, r3_ref, x1_ref, gt2_ref, g_ref, wg_ref, wu_ref, wd_ref, ex_ref,
                o_ref,
                hsbuf, hs_tile, ys_tile, pt_buf, pm_buf, tab,
                *, ts, nt, tpr):
    w = pl.program_id(0)
    i = pl.program_id(1)
    rt = ts + LANES

    def copy_rows(src, src0, dst, dst0, nrows, ncols):
        def body(j, _):
            s0 = pl.multiple_of(src0 + j * ROW_ALIGN, ROW_ALIGN)
            d0 = pl.multiple_of(dst0 + j * ROW_ALIGN, ROW_ALIGN)
            dst[pl.ds(d0, ROW_ALIGN), :] = src[pl.ds(s0, ROW_ALIGN), :ncols]
            return 0
        lax.fori_loop(0, nrows // ROW_ALIGN, body, 0)

    @pl.when((w == 0) & (i == 0))
    def _():
        hsbuf[...] = jnp.zeros_like(hsbuf)
        ys_tile[...] = jnp.zeros_like(ys_tile)

    @pl.when(i == 0)
    def _sort_and_run_experts():
        def run_len(tile, g):
            c = cnt_ref[(w * nt + tile) * N_GROUPS + g]
            return ((c + (ROW_ALIGN - 1)) // ROW_ALIGN) * ROW_ALIGN

        lens = [[run_len(t, g) for g in range(N_GROUPS)] for t in range(nt)]
        region = [sum(lens[t][g] for t in range(nt)) for g in range(N_GROUPS)]
        region = [((r + (RB - 1)) // RB) * RB for r in region]
        base = [sum(region[:g]) for g in range(N_GROUPS)]
        offs = list(base)
        woff = []
        for t in range(nt):
            woff.append(list(offs))
            for g in range(N_GROUPS):
                tab[t * 2 * N_GROUPS + g] = lens[t][g]
                tab[t * 2 * N_GROUPS + N_GROUPS + g] = offs[g]
                offs[g] = offs[g] + lens[t][g]

        lane = lax.broadcasted_iota(jnp.int32, (ts, LANES), 1)
        lane_f = lane.astype(f32)
        r_i = lax.broadcasted_iota(jnp.int32, (ts, ts), 0)
        c_i = lax.broadcasted_iota(jnp.int32, (ts, ts), 1)
        ltri = jnp.where(r_i > c_i, 1.0, 0.0).astype(bf16)
        lane8 = lax.broadcasted_iota(jnp.int32, (8, LANES), 1)
        sel = jnp.where(lane8 == 0, float(ROW_ALIGN), jnp.where(lane8 == 1, 1.0, 0.0)).astype(bf16)
        rt_lane = lax.broadcasted_iota(jnp.int32, (ts, rt), 1).astype(f32)
        rt_sub = lax.broadcasted_iota(jnp.int32, (rt, ts), 0).astype(f32)

        tiles = range(nt)
        r3s = [r3_ref[t * ts:(t + 1) * ts, :] for t in tiles]
        gids = [jnp.sum(jnp.where(lane == GIDX_LANE, r3s[t].astype(f32), 0.0), axis=-1,
                        keepdims=True) for t in tiles]
        onehots = [jnp.where(lane_f == gids[t], 1.0, 0.0) for t in tiles]
        ranks = [_dot(ltri, onehots[t].astype(bf16)) for t in tiles]
        poss = []
        for t in tiles:
            seg = jnp.zeros((1, LANES), f32)
            start = 0
            for g in range(N_GROUPS):
                seg = seg + jnp.where(lane[0:1] == g, jnp.asarray(start, jnp.int32).astype(f32), 0.0)
                start = start + lens[t][g]
            poss.append(jnp.sum(onehots[t] * (ranks[t] + seg), axis=-1, keepdims=True))
        for t in tiles:
            pt_buf[t] = jnp.where(rt_lane == poss[t], 1.0, 0.0).astype(bf16)
        pos_rows = []
        for t in tiles:
            q = jnp.floor(poss[t] * (1.0 / ROW_ALIGN))
            digits = jnp.where(lane == 0, q, jnp.where(lane == 1, poss[t] - q * ROW_ALIGN, 0.0))
            pos_rows.append(lax.dot_general(sel, digits.astype(bf16), (((1,), (1,)), ((), ())),
                                            preferred_element_type=f32))
        for t in tiles:
            pm_buf[t] = jnp.where(rt_sub == pos_rows[t][0:1], 1.0, 0.0).astype(bf16)

        for t in tiles:
            p_mat = pm_buf[t]
            hs_tile[:, :D] = _dot(p_mat, h2_ref[t * ts:(t + 1) * ts, :]).astype(bf16)
            hs_tile[:, D:] = _dot(p_mat, r3_ref[t * ts:(t + 1) * ts, :]).astype(bf16)
            start = 0
            for g in range(N_GROUPS):
                copy_rows(hs_tile, start, hsbuf, woff[t][g], lens[t][g], D + LANES)
                start = start + lens[t][g]

        def expert_rows(g, r0, nrows):
            rows = hsbuf[pl.ds(r0, nrows), :D]
            gate = _dot(rows, wg_ref[g * D:(g + 1) * D, :])
            up = _dot(rows, wu_ref[g * D:(g + 1) * D, :])
            cw = _dot(hsbuf[pl.ds(r0, nrows), D:], ex_ref[g * LANES:(g + 1) * LANES, :])
            act = gate * jax.nn.sigmoid(gate) * up * cw
            hsbuf[pl.ds(r0, nrows), :D] = _dot(
                act.astype(bf16), wd_ref[g * D:(g + 1) * D, :]).astype(bf16)

        for g in range(N_GROUPS):
            n_blocks = region[g] // RB

            def two_blocks(b, _, g=g):
                expert_rows(g, pl.multiple_of(base[g] + 2 * b * RB, RB), 2 * RB)
                return 0
            lax.fori_loop(0, n_blocks // 2, two_blocks, 0)

            @pl.when(n_blocks % 2 == 1)
            def _(g=g, n_blocks=n_blocks):
                expert_rows(g, pl.multiple_of(base[g] + (n_blocks - 1) * RB, RB), RB)

    start = 0
    for g in range(N_GROUPS):
        ln = tab[i * 2 * N_GROUPS + g]
        copy_rows(hsbuf, tab[i * 2 * N_GROUPS + N_GROUPS + g], ys_tile, start, ln, D)
        start = start + ln
    f = _dot(pt_buf[i], ys_tile[...])
    tok0 = (w * nt + i) * ts
    gt2 = gt2_ref[pl.ds(tok0, ts), :] if tpr == 1 else gt2_ref[pl.ds(tok0 // tpr, 1), :]
    o_ref[...] = x1_ref[...] + gt2 * _rms(f, g_ref[...])


def _moe_call(cnt, x1, h2, r3, gt2, g_post, wg, wu, wd, ex, ts, nt, tpr):
    n = x1.shape[0]
    win = ts * nt
    rbuf = _moe_buffer_rows(ts, nt)
    rt = ts + LANES
    grid_spec = pltpu.PrefetchScalarGridSpec(
        num_scalar_prefetch=1,
        grid=(n // win, nt),
        in_specs=[
            pl.BlockSpec((win, D), lambda w, i, c: (w, 0)),
            pl.BlockSpec((win, LANES), lambda w, i, c: (w, 0)),
            pl.BlockSpec((ts, D), lambda w, i, c: (w * nt + i, 0)),
            _const_spec(gt2.shape),
            _const_spec(g_post.shape),
            _const_spec(wg.shape),
            _const_spec(wu.shape),
            _const_spec(wd.shape),
            _const_spec(ex.shape),
        ],
        out_specs=pl.BlockSpec((ts, D), lambda w, i, c: (w * nt + i, 0)),
        scratch_shapes=[
            pltpu.VMEM((rbuf, D + LANES), bf16),
            pltpu.VMEM((rt, D + LANES), bf16),
            pltpu.VMEM((rt, D), bf16),
            pltpu.VMEM((nt, ts, rt), bf16),
            pltpu.VMEM((nt, rt, ts), bf16),
            pltpu.SMEM((nt * 2 * N_GROUPS,), jnp.int32),
        ],
    )
    return pl.pallas_call(
        functools.partial(_moe_kernel, ts=ts, nt=nt, tpr=tpr),
        grid_spec=grid_spec,
        out_shape=jax.ShapeDtypeStruct((n, D), f32),
        compiler_params=pltpu.CompilerParams(
            dimension_semantics=("arbitrary", "arbitrary"), vmem_limit_bytes=VMEM_LIMIT),
    )(cnt, h2, r3, x1, gt2, g_post, wg, wu, wd, ex)


def _count_table(cnt):
    return cnt[:, 0, :N_GROUPS].astype(jnp.int32).reshape(-1)


def kernel(x_prompt, x_sample, c_prompt, c_sample, state_pool, w_ada, b_ada, g_pre_mix, g_post_mix, g_pre_ffn, g_post_ffn, w_in, ln_v_g, ln_v_b, w_spatial, b_spatial, w_pool, pool_scale, w_out, w_router_grp, b_router_grp, w_router_exp, b_router_exp, w_exp_gate, w_exp_up, w_exp_down):
    depth = w_in.shape[0]
    assert depth == 1
    b, s, _ = x_prompt.shape
    n_s = x_sample.shape[0]
    l = 0

    c_all = jnp.concatenate([c_prompt, c_sample], axis=0)
    mod_p, mod_s = _mod_call(c_all, w_ada[l], b_ada[l], b)

    ws, bs = w_spatial[l], b_spatial[l]
    zeros = jnp.zeros((D,), f32)
    vecs = jnp.stack([
        g_pre_mix[l], g_post_mix[l], g_pre_ffn[l], g_post_ffn[l], ln_v_g[l], ln_v_b[l],
        pool_scale[l], zeros,
        jnp.repeat(ws[:, 0, 0], HEAD_DIM), jnp.repeat(bs[:, 0], HEAD_DIM),
        zeros, zeros, zeros, zeros, zeros, zeros])
    bias_full = jnp.repeat(bs.T, HEAD_DIM, axis=1)
    w_in_b = _cast_call(w_in[l], 512)
    w_out_b = _cast_call(w_out[l], D)
    pad = LANES - N_EXPERTS - N_GROUPS
    w_r = jnp.concatenate([w_router_exp[l], w_router_grp[l], jnp.zeros((D, pad), f32)], axis=1)
    w_r_hi = w_r.astype(bf16)
    w_r_lo = (w_r - w_r_hi.astype(f32)).astype(bf16)
    w_r2 = jnp.concatenate([w_r_hi, w_r_lo], axis=1)
    b_r = jnp.concatenate([b_router_exp[l], b_router_grp[l], jnp.zeros((pad,), f32)])[None]

    x1_p, h2_p, r3_p, cnt_p, plast, wg, wu, wd = _stage1_prompt_call(
        x_prompt, mod_p, vecs, w_in_b, ws, bias_full, w_pool[l], w_out_b, w_r2, b_r,
        w_exp_gate[l], w_exp_up[l], w_exp_down[l])
    state_t = jnp.transpose(state_pool[l], (1, 0, 2))
    x1_s, h2_s, r3_s, cnt_s, p_s, v_s = _stage1_sample_call(
        x_sample.reshape(n_s, D), mod_s, vecs, w_in_b, state_t, w_pool[l], w_out_b, w_r2, b_r)

    lane_e = (jnp.arange(LANES) % N_EXPERTS)[None, :, None]
    lane_ok = (jnp.arange(LANES) < 3 * N_EXPERTS)[None, :, None]
    col_e = (jnp.arange(EPG * D_EXPERT) // D_EXPERT)[None, None, :]
    ex = (lane_e == jnp.arange(N_GROUPS)[:, None, None] * EPG + col_e) & lane_ok
    ex = ex.astype(bf16).reshape(N_GROUPS * LANES, EPG * D_EXPERT)

    g_post = g_post_ffn[l].reshape(1, D)
    y_p = _moe_call(
        _count_table(cnt_p), x1_p.reshape(b * s, D), h2_p.reshape(b * s, D),
        r3_p.reshape(b * s, LANES), mod_p[5], g_post, wg, wu, wd, ex, TL, MOE_WINDOW // TL, s)
    y_s = _moe_call(
        _count_table(cnt_s[None]), x1_s, h2_s, r3_s, mod_s[5], g_post, wg, wu, wd, ex, n_s, 1, 1)

    state_pool_prompt = plast[:, 1:][None]
    state_pool_sample = jnp.concatenate([state_pool[l][:, 1:], p_s[:, None, :]], axis=1)[None]
    chunk_v_sample = v_s.reshape(1, n_s, 1, D)
    return (y_p.reshape(b, s, D), y_s.reshape(n_s, 1, D), state_pool_prompt,
            state_pool_sample, chunk_v_sample)
```

```python
import functools

import jax
import jax.numpy as jnp
from jax import lax
from jax.experimental import pallas as pl
from jax.experimental.pallas import tpu as pltpu

D = 1024
CHUNK = 128
HEADS = 8
HEAD_DIM = 128
WINDOWS = (2, 4, 8, 16)
PW = 512
PG = 128
W_MAX = 16
N_GROUPS = 4
EPG = 8
N_EXPERTS = 32
D_EXPERT = 128
EPS = 1e-6
LANES = 128
GROUP_LANE0 = 32
GIDX_LANE = 96

TL = 256
MOE_WINDOW = 2048
ROW_ALIGN = 16
RB = 128
BLOCKS_PER_TRIP = 4
VMEM_LIMIT = 56 * 1024 * 1024

bf16 = jnp.bfloat16
f32 = jnp.float32


def _rms(x, g):
    ms = jnp.mean(x * x, axis=-1, keepdims=True)
    return x * lax.rsqrt(ms + EPS) * g


def _dot(a, b):
    return jnp.dot(a, b, preferred_element_type=f32)


_GELU_C = 2.0 * 0.7978845608028654


def _gelu(x):
    t = x * ((-_GELU_C) + (-_GELU_C * 0.044715) * (x * x))
    return x / (1.0 + jnp.exp(t))


def _cast_kernel(w_ref, o_ref):
    o_ref[...] = w_ref[...].astype(bf16)


def _cast_call(w, block_cols):
    rows, cols = w.shape
    spec = pl.BlockSpec((rows, block_cols), lambda j: (0, j))
    return pl.pallas_call(
        _cast_kernel, grid=(cols // block_cols,), in_specs=[spec], out_specs=spec,
        out_shape=jax.ShapeDtypeStruct((rows, cols), bf16),
        compiler_params=pltpu.CompilerParams(dimension_semantics=("arbitrary",)),
    )(w)


def _expert_cast_specs(step_to_expert):
    e = step_to_expert
    col_block = pl.BlockSpec((D, D_EXPERT), lambda t: (e(t) // EPG, e(t) % EPG))
    in_specs = [
        pl.BlockSpec((1, D, D_EXPERT), lambda t: (e(t), 0, 0)),
        pl.BlockSpec((1, D, D_EXPERT), lambda t: (e(t), 0, 0)),
        pl.BlockSpec((1, D_EXPERT, D), lambda t: (e(t), 0, 0)),
    ]
    out_specs = [col_block, col_block, pl.BlockSpec((D_EXPERT, D), lambda t: (e(t), 0))]
    out_shapes = [
        jax.ShapeDtypeStruct((N_GROUPS * D, EPG * D_EXPERT), bf16),
        jax.ShapeDtypeStruct((N_GROUPS * D, EPG * D_EXPERT), bf16),
        jax.ShapeDtypeStruct((N_EXPERTS * D_EXPERT, D), bf16),
    ]
    return in_specs, out_specs, out_shapes


def _mod_kernel(c_ref, w_ref, b_ref, op_ref, os_ref):
    c = c_ref[...]
    a = (c * jax.nn.sigmoid(c)).astype(bf16)
    m = _dot(a, w_ref[...].astype(bf16)) + b_ref[0]
    nb = op_ref.shape[1]
    op_ref[0] = m[:nb]
    os_ref[0] = m[nb:]


def _mod_call(c_all, w_ada, b_ada, nb):
    n = c_all.shape[0]
    return pl.pallas_call(
        _mod_kernel,
        grid=(6,),
        in_specs=[
            pl.BlockSpec((n, D), lambda j: (0, 0)),
            pl.BlockSpec((D, D), lambda j: (0, j)),
            pl.BlockSpec((1, 1, D), lambda j: (j, 0, 0)),
        ],
        out_specs=[pl.BlockSpec((1, nb, D), lambda j: (j, 0, 0)),
                   pl.BlockSpec((1, n - nb, D), lambda j: (j, 0, 0))],
        out_shape=[jax.ShapeDtypeStruct((6, nb, D), f32),
                   jax.ShapeDtypeStruct((6, n - nb, D), f32)],
        compiler_params=pltpu.CompilerParams(
            dimension_semantics=("arbitrary",), vmem_limit_bytes=VMEM_LIMIT),
    )(c_all, w_ada, b_ada.reshape(6, 1, D))


def _route(logits):
    t = logits.shape[0]
    lane = lax.broadcasted_iota(jnp.int32, (t, LANES), 1)
    lane_f = lane.astype(f32)
    neg = -jnp.inf
    big = 1e9
    gmask = (lane >= GROUP_LANE0) & (lane < GROUP_LANE0 + N_GROUPS)
    gl = jnp.where(gmask, logits, neg)
    gmax = jnp.max(gl, axis=-1, keepdims=True)
    g_idx = jnp.min(jnp.where(gl == gmax, lane_f - GROUP_LANE0, big), axis=-1, keepdims=True)
    sumexp = jnp.sum(jnp.where(gmask, jnp.exp(gl - gmax), 0.0), axis=-1, keepdims=True)
    p_g = 1.0 / sumexp
    lane_grp = (lane >> 3).astype(f32)
    emask = (lane < N_EXPERTS) & (lane_grp == g_idx)
    el = jnp.where(emask, logits, neg)
    m1 = jnp.max(el, axis=-1, keepdims=True)
    i1 = jnp.min(jnp.where(el == m1, lane_f, big), axis=-1, keepdims=True)
    el2 = jnp.where(lane_f == i1, neg, el)
    m2 = jnp.max(el2, axis=-1, keepdims=True)
    i2 = jnp.min(jnp.where(el2 == m2, lane_f, big), axis=-1, keepdims=True)
    e = jnp.exp(m2 - m1)
    w1 = p_g / (1.0 + e)
    w2 = w1 * e

    def split3(w):
        hi = w.astype(bf16).astype(f32)
        mid = (w - hi).astype(bf16).astype(f32)
        lo = w - hi - mid
        return hi, mid, lo

    r3 = jnp.where(lane == GIDX_LANE, g_idx, 0.0)
    for idx, w in ((i1, w1), (i2, w2)):
        for part, wp in enumerate(split3(w)):
            r3 = r3 + jnp.where(lane_f == idx + float(part * N_EXPERTS), wp, 0.0)
    counts = jnp.sum(jnp.where(lane_f == g_idx, 1.0, 0.0), axis=0, keepdims=True)
    return r3.astype(bf16), jnp.broadcast_to(counts, (8, LANES))


def _merge_project(u, ga, gb, mix, y_b, w_out_ref):
    y_a = u * mix
    merged = jax.nn.sigmoid(ga) * y_a + jax.nn.sigmoid(gb) * y_b
    return _dot(merged.astype(bf16), w_out_ref[...])


def _residual_route(x, y, mods, vec_ref, w_r_ref, b_r_ref):
    sh1, sc1, gt1, sh2, sc2, gt2 = mods
    x1 = x + gt1 * _rms(y, vec_ref[1:2])
    h2 = _rms(x1, vec_ref[2:3] * (1.0 + sc2)) + sh2
    h2_hi = h2.astype(bf16)
    h2_lo = (h2 - h2_hi.astype(f32)).astype(bf16)
    r = _dot(h2_hi, w_r_ref[...]) + _dot(h2_lo, w_r_ref[...])
    logits = r[:, :LANES] + r[:, LANES:] + b_r_ref[...]
    r3, counts = _route(logits)
    return x1, h2_hi, r3, counts


def _in_proj(x, mods, vec_ref, w_in_ref, after_first_dot=lambda: None):
    sh1, sc1 = mods[0], mods[1]
    h = _rms(x, vec_ref[0:1] * (1.0 + sc1)) + sh1
    hb = h.astype(bf16)
    zu = _dot(hb, w_in_ref[:, 0:D])
    after_first_dot()
    zv = _dot(hb, w_in_ref[:, D:2 * D])
    p = _dot(hb, w_in_ref[:, 2 * D:2 * D + PW])
    ga = _dot(hb, w_in_ref[:, 2 * D + PW:3 * D + PW])
    gb = _dot(hb, w_in_ref[:, 3 * D + PW:4 * D + PW])
    return zu, zv, p, ga, gb


def _activate(zu, zv, vec_ref):
    u = _gelu(zu)
    v = _gelu(zv)
    mu = jnp.mean(v, axis=-1, keepdims=True)
    vc = v - mu
    var = jnp.mean(vc * vc, axis=-1, keepdims=True)
    v = vc * lax.rsqrt(var + EPS) * vec_ref[4:5] + vec_ref[5:6]
    return u, v


def _pool_out(d_groups, vec_ref, w_pool_ref):
    parts = [_dot(d.astype(bf16), w_pool_ref[gi].astype(bf16)) for gi, d in enumerate(d_groups)]
    return jnp.concatenate(parts, axis=1) * vec_ref[6:7]


def _stage1_prompt_kernel(x_ref, xprev_ref, mod_ref, vec_ref, w_in_ref, w_sp_ref, bias_ref,
                          w_pool_ref, w_out_ref, w_r_ref, b_r_ref, eg_ref, eu_ref, ed_ref,
                          x1_ref, h2_ref, r3_ref, cnt_ref, plast_ref, wg_ref, wu_ref, wd_ref,
                          pbuf, ybuf, *, tiles_per_seq):
    t = pl.program_id(0)
    n_tiles = pl.num_programs(0) - 1

    @pl.when(t == 0)
    def _():
        ybuf[...] = jnp.zeros_like(ybuf)
        pbuf[...] = jnp.zeros_like(pbuf)

    t_cur = jnp.minimum(t, n_tiles - 1)
    bi = t_cur // tiles_per_seq
    s = t_cur % tiles_per_seq
    x = x_ref[...]
    mods = [mod_ref[i, pl.ds(bi, 1), :] for i in range(6)]

    def prev_tile_second_half():
        wg_ref[...] = eg_ref[0].astype(bf16)
        wu_ref[...] = eu_ref[0].astype(bf16)
        wd_ref[...] = ed_ref[0].astype(bf16)
        t_prev = jnp.maximum(t - 1, 0)
        b_prev = t_prev // tiles_per_seq
        mods_prev = [mod_ref[i, pl.ds(b_prev, 1), :] for i in range(6)]
        x1, h2b, r3, counts = _residual_route(xprev_ref[...], ybuf[...], mods_prev, vec_ref,
                                              w_r_ref, b_r_ref)
        x1_ref[...] = x1
        h2_ref[...] = h2b
        r3_ref[...] = r3
        cnt_ref[0] = counts

    zu, zv, p, ga, gb = _in_proj(x, mods, vec_ref, w_in_ref, prev_tile_second_half)

    u, v = _activate(zu, zv, vec_ref)

    vb = v.astype(bf16)
    row = lax.broadcasted_iota(jnp.int32, (CHUNK, CHUNK), 0)
    col = lax.broadcasted_iota(jnp.int32, (CHUNK, CHUNK), 1)
    w_tril = [jnp.where(row >= col, w_sp_ref[hd], 0.0).astype(bf16) for hd in range(HEADS)]
    bias = bias_ref[...]
    chunks = []
    for c in range(TL // CHUNK):
        heads = [_dot(w_tril[hd], vb[c * CHUNK:(c + 1) * CHUNK, hd * HEAD_DIM:(hd + 1) * HEAD_DIM])
                 for hd in range(HEADS)]
        chunks.append(jnp.concatenate(heads, axis=1) + bias)
    mix = jnp.concatenate(chunks, axis=0)

    carry = jnp.where(s == 0, 0.0, pbuf[...])
    ext = jnp.concatenate([carry, p], axis=0)
    pos = s * TL + lax.broadcasted_iota(jnp.int32, (TL, PG), 0)
    d_groups = []
    for gi, w in enumerate(WINDOWS):
        acc = ext[:, gi * PG:(gi + 1) * PG]
        k = 1
        while k < w:
            acc = acc + pltpu.roll(acc, k, 0)
            k *= 2
        cnt = jnp.minimum(pos + 1, w).astype(f32)
        d_groups.append(acc[W_MAX:] / cnt - p[:, gi * PG:(gi + 1) * PG])
    pbuf[...] = p[TL - W_MAX:]
    plast_ref[0] = p[TL - W_MAX:]
    y_b = _pool_out(d_groups, vec_ref, w_pool_ref)
    ybuf[...] = _merge_project(u, ga, gb, mix, y_b, w_out_ref)


def _stage1_sample_kernel(x_ref, mod_ref, vec_ref, w_in_ref, state_ref, w_pool_ref,
                          w_out_ref, w_r_ref, b_r_ref,
                          x1_ref, h2_ref, r3_ref, cnt_ref, p_ref, v_ref):
    x = x_ref[...]
    mods = [mod_ref[i] for i in range(6)]
    zu, zv, p, ga, gb = _in_proj(x, mods, vec_ref, w_in_ref)
    u, v = _activate(zu, zv, vec_ref)
    v_ref[...] = v
    p_ref[...] = p
    mix = v * vec_ref[8:9] + vec_ref[9:10]
    d_groups = []
    for gi, w in enumerate(WINDOWS):
        sl = slice(gi * PG, (gi + 1) * PG)
        acc = p[:, sl]
        for r in range(W_MAX - w, W_MAX - 1):
            acc = acc + state_ref[r][:, sl]
        d_groups.append(acc / float(w) - p[:, sl])
    y_b = _pool_out(d_groups, vec_ref, w_pool_ref)
    y = _merge_project(u, ga, gb, mix, y_b, w_out_ref)
    x1, h2b, r3, counts = _residual_route(x, y, mods, vec_ref, w_r_ref, b_r_ref)
    x1_ref[...] = x1
    h2_ref[...] = h2b
    r3_ref[...] = r3
    cnt_ref[...] = counts


def _const_spec(shape):
    nd = len(shape)
    return pl.BlockSpec(shape, lambda *_: (0,) * nd, pipeline_mode=pl.Buffered(1))


def _stage1_prompt_call(x, mod_p, vecs, w_in_b, w_sp, bias_full, w_pool_b, w_out_b, w_r, b_r,
                        w_gate, w_up, w_down):
    b, s, _ = x.shape
    ns = s // TL
    nt = b * ns
    assert nt + 1 >= N_EXPERTS
    x2 = x.reshape(b * s, D)
    cur = lambda t: (jnp.minimum(t, nt - 1), 0)
    prev = lambda t: (jnp.maximum(t - 1, 0), 0)
    e_in, e_out, e_shapes = _expert_cast_specs(lambda t: jnp.minimum(t, N_EXPERTS - 1))
    return pl.pallas_call(
        functools.partial(_stage1_prompt_kernel, tiles_per_seq=ns),
        grid=(nt + 1,),
        in_specs=[
            pl.BlockSpec((TL, D), cur),
            pl.BlockSpec((TL, D), prev),
            _const_spec(mod_p.shape),
            _const_spec(vecs.shape),
            _const_spec(w_in_b.shape),
            _const_spec(w_sp.shape),
            _const_spec(bias_full.shape),
            _const_spec(w_pool_b.shape),
            _const_spec(w_out_b.shape),
            _const_spec(w_r.shape),
            _const_spec(b_r.shape),
        ] + e_in,
        out_specs=[
            pl.BlockSpec((TL, D), prev),
            pl.BlockSpec((TL, D), prev),
            pl.BlockSpec((TL, LANES), prev),
            pl.BlockSpec((1, 8, LANES), lambda t: (jnp.maximum(t - 1, 0), 0, 0)),
            pl.BlockSpec((1, W_MAX, PW), lambda t: (jnp.minimum(t, nt - 1) // ns, 0, 0)),
        ] + e_out,
        out_shape=[
            jax.ShapeDtypeStruct((b * s, D), f32),
            jax.ShapeDtypeStruct((b * s, D), bf16),
            jax.ShapeDtypeStruct((b * s, LANES), bf16),
            jax.ShapeDtypeStruct((nt, 8, LANES), f32),
            jax.ShapeDtypeStruct((b, W_MAX, PW), f32),
        ] + e_shapes,
        scratch_shapes=[pltpu.VMEM((W_MAX, PW), f32), pltpu.VMEM((TL, D), f32)],
        compiler_params=pltpu.CompilerParams(
            dimension_semantics=("arbitrary",), vmem_limit_bytes=VMEM_LIMIT),
    )(x2, x2, mod_p, vecs, w_in_b, w_sp, bias_full, w_pool_b, w_out_b, w_r, b_r,
      w_gate, w_up, w_down)


def _stage1_sample_call(x, mod_s, vecs, w_in_b, state_t, w_pool_b, w_out_b, w_r, b_r):
    n = x.shape[0]
    return pl.pallas_call(
        _stage1_sample_kernel,
        out_shape=[
            jax.ShapeDtypeStruct((n, D), f32),
            jax.ShapeDtypeStruct((n, D), bf16),
            jax.ShapeDtypeStruct((n, LANES), bf16),
            jax.ShapeDtypeStruct((8, LANES), f32),
            jax.ShapeDtypeStruct((n, PW), f32),
            jax.ShapeDtypeStruct((n, D), f32),
        ],
        compiler_params=pltpu.CompilerParams(vmem_limit_bytes=VMEM_LIMIT),
    )(x, mod_s, vecs, w_in_b, state_t, w_pool_b, w_out_b, w_r, b_r)


def _moe_buffer_rows(ts, nt):
    worst = ts * nt + nt * N_GROUPS * (ROW_ALIGN - 1) + N_GROUPS * (RB - 1)
    return -(-worst // RB) * RB


def _moe_kernel(cnt_ref, h2_ref, r3_ref, x1_ref, gt2_ref, g_ref, wg_ref, wu_ref, wd_ref, ex_ref,
                o_ref,
                hsbuf, hs_tile, ys_tile, pt_buf, pm_buf, tab,
                *, ts, nt, tpr):
    w = pl.program_id(0)
    i = pl.program_id(1)
    rt = ts + LANES

    def copy_rows(src, src0, dst, dst0, nrows, ncols):
        def body(j, _):
            s0 = pl.multiple_of(src0 + j * ROW_ALIGN, ROW_ALIGN)
            d0 = pl.multiple_of(dst0 + j * ROW_ALIGN, ROW_ALIGN)
            dst[pl.ds(d0, ROW_ALIGN), :] = src[pl.ds(s0, ROW_ALIGN), :ncols]
            return 0
        lax.fori_loop(0, nrows // ROW_ALIGN, body, 0)

    @pl.when((w == 0) & (i == 0))
    def _():
        hsbuf[...] = jnp.zeros_like(hsbuf)
        ys_tile[...] = jnp.zeros_like(ys_tile)

    @pl.when(i == 0)
    def _sort_and_run_experts():
        def run_len(tile, g):
            c = cnt_ref[(w * nt + tile) * N_GROUPS + g]
            return ((c + (ROW_ALIGN - 1)) // ROW_ALIGN) * ROW_ALIGN

        lens = [[run_len(t, g) for g in range(N_GROUPS)] for t in range(nt)]
        region = [sum(lens[t][g] for t in range(nt)) for g in range(N_GROUPS)]
        region = [((r + (RB - 1)) // RB) * RB for r in region]
        base = [sum(region[:g]) for g in range(N_GROUPS)]
        offs = list(base)
        woff = []
        for t in range(nt):
            woff.append(list(offs))
            for g in range(N_GROUPS):
                tab[t * 2 * N_GROUPS + g] = lens[t][g]
                tab[t * 2 * N_GROUPS + N_GROUPS + g] = offs[g]
                offs[g] = offs[g] + lens[t][g]

        lane = lax.broadcasted_iota(jnp.int32, (ts, LANES), 1)
        lane_f = lane.astype(f32)
        r_i = lax.broadcasted_iota(jnp.int32, (ts, ts), 0)
        c_i = lax.broadcasted_iota(jnp.int32, (ts, ts), 1)
        ltri = jnp.where(r_i > c_i, 1.0, 0.0).astype(bf16)
        lane8 = lax.broadcasted_iota(jnp.int32, (8, LANES), 1)
        sel = jnp.where(lane8 == 0, float(ROW_ALIGN), jnp.where(lane8 == 1, 1.0, 0.0)).astype(bf16)
        rt_lane = lax.broadcasted_iota(jnp.int32, (ts, rt), 1).astype(f32)
        rt_sub = lax.broadcasted_iota(jnp.int32, (rt, ts), 0).astype(f32)

        tiles = range(nt)
        r3s = [r3_ref[t * ts:(t + 1) * ts, :] for t in tiles]
        gids = [jnp.sum(jnp.where(lane == GIDX_LANE, r3s[t].astype(f32), 0.0), axis=-1,
                        keepdims=True) for t in tiles]
        onehots = [jnp.where(lane_f == gids[t], 1.0, 0.0) for t in tiles]
        ranks = [_dot(ltri, onehots[t].astype(bf16)) for t in tiles]
        poss = []
        for t in tiles:
            seg = jnp.zeros((1, LANES), f32)
            start = 0
            for g in range(N_GROUPS):
                seg = seg + jnp.where(lane[0:1] == g, jnp.asarray(start, jnp.int32).astype(f32), 0.0)
                start = start + lens[t][g]
            poss.append(jnp.sum(onehots[t] * (ranks[t] + seg), axis=-1, keepdims=True))
        for t in tiles:
            pt_buf[t] = jnp.where(rt_lane == poss[t], 1.0, 0.0).astype(bf16)
        pos_rows = []
        for t in tiles:
            q = jnp.floor(poss[t] * (1.0 / ROW_ALIGN))
            digits = jnp.where(lane == 0, q, jnp.where(lane == 1, poss[t] - q * ROW_ALIGN, 0.0))
            pos_rows.append(lax.dot_general(sel, digits.astype(bf16), (((1,), (1,)), ((), ())),
                                            preferred_element_type=f32))
        for t in tiles:
            pm_buf[t] = jnp.where(rt_sub == pos_rows[t][0:1], 1.0, 0.0).astype(bf16)

        for t in tiles:
            p_mat = pm_buf[t]
            hs_tile[:, :D] = _dot(p_mat, h2_ref[t * ts:(t + 1) * ts, :]).astype(bf16)
            hs_tile[:, D:] = _dot(p_mat, r3_ref[t * ts:(t + 1) * ts, :]).astype(bf16)
            start = 0
            for g in range(N_GROUPS):
                copy_rows(hs_tile, start, hsbuf, woff[t][g], lens[t][g], D + LANES)
                start = start + lens[t][g]

        def expert_rows(g, r0, nrows):
            rows = hsbuf[pl.ds(r0, nrows), :D]
            gate = _dot(rows, wg_ref[g * D:(g + 1) * D, :])
            up = _dot(rows, wu_ref[g * D:(g + 1) * D, :])
            cw = _dot(hsbuf[pl.ds(r0, nrows), D:], ex_ref[g * LANES:(g + 1) * LANES, :])
            act = gate * jax.nn.sigmoid(gate) * up * cw
            hsbuf[pl.ds(r0, nrows), :D] = _dot(
                act.astype(bf16), wd_ref[g * D:(g + 1) * D, :]).astype(bf16)

        for g in range(N_GROUPS):
            n_blocks = region[g] // RB

            def wide(b, _, g=g):
                expert_rows(g, pl.multiple_of(base[g] + b * BLOCKS_PER_TRIP * RB, RB),
                            BLOCKS_PER_TRIP * RB)
                return 0
            lax.fori_loop(0, n_blocks // BLOCKS_PER_TRIP, wide, 0)
            done = (n_blocks // BLOCKS_PER_TRIP) * BLOCKS_PER_TRIP
            width = BLOCKS_PER_TRIP // 2
            while width >= 1:
                @pl.when(((n_blocks - done) & width) != 0)
                def _(g=g, done=done, width=width):
                    expert_rows(g, pl.multiple_of(base[g] + done * RB, RB), width * RB)
                done = done + ((n_blocks - done) & width)
                width //= 2

    start = 0
    for g in range(N_GROUPS):
        ln = tab[i * 2 * N_GROUPS + g]
        copy_rows(hsbuf, tab[i * 2 * N_GROUPS + N_GROUPS + g], ys_tile, start, ln, D)
        start = start + ln
    f = _dot(pt_buf[i], ys_tile[...])
    tok0 = (w * nt + i) * ts
    gt2 = gt2_ref[pl.ds(tok0, ts), :] if tpr == 1 else gt2_ref[pl.ds(tok0 // tpr, 1), :]
    o_ref[...] = x1_ref[...] + gt2 * _rms(f, g_ref[...])


def _moe_call(cnt, x1, h2, r3, gt2, g_post, wg, wu, wd, ex, ts, nt, tpr):
    n = x1.shape[0]
    win = ts * nt
    rbuf = _moe_buffer_rows(ts, nt)
    rt = ts + LANES
    grid_spec = pltpu.PrefetchScalarGridSpec(
        num_scalar_prefetch=1,
        grid=(n // win, nt),
        in_specs=[
            pl.BlockSpec((win, D), lambda w, i, c: (w, 0)),
            pl.BlockSpec((win, LANES), lambda w, i, c: (w, 0)),
            pl.BlockSpec((ts, D), lambda w, i, c: (w * nt + i, 0)),
            _const_spec(gt2.shape),
            _const_spec(g_post.shape),
            _const_spec(wg.shape),
            _const_spec(wu.shape),
            _const_spec(wd.shape),
            _const_spec(ex.shape),
        ],
        out_specs=pl.BlockSpec((ts, D), lambda w, i, c: (w * nt + i, 0)),
        scratch_shapes=[
            pltpu.VMEM((rbuf, D + LANES), bf16),
            pltpu.VMEM((rt, D + LANES), bf16),
            pltpu.VMEM((rt, D), bf16),
            pltpu.VMEM((nt, ts, rt), bf16),
            pltpu.VMEM((nt, rt, ts), bf16),
            pltpu.SMEM((nt * 2 * N_GROUPS,), jnp.int32),
        ],
    )
    return pl.pallas_call(
        functools.partial(_moe_kernel, ts=ts, nt=nt, tpr=tpr),
        grid_spec=grid_spec,
        out_shape=jax.ShapeDtypeStruct((n, D), f32),
        compiler_params=pltpu.CompilerParams(
            dimension_semantics=("arbitrary", "arbitrary"), vmem_limit_bytes=VMEM_LIMIT),
    )(cnt, h2, r3, x1, gt2, g_post, wg, wu, wd, ex)


def _count_table(cnt):
    return cnt[:, 0, :N_GROUPS].astype(jnp.int32).reshape(-1)


def kernel(x_prompt, x_sample, c_prompt, c_sample, state_pool, w_ada, b_ada, g_pre_mix, g_post_mix, g_pre_ffn, g_post_ffn, w_in, ln_v_g, ln_v_b, w_spatial, b_spatial, w_pool, pool_scale, w_out, w_router_grp, b_router_grp, w_router_exp, b_router_exp, w_exp_gate, w_exp_up, w_exp_down):
    depth = w_in.shape[0]
    assert depth == 1
    b, s, _ = x_prompt.shape
    n_s = x_sample.shape[0]
    l = 0

    c_all = jnp.concatenate([c_prompt, c_sample], axis=0)
    mod_p, mod_s = _mod_call(c_all, w_ada[l], b_ada[l], b)

    ws, bs = w_spatial[l], b_spatial[l]
    zeros = jnp.zeros((D,), f32)
    vecs = jnp.stack([
        g_pre_mix[l], g_post_mix[l], g_pre_ffn[l], g_post_ffn[l], ln_v_g[l], ln_v_b[l],
        pool_scale[l], zeros,
        jnp.repeat(ws[:, 0, 0], HEAD_DIM), jnp.repeat(bs[:, 0], HEAD_DIM),
        zeros, zeros, zeros, zeros, zeros, zeros])
    bias_full = jnp.repeat(bs.T, HEAD_DIM, axis=1)
    w_in_b = _cast_call(w_in[l], 512)
    w_out_b = _cast_call(w_out[l], D)
    pad = LANES - N_EXPERTS - N_GROUPS
    w_r = jnp.concatenate([w_router_exp[l], w_router_grp[l], jnp.zeros((D, pad), f32)], axis=1)
    w_r_hi = w_r.astype(bf16)
    w_r_lo = (w_r - w_r_hi.astype(f32)).astype(bf16)
    w_r2 = jnp.concatenate([w_r_hi, w_r_lo], axis=1)
    b_r = jnp.concatenate([b_router_exp[l], b_router_grp[l], jnp.zeros((pad,), f32)])[None]

    x1_p, h2_p, r3_p, cnt_p, plast, wg, wu, wd = _stage1_prompt_call(
        x_prompt, mod_p, vecs, w_in_b, ws, bias_full, w_pool[l], w_out_b, w_r2, b_r,
        w_exp_gate[l], w_exp_up[l], w_exp_down[l])
    state_t = jnp.transpose(state_pool[l], (1, 0, 2))
    x1_s, h2_s, r3_s, cnt_s, p_s, v_s = _stage1_sample_call(
        x_sample.reshape(n_s, D), mod_s, vecs, w_in_b, state_t, w_pool[l], w_out_b, w_r2, b_r)

    lane_e = (jnp.arange(LANES) % N_EXPERTS)[None, :, None]
    lane_ok = (jnp.arange(LANES) < 3 * N_EXPERTS)[None, :, None]
    col_e = (jnp.arange(EPG * D_EXPERT) // D_EXPERT)[None, None, :]
    ex = (lane_e == jnp.arange(N_GROUPS)[:, None, None] * EPG + col_e) & lane_ok
    ex = ex.astype(bf16).reshape(N_GROUPS * LANES, EPG * D_EXPERT)

    g_post = g_post_ffn[l].reshape(1, D)
    y_p = _moe_call(
        _count_table(cnt_p), x1_p.reshape(b * s, D), h2_p.reshape(b * s, D),
        r3_p.reshape(b * s, LANES), mod_p[5], g_post, wg, wu, wd, ex, TL, MOE_WINDOW // TL, s)
    y_s = _moe_call(
        _count_table(cnt_s[None]), x1_s, h2_s, r3_s, mod_s[5], g_post, wg, wu, wd, ex, n_s, 1, 1)

    state_pool_prompt = plast[:, 1:][None]
    state_pool_sample = jnp.concatenate([state_pool[l][:, 1:], p_s[:, None, :]], axis=1)[None]
    chunk_v_sample = v_s.reshape(1, n_s, 1, D)
    return (y_p.reshape(b, s, D), y_s.reshape(n_s, 1, D), state_pool_prompt,
            state_pool_sample, chunk_v_sample)
```

```python
import functools

import jax
import jax.numpy as jnp
from jax import lax
from jax.experimental import pallas as pl
from jax.experimental.pallas import tpu as pltpu

D = 1024
CHUNK = 128
HEADS = 8
HEAD_DIM = 128
WINDOWS = (2, 4, 8, 16)
PW = 512
PG = 128
W_MAX = 16
N_GROUPS = 4
EPG = 8
N_EXPERTS = 32
D_EXPERT = 128
EPS = 1e-6
N_MOD = 6
LANES = 128
GROUP_LANE0 = 32
GIDX_LANE = 96

TL = 256
MOE_WINDOW = 2048
ROW_ALIGN = 16
RB = 128
VMEM_LIMIT = 56 * 1024 * 1024

bf16 = jnp.bfloat16
f32 = jnp.float32


def _rms(x, g):
    ms = jnp.mean(x * x, axis=-1, keepdims=True)
    return x * lax.rsqrt(ms + EPS) * g


def _dot(a, b):
    return jnp.dot(a, b, preferred_element_type=f32)


_GELU_C = 2.0 * 0.7978845608028654


def _gelu(x):
    t = x * ((-_GELU_C) + (-_GELU_C * 0.044715) * (x * x))
    return x / (1.0 + jnp.exp(t))


def _expert_cast_specs(step_to_expert):
    e = step_to_expert
    col_block = pl.BlockSpec((D, D_EXPERT), lambda t: (e(t) // EPG, e(t) % EPG))
    in_specs = [
        pl.BlockSpec((1, D, D_EXPERT), lambda t: (e(t), 0, 0)),
        pl.BlockSpec((1, D, D_EXPERT), lambda t: (e(t), 0, 0)),
        pl.BlockSpec((1, D_EXPERT, D), lambda t: (e(t), 0, 0)),
    ]
    out_specs = [col_block, col_block, pl.BlockSpec((D_EXPERT, D), lambda t: (e(t), 0))]
    out_shapes = [
        jax.ShapeDtypeStruct((N_GROUPS * D, EPG * D_EXPERT), bf16),
        jax.ShapeDtypeStruct((N_GROUPS * D, EPG * D_EXPERT), bf16),
        jax.ShapeDtypeStruct((N_EXPERTS * D_EXPERT, D), bf16),
    ]
    return in_specs, out_specs, out_shapes


def _mod_kernel(c_ref, w_ref, b_ref, w_in_ref, w_out_ref, op_ref, os_ref, w_in_o, w_out_o):
    c = c_ref[...]
    a = (c * jax.nn.sigmoid(c)).astype(bf16)
    m = _dot(a, w_ref[...].astype(bf16)) + b_ref[0]
    nb = op_ref.shape[1]
    op_ref[0] = m[:nb]
    os_ref[0] = m[nb:]
    w_in_o[...] = w_in_ref[...].astype(bf16)
    w_out_o[...] = w_out_ref[...].astype(bf16)


def _mod_call(c_all, w_ada, b_ada, nb, w_in, w_out):
    n = c_all.shape[0]
    in_cols = w_in.shape[1] // N_MOD
    assert w_in.shape[1] == N_MOD * in_cols and in_cols % LANES == 0
    return pl.pallas_call(
        _mod_kernel,
        grid=(N_MOD,),
        in_specs=[
            pl.BlockSpec((n, D), lambda j: (0, 0)),
            pl.BlockSpec((D, D), lambda j: (0, j)),
            pl.BlockSpec((1, 1, D), lambda j: (j, 0, 0)),
            pl.BlockSpec((D, in_cols), lambda j: (0, j)),
            pl.BlockSpec((D, D), lambda j: (0, 0)),
        ],
        out_specs=[pl.BlockSpec((1, nb, D), lambda j: (j, 0, 0)),
                   pl.BlockSpec((1, n - nb, D), lambda j: (j, 0, 0)),
                   pl.BlockSpec((D, in_cols), lambda j: (0, j)),
                   pl.BlockSpec((D, D), lambda j: (0, 0))],
        out_shape=[jax.ShapeDtypeStruct((N_MOD, nb, D), f32),
                   jax.ShapeDtypeStruct((N_MOD, n - nb, D), f32),
                   jax.ShapeDtypeStruct(w_in.shape, bf16),
                   jax.ShapeDtypeStruct(w_out.shape, bf16)],
        compiler_params=pltpu.CompilerParams(
            dimension_semantics=("arbitrary",), vmem_limit_bytes=VMEM_LIMIT),
    )(c_all, w_ada, b_ada.reshape(N_MOD, 1, D), w_in, w_out)


def _route(logits):
    t = logits.shape[0]
    lane = lax.broadcasted_iota(jnp.int32, (t, LANES), 1)
    lane_f = lane.astype(f32)
    neg = -jnp.inf
    big = 1e9
    gmask = (lane >= GROUP_LANE0) & (lane < GROUP_LANE0 + N_GROUPS)
    gl = jnp.where(gmask, logits, neg)
    gmax = jnp.max(gl, axis=-1, keepdims=True)
    g_idx = jnp.min(jnp.where(gl == gmax, lane_f - GROUP_LANE0, big), axis=-1, keepdims=True)
    sumexp = jnp.sum(jnp.where(gmask, jnp.exp(gl - gmax), 0.0), axis=-1, keepdims=True)
    p_g = 1.0 / sumexp
    lane_grp = (lane >> 3).astype(f32)
    emask = (lane < N_EXPERTS) & (lane_grp == g_idx)
    el = jnp.where(emask, logits, neg)
    m1 = jnp.max(el, axis=-1, keepdims=True)
    i1 = jnp.min(jnp.where(el == m1, lane_f, big), axis=-1, keepdims=True)
    el2 = jnp.where(lane_f == i1, neg, el)
    m2 = jnp.max(el2, axis=-1, keepdims=True)
    i2 = jnp.min(jnp.where(el2 == m2, lane_f, big), axis=-1, keepdims=True)
    e = jnp.exp(m2 - m1)
    w1 = p_g / (1.0 + e)
    w2 = w1 * e

    def split3(w):
        hi = w.astype(bf16).astype(f32)
        mid = (w - hi).astype(bf16).astype(f32)
        lo = w - hi - mid
        return hi, mid, lo

    r3 = jnp.where(lane == GIDX_LANE, g_idx, 0.0)
    for idx, w in ((i1, w1), (i2, w2)):
        for part, wp in enumerate(split3(w)):
            r3 = r3 + jnp.where(lane_f == idx + float(part * N_EXPERTS), wp, 0.0)
    counts = jnp.sum(jnp.where(lane_f == g_idx, 1.0, 0.0), axis=0, keepdims=True)
    return r3.astype(bf16), jnp.broadcast_to(counts, (8, LANES))


def _merge_project(u, ga, gb, mix, y_b, w_out_ref):
    y_a = u * mix
    merged = jax.nn.sigmoid(ga) * y_a + jax.nn.sigmoid(gb) * y_b
    return _dot(merged.astype(bf16), w_out_ref[...])


def _residual_route(x, y, mods, vec_ref, w_r_ref, b_r_ref):
    sh1, sc1, gt1, sh2, sc2, gt2 = mods
    x1 = x + gt1 * _rms(y, vec_ref[1:2])
    h2 = _rms(x1, vec_ref[2:3] * (1.0 + sc2)) + sh2
    h2_hi = h2.astype(bf16)
    h2_lo = (h2 - h2_hi.astype(f32)).astype(bf16)
    r = _dot(h2_hi, w_r_ref[...]) + _dot(h2_lo, w_r_ref[...])
    logits = r[:, :LANES] + r[:, LANES:] + b_r_ref[...]
    r3, counts = _route(logits)
    return x1, h2_hi, r3, counts


def _in_proj(x, mods, vec_ref, w_in_ref, after_first_dot=lambda: None):
    sh1, sc1 = mods[0], mods[1]
    h = _rms(x, vec_ref[0:1] * (1.0 + sc1)) + sh1
    hb = h.astype(bf16)
    zu = _dot(hb, w_in_ref[:, 0:D])
    after_first_dot()
    zv = _dot(hb, w_in_ref[:, D:2 * D])
    p = _dot(hb, w_in_ref[:, 2 * D:2 * D + PW])
    ga = _dot(hb, w_in_ref[:, 2 * D + PW:3 * D + PW])
    gb = _dot(hb, w_in_ref[:, 3 * D + PW:4 * D + PW])
    return zu, zv, p, ga, gb


def _activate(zu, zv, vec_ref):
    u = _gelu(zu)
    v = _gelu(zv)
    mu = jnp.mean(v, axis=-1, keepdims=True)
    vc = v - mu
    var = jnp.mean(vc * vc, axis=-1, keepdims=True)
    v = vc * lax.rsqrt(var + EPS) * vec_ref[4:5] + vec_ref[5:6]
    return u, v


def _pool_out(d_groups, vec_ref, w_pool_ref):
    parts = [_dot(d.astype(bf16), w_pool_ref[gi].astype(bf16)) for gi, d in enumerate(d_groups)]
    return jnp.concatenate(parts, axis=1) * vec_ref[6:7]


def _stage1_prompt_kernel(x_ref, xprev_ref, mod_ref, vec_ref, w_in_ref, w_sp_ref, bias_ref,
                          w_pool_ref, w_out_ref, w_r_ref, b_r_ref, eg_ref, eu_ref, ed_ref,
                          x1_ref, h2_ref, r3_ref, cnt_ref, plast_ref, wg_ref, wu_ref, wd_ref,
                          pbuf, ybuf, *, tiles_per_seq):
    t = pl.program_id(0)
    n_tiles = pl.num_programs(0) - 1

    @pl.when(t == 0)
    def _():
        ybuf[...] = jnp.zeros_like(ybuf)
        pbuf[...] = jnp.zeros_like(pbuf)

    t_cur = jnp.minimum(t, n_tiles - 1)
    bi = t_cur // tiles_per_seq
    s = t_cur % tiles_per_seq
    x = x_ref[...]
    mods = [mod_ref[i, pl.ds(bi, 1), :] for i in range(N_MOD)]

    def prev_tile_second_half():
        wg_ref[...] = eg_ref[0].astype(bf16)
        wu_ref[...] = eu_ref[0].astype(bf16)
        wd_ref[...] = ed_ref[0].astype(bf16)
        t_prev = jnp.maximum(t - 1, 0)
        b_prev = t_prev // tiles_per_seq
        mods_prev = [mod_ref[i, pl.ds(b_prev, 1), :] for i in range(N_MOD)]
        x1, h2b, r3, counts = _residual_route(xprev_ref[...], ybuf[...], mods_prev, vec_ref,
                                              w_r_ref, b_r_ref)
        x1_ref[...] = x1
        h2_ref[...] = h2b
        r3_ref[...] = r3
        cnt_ref[0] = counts

    zu, zv, p, ga, gb = _in_proj(x, mods, vec_ref, w_in_ref, prev_tile_second_half)

    u, v = _activate(zu, zv, vec_ref)

    vb = v.astype(bf16)
    row = lax.broadcasted_iota(jnp.int32, (CHUNK, CHUNK), 0)
    col = lax.broadcasted_iota(jnp.int32, (CHUNK, CHUNK), 1)
    w_tril = [jnp.where(row >= col, w_sp_ref[hd], 0.0).astype(bf16) for hd in range(HEADS)]
    bias = bias_ref[...]
    chunks = []
    for c in range(TL // CHUNK):
        heads = [_dot(w_tril[hd], vb[c * CHUNK:(c + 1) * CHUNK, hd * HEAD_DIM:(hd + 1) * HEAD_DIM])
                 for hd in range(HEADS)]
        chunks.append(jnp.concatenate(heads, axis=1) + bias)
    mix = jnp.concatenate(chunks, axis=0)

    carry = jnp.where(s == 0, 0.0, pbuf[...])
    ext = jnp.concatenate([carry, p], axis=0)
    pos = s * TL + lax.broadcasted_iota(jnp.int32, (TL, PG), 0)
    d_groups = []
    for gi, w in enumerate(WINDOWS):
        acc = ext[:, gi * PG:(gi + 1) * PG]
        k = 1
        while k < w:
            acc = acc + pltpu.roll(acc, k, 0)
            k *= 2
        cnt = jnp.minimum(pos + 1, w).astype(f32)
        d_groups.append(acc[W_MAX:] / cnt - p[:, gi * PG:(gi + 1) * PG])
    pbuf[...] = p[TL - W_MAX:]
    plast_ref[0] = p[TL - W_MAX:]
    y_b = _pool_out(d_groups, vec_ref, w_pool_ref)
    ybuf[...] = _merge_project(u, ga, gb, mix, y_b, w_out_ref)


def _stage1_sample_kernel(x_ref, mod_ref, vec_ref, w_in_ref, state_ref, w_pool_ref,
                          w_out_ref, w_r_ref, b_r_ref,
                          x1_ref, h2_ref, r3_ref, cnt_ref, p_ref, v_ref):
    x = x_ref[...]
    mods = [mod_ref[i] for i in range(N_MOD)]
    zu, zv, p, ga, gb = _in_proj(x, mods, vec_ref, w_in_ref)
    u, v = _activate(zu, zv, vec_ref)
    v_ref[...] = v
    p_ref[...] = p
    mix = v * vec_ref[8:9] + vec_ref[9:10]
    d_groups = []
    for gi, w in enumerate(WINDOWS):
        sl = slice(gi * PG, (gi + 1) * PG)
        acc = p[:, sl]
        for r in range(W_MAX - w, W_MAX - 1):
            acc = acc + state_ref[r][:, sl]
        d_groups.append(acc / float(w) - p[:, sl])
    y_b = _pool_out(d_groups, vec_ref, w_pool_ref)
    y = _merge_project(u, ga, gb, mix, y_b, w_out_ref)
    x1, h2b, r3, counts = _residual_route(x, y, mods, vec_ref, w_r_ref, b_r_ref)
    x1_ref[...] = x1
    h2_ref[...] = h2b
    r3_ref[...] = r3
    cnt_ref[...] = counts


def _const_spec(shape):
    nd = len(shape)
    return pl.BlockSpec(shape, lambda *_: (0,) * nd, pipeline_mode=pl.Buffered(1))


def _stage1_prompt_call(x, mod_p, vecs, w_in_b, w_sp, bias_full, w_pool_b, w_out_b, w_r, b_r,
                        w_gate, w_up, w_down):
    b, s, _ = x.shape
    ns = s // TL
    nt = b * ns
    assert nt + 1 >= N_EXPERTS
    x2 = x.reshape(b * s, D)
    cur = lambda t: (jnp.minimum(t, nt - 1), 0)
    prev = lambda t: (jnp.maximum(t - 1, 0), 0)
    e_in, e_out, e_shapes = _expert_cast_specs(lambda t: jnp.minimum(t, N_EXPERTS - 1))
    return pl.pallas_call(
        functools.partial(_stage1_prompt_kernel, tiles_per_seq=ns),
        grid=(nt + 1,),
        in_specs=[
            pl.BlockSpec((TL, D), cur),
            pl.BlockSpec((TL, D), prev),
            _const_spec(mod_p.shape),
            _const_spec(vecs.shape),
            _const_spec(w_in_b.shape),
            _const_spec(w_sp.shape),
            _const_spec(bias_full.shape),
            _const_spec(w_pool_b.shape),
            _const_spec(w_out_b.shape),
            _const_spec(w_r.shape),
            _const_spec(b_r.shape),
        ] + e_in,
        out_specs=[
            pl.BlockSpec((TL, D), prev),
            pl.BlockSpec((TL, D), prev),
            pl.BlockSpec((TL, LANES), prev),
            pl.BlockSpec((1, 8, LANES), lambda t: (jnp.maximum(t - 1, 0), 0, 0)),
            pl.BlockSpec((1, W_MAX, PW), lambda t: (jnp.minimum(t, nt - 1) // ns, 0, 0)),
        ] + e_out,
        out_shape=[
            jax.ShapeDtypeStruct((b * s, D), f32),
            jax.ShapeDtypeStruct((b * s, D), bf16),
            jax.ShapeDtypeStruct((b * s, LANES), bf16),
            jax.ShapeDtypeStruct((nt, 8, LANES), f32),
            jax.ShapeDtypeStruct((b, W_MAX, PW), f32),
        ] + e_shapes,
        scratch_shapes=[pltpu.VMEM((W_MAX, PW), f32), pltpu.VMEM((TL, D), f32)],
        compiler_params=pltpu.CompilerParams(
            dimension_semantics=("arbitrary",), vmem_limit_bytes=VMEM_LIMIT),
    )(x2, x2, mod_p, vecs, w_in_b, w_sp, bias_full, w_pool_b, w_out_b, w_r, b_r,
      w_gate, w_up, w_down)


def _stage1_sample_call(x, mod_s, vecs, w_in_b, state_t, w_pool_b, w_out_b, w_r, b_r):
    n = x.shape[0]
    return pl.pallas_call(
        _stage1_sample_kernel,
        out_shape=[
            jax.ShapeDtypeStruct((n, D), f32),
            jax.ShapeDtypeStruct((n, D), bf16),
            jax.ShapeDtypeStruct((n, LANES), bf16),
            jax.ShapeDtypeStruct((8, LANES), f32),
            jax.ShapeDtypeStruct((n, PW), f32),
            jax.ShapeDtypeStruct((n, D), f32),
        ],
        compiler_params=pltpu.CompilerParams(vmem_limit_bytes=VMEM_LIMIT),
    )(x, mod_s, vecs, w_in_b, state_t, w_pool_b, w_out_b, w_r, b_r)


def _moe_buffer_rows(ts, nt):
    worst = ts * nt + nt * N_GROUPS * (ROW_ALIGN - 1) + N_GROUPS * (RB - 1)
    return -(-worst // RB) * RB


def _moe_kernel(cnt_ref, h2_ref, r3_ref, x1_ref, mod_ref, g_ref, wg_ref, wu_ref, wd_ref, ex_ref,
                o_ref,
                hsbuf, hs_tile, ys_tile, pt_buf, pm_buf, tab,
                *, ts, nt, tpr):
    w = pl.program_id(0)
    i = pl.program_id(1)
    rt = ts + LANES

    def copy_rows(src, src0, dst, dst0, nrows, ncols):
        def body(j, _):
            s0 = pl.multiple_of(src0 + j * ROW_ALIGN, ROW_ALIGN)
            d0 = pl.multiple_of(dst0 + j * ROW_ALIGN, ROW_ALIGN)
            dst[pl.ds(d0, ROW_ALIGN), :] = src[pl.ds(s0, ROW_ALIGN), :ncols]
            return 0
        lax.fori_loop(0, nrows // ROW_ALIGN, body, 0)

    @pl.when((w == 0) & (i == 0))
    def _():
        hsbuf[...] = jnp.zeros_like(hsbuf)
        ys_tile[...] = jnp.zeros_like(ys_tile)

    @pl.when(i == 0)
    def _sort_and_run_experts():
        def run_len(tile, g):
            c = cnt_ref[(w * nt + tile) * N_GROUPS + g]
            return ((c + (ROW_ALIGN - 1)) // ROW_ALIGN) * ROW_ALIGN

        lens = [[run_len(t, g) for g in range(N_GROUPS)] for t in range(nt)]
        region = [sum(lens[t][g] for t in range(nt)) for g in range(N_GROUPS)]
        region = [((r + (RB - 1)) // RB) * RB for r in region]
        base = [sum(region[:g]) for g in range(N_GROUPS)]
        offs = list(base)
        woff = []
        for t in range(nt):
            woff.append(list(offs))
            for g in range(N_GROUPS):
                tab[t * 2 * N_GROUPS + g] = lens[t][g]
                tab[t * 2 * N_GROUPS + N_GROUPS + g] = offs[g]
                offs[g] = offs[g] + lens[t][g]

        lane = lax.broadcasted_iota(jnp.int32, (ts, LANES), 1)
        lane_f = lane.astype(f32)
        r_i = lax.broadcasted_iota(jnp.int32, (ts, ts), 0)
        c_i = lax.broadcasted_iota(jnp.int32, (ts, ts), 1)
        ltri = jnp.where(r_i > c_i, 1.0, 0.0).astype(bf16)
        lane8 = lax.broadcasted_iota(jnp.int32, (8, LANES), 1)
        sel = jnp.where(lane8 == 0, float(ROW_ALIGN), jnp.where(lane8 == 1, 1.0, 0.0)).astype(bf16)
        rt_lane = lax.broadcasted_iota(jnp.int32, (ts, rt), 1).astype(f32)
        rt_sub = lax.broadcasted_iota(jnp.int32, (rt, ts), 0).astype(f32)

        tiles = range(nt)
        r3s = [r3_ref[t * ts:(t + 1) * ts, :] for t in tiles]
        gids = [jnp.sum(jnp.where(lane == GIDX_LANE, r3s[t].astype(f32), 0.0), axis=-1,
                        keepdims=True) for t in tiles]
        onehots = [jnp.where(lane_f == gids[t], 1.0, 0.0) for t in tiles]
        ranks = [_dot(ltri, onehots[t].astype(bf16)) for t in tiles]
        poss = []
        for t in tiles:
            seg = jnp.zeros((1, LANES), f32)
            start = 0
            for g in range(N_GROUPS):
                seg = seg + jnp.where(lane[0:1] == g, jnp.asarray(start, jnp.int32).astype(f32), 0.0)
                start = start + lens[t][g]
            poss.append(jnp.sum(onehots[t] * (ranks[t] + seg), axis=-1, keepdims=True))
        for t in tiles:
            pt_buf[t] = jnp.where(rt_lane == poss[t], 1.0, 0.0).astype(bf16)
        pos_rows = []
        for t in tiles:
            q = jnp.floor(poss[t] * (1.0 / ROW_ALIGN))
            digits = jnp.where(lane == 0, q, jnp.where(lane == 1, poss[t] - q * ROW_ALIGN, 0.0))
            pos_rows.append(lax.dot_general(sel, digits.astype(bf16), (((1,), (1,)), ((), ())),
                                            preferred_element_type=f32))
        for t in tiles:
            pm_buf[t] = jnp.where(rt_sub == pos_rows[t][0:1], 1.0, 0.0).astype(bf16)

        for t in tiles:
            p_mat = pm_buf[t]
            hs_tile[:, :D] = _dot(p_mat, h2_ref[t * ts:(t + 1) * ts, :]).astype(bf16)
            hs_tile[:, D:] = _dot(p_mat, r3_ref[t * ts:(t + 1) * ts, :]).astype(bf16)
            start = 0
            for g in range(N_GROUPS):
                copy_rows(hs_tile, start, hsbuf, woff[t][g], lens[t][g], D + LANES)
                start = start + lens[t][g]

        def expert_rows(g, r0, nrows):
            rows = hsbuf[pl.ds(r0, nrows), :D]
            gate = _dot(rows, wg_ref[g * D:(g + 1) * D, :])
            up = _dot(rows, wu_ref[g * D:(g + 1) * D, :])
            cw = _dot(hsbuf[pl.ds(r0, nrows), D:], ex_ref[g * LANES:(g + 1) * LANES, :])
            act = gate * jax.nn.sigmoid(gate) * up * cw
            hsbuf[pl.ds(r0, nrows), :D] = _dot(
                act.astype(bf16), wd_ref[g * D:(g + 1) * D, :]).astype(bf16)

        for g in range(N_GROUPS):
            n_blocks = region[g] // RB

            def two_blocks(b, _, g=g):
                expert_rows(g, pl.multiple_of(base[g] + 2 * b * RB, RB), 2 * RB)
                return 0
            lax.fori_loop(0, n_blocks // 2, two_blocks, 0)

            @pl.when(n_blocks % 2 == 1)
            def _(g=g, n_blocks=n_blocks):
                expert_rows(g, pl.multiple_of(base[g] + (n_blocks - 1) * RB, RB), RB)

    start = 0
    for g in range(N_GROUPS):
        ln = tab[i * 2 * N_GROUPS + g]
        copy_rows(hsbuf, tab[i * 2 * N_GROUPS + N_GROUPS + g], ys_tile, start, ln, D)
        start = start + ln
    f = _dot(pt_buf[i], ys_tile[...])
    tok0 = (w * nt + i) * ts
    gt2_row = N_MOD - 1
    gt2 = (mod_ref[gt2_row, pl.ds(tok0, ts), :] if tpr == 1
           else mod_ref[gt2_row, pl.ds(tok0 // tpr, 1), :])
    o_ref[...] = x1_ref[...] + gt2 * _rms(f, g_ref[...])


def _moe_call(cnt, x1, h2, r3, mod, g_post, wg, wu, wd, ex, ts, nt, tpr):
    n = x1.shape[0]
    win = ts * nt
    rbuf = _moe_buffer_rows(ts, nt)
    rt = ts + LANES
    grid_spec = pltpu.PrefetchScalarGridSpec(
        num_scalar_prefetch=1,
        grid=(n // win, nt),
        in_specs=[
            pl.BlockSpec((win, D), lambda w, i, c: (w, 0)),
            pl.BlockSpec((win, LANES), lambda w, i, c: (w, 0)),
            pl.BlockSpec((ts, D), lambda w, i, c: (w * nt + i, 0)),
            _const_spec(mod.shape),
            _const_spec(g_post.shape),
            _const_spec(wg.shape),
            _const_spec(wu.shape),
            _const_spec(wd.shape),
            _const_spec(ex.shape),
        ],
        out_specs=pl.BlockSpec((ts, D), lambda w, i, c: (w * nt + i, 0)),
        scratch_shapes=[
            pltpu.VMEM((rbuf, D + LANES), bf16),
            pltpu.VMEM((rt, D + LANES), bf16),
            pltpu.VMEM((rt, D), bf16),
            pltpu.VMEM((nt, ts, rt), bf16),
            pltpu.VMEM((nt, rt, ts), bf16),
            pltpu.SMEM((nt * 2 * N_GROUPS,), jnp.int32),
        ],
    )
    return pl.pallas_call(
        functools.partial(_moe_kernel, ts=ts, nt=nt, tpr=tpr),
        grid_spec=grid_spec,
        out_shape=jax.ShapeDtypeStruct((n, D), f32),
        compiler_params=pltpu.CompilerParams(
            dimension_semantics=("arbitrary", "arbitrary"), vmem_limit_bytes=VMEM_LIMIT),
    )(cnt, h2, r3, x1, mod, g_post, wg, wu, wd, ex)


def _count_table(cnt):
    return cnt[:, 0, :N_GROUPS].astype(jnp.int32).reshape(-1)


def kernel(x_prompt, x_sample, c_prompt, c_sample, state_pool, w_ada, b_ada, g_pre_mix, g_post_mix, g_pre_ffn, g_post_ffn, w_in, ln_v_g, ln_v_b, w_spatial, b_spatial, w_pool, pool_scale, w_out, w_router_grp, b_router_grp, w_router_exp, b_router_exp, w_exp_gate, w_exp_up, w_exp_down):
    depth = w_in.shape[0]
    assert depth == 1
    b, s, _ = x_prompt.shape
    n_s = x_sample.shape[0]
    l = 0

    c_all = jnp.concatenate([c_prompt, c_sample], axis=0)
    mod_p, mod_s, w_in_b, w_out_b = _mod_call(c_all, w_ada[l], b_ada[l], b, w_in[l], w_out[l])

    ws, bs = w_spatial[l], b_spatial[l]
    zeros = jnp.zeros((D,), f32)
    vecs = jnp.stack([
        g_pre_mix[l], g_post_mix[l], g_pre_ffn[l], g_post_ffn[l], ln_v_g[l], ln_v_b[l],
        pool_scale[l], zeros,
        jnp.repeat(ws[:, 0, 0], HEAD_DIM), jnp.repeat(bs[:, 0], HEAD_DIM),
        zeros, zeros, zeros, zeros, zeros, zeros])
    bias_full = jnp.repeat(bs.T, HEAD_DIM, axis=1)
    pad = LANES - N_EXPERTS - N_GROUPS
    w_r = jnp.concatenate([w_router_exp[l], w_router_grp[l], jnp.zeros((D, pad), f32)], axis=1)
    w_r_hi = w_r.astype(bf16)
    w_r_lo = (w_r - w_r_hi.astype(f32)).astype(bf16)
    w_r2 = jnp.concatenate([w_r_hi, w_r_lo], axis=1)
    b_r = jnp.concatenate([b_router_exp[l], b_router_grp[l], jnp.zeros((pad,), f32)])[None]

    x1_p, h2_p, r3_p, cnt_p, plast, wg, wu, wd = _stage1_prompt_call(
        x_prompt, mod_p, vecs, w_in_b, ws, bias_full, w_pool[l], w_out_b, w_r2, b_r,
        w_exp_gate[l], w_exp_up[l], w_exp_down[l])
    state_t = jnp.transpose(state_pool[l], (1, 0, 2))
    x1_s, h2_s, r3_s, cnt_s, p_s, v_s = _stage1_sample_call(
        x_sample.reshape(n_s, D), mod_s, vecs, w_in_b, state_t, w_pool[l], w_out_b, w_r2, b_r)

    lane_e = (jnp.arange(LANES) % N_EXPERTS)[None, :, None]
    lane_ok = (jnp.arange(LANES) < 3 * N_EXPERTS)[None, :, None]
    col_e = (jnp.arange(EPG * D_EXPERT) // D_EXPERT)[None, None, :]
    ex = (lane_e == jnp.arange(N_GROUPS)[:, None, None] * EPG + col_e) & lane_ok
    ex = ex.astype(bf16).reshape(N_GROUPS * LANES, EPG * D_EXPERT)

    g_post = g_post_ffn[l].reshape(1, D)
    y_p = _moe_call(
        _count_table(cnt_p), x1_p.reshape(b * s, D), h2_p.reshape(b * s, D),
        r3_p.reshape(b * s, LANES), mod_p, g_post, wg, wu, wd, ex, TL, MOE_WINDOW // TL, s)
    y_s = _moe_call(
        _count_table(cnt_s[None]), x1_s, h2_s, r3_s, mod_s, g_post, wg, wu, wd, ex, n_s, 1, 1)

    state_pool_prompt = plast[:, 1:][None]
    state_pool_sample = jnp.concatenate([state_pool[l][:, 1:], p_s[:, None, :]], axis=1)[None]
    chunk_v_sample = v_s.reshape(1, n_s, 1, D)
    return (y_p.reshape(b, s, D), y_s.reshape(n_s, 1, D), state_pool_prompt,
            state_pool_sample, chunk_v_sample)
```

```python
import functools

import jax
import jax.numpy as jnp
from jax import lax
from jax.experimental import pallas as pl
from jax.experimental.pallas import tpu as pltpu

D = 1024
CHUNK = 128
HEADS = 8
HEAD_DIM = 128
WINDOWS = (2, 4, 8, 16)
PW = 512
PG = 128
W_MAX = 16
N_GROUPS = 4
EPG = 8
N_EXPERTS = 32
D_EXPERT = 128
EPS = 1e-6
N_MOD = 6
LANES = 128
GROUP_LANE0 = 32
GIDX_LANE = 96

TL = 256
MOE_WINDOW = 2048
ROW_ALIGN = 16
RB = 128
VMEM_LIMIT = 56 * 1024 * 1024

bf16 = jnp.bfloat16
f32 = jnp.float32


def _rms(x, g):
    ms = jnp.mean(x * x, axis=-1, keepdims=True)
    return x * lax.rsqrt(ms + EPS) * g


def _dot(a, b):
    return jnp.dot(a, b, preferred_element_type=f32)


_GELU_C = 2.0 * 0.7978845608028654


def _gelu(x):
    t = x * ((-_GELU_C) + (-_GELU_C * 0.044715) * (x * x))
    return x / (1.0 + jnp.exp(t))


def _expert_cast_specs(step_to_expert):
    e = step_to_expert
    col_block = pl.BlockSpec((D, D_EXPERT), lambda t: (e(t) // EPG, e(t) % EPG))
    in_specs = [
        pl.BlockSpec((1, D, D_EXPERT), lambda t: (e(t), 0, 0)),
        pl.BlockSpec((1, D, D_EXPERT), lambda t: (e(t), 0, 0)),
        pl.BlockSpec((1, D_EXPERT, D), lambda t: (e(t), 0, 0)),
    ]
    out_specs = [col_block, col_block, pl.BlockSpec((D_EXPERT, D), lambda t: (e(t), 0))]
    out_shapes = [
        jax.ShapeDtypeStruct((N_GROUPS * D, EPG * D_EXPERT), bf16),
        jax.ShapeDtypeStruct((N_GROUPS * D, EPG * D_EXPERT), bf16),
        jax.ShapeDtypeStruct((N_EXPERTS * D_EXPERT, D), bf16),
    ]
    return in_specs, out_specs, out_shapes


def _mod_kernel(c_ref, w_ref, b_ref, w_in_ref, w_out_ref, op_ref, os_ref, w_in_o, w_out_o):
    c = c_ref[...]
    a = (c * jax.nn.sigmoid(c)).astype(bf16)
    m = _dot(a, w_ref[...].astype(bf16)) + b_ref[0]
    nb = op_ref.shape[1]
    op_ref[0] = m[:nb]
    os_ref[0] = m[nb:]
    w_in_o[...] = w_in_ref[...].astype(bf16)
    w_out_o[...] = w_out_ref[...].astype(bf16)


def _mod_call(c_all, w_ada, b_ada, nb, w_in, w_out):
    n = c_all.shape[0]
    in_cols = w_in.shape[1] // N_MOD
    assert w_in.shape[1] == N_MOD * in_cols and in_cols % LANES == 0
    return pl.pallas_call(
        _mod_kernel,
        grid=(N_MOD,),
        in_specs=[
            pl.BlockSpec((n, D), lambda j: (0, 0)),
            pl.BlockSpec((D, D), lambda j: (0, j)),
            pl.BlockSpec((1, 1, D), lambda j: (j, 0, 0)),
            pl.BlockSpec((D, in_cols), lambda j: (0, j)),
            pl.BlockSpec((D, D), lambda j: (0, 0)),
        ],
        out_specs=[pl.BlockSpec((1, nb, D), lambda j: (j, 0, 0)),
                   pl.BlockSpec((1, n - nb, D), lambda j: (j, 0, 0)),
                   pl.BlockSpec((D, in_cols), lambda j: (0, j)),
                   pl.BlockSpec((D, D), lambda j: (0, 0))],
        out_shape=[jax.ShapeDtypeStruct((N_MOD, nb, D), f32),
                   jax.ShapeDtypeStruct((N_MOD, n - nb, D), f32),
                   jax.ShapeDtypeStruct(w_in.shape, bf16),
                   jax.ShapeDtypeStruct(w_out.shape, bf16)],
        compiler_params=pltpu.CompilerParams(
            dimension_semantics=("arbitrary",), vmem_limit_bytes=VMEM_LIMIT),
    )(c_all, w_ada, b_ada.reshape(N_MOD, 1, D), w_in, w_out)


def _route(logits):
    t = logits.shape[0]
    lane = lax.broadcasted_iota(jnp.int32, (t, LANES), 1)
    lane_f = lane.astype(f32)
    neg = -jnp.inf
    big = 1e9
    gmask = (lane >= GROUP_LANE0) & (lane < GROUP_LANE0 + N_GROUPS)
    gl = jnp.where(gmask, logits, neg)
    gmax = jnp.max(gl, axis=-1, keepdims=True)
    g_idx = jnp.min(jnp.where(gl == gmax, lane_f - GROUP_LANE0, big), axis=-1, keepdims=True)
    sumexp = jnp.sum(jnp.where(gmask, jnp.exp(gl - gmax), 0.0), axis=-1, keepdims=True)
    p_g = 1.0 / sumexp
    lane_grp = (lane >> 3).astype(f32)
    emask = (lane < N_EXPERTS) & (lane_grp == g_idx)
    el = jnp.where(emask, logits, neg)
    m1 = jnp.max(el, axis=-1, keepdims=True)
    i1 = jnp.min(jnp.where(el == m1, lane_f, big), axis=-1, keepdims=True)
    el2 = jnp.where(lane_f == i1, neg, el)
    m2 = jnp.max(el2, axis=-1, keepdims=True)
    i2 = jnp.min(jnp.where(el2 == m2, lane_f, big), axis=-1, keepdims=True)
    e = jnp.exp(m2 - m1)
    w1 = p_g / (1.0 + e)
    w2 = w1 * e

    def split3(w):
        hi = w.astype(bf16).astype(f32)
        mid = (w - hi).astype(bf16).astype(f32)
        lo = w - hi - mid
        return hi, mid, lo

    r3 = jnp.where(lane == GIDX_LANE, g_idx, 0.0)
    for idx, w in ((i1, w1), (i2, w2)):
        for part, wp in enumerate(split3(w)):
            r3 = r3 + jnp.where(lane_f == idx + float(part * N_EXPERTS), wp, 0.0)
    counts = jnp.sum(jnp.where(lane_f == g_idx, 1.0, 0.0), axis=0, keepdims=True)
    return r3.astype(bf16), jnp.broadcast_to(counts, (8, LANES))


def _merge_project(u, ga, gb, mix, y_b, w_out_ref):
    y_a = u * mix
    merged = jax.nn.sigmoid(ga) * y_a + jax.nn.sigmoid(gb) * y_b
    return _dot(merged.astype(bf16), w_out_ref[...])


def _residual_route(x, y, mods, vec_ref, w_r_ref, b_r_ref):
    sh1, sc1, gt1, sh2, sc2, gt2 = mods
    x1 = x + gt1 * _rms(y, vec_ref[1:2])
    h2 = _rms(x1, vec_ref[2:3] * (1.0 + sc2)) + sh2
    h2_hi = h2.astype(bf16)
    h2_lo = (h2 - h2_hi.astype(f32)).astype(bf16)
    r = _dot(h2_hi, w_r_ref[...]) + _dot(h2_lo, w_r_ref[...])
    logits = r[:, :LANES] + r[:, LANES:] + b_r_ref[...]
    r3, counts = _route(logits)
    return x1, h2_hi, r3, counts


def _in_proj(x, mods, vec_ref, w_in_ref, after_first_dot=lambda: None):
    sh1, sc1 = mods[0], mods[1]
    h = _rms(x, vec_ref[0:1] * (1.0 + sc1)) + sh1
    hb = h.astype(bf16)
    zu = _dot(hb, w_in_ref[:, 0:D])
    after_first_dot()
    zv = _dot(hb, w_in_ref[:, D:2 * D])
    p = _dot(hb, w_in_ref[:, 2 * D:2 * D + PW])
    ga = _dot(hb, w_in_ref[:, 2 * D + PW:3 * D + PW])
    gb = _dot(hb, w_in_ref[:, 3 * D + PW:4 * D + PW])
    return zu, zv, p, ga, gb


def _activate(zu, zv, vec_ref):
    u = _gelu(zu)
    v = _gelu(zv)
    mu = jnp.mean(v, axis=-1, keepdims=True)
    vc = v - mu
    var = jnp.mean(vc * vc, axis=-1, keepdims=True)
    v = vc * lax.rsqrt(var + EPS) * vec_ref[4:5] + vec_ref[5:6]
    return u, v


def _pool_out(d_groups, vec_ref, w_pool_ref):
    parts = [_dot(d.astype(bf16), w_pool_ref[gi].astype(bf16)) for gi, d in enumerate(d_groups)]
    return jnp.concatenate(parts, axis=1) * vec_ref[6:7]


def _stage1_prompt_kernel(x_ref, xprev_ref, mod_ref, vec_ref, w_in_ref, w_sp_ref, bias_ref,
                          w_pool_ref, w_out_ref, w_r_ref, b_r_ref, eg_ref, eu_ref, ed_ref,
                          x1_ref, h2_ref, r3_ref, cnt_ref, plast_ref, wg_ref, wu_ref, wd_ref,
                          pbuf, ybuf, *, tiles_per_seq):
    t = pl.program_id(0)
    n_tiles = pl.num_programs(0) - 1

    @pl.when(t == 0)
    def _():
        ybuf[...] = jnp.zeros_like(ybuf)
        pbuf[...] = jnp.zeros_like(pbuf)

    t_cur = jnp.minimum(t, n_tiles - 1)
    bi = t_cur // tiles_per_seq
    s = t_cur % tiles_per_seq
    x = x_ref[...]
    mods = [mod_ref[i, pl.ds(bi, 1), :] for i in range(N_MOD)]

    def prev_tile_second_half():
        wg_ref[...] = eg_ref[0].astype(bf16)
        wu_ref[...] = eu_ref[0].astype(bf16)
        wd_ref[...] = ed_ref[0].astype(bf16)
        t_prev = jnp.maximum(t - 1, 0)
        b_prev = t_prev // tiles_per_seq
        mods_prev = [mod_ref[i, pl.ds(b_prev, 1), :] for i in range(N_MOD)]
        x1, h2b, r3, counts = _residual_route(xprev_ref[...], ybuf[...], mods_prev, vec_ref,
                                              w_r_ref, b_r_ref)
        x1_ref[...] = x1
        h2_ref[...] = h2b
        r3_ref[...] = r3
        cnt_ref[0] = counts

    zu, zv, p, ga, gb = _in_proj(x, mods, vec_ref, w_in_ref, prev_tile_second_half)

    u, v = _activate(zu, zv, vec_ref)

    vb = v.astype(bf16)
    row = lax.broadcasted_iota(jnp.int32, (CHUNK, CHUNK), 0)
    col = lax.broadcasted_iota(jnp.int32, (CHUNK, CHUNK), 1)
    w_tril = [jnp.where(row >= col, w_sp_ref[hd], 0.0).astype(bf16) for hd in range(HEADS)]
    bias = bias_ref[...]
    chunks = []
    for c in range(TL // CHUNK):
        heads = [_dot(w_tril[hd], vb[c * CHUNK:(c + 1) * CHUNK, hd * HEAD_DIM:(hd + 1) * HEAD_DIM])
                 for hd in range(HEADS)]
        chunks.append(jnp.concatenate(heads, axis=1) + bias)
    mix = jnp.concatenate(chunks, axis=0)

    carry = jnp.where(s == 0, 0.0, pbuf[...])
    ext = jnp.concatenate([carry, p], axis=0)
    pos = s * TL + lax.broadcasted_iota(jnp.int32, (TL, PG), 0)
    d_groups = []
    for gi, w in enumerate(WINDOWS):
        acc = ext[:, gi * PG:(gi + 1) * PG]
        k = 1
        while k < w:
            acc = acc + pltpu.roll(acc, k, 0)
            k *= 2
        cnt = jnp.minimum(pos + 1, w).astype(f32)
        d_groups.append(acc[W_MAX:] / cnt - p[:, gi * PG:(gi + 1) * PG])
    pbuf[...] = p[TL - W_MAX:]
    plast_ref[0] = p[TL - W_MAX:]
    y_b = _pool_out(d_groups, vec_ref, w_pool_ref)
    ybuf[...] = _merge_project(u, ga, gb, mix, y_b, w_out_ref)


def _stage1_sample_kernel(x_ref, mod_ref, vec_ref, w_in_ref, state_ref, w_pool_ref,
                          w_out_ref, w_r_ref, b_r_ref,
                          x1_ref, h2_ref, r3_ref, cnt_ref, p_ref, v_ref):
    x = x_ref[...]
    mods = [mod_ref[i] for i in range(N_MOD)]
    zu, zv, p, ga, gb = _in_proj(x, mods, vec_ref, w_in_ref)
    u, v = _activate(zu, zv, vec_ref)
    v_ref[...] = v
    p_ref[...] = p
    mix = v * vec_ref[8:9] + vec_ref[9:10]
    d_groups = []
    for gi, w in enumerate(WINDOWS):
        sl = slice(gi * PG, (gi + 1) * PG)
        acc = p[:, sl]
        for r in range(W_MAX - w, W_MAX - 1):
            acc = acc + state_ref[r][:, sl]
        d_groups.append(acc / float(w) - p[:, sl])
    y_b = _pool_out(d_groups, vec_ref, w_pool_ref)
    y = _merge_project(u, ga, gb, mix, y_b, w_out_ref)
    x1, h2b, r3, counts = _residual_route(x, y, mods, vec_ref, w_r_ref, b_r_ref)
    x1_ref[...] = x1
    h2_ref[...] = h2b
    r3_ref[...] = r3
    cnt_ref[...] = counts


def _const_spec(shape):
    nd = len(shape)
    return pl.BlockSpec(shape, lambda *_: (0,) * nd, pipeline_mode=pl.Buffered(1))


def _stage1_prompt_call(x, mod_p, vecs, w_in_b, w_sp, bias_full, w_pool_b, w_out_b, w_r, b_r,
                        w_gate, w_up, w_down):
    b, s, _ = x.shape
    ns = s // TL
    nt = b * ns
    assert nt + 1 >= N_EXPERTS
    x2 = x.reshape(b * s, D)
    cur = lambda t: (jnp.minimum(t, nt - 1), 0)
    prev = lambda t: (jnp.maximum(t - 1, 0), 0)
    e_in, e_out, e_shapes = _expert_cast_specs(lambda t: jnp.minimum(t, N_EXPERTS - 1))
    return pl.pallas_call(
        functools.partial(_stage1_prompt_kernel, tiles_per_seq=ns),
        grid=(nt + 1,),
        in_specs=[
            pl.BlockSpec((TL, D), cur),
            pl.BlockSpec((TL, D), prev),
            _const_spec(mod_p.shape),
            _const_spec(vecs.shape),
            _const_spec(w_in_b.shape),
            _const_spec(w_sp.shape),
            _const_spec(bias_full.shape),
            _const_spec(w_pool_b.shape),
            _const_spec(w_out_b.shape),
            _const_spec(w_r.shape),
            _const_spec(b_r.shape),
        ] + e_in,
        out_specs=[
            pl.BlockSpec((TL, D), prev),
            pl.BlockSpec((TL, D), prev),
            pl.BlockSpec((TL, LANES), prev),
            pl.BlockSpec((1, 8, LANES), lambda t: (jnp.maximum(t - 1, 0), 0, 0)),
            pl.BlockSpec((1, W_MAX, PW), lambda t: (jnp.minimum(t, nt - 1) // ns, 0, 0)),
        ] + e_out,
        out_shape=[
            jax.ShapeDtypeStruct((b * s, D), f32),
            jax.ShapeDtypeStruct((b * s, D), bf16),
            jax.ShapeDtypeStruct((b * s, LANES), bf16),
            jax.ShapeDtypeStruct((nt, 8, LANES), f32),
            jax.ShapeDtypeStruct((b, W_MAX, PW), f32),
        ] + e_shapes,
        scratch_shapes=[pltpu.VMEM((W_MAX, PW), f32), pltpu.VMEM((TL, D), f32)],
        compiler_params=pltpu.CompilerParams(
            dimension_semantics=("arbitrary",), vmem_limit_bytes=VMEM_LIMIT),
    )(x2, x2, mod_p, vecs, w_in_b, w_sp, bias_full, w_pool_b, w_out_b, w_r, b_r,
      w_gate, w_up, w_down)


def _stage1_sample_call(x, mod_s, vecs, w_in_b, state_t, w_pool_b, w_out_b, w_r, b_r):
    n = x.shape[0]
    return pl.pallas_call(
        _stage1_sample_kernel,
        out_shape=[
            jax.ShapeDtypeStruct((n, D), f32),
            jax.ShapeDtypeStruct((n, D), bf16),
            jax.ShapeDtypeStruct((n, LANES), bf16),
            jax.ShapeDtypeStruct((8, LANES), f32),
            jax.ShapeDtypeStruct((n, PW), f32),
            jax.ShapeDtypeStruct((n, D), f32),
        ],
        compiler_params=pltpu.CompilerParams(vmem_limit_bytes=VMEM_LIMIT),
    )(x, mod_s, vecs, w_in_b, state_t, w_pool_b, w_out_b, w_r, b_r)


def _moe_buffer_rows(ts, nt):
    worst = ts * nt + nt * N_GROUPS * (ROW_ALIGN - 1) + N_GROUPS * (RB - 1)
    return -(-worst // RB) * RB


def _moe_kernel(cnt_ref, h2_ref, r3_ref, x1_ref, mod_ref, g_ref, wg_ref, wu_ref, wd_ref, o_ref,
                hsbuf, hs_tile, ys_tile, pt_buf, pm_buf, tab,
                *, ts, nt, tpr):
    w = pl.program_id(0)
    i = pl.program_id(1)
    rt = ts + LANES

    def copy_rows(src, src0, dst, dst0, nrows, ncols):
        def body(j, _):
            s0 = pl.multiple_of(src0 + j * ROW_ALIGN, ROW_ALIGN)
            d0 = pl.multiple_of(dst0 + j * ROW_ALIGN, ROW_ALIGN)
            dst[pl.ds(d0, ROW_ALIGN), :] = src[pl.ds(s0, ROW_ALIGN), :ncols]
            return 0
        lax.fori_loop(0, nrows // ROW_ALIGN, body, 0)

    @pl.when((w == 0) & (i == 0))
    def _():
        hsbuf[...] = jnp.zeros_like(hsbuf)
        ys_tile[...] = jnp.zeros_like(ys_tile)

    @pl.when(i == 0)
    def _sort_and_run_experts():
        def run_len(tile, g):
            c = cnt_ref[(w * nt + tile) * N_GROUPS + g]
            return ((c + (ROW_ALIGN - 1)) // ROW_ALIGN) * ROW_ALIGN

        lens = [[run_len(t, g) for g in range(N_GROUPS)] for t in range(nt)]
        region = [sum(lens[t][g] for t in range(nt)) for g in range(N_GROUPS)]
        region = [((r + (RB - 1)) // RB) * RB for r in region]
        base = [sum(region[:g]) for g in range(N_GROUPS)]
        offs = list(base)
        woff = []
        for t in range(nt):
            woff.append(list(offs))
            for g in range(N_GROUPS):
                tab[t * 2 * N_GROUPS + g] = lens[t][g]
                tab[t * 2 * N_GROUPS + N_GROUPS + g] = offs[g]
                offs[g] = offs[g] + lens[t][g]

        lane = lax.broadcasted_iota(jnp.int32, (ts, LANES), 1)
        lane_f = lane.astype(f32)
        r_i = lax.broadcasted_iota(jnp.int32, (ts, ts), 0)
        c_i = lax.broadcasted_iota(jnp.int32, (ts, ts), 1)
        ltri = jnp.where(r_i > c_i, 1.0, 0.0).astype(bf16)
        lane8 = lax.broadcasted_iota(jnp.int32, (8, LANES), 1)
        sel = jnp.where(lane8 == 0, float(ROW_ALIGN), jnp.where(lane8 == 1, 1.0, 0.0)).astype(bf16)
        rt_lane = lax.broadcasted_iota(jnp.int32, (ts, rt), 1).astype(f32)
        rt_sub = lax.broadcasted_iota(jnp.int32, (rt, ts), 0).astype(f32)

        tiles = range(nt)
        r3s = [r3_ref[t * ts:(t + 1) * ts, :] for t in tiles]
        gids = [jnp.sum(jnp.where(lane == GIDX_LANE, r3s[t].astype(f32), 0.0), axis=-1,
                        keepdims=True) for t in tiles]
        onehots = [jnp.where(lane_f == gids[t], 1.0, 0.0) for t in tiles]
        ranks = [_dot(ltri, onehots[t].astype(bf16)) for t in tiles]
        poss = []
        for t in tiles:
            seg = jnp.zeros((1, LANES), f32)
            start = 0
            for g in range(N_GROUPS):
                seg = seg + jnp.where(lane[0:1] == g, jnp.asarray(start, jnp.int32).astype(f32), 0.0)
                start = start + lens[t][g]
            poss.append(jnp.sum(onehots[t] * (ranks[t] + seg), axis=-1, keepdims=True))
        for t in tiles:
            pt_buf[t] = jnp.where(rt_lane == poss[t], 1.0, 0.0).astype(bf16)
        pos_rows = []
        for t in tiles:
            q = jnp.floor(poss[t] * (1.0 / ROW_ALIGN))
            digits = jnp.where(lane == 0, q, jnp.where(lane == 1, poss[t] - q * ROW_ALIGN, 0.0))
            pos_rows.append(lax.dot_general(sel, digits.astype(bf16), (((1,), (1,)), ((), ())),
                                            preferred_element_type=f32))
        for t in tiles:
            pm_buf[t] = jnp.where(rt_sub == pos_rows[t][0:1], 1.0, 0.0).astype(bf16)

        for t in tiles:
            p_mat = pm_buf[t]
            hs_tile[:, :D] = _dot(p_mat, h2_ref[t * ts:(t + 1) * ts, :]).astype(bf16)
            hs_tile[:, D:] = _dot(p_mat, r3_ref[t * ts:(t + 1) * ts, :]).astype(bf16)
            start = 0
            for g in range(N_GROUPS):
                copy_rows(hs_tile, start, hsbuf, woff[t][g], lens[t][g], D + LANES)
                start = start + lens[t][g]

        def expert_rows(g, r0, nrows):
            rows = hsbuf[pl.ds(r0, nrows), :D]
            gate = _dot(rows, wg_ref[g * D:(g + 1) * D, :])
            up = _dot(rows, wu_ref[g * D:(g + 1) * D, :])
            c3 = hsbuf[pl.ds(r0, nrows), D:].astype(f32)
            cw_lanes = c3 + pltpu.roll(c3, LANES - N_EXPERTS, 1) + pltpu.roll(c3, LANES - 2 * N_EXPERTS, 1)
            cw = jnp.concatenate(
                [jnp.broadcast_to(cw_lanes[:, g * EPG + j:g * EPG + j + 1], (nrows, D_EXPERT))
                 for j in range(EPG)], axis=1)
            act = gate * jax.nn.sigmoid(gate) * up * cw
            hsbuf[pl.ds(r0, nrows), :D] = _dot(
                act.astype(bf16), wd_ref[g * D:(g + 1) * D, :]).astype(bf16)

        for g in range(N_GROUPS):
            n_blocks = region[g] // RB

            def two_blocks(b, _, g=g):
                expert_rows(g, pl.multiple_of(base[g] + 2 * b * RB, RB), 2 * RB)
                return 0
            lax.fori_loop(0, n_blocks // 2, two_blocks, 0)

            @pl.when(n_blocks % 2 == 1)
            def _(g=g, n_blocks=n_blocks):
                expert_rows(g, pl.multiple_of(base[g] + (n_blocks - 1) * RB, RB), RB)

    start = 0
    for g in range(N_GROUPS):
        ln = tab[i * 2 * N_GROUPS + g]
        copy_rows(hsbuf, tab[i * 2 * N_GROUPS + N_GROUPS + g], ys_tile, start, ln, D)
        start = start + ln
    f = _dot(pt_buf[i], ys_tile[...])
    tok0 = (w * nt + i) * ts
    gt2_row = N_MOD - 1
    gt2 = (mod_ref[gt2_row, pl.ds(tok0, ts), :] if tpr == 1
           else mod_ref[gt2_row, pl.ds(tok0 // tpr, 1), :])
    o_ref[...] = x1_ref[...] + gt2 * _rms(f, g_ref[...])


def _moe_call(cnt, x1, h2, r3, mod, g_post, wg, wu, wd, ts, nt, tpr):
    n = x1.shape[0]
    win = ts * nt
    rbuf = _moe_buffer_rows(ts, nt)
    rt = ts + LANES
    grid_spec = pltpu.PrefetchScalarGridSpec(
        num_scalar_prefetch=1,
        grid=(n // win, nt),
        in_specs=[
            pl.BlockSpec((win, D), lambda w, i, c: (w, 0)),
            pl.BlockSpec((win, LANES), lambda w, i, c: (w, 0)),
            pl.BlockSpec((ts, D), lambda w, i, c: (w * nt + i, 0)),
            _const_spec(mod.shape),
            _const_spec(g_post.shape),
            _const_spec(wg.shape),
            _const_spec(wu.shape),
            _const_spec(wd.shape),
        ],
        out_specs=pl.BlockSpec((ts, D), lambda w, i, c: (w * nt + i, 0)),
        scratch_shapes=[
            pltpu.VMEM((rbuf, D + LANES), bf16),
            pltpu.VMEM((rt, D + LANES), bf16),
            pltpu.VMEM((rt, D), bf16),
            pltpu.VMEM((nt, ts, rt), bf16),
            pltpu.VMEM((nt, rt, ts), bf16),
            pltpu.SMEM((nt * 2 * N_GROUPS,), jnp.int32),
        ],
    )
    return pl.pallas_call(
        functools.partial(_moe_kernel, ts=ts, nt=nt, tpr=tpr),
        grid_spec=grid_spec,
        out_shape=jax.ShapeDtypeStruct((n, D), f32),
        compiler_params=pltpu.CompilerParams(
            dimension_semantics=("arbitrary", "arbitrary"), vmem_limit_bytes=VMEM_LIMIT),
    )(cnt, h2, r3, x1, mod, g_post, wg, wu, wd)


def _count_table(cnt):
    return cnt[:, 0, :N_GROUPS].astype(jnp.int32).reshape(-1)


def kernel(x_prompt, x_sample, c_prompt, c_sample, state_pool, w_ada, b_ada, g_pre_mix, g_post_mix, g_pre_ffn, g_post_ffn, w_in, ln_v_g, ln_v_b, w_spatial, b_spatial, w_pool, pool_scale, w_out, w_router_grp, b_router_grp, w_router_exp, b_router_exp, w_exp_gate, w_exp_up, w_exp_down):
    depth = w_in.shape[0]
    assert depth == 1
    b, s, _ = x_prompt.shape
    n_s = x_sample.shape[0]
    l = 0

    c_all = jnp.concatenate([c_prompt, c_sample], axis=0)
    mod_p, mod_s, w_in_b, w_out_b = _mod_call(c_all, w_ada[l], b_ada[l], b, w_in[l], w_out[l])

    ws, bs = w_spatial[l], b_spatial[l]
    zeros = jnp.zeros((D,), f32)
    vecs = jnp.stack([
        g_pre_mix[l], g_post_mix[l], g_pre_ffn[l], g_post_ffn[l], ln_v_g[l], ln_v_b[l],
        pool_scale[l], zeros,
        jnp.repeat(ws[:, 0, 0], HEAD_DIM), jnp.repeat(bs[:, 0], HEAD_DIM),
        zeros, zeros, zeros, zeros, zeros, zeros])
    bias_full = jnp.repeat(bs.T, HEAD_DIM, axis=1)
    pad = LANES - N_EXPERTS - N_GROUPS
    w_r = jnp.concatenate([w_router_exp[l], w_router_grp[l], jnp.zeros((D, pad), f32)], axis=1)
    w_r_hi = w_r.astype(bf16)
    w_r_lo = (w_r - w_r_hi.astype(f32)).astype(bf16)
    w_r2 = jnp.concatenate([w_r_hi, w_r_lo], axis=1)
    b_r = jnp.concatenate([b_router_exp[l], b_router_grp[l], jnp.zeros((pad,), f32)])[None]

    x1_p, h2_p, r3_p, cnt_p, plast, wg, wu, wd = _stage1_prompt_call(
        x_prompt, mod_p, vecs, w_in_b, ws, bias_full, w_pool[l], w_out_b, w_r2, b_r,
        w_exp_gate[l], w_exp_up[l], w_exp_down[l])
    state_t = jnp.transpose(state_pool[l], (1, 0, 2))
    x1_s, h2_s, r3_s, cnt_s, p_s, v_s = _stage1_sample_call(
        x_sample.reshape(n_s, D), mod_s, vecs, w_in_b, state_t, w_pool[l], w_out_b, w_r2, b_r)

    g_post = g_post_ffn[l].reshape(1, D)
    y_p = _moe_call(
        _count_table(cnt_p), x1_p.reshape(b * s, D), h2_p.reshape(b * s, D),
        r3_p.reshape(b * s, LANES), mod_p, g_post, wg, wu, wd, TL, MOE_WINDOW // TL, s)
    y_s = _moe_call(
        _count_table(cnt_s[None]), x1_s, h2_s, r3_s, mod_s, g_post, wg, wu, wd, n_s, 1, 1)

    state_pool_prompt = plast[:, 1:][None]
    state_pool_sample = jnp.concatenate([state_pool[l][:, 1:], p_s[:, None, :]], axis=1)[None]
    chunk_v_sample = v_s.reshape(1, n_s, 1, D)
    return (y_p.reshape(b, s, D), y_s.reshape(n_s, 1, D), state_pool_prompt,
            state_pool_sample, chunk_v_sample)
```

```python
import functools

import jax
import jax.numpy as jnp
from jax import lax
from jax.experimental import pallas as pl
from jax.experimental.pallas import tpu as pltpu

D = 1024
CHUNK = 128
HEADS = 8
HEAD_DIM = 128
WINDOWS = (2, 4, 8, 16)
PW = 512
PG = 128
W_MAX = 16
N_GROUPS = 4
EPG = 8
N_EXPERTS = 32
D_EXPERT = 128
EPS = 1e-6
N_MOD = 6
LANES = 128
GROUP_LANE0 = 32
GIDX_LANE = 96

TL = 256
SUB = 2
MOE_WINDOW = 2048
ROW_ALIGN = 16
RB = 128
VMEM_LIMIT = 56 * 1024 * 1024

bf16 = jnp.bfloat16
f32 = jnp.float32


def _rms(x, g):
    ms = jnp.mean(x * x, axis=-1, keepdims=True)
    return x * lax.rsqrt(ms + EPS) * g


def _dot(a, b):
    return jnp.dot(a, b, preferred_element_type=f32)


_GELU_C = 2.0 * 0.7978845608028654


def _gelu(x):
    t = x * ((-_GELU_C) + (-_GELU_C * 0.044715) * (x * x))
    return x / (1.0 + jnp.exp(t))


def _expert_cast_specs(step_to_expert):
    e = step_to_expert
    col_block = pl.BlockSpec((D, D_EXPERT), lambda t: (e(t) // EPG, e(t) % EPG))
    in_specs = [
        pl.BlockSpec((1, D, D_EXPERT), lambda t: (e(t), 0, 0)),
        pl.BlockSpec((1, D, D_EXPERT), lambda t: (e(t), 0, 0)),
        pl.BlockSpec((1, D_EXPERT, D), lambda t: (e(t), 0, 0)),
    ]
    out_specs = [col_block, col_block, pl.BlockSpec((D_EXPERT, D), lambda t: (e(t), 0))]
    out_shapes = [
        jax.ShapeDtypeStruct((N_GROUPS * D, EPG * D_EXPERT), bf16),
        jax.ShapeDtypeStruct((N_GROUPS * D, EPG * D_EXPERT), bf16),
        jax.ShapeDtypeStruct((N_EXPERTS * D_EXPERT, D), bf16),
    ]
    return in_specs, out_specs, out_shapes


def _mod_kernel(c_ref, w_ref, b_ref, w_in_ref, w_out_ref, op_ref, os_ref, w_in_o, w_out_o):
    c = c_ref[...]
    a = (c * jax.nn.sigmoid(c)).astype(bf16)
    m = _dot(a, w_ref[...].astype(bf16)) + b_ref[0]
    nb = op_ref.shape[1]
    op_ref[0] = m[:nb]
    os_ref[0] = m[nb:]
    w_in_o[...] = w_in_ref[...].astype(bf16)
    w_out_o[...] = w_out_ref[...].astype(bf16)


def _mod_call(c_all, w_ada, b_ada, nb, w_in, w_out):
    n = c_all.shape[0]
    in_cols = w_in.shape[1] // N_MOD
    assert w_in.shape[1] == N_MOD * in_cols and in_cols % LANES == 0
    return pl.pallas_call(
        _mod_kernel,
        grid=(N_MOD,),
        in_specs=[
            pl.BlockSpec((n, D), lambda j: (0, 0)),
            pl.BlockSpec((D, D), lambda j: (0, j)),
            pl.BlockSpec((1, 1, D), lambda j: (j, 0, 0)),
            pl.BlockSpec((D, in_cols), lambda j: (0, j)),
            pl.BlockSpec((D, D), lambda j: (0, 0)),
        ],
        out_specs=[pl.BlockSpec((1, nb, D), lambda j: (j, 0, 0)),
                   pl.BlockSpec((1, n - nb, D), lambda j: (j, 0, 0)),
                   pl.BlockSpec((D, in_cols), lambda j: (0, j)),
                   pl.BlockSpec((D, D), lambda j: (0, 0))],
        out_shape=[jax.ShapeDtypeStruct((N_MOD, nb, D), f32),
                   jax.ShapeDtypeStruct((N_MOD, n - nb, D), f32),
                   jax.ShapeDtypeStruct(w_in.shape, bf16),
                   jax.ShapeDtypeStruct(w_out.shape, bf16)],
        compiler_params=pltpu.CompilerParams(
            dimension_semantics=("arbitrary",), vmem_limit_bytes=VMEM_LIMIT),
    )(c_all, w_ada, b_ada.reshape(N_MOD, 1, D), w_in, w_out)


def _route(logits):
    t = logits.shape[0]
    lane = lax.broadcasted_iota(jnp.int32, (t, LANES), 1)
    lane_f = lane.astype(f32)
    neg = -jnp.inf
    big = 1e9
    gmask = (lane >= GROUP_LANE0) & (lane < GROUP_LANE0 + N_GROUPS)
    gl = jnp.where(gmask, logits, neg)
    gmax = jnp.max(gl, axis=-1, keepdims=True)
    g_idx = jnp.min(jnp.where(gl == gmax, lane_f - GROUP_LANE0, big), axis=-1, keepdims=True)
    sumexp = jnp.sum(jnp.where(gmask, jnp.exp(gl - gmax), 0.0), axis=-1, keepdims=True)
    p_g = 1.0 / sumexp
    lane_grp = (lane >> 3).astype(f32)
    emask = (lane < N_EXPERTS) & (lane_grp == g_idx)
    el = jnp.where(emask, logits, neg)
    m1 = jnp.max(el, axis=-1, keepdims=True)
    i1 = jnp.min(jnp.where(el == m1, lane_f, big), axis=-1, keepdims=True)
    el2 = jnp.where(lane_f == i1, neg, el)
    m2 = jnp.max(el2, axis=-1, keepdims=True)
    i2 = jnp.min(jnp.where(el2 == m2, lane_f, big), axis=-1, keepdims=True)
    e = jnp.exp(m2 - m1)
    w1 = p_g / (1.0 + e)
    w2 = w1 * e

    def split3(w):
        hi = w.astype(bf16).astype(f32)
        mid = (w - hi).astype(bf16).astype(f32)
        lo = w - hi - mid
        return hi, mid, lo

    r3 = jnp.where(lane == GIDX_LANE, g_idx, 0.0)
    for idx, w in ((i1, w1), (i2, w2)):
        for part, wp in enumerate(split3(w)):
            r3 = r3 + jnp.where(lane_f == idx + float(part * N_EXPERTS), wp, 0.0)
    counts = jnp.sum(jnp.where(lane_f == g_idx, 1.0, 0.0), axis=0, keepdims=True)
    return r3.astype(bf16), jnp.broadcast_to(counts, (8, LANES))


def _merge_project(u, ga, gb, mix, y_b, w_out_ref):
    y_a = u * mix
    merged = jax.nn.sigmoid(ga) * y_a + jax.nn.sigmoid(gb) * y_b
    return _dot(merged.astype(bf16), w_out_ref[...])


def _residual_route(x, y, mods, vec_ref, w_r_ref, b_r_ref):
    sh1, sc1, gt1, sh2, sc2, gt2 = mods
    x1 = x + gt1 * _rms(y, vec_ref[1:2])
    h2 = _rms(x1, vec_ref[2:3] * (1.0 + sc2)) + sh2
    h2_hi = h2.astype(bf16)
    h2_lo = (h2 - h2_hi.astype(f32)).astype(bf16)
    r = _dot(h2_hi, w_r_ref[...]) + _dot(h2_lo, w_r_ref[...])
    logits = r[:, :LANES] + r[:, LANES:] + b_r_ref[...]
    r3, counts = _route(logits)
    return x1, h2_hi, r3, counts


def _in_proj(x, mods, vec_ref, w_in_ref, after_first_dot=lambda: None):
    sh1, sc1 = mods[0], mods[1]
    h = _rms(x, vec_ref[0:1] * (1.0 + sc1)) + sh1
    hb = h.astype(bf16)
    zu = _dot(hb, w_in_ref[:, 0:D])
    after_first_dot()
    zv = _dot(hb, w_in_ref[:, D:2 * D])
    p = _dot(hb, w_in_ref[:, 2 * D:2 * D + PW])
    ga = _dot(hb, w_in_ref[:, 2 * D + PW:3 * D + PW])
    gb = _dot(hb, w_in_ref[:, 3 * D + PW:4 * D + PW])
    return zu, zv, p, ga, gb


def _activate(zu, zv, vec_ref):
    u = _gelu(zu)
    v = _gelu(zv)
    mu = jnp.mean(v, axis=-1, keepdims=True)
    vc = v - mu
    var = jnp.mean(vc * vc, axis=-1, keepdims=True)
    v = vc * lax.rsqrt(var + EPS) * vec_ref[4:5] + vec_ref[5:6]
    return u, v


def _pool_out(d_groups, vec_ref, w_pool_ref):
    parts = [_dot(d.astype(bf16), w_pool_ref[gi].astype(bf16)) for gi, d in enumerate(d_groups)]
    return jnp.concatenate(parts, axis=1) * vec_ref[6:7]


def _stage1_prompt_kernel(x_ref, xprev_ref, mod_ref, vec_ref, w_in_ref, w_sp_ref, bias_ref,
                          w_pool_ref, w_out_ref, w_r_ref, b_r_ref, eg_ref, eu_ref, ed_ref,
                          x1_ref, h2_ref, r3_ref, cnt_ref, plast_ref, wg_ref, wu_ref, wd_ref,
                          pbuf, ybuf, *, tiles_per_seq):
    t = pl.program_id(0)
    last = pl.num_programs(0) - 1

    def mods_of(tile):
        b = tile // tiles_per_seq
        return [mod_ref[i, pl.ds(b, 1), :] for i in range(N_MOD)]

    def second_half(k):
        rows = slice(k * TL, (k + 1) * TL)
        tile = jnp.maximum((t - 1) * SUB + k, 0)
        x1, h2b, r3, counts = _residual_route(xprev_ref[rows, :], ybuf[k], mods_of(tile), vec_ref,
                                              w_r_ref, b_r_ref)
        x1_ref[rows, :] = x1
        h2_ref[rows, :] = h2b
        r3_ref[rows, :] = r3
        cnt_ref[k] = counts

    def cast_expert():
        wg_ref[...] = eg_ref[0].astype(bf16)
        wu_ref[...] = eu_ref[0].astype(bf16)
        wd_ref[...] = ed_ref[0].astype(bf16)

    def first_half(k, under_projection):
        tile = t * SUB + k
        s = tile % tiles_per_seq
        x = x_ref[k * TL:(k + 1) * TL, :]
        zu, zv, p, ga, gb = _in_proj(x, mods_of(tile), vec_ref, w_in_ref, under_projection)
        u, v = _activate(zu, zv, vec_ref)

        vb = v.astype(bf16)
        row = lax.broadcasted_iota(jnp.int32, (CHUNK, CHUNK), 0)
        col = lax.broadcasted_iota(jnp.int32, (CHUNK, CHUNK), 1)
        w_tril = [jnp.where(row >= col, w_sp_ref[hd], 0.0).astype(bf16) for hd in range(HEADS)]
        bias = bias_ref[...]
        chunks = []
        for c in range(TL // CHUNK):
            heads = [_dot(w_tril[hd],
                          vb[c * CHUNK:(c + 1) * CHUNK, hd * HEAD_DIM:(hd + 1) * HEAD_DIM])
                     for hd in range(HEADS)]
            chunks.append(jnp.concatenate(heads, axis=1) + bias)
        mix = jnp.concatenate(chunks, axis=0)

        carry = jnp.where(s == 0, 0.0, pbuf[...])
        ext = jnp.concatenate([carry, p], axis=0)
        pos = s * TL + lax.broadcasted_iota(jnp.int32, (TL, PG), 0)
        d_groups = []
        for gi, w in enumerate(WINDOWS):
            acc = ext[:, gi * PG:(gi + 1) * PG]
            shift = 1
            while shift < w:
                acc = acc + pltpu.roll(acc, shift, 0)
                shift *= 2
            cnt = jnp.minimum(pos + 1, w).astype(f32)
            d_groups.append(acc[W_MAX:] / cnt - p[:, gi * PG:(gi + 1) * PG])
        pbuf[...] = p[TL - W_MAX:]
        plast_ref[0] = p[TL - W_MAX:]
        y_b = _pool_out(d_groups, vec_ref, w_pool_ref)
        ybuf[k] = _merge_project(u, ga, gb, mix, y_b, w_out_ref)

    @pl.when(t == 0)
    def _():
        ybuf[...] = jnp.zeros_like(ybuf)
        pbuf[...] = jnp.zeros_like(pbuf)

    @pl.when(t < last)
    def _():
        for k in range(SUB):
            def under_projection(k=k):
                if k == 0:
                    cast_expert()
                second_half(k)
            first_half(k, under_projection)

    @pl.when(t == last)
    def _():
        for k in range(SUB):
            second_half(k)


def _stage1_sample_kernel(x_ref, mod_ref, vec_ref, w_in_ref, state_ref, w_pool_ref,
                          w_out_ref, w_r_ref, b_r_ref,
                          x1_ref, h2_ref, r3_ref, cnt_ref, p_ref, v_ref):
    x = x_ref[...]
    mods = [mod_ref[i] for i in range(N_MOD)]
    zu, zv, p, ga, gb = _in_proj(x, mods, vec_ref, w_in_ref)
    u, v = _activate(zu, zv, vec_ref)
    v_ref[...] = v
    p_ref[...] = p
    mix = v * vec_ref[8:9] + vec_ref[9:10]
    d_groups = []
    for gi, w in enumerate(WINDOWS):
        sl = slice(gi * PG, (gi + 1) * PG)
        acc = p[:, sl]
        for r in range(W_MAX - w, W_MAX - 1):
            acc = acc + state_ref[r][:, sl]
        d_groups.append(acc / float(w) - p[:, sl])
    y_b = _pool_out(d_groups, vec_ref, w_pool_ref)
    y = _merge_project(u, ga, gb, mix, y_b, w_out_ref)
    x1, h2b, r3, counts = _residual_route(x, y, mods, vec_ref, w_r_ref, b_r_ref)
    x1_ref[...] = x1
    h2_ref[...] = h2b
    r3_ref[...] = r3
    cnt_ref[...] = counts


def _const_spec(shape):
    nd = len(shape)
    return pl.BlockSpec(shape, lambda *_: (0,) * nd, pipeline_mode=pl.Buffered(1))


def _stage1_prompt_call(x, mod_p, vecs, w_in_b, w_sp, bias_full, w_pool_b, w_out_b, w_r, b_r,
                        w_gate, w_up, w_down):
    b, s, _ = x.shape
    ns = s // TL
    nt = b * ns
    steps = nt // SUB
    assert ns % SUB == 0
    assert steps >= N_EXPERTS
    x2 = x.reshape(b * s, D)
    rows = SUB * TL
    cur = lambda t: (jnp.minimum(t, steps - 1), 0)
    prev = lambda t: (jnp.maximum(t - 1, 0), 0)
    e_in, e_out, e_shapes = _expert_cast_specs(lambda t: jnp.minimum(t, N_EXPERTS - 1))
    return pl.pallas_call(
        functools.partial(_stage1_prompt_kernel, tiles_per_seq=ns),
        grid=(steps + 1,),
        in_specs=[
            pl.BlockSpec((rows, D), cur),
            pl.BlockSpec((rows, D), prev),
            _const_spec(mod_p.shape),
            _const_spec(vecs.shape),
            _const_spec(w_in_b.shape),
            _const_spec(w_sp.shape),
            _const_spec(bias_full.shape),
            _const_spec(w_pool_b.shape),
            _const_spec(w_out_b.shape),
            _const_spec(w_r.shape),
            _const_spec(b_r.shape),
        ] + e_in,
        out_specs=[
            pl.BlockSpec((rows, D), prev),
            pl.BlockSpec((rows, D), prev),
            pl.BlockSpec((rows, LANES), prev),
            pl.BlockSpec((SUB, 8, LANES), lambda t: (jnp.maximum(t - 1, 0), 0, 0)),
            pl.BlockSpec((1, W_MAX, PW),
                         lambda t: (jnp.minimum(t, steps - 1) * SUB // ns, 0, 0)),
        ] + e_out,
        out_shape=[
            jax.ShapeDtypeStruct((b * s, D), f32),
            jax.ShapeDtypeStruct((b * s, D), bf16),
            jax.ShapeDtypeStruct((b * s, LANES), bf16),
            jax.ShapeDtypeStruct((nt, 8, LANES), f32),
            jax.ShapeDtypeStruct((b, W_MAX, PW), f32),
        ] + e_shapes,
        scratch_shapes=[pltpu.VMEM((W_MAX, PW), f32), pltpu.VMEM((SUB, TL, D), f32)],
        compiler_params=pltpu.CompilerParams(
            dimension_semantics=("arbitrary",), vmem_limit_bytes=VMEM_LIMIT),
    )(x2, x2, mod_p, vecs, w_in_b, w_sp, bias_full, w_pool_b, w_out_b, w_r, b_r,
      w_gate, w_up, w_down)


def _stage1_sample_call(x, mod_s, vecs, w_in_b, state_t, w_pool_b, w_out_b, w_r, b_r):
    n = x.shape[0]
    return pl.pallas_call(
        _stage1_sample_kernel,
        out_shape=[
            jax.ShapeDtypeStruct((n, D), f32),
            jax.ShapeDtypeStruct((n, D), bf16),
            jax.ShapeDtypeStruct((n, LANES), bf16),
            jax.ShapeDtypeStruct((8, LANES), f32),
            jax.ShapeDtypeStruct((n, PW), f32),
            jax.ShapeDtypeStruct((n, D), f32),
        ],
        compiler_params=pltpu.CompilerParams(vmem_limit_bytes=VMEM_LIMIT),
    )(x, mod_s, vecs, w_in_b, state_t, w_pool_b, w_out_b, w_r, b_r)


def _moe_buffer_rows(ts, nt):
    worst = ts * nt + nt * N_GROUPS * (ROW_ALIGN - 1) + N_GROUPS * (RB - 1)
    return -(-worst // RB) * RB


def _moe_kernel(cnt_ref, h2_ref, r3_ref, x1_ref, mod_ref, g_ref, wg_ref, wu_ref, wd_ref, o_ref,
                hsbuf, hs_tile, ys_tile, pt_buf, pm_buf, tab,
                *, ts, nt, tpr):
    w = pl.program_id(0)
    i = pl.program_id(1)
    rt = ts + LANES

    def copy_rows(src, src0, dst, dst0, nrows, ncols):
        def body(j, _):
            s0 = pl.multiple_of(src0 + j * ROW_ALIGN, ROW_ALIGN)
            d0 = pl.multiple_of(dst0 + j * ROW_ALIGN, ROW_ALIGN)
            dst[pl.ds(d0, ROW_ALIGN), :] = src[pl.ds(s0, ROW_ALIGN), :ncols]
            return 0
        lax.fori_loop(0, nrows // ROW_ALIGN, body, 0)

    @pl.when((w == 0) & (i == 0))
    def _():
        hsbuf[...] = jnp.zeros_like(hsbuf)
        ys_tile[...] = jnp.zeros_like(ys_tile)

    @pl.when(i == 0)
    def _sort_and_run_experts():
        def run_len(tile, g):
            c = cnt_ref[(w * nt + tile) * N_GROUPS + g]
            return ((c + (ROW_ALIGN - 1)) // ROW_ALIGN) * ROW_ALIGN

        lens = [[run_len(t, g) for g in range(N_GROUPS)] for t in range(nt)]
        region = [sum(lens[t][g] for t in range(nt)) for g in range(N_GROUPS)]
        region = [((r + (RB - 1)) // RB) * RB for r in region]
        base = [sum(region[:g]) for g in range(N_GROUPS)]
        offs = list(base)
        woff = []
        for t in range(nt):
            woff.append(list(offs))
            for g in range(N_GROUPS):
                tab[t * 2 * N_GROUPS + g] = lens[t][g]
                tab[t * 2 * N_GROUPS + N_GROUPS + g] = offs[g]
                offs[g] = offs[g] + lens[t][g]

        lane = lax.broadcasted_iota(jnp.int32, (ts, LANES), 1)
        lane_f = lane.astype(f32)
        r_i = lax.broadcasted_iota(jnp.int32, (ts, ts), 0)
        c_i = lax.broadcasted_iota(jnp.int32, (ts, ts), 1)
        ltri = jnp.where(r_i > c_i, 1.0, 0.0).astype(bf16)
        lane8 = lax.broadcasted_iota(jnp.int32, (8, LANES), 1)
        sel = jnp.where(lane8 == 0, float(ROW_ALIGN), jnp.where(lane8 == 1, 1.0, 0.0)).astype(bf16)
        rt_lane = lax.broadcasted_iota(jnp.int32, (ts, rt), 1).astype(f32)
        rt_sub = lax.broadcasted_iota(jnp.int32, (rt, ts), 0).astype(f32)

        tiles = range(nt)
        r3s = [r3_ref[t * ts:(t + 1) * ts, :] for t in tiles]
        gids = [jnp.sum(jnp.where(lane == GIDX_LANE, r3s[t].astype(f32), 0.0), axis=-1,
                        keepdims=True) for t in tiles]
        onehots = [jnp.where(lane_f == gids[t], 1.0, 0.0) for t in tiles]
        ranks = [_dot(ltri, onehots[t].astype(bf16)) for t in tiles]
        poss = []
        for t in tiles:
            seg = jnp.zeros((1, LANES), f32)
            start = 0
            for g in range(N_GROUPS):
                seg = seg + jnp.where(lane[0:1] == g, jnp.asarray(start, jnp.int32).astype(f32), 0.0)
                start = start + lens[t][g]
            poss.append(jnp.sum(onehots[t] * (ranks[t] + seg), axis=-1, keepdims=True))
        for t in tiles:
            pt_buf[t] = jnp.where(rt_lane == poss[t], 1.0, 0.0).astype(bf16)
        pos_rows = []
        for t in tiles:
            q = jnp.floor(poss[t] * (1.0 / ROW_ALIGN))
            digits = jnp.where(lane == 0, q, jnp.where(lane == 1, poss[t] - q * ROW_ALIGN, 0.0))
            pos_rows.append(lax.dot_general(sel, digits.astype(bf16), (((1,), (1,)), ((), ())),
                                            preferred_element_type=f32))
        for t in tiles:
            pm_buf[t] = jnp.where(rt_sub == pos_rows[t][0:1], 1.0, 0.0).astype(bf16)

        for t in tiles:
            p_mat = pm_buf[t]
            hs_tile[:, :D] = _dot(p_mat, h2_ref[t * ts:(t + 1) * ts, :]).astype(bf16)
            hs_tile[:, D:] = _dot(p_mat, r3_ref[t * ts:(t + 1) * ts, :]).astype(bf16)
            start = 0
            for g in range(N_GROUPS):
                copy_rows(hs_tile, start, hsbuf, woff[t][g], lens[t][g], D + LANES)
                start = start + lens[t][g]

        def expert_rows(g, r0, nrows):
            rows = hsbuf[pl.ds(r0, nrows), :D]
            gate = _dot(rows, wg_ref[g * D:(g + 1) * D, :])
            up = _dot(rows, wu_ref[g * D:(g + 1) * D, :])
            c3 = hsbuf[pl.ds(r0, nrows), D:].astype(f32)
            cw_lanes = c3 + pltpu.roll(c3, LANES - N_EXPERTS, 1) + pltpu.roll(c3, LANES - 2 * N_EXPERTS, 1)
            cw = jnp.concatenate(
                [jnp.broadcast_to(cw_lanes[:, g * EPG + j:g * EPG + j + 1], (nrows, D_EXPERT))
                 for j in range(EPG)], axis=1)
            act = gate * jax.nn.sigmoid(gate) * up * cw
            hsbuf[pl.ds(r0, nrows), :D] = _dot(
                act.astype(bf16), wd_ref[g * D:(g + 1) * D, :]).astype(bf16)

        for g in range(N_GROUPS):
            n_blocks = region[g] // RB

            def two_blocks(b, _, g=g):
                expert_rows(g, pl.multiple_of(base[g] + 2 * b * RB, RB), 2 * RB)
                return 0
            lax.fori_loop(0, n_blocks // 2, two_blocks, 0)

            @pl.when(n_blocks % 2 == 1)
            def _(g=g, n_blocks=n_blocks):
                expert_rows(g, pl.multiple_of(base[g] + (n_blocks - 1) * RB, RB), RB)

    start = 0
    for g in range(N_GROUPS):
        ln = tab[i * 2 * N_GROUPS + g]
        copy_rows(hsbuf, tab[i * 2 * N_GROUPS + N_GROUPS + g], ys_tile, start, ln, D)
        start = start + ln
    f = _dot(pt_buf[i], ys_tile[...])
    tok0 = (w * nt + i) * ts
    gt2_row = N_MOD - 1
    gt2 = (mod_ref[gt2_row, pl.ds(tok0, ts), :] if tpr == 1
           else mod_ref[gt2_row, pl.ds(tok0 // tpr, 1), :])
    o_ref[...] = x1_ref[...] + gt2 * _rms(f, g_ref[...])


def _moe_call(cnt, x1, h2, r3, mod, g_post, wg, wu, wd, ts, nt, tpr):
    n = x1.shape[0]
    win = ts * nt
    rbuf = _moe_buffer_rows(ts, nt)
    rt = ts + LANES
    grid_spec = pltpu.PrefetchScalarGridSpec(
        num_scalar_prefetch=1,
        grid=(n // win, nt),
        in_specs=[
            pl.BlockSpec((win, D), lambda w, i, c: (w, 0)),
            pl.BlockSpec((win, LANES), lambda w, i, c: (w, 0)),
            pl.BlockSpec((ts, D), lambda w, i, c: (w * nt + i, 0)),
            _const_spec(mod.shape),
            _const_spec(g_post.shape),
            _const_spec(wg.shape),
            _const_spec(wu.shape),
            _const_spec(wd.shape),
        ],
        out_specs=pl.BlockSpec((ts, D), lambda w, i, c: (w * nt + i, 0)),
        scratch_shapes=[
            pltpu.VMEM((rbuf, D + LANES), bf16),
            pltpu.VMEM((rt, D + LANES), bf16),
            pltpu.VMEM((rt, D), bf16),
            pltpu.VMEM((nt, ts, rt), bf16),
            pltpu.VMEM((nt, rt, ts), bf16),
            pltpu.SMEM((nt * 2 * N_GROUPS,), jnp.int32),
        ],
    )
    return pl.pallas_call(
        functools.partial(_moe_kernel, ts=ts, nt=nt, tpr=tpr),
        grid_spec=grid_spec,
        out_shape=jax.ShapeDtypeStruct((n, D), f32),
        compiler_params=pltpu.CompilerParams(
            dimension_semantics=("arbitrary", "arbitrary"), vmem_limit_bytes=VMEM_LIMIT),
    )(cnt, h2, r3, x1, mod, g_post, wg, wu, wd)


def _count_table(cnt):
    return cnt[:, 0, :N_GROUPS].astype(jnp.int32).reshape(-1)


def kernel(x_prompt, x_sample, c_prompt, c_sample, state_pool, w_ada, b_ada, g_pre_mix, g_post_mix, g_pre_ffn, g_post_ffn, w_in, ln_v_g, ln_v_b, w_spatial, b_spatial, w_pool, pool_scale, w_out, w_router_grp, b_router_grp, w_router_exp, b_router_exp, w_exp_gate, w_exp_up, w_exp_down):
    depth = w_in.shape[0]
    assert depth == 1
    b, s, _ = x_prompt.shape
    n_s = x_sample.shape[0]
    l = 0

    c_all = jnp.concatenate([c_prompt, c_sample], axis=0)
    mod_p, mod_s, w_in_b, w_out_b = _mod_call(c_all, w_ada[l], b_ada[l], b, w_in[l], w_out[l])

    ws, bs = w_spatial[l], b_spatial[l]
    zeros = jnp.zeros((D,), f32)
    vecs = jnp.stack([
        g_pre_mix[l], g_post_mix[l], g_pre_ffn[l], g_post_ffn[l], ln_v_g[l], ln_v_b[l],
        pool_scale[l], zeros,
        jnp.repeat(ws[:, 0, 0], HEAD_DIM), jnp.repeat(bs[:, 0], HEAD_DIM),
        zeros, zeros, zeros, zeros, zeros, zeros])
    bias_full = jnp.repeat(bs.T, HEAD_DIM, axis=1)
    pad = LANES - N_EXPERTS - N_GROUPS
    w_r = jnp.concatenate([w_router_exp[l], w_router_grp[l], jnp.zeros((D, pad), f32)], axis=1)
    w_r_hi = w_r.astype(bf16)
    w_r_lo = (w_r - w_r_hi.astype(f32)).astype(bf16)
    w_r2 = jnp.concatenate([w_r_hi, w_r_lo], axis=1)
    b_r = jnp.concatenate([b_router_exp[l], b_router_grp[l], jnp.zeros((pad,), f32)])[None]

    x1_p, h2_p, r3_p, cnt_p, plast, wg, wu, wd = _stage1_prompt_call(
        x_prompt, mod_p, vecs, w_in_b, ws, bias_full, w_pool[l], w_out_b, w_r2, b_r,
        w_exp_gate[l], w_exp_up[l], w_exp_down[l])
    state_t = jnp.transpose(state_pool[l], (1, 0, 2))
    x1_s, h2_s, r3_s, cnt_s, p_s, v_s = _stage1_sample_call(
        x_sample.reshape(n_s, D), mod_s, vecs, w_in_b, state_t, w_pool[l], w_out_b, w_r2, b_r)

    g_post = g_post_ffn[l].reshape(1, D)
    y_p = _moe_call(
        _count_table(cnt_p), x1_p.reshape(b * s, D), h2_p.reshape(b * s, D),
        r3_p.reshape(b * s, LANES), mod_p, g_post, wg, wu, wd, TL, MOE_WINDOW // TL, s)
    y_s = _moe_call(
        _count_table(cnt_s[None]), x1_s, h2_s, r3_s, mod_s, g_post, wg, wu, wd, n_s, 1, 1)

    state_pool_prompt = plast[:, 1:][None]
    state_pool_sample = jnp.concatenate([state_pool[l][:, 1:], p_s[:, None, :]], axis=1)[None]
    chunk_v_sample = v_s.reshape(1, n_s, 1, D)
    return (y_p.reshape(b, s, D), y_s.reshape(n_s, 1, D), state_pool_prompt,
            state_pool_sample, chunk_v_sample)
```

```python
import functools

import jax
import jax.numpy as jnp
from jax import lax
from jax.experimental import pallas as pl
from jax.experimental.pallas import tpu as pltpu

D = 1024
CHUNK = 128
HEADS = 8
HEAD_DIM = 128
WINDOWS = (2, 4, 8, 16)
PW = 512
PG = 128
W_MAX = 16
N_GROUPS = 4
EPG = 8
N_EXPERTS = 32
D_EXPERT = 128
EPS = 1e-6
N_MOD = 6
LANES = 128
GROUP_LANE0 = 32
GIDX_LANE = 96

TL = 256
SUB = 2
MOE_WINDOW = 2048
ROW_ALIGN = 16
RB = 128
VMEM_LIMIT = 60 * 1024 * 1024

bf16 = jnp.bfloat16
f32 = jnp.float32


def _rms(x, g):
    ms = jnp.mean(x * x, axis=-1, keepdims=True)
    return x * lax.rsqrt(ms + EPS) * g


def _dot(a, b):
    return jnp.dot(a, b, preferred_element_type=f32)


_GELU_C = 2.0 * 0.7978845608028654


def _gelu(x):
    t = x * ((-_GELU_C) + (-_GELU_C * 0.044715) * (x * x))
    return x / (1.0 + jnp.exp(t))


def _expert_cast_specs(step_to_expert):
    e = step_to_expert
    col_block = pl.BlockSpec((D, D_EXPERT), lambda t: (e(t) // EPG, e(t) % EPG))
    in_specs = [
        pl.BlockSpec((1, D, D_EXPERT), lambda t: (e(t), 0, 0)),
        pl.BlockSpec((1, D, D_EXPERT), lambda t: (e(t), 0, 0)),
        pl.BlockSpec((1, D_EXPERT, D), lambda t: (e(t), 0, 0)),
    ]
    out_specs = [col_block, col_block, pl.BlockSpec((D_EXPERT, D), lambda t: (e(t), 0))]
    out_shapes = [
        jax.ShapeDtypeStruct((N_GROUPS * D, EPG * D_EXPERT), bf16),
        jax.ShapeDtypeStruct((N_GROUPS * D, EPG * D_EXPERT), bf16),
        jax.ShapeDtypeStruct((N_EXPERTS * D_EXPERT, D), bf16),
    ]
    return in_specs, out_specs, out_shapes


def _mod_kernel(c_ref, w_ref, b_ref, w_in_ref, w_out_ref, op_ref, os_ref, w_in_o, w_out_o):
    c = c_ref[...]
    a = (c * jax.nn.sigmoid(c)).astype(bf16)
    m = _dot(a, w_ref[...].astype(bf16)) + b_ref[0]
    nb = op_ref.shape[1]
    op_ref[0] = m[:nb]
    os_ref[0] = m[nb:]
    w_in_o[...] = w_in_ref[...].astype(bf16)
    w_out_o[...] = w_out_ref[...].astype(bf16)


def _mod_call(c_all, w_ada, b_ada, nb, w_in, w_out):
    n = c_all.shape[0]
    in_cols = w_in.shape[1] // N_MOD
    assert w_in.shape[1] == N_MOD * in_cols and in_cols % LANES == 0
    return pl.pallas_call(
        _mod_kernel,
        grid=(N_MOD,),
        in_specs=[
            pl.BlockSpec((n, D), lambda j: (0, 0)),
            pl.BlockSpec((D, D), lambda j: (0, j)),
            pl.BlockSpec((1, 1, D), lambda j: (j, 0, 0)),
            pl.BlockSpec((D, in_cols), lambda j: (0, j)),
            pl.BlockSpec((D, D), lambda j: (0, 0)),
        ],
        out_specs=[pl.BlockSpec((1, nb, D), lambda j: (j, 0, 0)),
                   pl.BlockSpec((1, n - nb, D), lambda j: (j, 0, 0)),
                   pl.BlockSpec((D, in_cols), lambda j: (0, j)),
                   pl.BlockSpec((D, D), lambda j: (0, 0))],
        out_shape=[jax.ShapeDtypeStruct((N_MOD, nb, D), f32),
                   jax.ShapeDtypeStruct((N_MOD, n - nb, D), f32),
                   jax.ShapeDtypeStruct(w_in.shape, bf16),
                   jax.ShapeDtypeStruct(w_out.shape, bf16)],
        compiler_params=pltpu.CompilerParams(
            dimension_semantics=("arbitrary",), vmem_limit_bytes=VMEM_LIMIT),
    )(c_all, w_ada, b_ada.reshape(N_MOD, 1, D), w_in, w_out)


def _route(logits):
    t = logits.shape[0]
    lane = lax.broadcasted_iota(jnp.int32, (t, LANES), 1)
    lane_f = lane.astype(f32)
    neg = -jnp.inf
    big = 1e9
    gmask = (lane >= GROUP_LANE0) & (lane < GROUP_LANE0 + N_GROUPS)
    gl = jnp.where(gmask, logits, neg)
    gmax = jnp.max(gl, axis=-1, keepdims=True)
    g_idx = jnp.min(jnp.where(gl == gmax, lane_f - GROUP_LANE0, big), axis=-1, keepdims=True)
    sumexp = jnp.sum(jnp.where(gmask, jnp.exp(gl - gmax), 0.0), axis=-1, keepdims=True)
    p_g = 1.0 / sumexp
    lane_grp = (lane >> 3).astype(f32)
    emask = (lane < N_EXPERTS) & (lane_grp == g_idx)
    el = jnp.where(emask, logits, neg)
    m1 = jnp.max(el, axis=-1, keepdims=True)
    i1 = jnp.min(jnp.where(el == m1, lane_f, big), axis=-1, keepdims=True)
    el2 = jnp.where(lane_f == i1, neg, el)
    m2 = jnp.max(el2, axis=-1, keepdims=True)
    i2 = jnp.min(jnp.where(el2 == m2, lane_f, big), axis=-1, keepdims=True)
    e = jnp.exp(m2 - m1)
    w1 = p_g / (1.0 + e)
    w2 = w1 * e

    def split3(w):
        hi = w.astype(bf16).astype(f32)
        mid = (w - hi).astype(bf16).astype(f32)
        lo = w - hi - mid
        return hi, mid, lo

    r3 = jnp.where(lane == GIDX_LANE, g_idx, 0.0)
    for idx, w in ((i1, w1), (i2, w2)):
        for part, wp in enumerate(split3(w)):
            r3 = r3 + jnp.where(lane_f == idx + float(part * N_EXPERTS), wp, 0.0)
    counts = jnp.sum(jnp.where(lane_f == g_idx, 1.0, 0.0), axis=0, keepdims=True)
    return r3.astype(bf16), jnp.broadcast_to(counts, (8, LANES))


def _merge_project(u, ga, gb, mix, y_b, w_out_ref):
    y_a = u * mix
    merged = jax.nn.sigmoid(ga) * y_a + jax.nn.sigmoid(gb) * y_b
    return _dot(merged.astype(bf16), w_out_ref[...])


def _residual_route(x, y, mods, vec_ref, w_r_ref, b_r_ref):
    sh1, sc1, gt1, sh2, sc2, gt2 = mods
    x1 = x + gt1 * _rms(y, vec_ref[1:2])
    h2 = _rms(x1, vec_ref[2:3] * (1.0 + sc2)) + sh2
    h2_hi = h2.astype(bf16)
    h2_lo = (h2 - h2_hi.astype(f32)).astype(bf16)
    r = _dot(h2_hi, w_r_ref[...]) + _dot(h2_lo, w_r_ref[...])
    logits = r[:, :LANES] + r[:, LANES:] + b_r_ref[...]
    r3, counts = _route(logits)
    return x1, h2_hi, r3, counts


def _in_proj(x, mods, vec_ref, w_in_ref, after_first_dot=lambda: None):
    sh1, sc1 = mods[0], mods[1]
    h = _rms(x, vec_ref[0:1] * (1.0 + sc1)) + sh1
    hb = h.astype(bf16)
    zu = _dot(hb, w_in_ref[:, 0:D])
    after_first_dot()
    zv = _dot(hb, w_in_ref[:, D:2 * D])
    p = _dot(hb, w_in_ref[:, 2 * D:2 * D + PW])
    ga = _dot(hb, w_in_ref[:, 2 * D + PW:3 * D + PW])
    gb = _dot(hb, w_in_ref[:, 3 * D + PW:4 * D + PW])
    return zu, zv, p, ga, gb


def _activate(zu, zv, vec_ref):
    u = _gelu(zu)
    v = _gelu(zv)
    mu = jnp.mean(v, axis=-1, keepdims=True)
    vc = v - mu
    var = jnp.mean(vc * vc, axis=-1, keepdims=True)
    v = vc * lax.rsqrt(var + EPS) * vec_ref[4:5] + vec_ref[5:6]
    return u, v


def _pool_out(d_groups, vec_ref, w_pool_ref):
    parts = [_dot(d.astype(bf16), w_pool_ref[gi].astype(bf16)) for gi, d in enumerate(d_groups)]
    return jnp.concatenate(parts, axis=1) * vec_ref[6:7]


def _stage1_prompt_kernel(x_ref, xprev_ref, mod_ref, vec_ref, w_in_ref, w_sp_ref, bias_ref,
                          w_pool_ref, w_out_ref, w_r_ref, b_r_ref, eg_ref, eu_ref, ed_ref,
                          x1_ref, h2_ref, r3_ref, cnt_ref, plast_ref, wg_ref, wu_ref, wd_ref,
                          pbuf, ybuf, *, tiles_per_seq):
    t = pl.program_id(0)
    last = pl.num_programs(0) - 1

    def mods_of(tile):
        b = tile // tiles_per_seq
        return [mod_ref[i, pl.ds(b, 1), :] for i in range(N_MOD)]

    def second_half(k):
        rows = slice(k * TL, (k + 1) * TL)
        tile = jnp.maximum((t - 1) * SUB + k, 0)
        x1, h2b, r3, counts = _residual_route(xprev_ref[rows, :], ybuf[k], mods_of(tile), vec_ref,
                                              w_r_ref, b_r_ref)
        x1_ref[rows, :] = x1
        h2_ref[rows, :] = h2b
        r3_ref[rows, :] = r3
        cnt_ref[k] = counts

    def cast_expert():
        wg_ref[...] = eg_ref[0].astype(bf16)
        wu_ref[...] = eu_ref[0].astype(bf16)
        wd_ref[...] = ed_ref[0].astype(bf16)

    def first_half(k, under_projection):
        tile = t * SUB + k
        s = tile % tiles_per_seq
        x = x_ref[k * TL:(k + 1) * TL, :]
        zu, zv, p, ga, gb = _in_proj(x, mods_of(tile), vec_ref, w_in_ref, under_projection)
        u, v = _activate(zu, zv, vec_ref)

        vb = v.astype(bf16)
        row = lax.broadcasted_iota(jnp.int32, (CHUNK, CHUNK), 0)
        col = lax.broadcasted_iota(jnp.int32, (CHUNK, CHUNK), 1)
        w_tril = [jnp.where(row >= col, w_sp_ref[hd], 0.0).astype(bf16) for hd in range(HEADS)]
        bias = bias_ref[...]
        chunks = []
        for c in range(TL // CHUNK):
            heads = [_dot(w_tril[hd],
                          vb[c * CHUNK:(c + 1) * CHUNK, hd * HEAD_DIM:(hd + 1) * HEAD_DIM])
                     for hd in range(HEADS)]
            chunks.append(jnp.concatenate(heads, axis=1) + bias)
        mix = jnp.concatenate(chunks, axis=0)

        carry = jnp.where(s == 0, 0.0, pbuf[...])
        ext = jnp.concatenate([carry, p], axis=0)
        pos = s * TL + lax.broadcasted_iota(jnp.int32, (TL, PG), 0)
        d_groups = []
        for gi, w in enumerate(WINDOWS):
            acc = ext[:, gi * PG:(gi + 1) * PG]
            shift = 1
            while shift < w:
                acc = acc + pltpu.roll(acc, shift, 0)
                shift *= 2
            cnt = jnp.minimum(pos + 1, w).astype(f32)
            d_groups.append(acc[W_MAX:] / cnt - p[:, gi * PG:(gi + 1) * PG])
        pbuf[...] = p[TL - W_MAX:]
        plast_ref[0] = p[TL - W_MAX:]
        y_b = _pool_out(d_groups, vec_ref, w_pool_ref)
        ybuf[k] = _merge_project(u, ga, gb, mix, y_b, w_out_ref)

    @pl.when(t == 0)
    def _():
        ybuf[...] = jnp.zeros_like(ybuf)
        pbuf[...] = jnp.zeros_like(pbuf)

    @pl.when(t < last)
    def _():
        for k in range(SUB):
            def under_projection(k=k):
                if k == 0:
                    cast_expert()
                second_half(k)
            first_half(k, under_projection)

    @pl.when(t == last)
    def _():
        for k in range(SUB):
            second_half(k)


def _stage1_sample_kernel(x_ref, mod_ref, vec_ref, w_in_ref, state_ref, w_pool_ref,
                          w_out_ref, w_r_ref, b_r_ref,
                          x1_ref, h2_ref, r3_ref, cnt_ref, p_ref, v_ref):
    x = x_ref[...]
    mods = [mod_ref[i] for i in range(N_MOD)]
    zu, zv, p, ga, gb = _in_proj(x, mods, vec_ref, w_in_ref)
    u, v = _activate(zu, zv, vec_ref)
    v_ref[...] = v
    p_ref[...] = p
    mix = v * vec_ref[8:9] + vec_ref[9:10]
    d_groups = []
    for gi, w in enumerate(WINDOWS):
        sl = slice(gi * PG, (gi + 1) * PG)
        acc = p[:, sl]
        for r in range(W_MAX - w, W_MAX - 1):
            acc = acc + state_ref[r][:, sl]
        d_groups.append(acc / float(w) - p[:, sl])
    y_b = _pool_out(d_groups, vec_ref, w_pool_ref)
    y = _merge_project(u, ga, gb, mix, y_b, w_out_ref)
    x1, h2b, r3, counts = _residual_route(x, y, mods, vec_ref, w_r_ref, b_r_ref)
    x1_ref[...] = x1
    h2_ref[...] = h2b
    r3_ref[...] = r3
    cnt_ref[...] = counts


def _const_spec(shape):
    nd = len(shape)
    return pl.BlockSpec(shape, lambda *_: (0,) * nd, pipeline_mode=pl.Buffered(1))


def _stage1_prompt_call(x, mod_p, vecs, w_in_b, w_sp, bias_full, w_pool_b, w_out_b, w_r, b_r,
                        w_gate, w_up, w_down):
    b, s, _ = x.shape
    ns = s // TL
    nt = b * ns
    steps = nt // SUB
    assert ns % SUB == 0
    assert steps >= N_EXPERTS
    x2 = x.reshape(b * s, D)
    rows = SUB * TL
    cur = lambda t: (jnp.minimum(t, steps - 1), 0)
    prev = lambda t: (jnp.maximum(t - 1, 0), 0)
    e_in, e_out, e_shapes = _expert_cast_specs(lambda t: jnp.minimum(t, N_EXPERTS - 1))
    return pl.pallas_call(
        functools.partial(_stage1_prompt_kernel, tiles_per_seq=ns),
        grid=(steps + 1,),
        in_specs=[
            pl.BlockSpec((rows, D), cur),
            pl.BlockSpec((rows, D), prev),
            _const_spec(mod_p.shape),
            _const_spec(vecs.shape),
            _const_spec(w_in_b.shape),
            _const_spec(w_sp.shape),
            _const_spec(bias_full.shape),
            _const_spec(w_pool_b.shape),
            _const_spec(w_out_b.shape),
            _const_spec(w_r.shape),
            _const_spec(b_r.shape),
        ] + e_in,
        out_specs=[
            pl.BlockSpec((rows, D), prev),
            pl.BlockSpec((rows, D), prev),
            pl.BlockSpec((rows, LANES), prev),
            pl.BlockSpec((SUB, 8, LANES), lambda t: (jnp.maximum(t - 1, 0), 0, 0)),
            pl.BlockSpec((1, W_MAX, PW),
                         lambda t: (jnp.minimum(t, steps - 1) * SUB // ns, 0, 0)),
        ] + e_out,
        out_shape=[
            jax.ShapeDtypeStruct((b * s, D), f32),
            jax.ShapeDtypeStruct((b * s, D), bf16),
            jax.ShapeDtypeStruct((b * s, LANES), bf16),
            jax.ShapeDtypeStruct((nt, 8, LANES), f32),
            jax.ShapeDtypeStruct((b, W_MAX, PW), f32),
        ] + e_shapes,
        scratch_shapes=[pltpu.VMEM((W_MAX, PW), f32), pltpu.VMEM((SUB, TL, D), f32)],
        compiler_params=pltpu.CompilerParams(
            dimension_semantics=("arbitrary",), vmem_limit_bytes=VMEM_LIMIT),
    )(x2, x2, mod_p, vecs, w_in_b, w_sp, bias_full, w_pool_b, w_out_b, w_r, b_r,
      w_gate, w_up, w_down)


def _stage1_sample_call(x, mod_s, vecs, w_in_b, state_t, w_pool_b, w_out_b, w_r, b_r):
    n = x.shape[0]
    return pl.pallas_call(
        _stage1_sample_kernel,
        out_shape=[
            jax.ShapeDtypeStruct((n, D), f32),
            jax.ShapeDtypeStruct((n, D), bf16),
            jax.ShapeDtypeStruct((n, LANES), bf16),
            jax.ShapeDtypeStruct((8, LANES), f32),
            jax.ShapeDtypeStruct((n, PW), f32),
            jax.ShapeDtypeStruct((n, D), f32),
        ],
        compiler_params=pltpu.CompilerParams(vmem_limit_bytes=VMEM_LIMIT),
    )(x, mod_s, vecs, w_in_b, state_t, w_pool_b, w_out_b, w_r, b_r)


def _moe_buffer_rows(ts, nt):
    worst = ts * nt + nt * N_GROUPS * (ROW_ALIGN - 1) + N_GROUPS * (RB - 1)
    return -(-worst // RB) * RB


def _moe_kernel(cnt_ref, h2_ref, r3_ref, x1_ref, mod_ref, g_ref, wg_ref, wu_ref, wd_ref, o_ref,
                hsbuf, hs_tile, ys_tile, pt_buf, pm_buf, tab,
                *, ts, nt, tpr, tps):
    w = pl.program_id(0)
    i = pl.program_id(1)
    rt = ts + LANES

    def copy_rows(src, src0, dst, dst0, nrows, ncols):
        def body(j, _):
            s0 = pl.multiple_of(src0 + j * ROW_ALIGN, ROW_ALIGN)
            d0 = pl.multiple_of(dst0 + j * ROW_ALIGN, ROW_ALIGN)
            dst[pl.ds(d0, ROW_ALIGN), :] = src[pl.ds(s0, ROW_ALIGN), :ncols]
            return 0
        lax.fori_loop(0, nrows // ROW_ALIGN, body, 0)

    @pl.when((w == 0) & (i == 0))
    def _():
        hsbuf[...] = jnp.zeros_like(hsbuf)
        ys_tile[...] = jnp.zeros_like(ys_tile)

    @pl.when(i == 0)
    def _sort_and_run_experts():
        def run_len(tile, g):
            c = cnt_ref[(w * nt + tile) * N_GROUPS + g]
            return ((c + (ROW_ALIGN - 1)) // ROW_ALIGN) * ROW_ALIGN

        lens = [[run_len(t, g) for g in range(N_GROUPS)] for t in range(nt)]
        region = [sum(lens[t][g] for t in range(nt)) for g in range(N_GROUPS)]
        region = [((r + (RB - 1)) // RB) * RB for r in region]
        base = [sum(region[:g]) for g in range(N_GROUPS)]
        offs = list(base)
        woff = []
        for t in range(nt):
            woff.append(list(offs))
            for g in range(N_GROUPS):
                tab[t * 2 * N_GROUPS + g] = lens[t][g]
                tab[t * 2 * N_GROUPS + N_GROUPS + g] = offs[g]
                offs[g] = offs[g] + lens[t][g]

        lane = lax.broadcasted_iota(jnp.int32, (ts, LANES), 1)
        lane_f = lane.astype(f32)
        r_i = lax.broadcasted_iota(jnp.int32, (ts, ts), 0)
        c_i = lax.broadcasted_iota(jnp.int32, (ts, ts), 1)
        ltri = jnp.where(r_i > c_i, 1.0, 0.0).astype(bf16)
        lane8 = lax.broadcasted_iota(jnp.int32, (8, LANES), 1)
        sel = jnp.where(lane8 == 0, float(ROW_ALIGN), jnp.where(lane8 == 1, 1.0, 0.0)).astype(bf16)
        rt_lane = lax.broadcasted_iota(jnp.int32, (ts, rt), 1).astype(f32)
        rt_sub = lax.broadcasted_iota(jnp.int32, (rt, ts), 0).astype(f32)

        tiles = range(nt)
        r3s = [r3_ref[t * ts:(t + 1) * ts, :] for t in tiles]
        gids = [jnp.sum(jnp.where(lane == GIDX_LANE, r3s[t].astype(f32), 0.0), axis=-1,
                        keepdims=True) for t in tiles]
        onehots = [jnp.where(lane_f == gids[t], 1.0, 0.0) for t in tiles]
        ranks = [_dot(ltri, onehots[t].astype(bf16)) for t in tiles]
        poss = []
        for t in tiles:
            seg = jnp.zeros((1, LANES), f32)
            start = 0
            for g in range(N_GROUPS):
                seg = seg + jnp.where(lane[0:1] == g, jnp.asarray(start, jnp.int32).astype(f32), 0.0)
                start = start + lens[t][g]
            poss.append(jnp.sum(onehots[t] * (ranks[t] + seg), axis=-1, keepdims=True))
        for t in tiles:
            pt_buf[t] = jnp.where(rt_lane == poss[t], 1.0, 0.0).astype(bf16)
        pos_rows = []
        for t in tiles:
            q = jnp.floor(poss[t] * (1.0 / ROW_ALIGN))
            digits = jnp.where(lane == 0, q, jnp.where(lane == 1, poss[t] - q * ROW_ALIGN, 0.0))
            pos_rows.append(lax.dot_general(sel, digits.astype(bf16), (((1,), (1,)), ((), ())),
                                            preferred_element_type=f32))
        for t in tiles:
            pm_buf[t] = jnp.where(rt_sub == pos_rows[t][0:1], 1.0, 0.0).astype(bf16)

        for t in tiles:
            p_mat = pm_buf[t]
            hs_tile[:, :D] = _dot(p_mat, h2_ref[t * ts:(t + 1) * ts, :]).astype(bf16)
            hs_tile[:, D:] = _dot(p_mat, r3_ref[t * ts:(t + 1) * ts, :]).astype(bf16)
            start = 0
            for g in range(N_GROUPS):
                copy_rows(hs_tile, start, hsbuf, woff[t][g], lens[t][g], D + LANES)
                start = start + lens[t][g]

        def expert_rows(g, r0, nrows):
            rows = hsbuf[pl.ds(r0, nrows), :D]
            gate = _dot(rows, wg_ref[g * D:(g + 1) * D, :])
            up = _dot(rows, wu_ref[g * D:(g + 1) * D, :])
            c3 = hsbuf[pl.ds(r0, nrows), D:].astype(f32)
            cw_lanes = c3 + pltpu.roll(c3, LANES - N_EXPERTS, 1) + pltpu.roll(c3, LANES - 2 * N_EXPERTS, 1)
            cw = jnp.concatenate(
                [jnp.broadcast_to(cw_lanes[:, g * EPG + j:g * EPG + j + 1], (nrows, D_EXPERT))
                 for j in range(EPG)], axis=1)
            act = gate * jax.nn.sigmoid(gate) * up * cw
            hsbuf[pl.ds(r0, nrows), :D] = _dot(
                act.astype(bf16), wd_ref[g * D:(g + 1) * D, :]).astype(bf16)

        for g in range(N_GROUPS):
            n_blocks = region[g] // RB

            def two_blocks(b, _, g=g):
                expert_rows(g, pl.multiple_of(base[g] + 2 * b * RB, RB), 2 * RB)
                return 0
            lax.fori_loop(0, n_blocks // 2, two_blocks, 0)

            @pl.when(n_blocks % 2 == 1)
            def _(g=g, n_blocks=n_blocks):
                expert_rows(g, pl.multiple_of(base[g] + (n_blocks - 1) * RB, RB), RB)

    tiles = [i * tps + k for k in range(tps)]
    for k, tile in enumerate(tiles):
        start = 0
        for g in range(N_GROUPS):
            ln = tab[tile * 2 * N_GROUPS + g]
            copy_rows(hsbuf, tab[tile * 2 * N_GROUPS + N_GROUPS + g], ys_tile.at[k], start, ln, D)
            start = start + ln
    fs = [_dot(pt_buf[tile], ys_tile[k]) for k, tile in enumerate(tiles)]
    for k, tile in enumerate(tiles):
        rows = slice(k * ts, (k + 1) * ts)
        tok0 = (w * nt + tile) * ts
        gt2_row = N_MOD - 1
        gt2 = (mod_ref[gt2_row, pl.ds(tok0, ts), :] if tpr == 1
               else mod_ref[gt2_row, pl.ds(tok0 // tpr, 1), :])
        o_ref[rows, :] = x1_ref[rows, :] + gt2 * _rms(fs[k], g_ref[...])


def _moe_call(cnt, x1, h2, r3, mod, g_post, wg, wu, wd, ts, nt, tpr):
    n = x1.shape[0]
    win = ts * nt
    rbuf = _moe_buffer_rows(ts, nt)
    rt = ts + LANES
    tps = 2 if nt % 2 == 0 else 1
    steps = nt // tps
    grid_spec = pltpu.PrefetchScalarGridSpec(
        num_scalar_prefetch=1,
        grid=(n // win, steps),
        in_specs=[
            pl.BlockSpec((win, D), lambda w, i, c: (w, 0)),
            pl.BlockSpec((win, LANES), lambda w, i, c: (w, 0)),
            pl.BlockSpec((tps * ts, D), lambda w, i, c: (w * steps + i, 0)),
            _const_spec(mod.shape),
            _const_spec(g_post.shape),
            _const_spec(wg.shape),
            _const_spec(wu.shape),
            _const_spec(wd.shape),
        ],
        out_specs=pl.BlockSpec((tps * ts, D), lambda w, i, c: (w * steps + i, 0)),
        scratch_shapes=[
            pltpu.VMEM((rbuf, D + LANES), bf16),
            pltpu.VMEM((rt, D + LANES), bf16),
            pltpu.VMEM((tps, rt, D), bf16),
            pltpu.VMEM((nt, ts, rt), bf16),
            pltpu.VMEM((nt, rt, ts), bf16),
            pltpu.SMEM((nt * 2 * N_GROUPS,), jnp.int32),
        ],
    )
    return pl.pallas_call(
        functools.partial(_moe_kernel, ts=ts, nt=nt, tpr=tpr, tps=tps),
        grid_spec=grid_spec,
        out_shape=jax.ShapeDtypeStruct((n, D), f32),
        compiler_params=pltpu.CompilerParams(
            dimension_semantics=("arbitrary", "arbitrary"), vmem_limit_bytes=VMEM_LIMIT),
    )(cnt, h2, r3, x1, mod, g_post, wg, wu, wd)


def _count_table(cnt):
    return cnt[:, 0, :N_GROUPS].astype(jnp.int32).reshape(-1)


def kernel(x_prompt, x_sample, c_prompt, c_sample, state_pool, w_ada, b_ada, g_pre_mix, g_post_mix, g_pre_ffn, g_post_ffn, w_in, ln_v_g, ln_v_b, w_spatial, b_spatial, w_pool, pool_scale, w_out, w_router_grp, b_router_grp, w_router_exp, b_router_exp, w_exp_gate, w_exp_up, w_exp_down):
    depth = w_in.shape[0]
    assert depth == 1
    b, s, _ = x_prompt.shape
    n_s = x_sample.shape[0]
    l = 0

    c_all = jnp.concatenate([c_prompt, c_sample], axis=0)
    mod_p, mod_s, w_in_b, w_out_b = _mod_call(c_all, w_ada[l], b_ada[l], b, w_in[l], w_out[l])

    ws, bs = w_spatial[l], b_spatial[l]
    zeros = jnp.zeros((D,), f32)
    vecs = jnp.stack([
        g_pre_mix[l], g_post_mix[l], g_pre_ffn[l], g_post_ffn[l], ln_v_g[l], ln_v_b[l],
        pool_scale[l], zeros,
        jnp.repeat(ws[:, 0, 0], HEAD_DIM), jnp.repeat(bs[:, 0], HEAD_DIM),
        zeros, zeros, zeros, zeros, zeros, zeros])
    bias_full = jnp.repeat(bs.T, HEAD_DIM, axis=1)
    pad = LANES - N_EXPERTS - N_GROUPS
    w_r = jnp.concatenate([w_router_exp[l], w_router_grp[l], jnp.zeros((D, pad), f32)], axis=1)
    w_r_hi = w_r.astype(bf16)
    w_r_lo = (w_r - w_r_hi.astype(f32)).astype(bf16)
    w_r2 = jnp.concatenate([w_r_hi, w_r_lo], axis=1)
    b_r = jnp.concatenate([b_router_exp[l], b_router_grp[l], jnp.zeros((pad,), f32)])[None]

    x1_p, h2_p, r3_p, cnt_p, plast, wg, wu, wd = _stage1_prompt_call(
        x_prompt, mod_p, vecs, w_in_b, ws, bias_full, w_pool[l], w_out_b, w_r2, b_r,
        w_exp_gate[l], w_exp_up[l], w_exp_down[l])
    state_t = jnp.transpose(state_pool[l], (1, 0, 2))
    x1_s, h2_s, r3_s, cnt_s, p_s, v_s = _stage1_sample_call(
        x_sample.reshape(n_s, D), mod_s, vecs, w_in_b, state_t, w_pool[l], w_out_b, w_r2, b_r)

    g_post = g_post_ffn[l].reshape(1, D)
    y_p = _moe_call(
        _count_table(cnt_p), x1_p.reshape(b * s, D), h2_p.reshape(b * s, D),
        r3_p.reshape(b * s, LANES), mod_p, g_post, wg, wu, wd, TL, MOE_WINDOW // TL, s)
    y_s = _moe_call(
        _count_table(cnt_s[None]), x1_s, h2_s, r3_s, mod_s, g_post, wg, wu, wd, n_s, 1, 1)

    state_pool_prompt = plast[:, 1:][None]
    state_pool_sample = jnp.concatenate([state_pool[l][:, 1:], p_s[:, None, :]], axis=1)[None]
    chunk_v_sample = v_s.reshape(1, n_s, 1, D)
    return (y_p.reshape(b, s, D), y_s.reshape(n_s, 1, D), state_pool_prompt,
            state_pool_sample, chunk_v_sample)
```

```python
import functools

import jax
import jax.numpy as jnp
from jax import lax
from jax.experimental import pallas as pl
from jax.experimental.pallas import tpu as pltpu

D = 1024
CHUNK = 128
HEADS = 8
HEAD_DIM = 128
WINDOWS = (2, 4, 8, 16)
PW = 512
PG = 128
W_MAX = 16
N_GROUPS = 4
EPG = 8
N_EXPERTS = 32
D_EXPERT = 128
EPS = 1e-6
N_MOD = 6
LANES = 128
GROUP_LANE0 = 32
GIDX_LANE = 96

TL = 256
SUB = 2
MOE_WINDOW = 2048
ROW_ALIGN = 16
RB = 128
VMEM_LIMIT = 60 * 1024 * 1024

bf16 = jnp.bfloat16
f32 = jnp.float32


def _rms(x, g):
    ms = jnp.mean(x * x, axis=-1, keepdims=True)
    return x * lax.rsqrt(ms + EPS) * g


def _dot(a, b):
    return jnp.dot(a, b, preferred_element_type=f32)


_GELU_C = 2.0 * 0.7978845608028654


def _gelu(x):
    t = x * ((-_GELU_C) + (-_GELU_C * 0.044715) * (x * x))
    return x / (1.0 + jnp.exp(t))


def _expert_cast_specs(step_to_expert):
    e = step_to_expert
    col_block = pl.BlockSpec((D, D_EXPERT), lambda t: (e(t) // EPG, e(t) % EPG))
    in_specs = [
        pl.BlockSpec((1, D, D_EXPERT), lambda t: (e(t), 0, 0)),
        pl.BlockSpec((1, D, D_EXPERT), lambda t: (e(t), 0, 0)),
        pl.BlockSpec((1, D_EXPERT, D), lambda t: (e(t), 0, 0)),
    ]
    out_specs = [col_block, col_block, pl.BlockSpec((D_EXPERT, D), lambda t: (e(t), 0))]
    out_shapes = [
        jax.ShapeDtypeStruct((N_GROUPS * D, EPG * D_EXPERT), bf16),
        jax.ShapeDtypeStruct((N_GROUPS * D, EPG * D_EXPERT), bf16),
        jax.ShapeDtypeStruct((N_EXPERTS * D_EXPERT, D), bf16),
    ]
    return in_specs, out_specs, out_shapes


def _mod_kernel(c_ref, w_ref, b_ref, w_in_ref, w_out_ref, op_ref, os_ref, w_in_o, w_out_o):
    c = c_ref[...]
    a = (c * jax.nn.sigmoid(c)).astype(bf16)
    m = _dot(a, w_ref[...].astype(bf16)) + b_ref[0]
    nb = op_ref.shape[1]
    op_ref[0] = m[:nb]
    os_ref[0] = m[nb:]
    w_in_o[...] = w_in_ref[...].astype(bf16)
    w_out_o[...] = w_out_ref[...].astype(bf16)


def _mod_call(c_all, w_ada, b_ada, nb, w_in, w_out):
    n = c_all.shape[0]
    in_cols = w_in.shape[1] // N_MOD
    assert w_in.shape[1] == N_MOD * in_cols and in_cols % LANES == 0
    return pl.pallas_call(
        _mod_kernel,
        grid=(N_MOD,),
        in_specs=[
            pl.BlockSpec((n, D), lambda j: (0, 0)),
            pl.BlockSpec((D, D), lambda j: (0, j)),
            pl.BlockSpec((1, 1, D), lambda j: (j, 0, 0)),
            pl.BlockSpec((D, in_cols), lambda j: (0, j)),
            pl.BlockSpec((D, D), lambda j: (0, 0)),
        ],
        out_specs=[pl.BlockSpec((1, nb, D), lambda j: (j, 0, 0)),
                   pl.BlockSpec((1, n - nb, D), lambda j: (j, 0, 0)),
                   pl.BlockSpec((D, in_cols), lambda j: (0, j)),
                   pl.BlockSpec((D, D), lambda j: (0, 0))],
        out_shape=[jax.ShapeDtypeStruct((N_MOD, nb, D), f32),
                   jax.ShapeDtypeStruct((N_MOD, n - nb, D), f32),
                   jax.ShapeDtypeStruct(w_in.shape, bf16),
                   jax.ShapeDtypeStruct(w_out.shape, bf16)],
        compiler_params=pltpu.CompilerParams(
            dimension_semantics=("arbitrary",), vmem_limit_bytes=VMEM_LIMIT),
    )(c_all, w_ada, b_ada.reshape(N_MOD, 1, D), w_in, w_out)


def _route(logits):
    t = logits.shape[0]
    lane = lax.broadcasted_iota(jnp.int32, (t, LANES), 1)
    lane_f = lane.astype(f32)
    neg = -jnp.inf
    big = 1e9
    gmask = (lane >= GROUP_LANE0) & (lane < GROUP_LANE0 + N_GROUPS)
    gl = jnp.where(gmask, logits, neg)
    gmax = jnp.max(gl, axis=-1, keepdims=True)
    g_idx = jnp.min(jnp.where(gl == gmax, lane_f - GROUP_LANE0, big), axis=-1, keepdims=True)
    sumexp = jnp.sum(jnp.where(gmask, jnp.exp(gl - gmax), 0.0), axis=-1, keepdims=True)
    p_g = 1.0 / sumexp
    lane_grp = (lane >> 3).astype(f32)
    emask = (lane < N_EXPERTS) & (lane_grp == g_idx)
    el = jnp.where(emask, logits, neg)
    m1 = jnp.max(el, axis=-1, keepdims=True)
    i1 = jnp.min(jnp.where(el == m1, lane_f, big), axis=-1, keepdims=True)
    el2 = jnp.where(lane_f == i1, neg, el)
    m2 = jnp.max(el2, axis=-1, keepdims=True)
    i2 = jnp.min(jnp.where(el2 == m2, lane_f, big), axis=-1, keepdims=True)
    e = jnp.exp(m2 - m1)
    w1 = p_g / (1.0 + e)
    w2 = w1 * e

    def split3(w):
        hi = w.astype(bf16).astype(f32)
        mid = (w - hi).astype(bf16).astype(f32)
        lo = w - hi - mid
        return hi, mid, lo

    r3 = jnp.where(lane == GIDX_LANE, g_idx, 0.0)
    for idx, w in ((i1, w1), (i2, w2)):
        for part, wp in enumerate(split3(w)):
            r3 = r3 + jnp.where(lane_f == idx + float(part * N_EXPERTS), wp, 0.0)
    counts = jnp.sum(jnp.where(lane_f == g_idx, 1.0, 0.0), axis=0, keepdims=True)
    return r3.astype(bf16), jnp.broadcast_to(counts, (8, LANES))


def _merge_project(u, ga, gb, mix, y_b, w_out_ref):
    y_a = u * mix
    merged = jax.nn.sigmoid(ga) * y_a + jax.nn.sigmoid(gb) * y_b
    return _dot(merged.astype(bf16), w_out_ref[...])


def _residual_route(x, y, mods, vec_ref, w_r_ref, b_r_ref):
    sh1, sc1, gt1, sh2, sc2, gt2 = mods
    x1 = x + gt1 * _rms(y, vec_ref[1:2])
    h2 = _rms(x1, vec_ref[2:3] * (1.0 + sc2)) + sh2
    h2_hi = h2.astype(bf16)
    h2_lo = (h2 - h2_hi.astype(f32)).astype(bf16)
    r = _dot(h2_hi, w_r_ref[...]) + _dot(h2_lo, w_r_ref[...])
    logits = r[:, :LANES] + r[:, LANES:] + b_r_ref[...]
    r3, counts = _route(logits)
    return x1, h2_hi, r3, counts


def _in_proj(x, mods, vec_ref, w_in_ref, after_first_dot=lambda: None):
    sh1, sc1 = mods[0], mods[1]
    h = _rms(x, vec_ref[0:1] * (1.0 + sc1)) + sh1
    hb = h.astype(bf16)
    zu = _dot(hb, w_in_ref[:, 0:D])
    after_first_dot()
    zv = _dot(hb, w_in_ref[:, D:2 * D])
    p = _dot(hb, w_in_ref[:, 2 * D:2 * D + PW])
    ga = _dot(hb, w_in_ref[:, 2 * D + PW:3 * D + PW])
    gb = _dot(hb, w_in_ref[:, 3 * D + PW:4 * D + PW])
    return zu, zv, p, ga, gb


def _activate(zu, zv, vec_ref):
    u = _gelu(zu)
    v = _gelu(zv)
    mu = jnp.mean(v, axis=-1, keepdims=True)
    vc = v - mu
    var = jnp.mean(vc * vc, axis=-1, keepdims=True)
    v = vc * lax.rsqrt(var + EPS) * vec_ref[4:5] + vec_ref[5:6]
    return u, v


def _pool_out(d_groups, vec_ref, w_pool_ref):
    parts = [_dot(d.astype(bf16), w_pool_ref[gi].astype(bf16)) for gi, d in enumerate(d_groups)]
    return jnp.concatenate(parts, axis=1) * vec_ref[6:7]


def _stage1_prompt_kernel(x_ref, xprev_ref, mod_ref, vec_ref, w_in_ref, w_sp_ref, bias_ref,
                          w_pool_ref, w_out_ref, w_r_ref, b_r_ref, eg_ref, eu_ref, ed_ref,
                          x1_ref, h2_ref, r3_ref, cnt_ref, plast_ref, wg_ref, wu_ref, wd_ref,
                          pbuf, ybuf, *, tiles_per_seq):
    t = pl.program_id(0)
    last = pl.num_programs(0) - 1

    def mods_of(tile):
        b = tile // tiles_per_seq
        return [mod_ref[i, pl.ds(b, 1), :] for i in range(N_MOD)]

    def second_half(k):
        rows = slice(k * TL, (k + 1) * TL)
        tile = jnp.maximum((t - 1) * SUB + k, 0)
        x1, h2b, r3, counts = _residual_route(xprev_ref[rows, :], ybuf[k], mods_of(tile), vec_ref,
                                              w_r_ref, b_r_ref)
        x1_ref[rows, :] = x1
        h2_ref[rows, :] = h2b
        r3_ref[rows, :] = r3
        cnt_ref[k] = counts

    def cast_expert():
        wg_ref[...] = eg_ref[0].astype(bf16)
        wu_ref[...] = eu_ref[0].astype(bf16)
        wd_ref[...] = ed_ref[0].astype(bf16)

    def first_half(k, under_projection):
        tile = t * SUB + k
        s = tile % tiles_per_seq
        x = x_ref[k * TL:(k + 1) * TL, :]
        zu, zv, p, ga, gb = _in_proj(x, mods_of(tile), vec_ref, w_in_ref, under_projection)
        u, v = _activate(zu, zv, vec_ref)

        vb = v.astype(bf16)
        row = lax.broadcasted_iota(jnp.int32, (CHUNK, CHUNK), 0)
        col = lax.broadcasted_iota(jnp.int32, (CHUNK, CHUNK), 1)
        w_tril = [jnp.where(row >= col, w_sp_ref[hd], 0.0).astype(bf16) for hd in range(HEADS)]
        bias = bias_ref[...]
        chunks = []
        for c in range(TL // CHUNK):
            heads = [_dot(w_tril[hd],
                          vb[c * CHUNK:(c + 1) * CHUNK, hd * HEAD_DIM:(hd + 1) * HEAD_DIM])
                     for hd in range(HEADS)]
            chunks.append(jnp.concatenate(heads, axis=1) + bias)
        mix = jnp.concatenate(chunks, axis=0)

        carry = jnp.where(s == 0, 0.0, pbuf[...])
        ext = jnp.concatenate([carry, p], axis=0)
        pos = s * TL + lax.broadcasted_iota(jnp.int32, (TL, PG), 0)
        d_groups = []
        for gi, w in enumerate(WINDOWS):
            acc = ext[:, gi * PG:(gi + 1) * PG]
            shift = 1
            while shift < w:
                acc = acc + pltpu.roll(acc, shift, 0)
                shift *= 2
            cnt = jnp.minimum(pos + 1, w).astype(f32)
            d_groups.append(acc[W_MAX:] / cnt - p[:, gi * PG:(gi + 1) * PG])
        pbuf[...] = p[TL - W_MAX:]
        plast_ref[0] = p[TL - W_MAX:]
        y_b = _pool_out(d_groups, vec_ref, w_pool_ref)
        ybuf[k] = _merge_project(u, ga, gb, mix, y_b, w_out_ref)

    @pl.when(t == 0)
    def _():
        ybuf[...] = jnp.zeros_like(ybuf)
        pbuf[...] = jnp.zeros_like(pbuf)

    @pl.when(t < last)
    def _():
        for k in range(SUB):
            def under_projection(k=k):
                if k == 0:
                    cast_expert()
                second_half(k)
            first_half(k, under_projection)

    @pl.when(t == last)
    def _():
        for k in range(SUB):
            second_half(k)


def _stage1_sample_kernel(x_ref, mod_ref, vec_ref, w_in_ref, state_ref, w_pool_ref,
                          w_out_ref, w_r_ref, b_r_ref,
                          x1_ref, h2_ref, r3_ref, cnt_ref, p_ref, v_ref):
    x = x_ref[...]
    mods = [mod_ref[i] for i in range(N_MOD)]
    zu, zv, p, ga, gb = _in_proj(x, mods, vec_ref, w_in_ref)
    u, v = _activate(zu, zv, vec_ref)
    v_ref[...] = v
    p_ref[...] = p
    mix = v * vec_ref[8:9] + vec_ref[9:10]
    d_groups = []
    for gi, w in enumerate(WINDOWS):
        sl = slice(gi * PG, (gi + 1) * PG)
        acc = p[:, sl]
        for r in range(W_MAX - w, W_MAX - 1):
            acc = acc + state_ref[r][:, sl]
        d_groups.append(acc / float(w) - p[:, sl])
    y_b = _pool_out(d_groups, vec_ref, w_pool_ref)
    y = _merge_project(u, ga, gb, mix, y_b, w_out_ref)
    x1, h2b, r3, counts = _residual_route(x, y, mods, vec_ref, w_r_ref, b_r_ref)
    x1_ref[...] = x1
    h2_ref[...] = h2b
    r3_ref[...] = r3
    cnt_ref[...] = counts


def _const_spec(shape):
    nd = len(shape)
    return pl.BlockSpec(shape, lambda *_: (0,) * nd, pipeline_mode=pl.Buffered(1))


def _stage1_prompt_call(x, mod_p, vecs, w_in_b, w_sp, bias_full, w_pool_b, w_out_b, w_r, b_r,
                        w_gate, w_up, w_down):
    b, s, _ = x.shape
    ns = s // TL
    nt = b * ns
    steps = nt // SUB
    assert ns % SUB == 0
    assert steps >= N_EXPERTS
    x2 = x.reshape(b * s, D)
    rows = SUB * TL
    cur = lambda t: (jnp.minimum(t, steps - 1), 0)
    prev = lambda t: (jnp.maximum(t - 1, 0), 0)
    e_in, e_out, e_shapes = _expert_cast_specs(lambda t: jnp.minimum(t, N_EXPERTS - 1))
    return pl.pallas_call(
        functools.partial(_stage1_prompt_kernel, tiles_per_seq=ns),
        grid=(steps + 1,),
        in_specs=[
            pl.BlockSpec((rows, D), cur),
            pl.BlockSpec((rows, D), prev),
            _const_spec(mod_p.shape),
            _const_spec(vecs.shape),
            _const_spec(w_in_b.shape),
            _const_spec(w_sp.shape),
            _const_spec(bias_full.shape),
            _const_spec(w_pool_b.shape),
            _const_spec(w_out_b.shape),
            _const_spec(w_r.shape),
            _const_spec(b_r.shape),
        ] + e_in,
        out_specs=[
            pl.BlockSpec((rows, D), prev),
            pl.BlockSpec((rows, D), prev),
            pl.BlockSpec((rows, LANES), prev),
            pl.BlockSpec((SUB, 8, LANES), lambda t: (jnp.maximum(t - 1, 0), 0, 0)),
            pl.BlockSpec((1, W_MAX, PW),
                         lambda t: (jnp.minimum(t, steps - 1) * SUB // ns, 0, 0)),
        ] + e_out,
        out_shape=[
            jax.ShapeDtypeStruct((b * s, D), f32),
            jax.ShapeDtypeStruct((b * s, D), bf16),
            jax.ShapeDtypeStruct((b * s, LANES), bf16),
            jax.ShapeDtypeStruct((nt, 8, LANES), f32),
            jax.ShapeDtypeStruct((b, W_MAX, PW), f32),
        ] + e_shapes,
        scratch_shapes=[pltpu.VMEM((W_MAX, PW), f32), pltpu.VMEM((SUB, TL, D), f32)],
        compiler_params=pltpu.CompilerParams(
            dimension_semantics=("arbitrary",), vmem_limit_bytes=VMEM_LIMIT),
    )(x2, x2, mod_p, vecs, w_in_b, w_sp, bias_full, w_pool_b, w_out_b, w_r, b_r,
      w_gate, w_up, w_down)


def _stage1_sample_call(x, mod_s, vecs, w_in_b, state_t, w_pool_b, w_out_b, w_r, b_r):
    n = x.shape[0]
    return pl.pallas_call(
        _stage1_sample_kernel,
        out_shape=[
            jax.ShapeDtypeStruct((n, D), f32),
            jax.ShapeDtypeStruct((n, D), bf16),
            jax.ShapeDtypeStruct((n, LANES), bf16),
            jax.ShapeDtypeStruct((8, LANES), f32),
            jax.ShapeDtypeStruct((n, PW), f32),
            jax.ShapeDtypeStruct((n, D), f32),
        ],
        compiler_params=pltpu.CompilerParams(vmem_limit_bytes=VMEM_LIMIT),
    )(x, mod_s, vecs, w_in_b, state_t, w_pool_b, w_out_b, w_r, b_r)


def _moe_buffer_rows(ts, nt):
    worst = ts * nt + nt * N_GROUPS * (ROW_ALIGN - 1) + N_GROUPS * (RB - 1)
    return -(-worst // RB) * RB


def _moe_kernel(cnt_ref, h2_ref, r3_ref, x1_ref, mod_ref, g_ref, wg_ref, wu_ref, wd_ref, o_ref,
                hsbuf, hs_tile, ys_tile, pt_buf, pm_buf, tab,
                *, ts, nt, tpr, tps):
    w = pl.program_id(0)
    i = pl.program_id(1)
    rt = ts + LANES

    def copy_rows(src, src0, dst, dst0, nrows, ncols):
        def body(j, _):
            s0 = pl.multiple_of(src0 + j * ROW_ALIGN, ROW_ALIGN)
            d0 = pl.multiple_of(dst0 + j * ROW_ALIGN, ROW_ALIGN)
            dst[pl.ds(d0, ROW_ALIGN), :] = src[pl.ds(s0, ROW_ALIGN), :ncols]
            return 0
        lax.fori_loop(0, nrows // ROW_ALIGN, body, 0)

    @pl.when((w == 0) & (i == 0))
    def _():
        hsbuf[...] = jnp.zeros_like(hsbuf)
        ys_tile[...] = jnp.zeros_like(ys_tile)

    @pl.when(i == 0)
    def _sort_and_run_experts():
        def run_len(tile, g):
            c = cnt_ref[(w * nt + tile) * N_GROUPS + g]
            return ((c + (ROW_ALIGN - 1)) // ROW_ALIGN) * ROW_ALIGN

        lens = [[run_len(t, g) for g in range(N_GROUPS)] for t in range(nt)]
        region = [sum(lens[t][g] for t in range(nt)) for g in range(N_GROUPS)]
        region = [((r + (RB - 1)) // RB) * RB for r in region]
        base = [sum(region[:g]) for g in range(N_GROUPS)]
        offs = list(base)
        woff = []
        for t in range(nt):
            woff.append(list(offs))
            for g in range(N_GROUPS):
                tab[t * 2 * N_GROUPS + g] = lens[t][g]
                tab[t * 2 * N_GROUPS + N_GROUPS + g] = offs[g]
                offs[g] = offs[g] + lens[t][g]

        lane = lax.broadcasted_iota(jnp.int32, (ts, LANES), 1)
        lane_f = lane.astype(f32)
        r_i = lax.broadcasted_iota(jnp.int32, (ts, ts), 0)
        c_i = lax.broadcasted_iota(jnp.int32, (ts, ts), 1)
        ltri = jnp.where(r_i > c_i, 1.0, 0.0).astype(bf16)
        lane8 = lax.broadcasted_iota(jnp.int32, (8, LANES), 1)
        sel = jnp.where(lane8 == 0, float(ROW_ALIGN), jnp.where(lane8 == 1, 1.0, 0.0)).astype(bf16)
        rt_lane = lax.broadcasted_iota(jnp.int32, (ts, rt), 1).astype(f32)
        rt_sub = lax.broadcasted_iota(jnp.int32, (rt, ts), 0).astype(f32)

        tiles = range(nt)
        r3s = [r3_ref[t * ts:(t + 1) * ts, :] for t in tiles]
        gids = [jnp.sum(jnp.where(lane == GIDX_LANE, r3s[t].astype(f32), 0.0), axis=-1,
                        keepdims=True) for t in tiles]
        onehots = [jnp.where(lane_f == gids[t], 1.0, 0.0) for t in tiles]
        ranks = [_dot(ltri, onehots[t].astype(bf16)) for t in tiles]
        poss = []
        for t in tiles:
            seg = jnp.zeros((1, LANES), f32)
            start = 0
            for g in range(N_GROUPS):
                seg = seg + jnp.where(lane[0:1] == g, jnp.asarray(start, jnp.int32).astype(f32), 0.0)
                start = start + lens[t][g]
            poss.append(jnp.sum(onehots[t] * (ranks[t] + seg), axis=-1, keepdims=True))
        for t in tiles:
            pt_buf[t] = jnp.where(rt_lane == poss[t], 1.0, 0.0).astype(bf16)
        pos_rows = []
        for t in tiles:
            q = jnp.floor(poss[t] * (1.0 / ROW_ALIGN))
            digits = jnp.where(lane == 0, q, jnp.where(lane == 1, poss[t] - q * ROW_ALIGN, 0.0))
            pos_rows.append(lax.dot_general(sel, digits.astype(bf16), (((1,), (1,)), ((), ())),
                                            preferred_element_type=f32))
        for t in tiles:
            pm_buf[t] = jnp.where(rt_sub == pos_rows[t][0:1], 1.0, 0.0).astype(bf16)

        for t0 in range(0, nt, tps):
            for k in range(tps):
                t = t0 + k
                p_mat = pm_buf[t]
                hs_tile[k, :, :D] = _dot(p_mat, h2_ref[t * ts:(t + 1) * ts, :]).astype(bf16)
                hs_tile[k, :, D:] = _dot(p_mat, r3_ref[t * ts:(t + 1) * ts, :]).astype(bf16)
            for k in range(tps):
                t = t0 + k
                start = 0
                for g in range(N_GROUPS):
                    copy_rows(hs_tile.at[k], start, hsbuf, woff[t][g], lens[t][g], D + LANES)
                    start = start + lens[t][g]

        def expert_rows(g, r0, nrows):
            rows = hsbuf[pl.ds(r0, nrows), :D]
            gate = _dot(rows, wg_ref[g * D:(g + 1) * D, :])
            up = _dot(rows, wu_ref[g * D:(g + 1) * D, :])
            c3 = hsbuf[pl.ds(r0, nrows), D:].astype(f32)
            cw_lanes = c3 + pltpu.roll(c3, LANES - N_EXPERTS, 1) + pltpu.roll(c3, LANES - 2 * N_EXPERTS, 1)
            cw = jnp.concatenate(
                [jnp.broadcast_to(cw_lanes[:, g * EPG + j:g * EPG + j + 1], (nrows, D_EXPERT))
                 for j in range(EPG)], axis=1)
            act = gate * jax.nn.sigmoid(gate) * up * cw
            hsbuf[pl.ds(r0, nrows), :D] = _dot(
                act.astype(bf16), wd_ref[g * D:(g + 1) * D, :]).astype(bf16)

        for g in range(N_GROUPS):
            n_blocks = region[g] // RB

            def two_blocks(b, _, g=g):
                expert_rows(g, pl.multiple_of(base[g] + 2 * b * RB, RB), 2 * RB)
                return 0
            lax.fori_loop(0, n_blocks // 2, two_blocks, 0)

            @pl.when(n_blocks % 2 == 1)
            def _(g=g, n_blocks=n_blocks):
                expert_rows(g, pl.multiple_of(base[g] + (n_blocks - 1) * RB, RB), RB)

    tiles = [i * tps + k for k in range(tps)]
    for k, tile in enumerate(tiles):
        start = 0
        for g in range(N_GROUPS):
            ln = tab[tile * 2 * N_GROUPS + g]
            copy_rows(hsbuf, tab[tile * 2 * N_GROUPS + N_GROUPS + g], ys_tile.at[k], start, ln, D)
            start = start + ln
    fs = [_dot(pt_buf[tile], ys_tile[k]) for k, tile in enumerate(tiles)]
    for k, tile in enumerate(tiles):
        rows = slice(k * ts, (k + 1) * ts)
        tok0 = (w * nt + tile) * ts
        gt2_row = N_MOD - 1
        gt2 = (mod_ref[gt2_row, pl.ds(tok0, ts), :] if tpr == 1
               else mod_ref[gt2_row, pl.ds(tok0 // tpr, 1), :])
        o_ref[rows, :] = x1_ref[rows, :] + gt2 * _rms(fs[k], g_ref[...])


def _moe_call(cnt, x1, h2, r3, mod, g_post, wg, wu, wd, ts, nt, tpr):
    n = x1.shape[0]
    win = ts * nt
    rbuf = _moe_buffer_rows(ts, nt)
    rt = ts + LANES
    tps = 2 if nt % 2 == 0 else 1
    steps = nt // tps
    grid_spec = pltpu.PrefetchScalarGridSpec(
        num_scalar_prefetch=1,
        grid=(n // win, steps),
        in_specs=[
            pl.BlockSpec((win, D), lambda w, i, c: (w, 0)),
            pl.BlockSpec((win, LANES), lambda w, i, c: (w, 0)),
            pl.BlockSpec((tps * ts, D), lambda w, i, c: (w * steps + i, 0)),
            _const_spec(mod.shape),
            _const_spec(g_post.shape),
            _const_spec(wg.shape),
            _const_spec(wu.shape),
            _const_spec(wd.shape),
        ],
        out_specs=pl.BlockSpec((tps * ts, D), lambda w, i, c: (w * steps + i, 0)),
        scratch_shapes=[
            pltpu.VMEM((rbuf, D + LANES), bf16),
            pltpu.VMEM((tps, rt, D + LANES), bf16),
            pltpu.VMEM((tps, rt, D), bf16),
            pltpu.VMEM((nt, ts, rt), bf16),
            pltpu.VMEM((nt, rt, ts), bf16),
            pltpu.SMEM((nt * 2 * N_GROUPS,), jnp.int32),
        ],
    )
    return pl.pallas_call(
        functools.partial(_moe_kernel, ts=ts, nt=nt, tpr=tpr, tps=tps),
        grid_spec=grid_spec,
        out_shape=jax.ShapeDtypeStruct((n, D), f32),
        compiler_params=pltpu.CompilerParams(
            dimension_semantics=("arbitrary", "arbitrary"), vmem_limit_bytes=VMEM_LIMIT),
    )(cnt, h2, r3, x1, mod, g_post, wg, wu, wd)


def _count_table(cnt):
    return cnt[:, 0, :N_GROUPS].astype(jnp.int32).reshape(-1)


def kernel(x_prompt, x_sample, c_prompt, c_sample, state_pool, w_ada, b_ada, g_pre_mix, g_post_mix, g_pre_ffn, g_post_ffn, w_in, ln_v_g, ln_v_b, w_spatial, b_spatial, w_pool, pool_scale, w_out, w_router_grp, b_router_grp, w_router_exp, b_router_exp, w_exp_gate, w_exp_up, w_exp_down):
    depth = w_in.shape[0]
    assert depth == 1
    b, s, _ = x_prompt.shape
    n_s = x_sample.shape[0]
    l = 0

    c_all = jnp.concatenate([c_prompt, c_sample], axis=0)
    mod_p, mod_s, w_in_b, w_out_b = _mod_call(c_all, w_ada[l], b_ada[l], b, w_in[l], w_out[l])

    ws, bs = w_spatial[l], b_spatial[l]
    zeros = jnp.zeros((D,), f32)
    vecs = jnp.stack([
        g_pre_mix[l], g_post_mix[l], g_pre_ffn[l], g_post_ffn[l], ln_v_g[l], ln_v_b[l],
        pool_scale[l], zeros,
        jnp.repeat(ws[:, 0, 0], HEAD_DIM), jnp.repeat(bs[:, 0], HEAD_DIM),
        zeros, zeros, zeros, zeros, zeros, zeros])
    bias_full = jnp.repeat(bs.T, HEAD_DIM, axis=1)
    pad = LANES - N_EXPERTS - N_GROUPS
    w_r = jnp.concatenate([w_router_exp[l], w_router_grp[l], jnp.zeros((D, pad), f32)], axis=1)
    w_r_hi = w_r.astype(bf16)
    w_r_lo = (w_r - w_r_hi.astype(f32)).astype(bf16)
    w_r2 = jnp.concatenate([w_r_hi, w_r_lo], axis=1)
    b_r = jnp.concatenate([b_router_exp[l], b_router_grp[l], jnp.zeros((pad,), f32)])[None]

    x1_p, h2_p, r3_p, cnt_p, plast, wg, wu, wd = _stage1_prompt_call(
        x_prompt, mod_p, vecs, w_in_b, ws, bias_full, w_pool[l], w_out_b, w_r2, b_r,
        w_exp_gate[l], w_exp_up[l], w_exp_down[l])
    state_t = jnp.transpose(state_pool[l], (1, 0, 2))
    x1_s, h2_s, r3_s, cnt_s, p_s, v_s = _stage1_sample_call(
        x_sample.reshape(n_s, D), mod_s, vecs, w_in_b, state_t, w_pool[l], w_out_b, w_r2, b_r)

    g_post = g_post_ffn[l].reshape(1, D)
    y_p = _moe_call(
        _count_table(cnt_p), x1_p.reshape(b * s, D), h2_p.reshape(b * s, D),
        r3_p.reshape(b * s, LANES), mod_p, g_post, wg, wu, wd, TL, MOE_WINDOW // TL, s)
    y_s = _moe_call(
        _count_table(cnt_s[None]), x1_s, h2_s, r3_s, mod_s, g_post, wg, wu, wd, n_s, 1, 1)

    state_pool_prompt = plast[:, 1:][None]
    state_pool_sample = jnp.concatenate([state_pool[l][:, 1:], p_s[:, None, :]], axis=1)[None]
    chunk_v_sample = v_s.reshape(1, n_s, 1, D)
    return (y_p.reshape(b, s, D), y_s.reshape(n_s, 1, D), state_pool_prompt,
            state_pool_sample, chunk_v_sample)
```

```python
import functools

import jax
import jax.numpy as jnp
from jax import lax
from jax.experimental import pallas as pl
from jax.experimental.pallas import tpu as pltpu

D = 1024
CHUNK = 128
HEADS = 8
HEAD_DIM = 128
WINDOWS = (2, 4, 8, 16)
PW = 512
PG = 128
W_MAX = 16
N_GROUPS = 4
EPG = 8
N_EXPERTS = 32
D_EXPERT = 128
EPS = 1e-6
N_MOD = 6
LANES = 128
GROUP_LANE0 = 32
GIDX_LANE = 96

TL = 256
SUB = 2
MOE_WINDOW = 2048
ROW_ALIGN = 16
RB = 128
VMEM_LIMIT = 60 * 1024 * 1024

bf16 = jnp.bfloat16
f32 = jnp.float32


def _rms(x, g):
    ms = jnp.mean(x * x, axis=-1, keepdims=True)
    return x * lax.rsqrt(ms + EPS) * g


def _dot(a, b):
    return jnp.dot(a, b, preferred_element_type=f32)


_GELU_C = 2.0 * 0.7978845608028654


def _gelu(x):
    t = x * ((-_GELU_C) + (-_GELU_C * 0.044715) * (x * x))
    return x / (1.0 + jnp.exp(t))


def _expert_cast_specs(step_to_expert):
    e = step_to_expert
    col_block = pl.BlockSpec((D, D_EXPERT), lambda t: (e(t) // EPG, e(t) % EPG))
    in_specs = [
        pl.BlockSpec((1, D, D_EXPERT), lambda t: (e(t), 0, 0)),
        pl.BlockSpec((1, D, D_EXPERT), lambda t: (e(t), 0, 0)),
        pl.BlockSpec((1, D_EXPERT, D), lambda t: (e(t), 0, 0)),
    ]
    out_specs = [col_block, col_block, pl.BlockSpec((D_EXPERT, D), lambda t: (e(t), 0))]
    out_shapes = [
        jax.ShapeDtypeStruct((N_GROUPS * D, EPG * D_EXPERT), bf16),
        jax.ShapeDtypeStruct((N_GROUPS * D, EPG * D_EXPERT), bf16),
        jax.ShapeDtypeStruct((N_EXPERTS * D_EXPERT, D), bf16),
    ]
    return in_specs, out_specs, out_shapes


def _mod_kernel(c_ref, w_ref, b_ref, w_in_ref, w_out_ref, op_ref, os_ref, w_in_o, w_out_o):
    c = c_ref[...]
    a = (c * jax.nn.sigmoid(c)).astype(bf16)
    m = _dot(a, w_ref[...].astype(bf16)) + b_ref[0]
    nb = op_ref.shape[1]
    op_ref[0] = m[:nb]
    os_ref[0] = m[nb:]
    w_in_o[...] = w_in_ref[...].astype(bf16)
    w_out_o[...] = w_out_ref[...].astype(bf16)


def _mod_call(c_all, w_ada, b_ada, nb, w_in, w_out):
    n = c_all.shape[0]
    in_cols = w_in.shape[1] // N_MOD
    assert w_in.shape[1] == N_MOD * in_cols and in_cols % LANES == 0
    return pl.pallas_call(
        _mod_kernel,
        grid=(N_MOD,),
        in_specs=[
            pl.BlockSpec((n, D), lambda j: (0, 0)),
            pl.BlockSpec((D, D), lambda j: (0, j)),
            pl.BlockSpec((1, 1, D), lambda j: (j, 0, 0)),
            pl.BlockSpec((D, in_cols), lambda j: (0, j)),
            pl.BlockSpec((D, D), lambda j: (0, 0)),
        ],
        out_specs=[pl.BlockSpec((1, nb, D), lambda j: (j, 0, 0)),
                   pl.BlockSpec((1, n - nb, D), lambda j: (j, 0, 0)),
                   pl.BlockSpec((D, in_cols), lambda j: (0, j)),
                   pl.BlockSpec((D, D), lambda j: (0, 0))],
        out_shape=[jax.ShapeDtypeStruct((N_MOD, nb, D), f32),
                   jax.ShapeDtypeStruct((N_MOD, n - nb, D), f32),
                   jax.ShapeDtypeStruct(w_in.shape, bf16),
                   jax.ShapeDtypeStruct(w_out.shape, bf16)],
        compiler_params=pltpu.CompilerParams(
            dimension_semantics=("arbitrary",), vmem_limit_bytes=VMEM_LIMIT),
    )(c_all, w_ada, b_ada.reshape(N_MOD, 1, D), w_in, w_out)


def _route(logits):
    t = logits.shape[0]
    lane = lax.broadcasted_iota(jnp.int32, (t, LANES), 1)
    lane_f = lane.astype(f32)
    neg = -jnp.inf
    big = 1e9
    gmask = (lane >= GROUP_LANE0) & (lane < GROUP_LANE0 + N_GROUPS)
    gl = jnp.where(gmask, logits, neg)
    gmax = jnp.max(gl, axis=-1, keepdims=True)
    g_idx = jnp.min(jnp.where(gl == gmax, lane_f - GROUP_LANE0, big), axis=-1, keepdims=True)
    sumexp = jnp.sum(jnp.where(gmask, jnp.exp(gl - gmax), 0.0), axis=-1, keepdims=True)
    p_g = 1.0 / sumexp
    lane_grp = (lane >> 3).astype(f32)
    emask = (lane < N_EXPERTS) & (lane_grp == g_idx)
    el = jnp.where(emask, logits, neg)
    m1 = jnp.max(el, axis=-1, keepdims=True)
    i1 = jnp.min(jnp.where(el == m1, lane_f, big), axis=-1, keepdims=True)
    el2 = jnp.where(lane_f == i1, neg, el)
    m2 = jnp.max(el2, axis=-1, keepdims=True)
    i2 = jnp.min(jnp.where(el2 == m2, lane_f, big), axis=-1, keepdims=True)
    e = jnp.exp(m2 - m1)
    w1 = p_g / (1.0 + e)
    w2 = w1 * e

    def split3(w):
        hi = w.astype(bf16).astype(f32)
        mid = (w - hi).astype(bf16).astype(f32)
        lo = w - hi - mid
        return hi, mid, lo

    r3 = jnp.where(lane == GIDX_LANE, g_idx, 0.0)
    for idx, w in ((i1, w1), (i2, w2)):
        for part, wp in enumerate(split3(w)):
            r3 = r3 + jnp.where(lane_f == idx + float(part * N_EXPERTS), wp, 0.0)
    counts = jnp.sum(jnp.where(lane_f == g_idx, 1.0, 0.0), axis=0, keepdims=True)
    return r3.astype(bf16), jnp.broadcast_to(counts, (8, LANES))


def _merge_project(u, ga, gb, mix, y_b, w_out_ref):
    y_a = u * mix
    merged = jax.nn.sigmoid(ga) * y_a + jax.nn.sigmoid(gb) * y_b
    return _dot(merged.astype(bf16), w_out_ref[...])


def _residual_route(x, y, mods, vec_ref, w_r_ref, b_r_ref):
    sh1, sc1, gt1, sh2, sc2, gt2 = mods
    x1 = x + gt1 * _rms(y, vec_ref[1:2])
    h2 = _rms(x1, vec_ref[2:3] * (1.0 + sc2)) + sh2
    h2_hi = h2.astype(bf16)
    h2_lo = (h2 - h2_hi.astype(f32)).astype(bf16)
    r = _dot(h2_hi, w_r_ref[...]) + _dot(h2_lo, w_r_ref[...])
    logits = r[:, :LANES] + r[:, LANES:] + b_r_ref[...]
    r3, counts = _route(logits)
    return x1, h2_hi, r3, counts


def _in_proj(x, mods, vec_ref, w_in_ref, after_first_dot=lambda: None):
    sh1, sc1 = mods[0], mods[1]
    h = _rms(x, vec_ref[0:1] * (1.0 + sc1)) + sh1
    hb = h.astype(bf16)
    zu = _dot(hb, w_in_ref[:, 0:D])
    after_first_dot()
    zv = _dot(hb, w_in_ref[:, D:2 * D])
    p = _dot(hb, w_in_ref[:, 2 * D:2 * D + PW])
    ga = _dot(hb, w_in_ref[:, 2 * D + PW:3 * D + PW])
    gb = _dot(hb, w_in_ref[:, 3 * D + PW:4 * D + PW])
    return zu, zv, p, ga, gb


def _activate(zu, zv, vec_ref):
    u = _gelu(zu)
    v = _gelu(zv)
    mu = jnp.mean(v, axis=-1, keepdims=True)
    vc = v - mu
    var = jnp.mean(vc * vc, axis=-1, keepdims=True)
    v = vc * lax.rsqrt(var + EPS) * vec_ref[4:5] + vec_ref[5:6]
    return u, v


def _pool_out(d_groups, vec_ref, w_pool_ref):
    parts = [_dot(d.astype(bf16), w_pool_ref[gi].astype(bf16)) for gi, d in enumerate(d_groups)]
    return jnp.concatenate(parts, axis=1) * vec_ref[6:7]


def _stage1_kernel(x_ref, xprev_ref, mod_ref, vec_ref, w_in_ref, w_sp_ref, bias_ref,
                   w_pool_ref, w_out_ref, w_r_ref, b_r_ref, eg_ref, eu_ref, ed_ref,
                   xs_ref, mods_ref, states_ref,
                   x1_ref, h2_ref, r3_ref, cnt_ref, plast_ref, wg_ref, wu_ref, wd_ref,
                   x1s_ref, h2s_ref, r3s_ref, cnts_ref, ps_ref, vs_ref,
                   pbuf, ybuf, *, tiles_per_seq):
    t = pl.program_id(0)
    last = pl.num_programs(0) - 1

    def mods_of(tile):
        b = tile // tiles_per_seq
        return [mod_ref[i, pl.ds(b, 1), :] for i in range(N_MOD)]

    def second_half(k):
        rows = slice(k * TL, (k + 1) * TL)
        tile = jnp.maximum((t - 1) * SUB + k, 0)
        x1, h2b, r3, counts = _residual_route(xprev_ref[rows, :], ybuf[k], mods_of(tile), vec_ref,
                                              w_r_ref, b_r_ref)
        x1_ref[rows, :] = x1
        h2_ref[rows, :] = h2b
        r3_ref[rows, :] = r3
        cnt_ref[k] = counts

    def cast_expert():
        wg_ref[...] = eg_ref[0].astype(bf16)
        wu_ref[...] = eu_ref[0].astype(bf16)
        wd_ref[...] = ed_ref[0].astype(bf16)

    def first_half(k, under_projection):
        tile = t * SUB + k
        s = tile % tiles_per_seq
        x = x_ref[k * TL:(k + 1) * TL, :]
        zu, zv, p, ga, gb = _in_proj(x, mods_of(tile), vec_ref, w_in_ref, under_projection)
        u, v = _activate(zu, zv, vec_ref)

        vb = v.astype(bf16)
        row = lax.broadcasted_iota(jnp.int32, (CHUNK, CHUNK), 0)
        col = lax.broadcasted_iota(jnp.int32, (CHUNK, CHUNK), 1)
        w_tril = [jnp.where(row >= col, w_sp_ref[hd], 0.0).astype(bf16) for hd in range(HEADS)]
        bias = bias_ref[...]
        chunks = []
        for c in range(TL // CHUNK):
            heads = [_dot(w_tril[hd],
                          vb[c * CHUNK:(c + 1) * CHUNK, hd * HEAD_DIM:(hd + 1) * HEAD_DIM])
                     for hd in range(HEADS)]
            chunks.append(jnp.concatenate(heads, axis=1) + bias)
        mix = jnp.concatenate(chunks, axis=0)

        carry = jnp.where(s == 0, 0.0, pbuf[...])
        ext = jnp.concatenate([carry, p], axis=0)
        pos = s * TL + lax.broadcasted_iota(jnp.int32, (TL, PG), 0)
        d_groups = []
        for gi, w in enumerate(WINDOWS):
            acc = ext[:, gi * PG:(gi + 1) * PG]
            shift = 1
            while shift < w:
                acc = acc + pltpu.roll(acc, shift, 0)
                shift *= 2
            cnt = jnp.minimum(pos + 1, w).astype(f32)
            d_groups.append(acc[W_MAX:] / cnt - p[:, gi * PG:(gi + 1) * PG])
        pbuf[...] = p[TL - W_MAX:]
        plast_ref[0] = p[TL - W_MAX:]
        y_b = _pool_out(d_groups, vec_ref, w_pool_ref)
        ybuf[k] = _merge_project(u, ga, gb, mix, y_b, w_out_ref)

    @pl.when(t == 0)
    def _():
        ybuf[...] = jnp.zeros_like(ybuf)
        pbuf[...] = jnp.zeros_like(pbuf)

    @pl.when(t < last)
    def _():
        for k in range(SUB):
            def under_projection(k=k):
                if k == 0:
                    cast_expert()
                second_half(k)
            first_half(k, under_projection)

    @pl.when(t == last)
    def _():
        for k in range(SUB):
            second_half(k)
        _sample_tokens(xs_ref, mods_ref, vec_ref, w_in_ref, states_ref, w_pool_ref, w_out_ref,
                       w_r_ref, b_r_ref, x1s_ref, h2s_ref, r3s_ref, cnts_ref, ps_ref, vs_ref)


def _sample_tokens(x_ref, mod_ref, vec_ref, w_in_ref, state_ref, w_pool_ref,
                   w_out_ref, w_r_ref, b_r_ref,
                   x1_ref, h2_ref, r3_ref, cnt_ref, p_ref, v_ref):
    x = x_ref[...]
    mods = [mod_ref[i] for i in range(N_MOD)]
    zu, zv, p, ga, gb = _in_proj(x, mods, vec_ref, w_in_ref)
    u, v = _activate(zu, zv, vec_ref)
    v_ref[...] = v
    p_ref[...] = p
    mix = v * vec_ref[8:9] + vec_ref[9:10]
    d_groups = []
    for gi, w in enumerate(WINDOWS):
        sl = slice(gi * PG, (gi + 1) * PG)
        acc = p[:, sl]
        for r in range(W_MAX - w, W_MAX - 1):
            acc = acc + state_ref[r][:, sl]
        d_groups.append(acc / float(w) - p[:, sl])
    y_b = _pool_out(d_groups, vec_ref, w_pool_ref)
    y = _merge_project(u, ga, gb, mix, y_b, w_out_ref)
    x1, h2b, r3, counts = _residual_route(x, y, mods, vec_ref, w_r_ref, b_r_ref)
    x1_ref[...] = x1
    h2_ref[...] = h2b
    r3_ref[...] = r3
    cnt_ref[...] = counts


def _const_spec(shape):
    nd = len(shape)
    return pl.BlockSpec(shape, lambda *_: (0,) * nd, pipeline_mode=pl.Buffered(1))


def _stage1_call(x, mod_p, vecs, w_in_b, w_sp, bias_full, w_pool_b, w_out_b, w_r, b_r,
                 w_gate, w_up, w_down, x_s, mod_s, state_t):
    b, s, _ = x.shape
    n_s = x_s.shape[0]
    const_out = lambda shape: pl.BlockSpec(shape, lambda t: (0,) * len(shape))
    ns = s // TL
    nt = b * ns
    steps = nt // SUB
    assert ns % SUB == 0
    assert steps >= N_EXPERTS
    x2 = x.reshape(b * s, D)
    rows = SUB * TL
    cur = lambda t: (jnp.minimum(t, steps - 1), 0)
    prev = lambda t: (jnp.maximum(t - 1, 0), 0)
    e_in, e_out, e_shapes = _expert_cast_specs(lambda t: jnp.minimum(t, N_EXPERTS - 1))
    return pl.pallas_call(
        functools.partial(_stage1_kernel, tiles_per_seq=ns),
        grid=(steps + 1,),
        in_specs=[
            pl.BlockSpec((rows, D), cur),
            pl.BlockSpec((rows, D), prev),
            _const_spec(mod_p.shape),
            _const_spec(vecs.shape),
            _const_spec(w_in_b.shape),
            _const_spec(w_sp.shape),
            _const_spec(bias_full.shape),
            _const_spec(w_pool_b.shape),
            _const_spec(w_out_b.shape),
            _const_spec(w_r.shape),
            _const_spec(b_r.shape),
        ] + e_in + [_const_spec(x_s.shape), _const_spec(mod_s.shape), _const_spec(state_t.shape)],
        out_specs=[
            pl.BlockSpec((rows, D), prev),
            pl.BlockSpec((rows, D), prev),
            pl.BlockSpec((rows, LANES), prev),
            pl.BlockSpec((SUB, 8, LANES), lambda t: (jnp.maximum(t - 1, 0), 0, 0)),
            pl.BlockSpec((1, W_MAX, PW),
                         lambda t: (jnp.minimum(t, steps - 1) * SUB // ns, 0, 0)),
        ] + e_out + [const_out((n_s, D)), const_out((n_s, D)), const_out((n_s, LANES)),
                     const_out((8, LANES)), const_out((n_s, PW)), const_out((n_s, D))],
        out_shape=[
            jax.ShapeDtypeStruct((b * s, D), f32),
            jax.ShapeDtypeStruct((b * s, D), bf16),
            jax.ShapeDtypeStruct((b * s, LANES), bf16),
            jax.ShapeDtypeStruct((nt, 8, LANES), f32),
            jax.ShapeDtypeStruct((b, W_MAX, PW), f32),
        ] + e_shapes + [
            jax.ShapeDtypeStruct((n_s, D), f32),
            jax.ShapeDtypeStruct((n_s, D), bf16),
            jax.ShapeDtypeStruct((n_s, LANES), bf16),
            jax.ShapeDtypeStruct((8, LANES), f32),
            jax.ShapeDtypeStruct((n_s, PW), f32),
            jax.ShapeDtypeStruct((n_s, D), f32),
        ],
        scratch_shapes=[pltpu.VMEM((W_MAX, PW), f32), pltpu.VMEM((SUB, TL, D), f32)],
        compiler_params=pltpu.CompilerParams(
            dimension_semantics=("arbitrary",), vmem_limit_bytes=VMEM_LIMIT),
    )(x2, x2, mod_p, vecs, w_in_b, w_sp, bias_full, w_pool_b, w_out_b, w_r, b_r,
      w_gate, w_up, w_down, x_s, mod_s, state_t)


def _moe_buffer_rows(ts, nt):
    worst = ts * nt + nt * N_GROUPS * (ROW_ALIGN - 1) + N_GROUPS * (RB - 1)
    return -(-worst // RB) * RB


def _moe_kernel(cnt_ref, h2_ref, r3_ref, x1_ref, mod_ref, g_ref, wg_ref, wu_ref, wd_ref, o_ref,
                hsbuf, hs_tile, ys_tile, pt_buf, pm_buf, tab,
                *, ts, nt, tpr, tps):
    w = pl.program_id(0)
    i = pl.program_id(1)
    rt = ts + LANES

    def copy_rows(src, src0, dst, dst0, nrows, ncols):
        def body(j, _):
            s0 = pl.multiple_of(src0 + j * ROW_ALIGN, ROW_ALIGN)
            d0 = pl.multiple_of(dst0 + j * ROW_ALIGN, ROW_ALIGN)
            dst[pl.ds(d0, ROW_ALIGN), :] = src[pl.ds(s0, ROW_ALIGN), :ncols]
            return 0
        lax.fori_loop(0, nrows // ROW_ALIGN, body, 0)

    @pl.when((w == 0) & (i == 0))
    def _():
        hsbuf[...] = jnp.zeros_like(hsbuf)
        ys_tile[...] = jnp.zeros_like(ys_tile)

    @pl.when(i == 0)
    def _sort_and_run_experts():
        def run_len(tile, g):
            c = cnt_ref[(w * nt + tile) * N_GROUPS + g]
            return ((c + (ROW_ALIGN - 1)) // ROW_ALIGN) * ROW_ALIGN

        lens = [[run_len(t, g) for g in range(N_GROUPS)] for t in range(nt)]
        region = [sum(lens[t][g] for t in range(nt)) for g in range(N_GROUPS)]
        region = [((r + (RB - 1)) // RB) * RB for r in region]
        base = [sum(region[:g]) for g in range(N_GROUPS)]
        offs = list(base)
        woff = []
        for t in range(nt):
            woff.append(list(offs))
            for g in range(N_GROUPS):
                tab[t * 2 * N_GROUPS + g] = lens[t][g]
                tab[t * 2 * N_GROUPS + N_GROUPS + g] = offs[g]
                offs[g] = offs[g] + lens[t][g]

        lane = lax.broadcasted_iota(jnp.int32, (ts, LANES), 1)
        lane_f = lane.astype(f32)
        r_i = lax.broadcasted_iota(jnp.int32, (ts, ts), 0)
        c_i = lax.broadcasted_iota(jnp.int32, (ts, ts), 1)
        ltri = jnp.where(r_i > c_i, 1.0, 0.0).astype(bf16)
        lane8 = lax.broadcasted_iota(jnp.int32, (8, LANES), 1)
        sel = jnp.where(lane8 == 0, float(ROW_ALIGN), jnp.where(lane8 == 1, 1.0, 0.0)).astype(bf16)
        rt_lane = lax.broadcasted_iota(jnp.int32, (ts, rt), 1).astype(f32)
        rt_sub = lax.broadcasted_iota(jnp.int32, (rt, ts), 0).astype(f32)

        tiles = range(nt)
        r3s = [r3_ref[t * ts:(t + 1) * ts, :] for t in tiles]
        gids = [jnp.sum(jnp.where(lane == GIDX_LANE, r3s[t].astype(f32), 0.0), axis=-1,
                        keepdims=True) for t in tiles]
        onehots = [jnp.where(lane_f == gids[t], 1.0, 0.0) for t in tiles]
        ranks = [_dot(ltri, onehots[t].astype(bf16)) for t in tiles]
        poss = []
        for t in tiles:
            seg = jnp.zeros((1, LANES), f32)
            start = 0
            for g in range(N_GROUPS):
                seg = seg + jnp.where(lane[0:1] == g, jnp.asarray(start, jnp.int32).astype(f32), 0.0)
                start = start + lens[t][g]
            poss.append(jnp.sum(onehots[t] * (ranks[t] + seg), axis=-1, keepdims=True))
        for t in tiles:
            pt_buf[t] = jnp.where(rt_lane == poss[t], 1.0, 0.0).astype(bf16)
        pos_rows = []
        for t in tiles:
            q = jnp.floor(poss[t] * (1.0 / ROW_ALIGN))
            digits = jnp.where(lane == 0, q, jnp.where(lane == 1, poss[t] - q * ROW_ALIGN, 0.0))
            pos_rows.append(lax.dot_general(sel, digits.astype(bf16), (((1,), (1,)), ((), ())),
                                            preferred_element_type=f32))
        for t in tiles:
            pm_buf[t] = jnp.where(rt_sub == pos_rows[t][0:1], 1.0, 0.0).astype(bf16)

        for t0 in range(0, nt, tps):
            for k in range(tps):
                t = t0 + k
                p_mat = pm_buf[t]
                hs_tile[k, :, :D] = _dot(p_mat, h2_ref[t * ts:(t + 1) * ts, :]).astype(bf16)
                hs_tile[k, :, D:] = _dot(p_mat, r3_ref[t * ts:(t + 1) * ts, :]).astype(bf16)
            for k in range(tps):
                t = t0 + k
                start = 0
                for g in range(N_GROUPS):
                    copy_rows(hs_tile.at[k], start, hsbuf, woff[t][g], lens[t][g], D + LANES)
                    start = start + lens[t][g]

        def expert_rows(g, r0, nrows):
            rows = hsbuf[pl.ds(r0, nrows), :D]
            gate = _dot(rows, wg_ref[g * D:(g + 1) * D, :])
            up = _dot(rows, wu_ref[g * D:(g + 1) * D, :])
            c3 = hsbuf[pl.ds(r0, nrows), D:].astype(f32)
            cw_lanes = c3 + pltpu.roll(c3, LANES - N_EXPERTS, 1) + pltpu.roll(c3, LANES - 2 * N_EXPERTS, 1)
            cw = jnp.concatenate(
                [jnp.broadcast_to(cw_lanes[:, g * EPG + j:g * EPG + j + 1], (nrows, D_EXPERT))
                 for j in range(EPG)], axis=1)
            act = gate * jax.nn.sigmoid(gate) * up * cw
            hsbuf[pl.ds(r0, nrows), :D] = _dot(
                act.astype(bf16), wd_ref[g * D:(g + 1) * D, :]).astype(bf16)

        for g in range(N_GROUPS):
            n_blocks = region[g] // RB

            def two_blocks(b, _, g=g):
                expert_rows(g, pl.multiple_of(base[g] + 2 * b * RB, RB), 2 * RB)
                return 0
            lax.fori_loop(0, n_blocks // 2, two_blocks, 0)

            @pl.when(n_blocks % 2 == 1)
            def _(g=g, n_blocks=n_blocks):
                expert_rows(g, pl.multiple_of(base[g] + (n_blocks - 1) * RB, RB), RB)

    tiles = [i * tps + k for k in range(tps)]
    for k, tile in enumerate(tiles):
        start = 0
        for g in range(N_GROUPS):
            ln = tab[tile * 2 * N_GROUPS + g]
            copy_rows(hsbuf, tab[tile * 2 * N_GROUPS + N_GROUPS + g], ys_tile.at[k], start, ln, D)
            start = start + ln
    fs = [_dot(pt_buf[tile], ys_tile[k]) for k, tile in enumerate(tiles)]
    for k, tile in enumerate(tiles):
        rows = slice(k * ts, (k + 1) * ts)
        tok0 = (w * nt + tile) * ts
        gt2_row = N_MOD - 1
        gt2 = (mod_ref[gt2_row, pl.ds(tok0, ts), :] if tpr == 1
               else mod_ref[gt2_row, pl.ds(tok0 // tpr, 1), :])
        o_ref[rows, :] = x1_ref[rows, :] + gt2 * _rms(fs[k], g_ref[...])


def _moe_call(cnt, x1, h2, r3, mod, g_post, wg, wu, wd, ts, nt, tpr):
    n = x1.shape[0]
    win = ts * nt
    rbuf = _moe_buffer_rows(ts, nt)
    rt = ts + LANES
    tps = 2 if nt % 2 == 0 else 1
    steps = nt // tps
    grid_spec = pltpu.PrefetchScalarGridSpec(
        num_scalar_prefetch=1,
        grid=(n // win, steps),
        in_specs=[
            pl.BlockSpec((win, D), lambda w, i, c: (w, 0)),
            pl.BlockSpec((win, LANES), lambda w, i, c: (w, 0)),
            pl.BlockSpec((tps * ts, D), lambda w, i, c: (w * steps + i, 0)),
            _const_spec(mod.shape),
            _const_spec(g_post.shape),
            _const_spec(wg.shape),
            _const_spec(wu.shape),
            _const_spec(wd.shape),
        ],
        out_specs=pl.BlockSpec((tps * ts, D), lambda w, i, c: (w * steps + i, 0)),
        scratch_shapes=[
            pltpu.VMEM((rbuf, D + LANES), bf16),
            pltpu.VMEM((tps, rt, D + LANES), bf16),
            pltpu.VMEM((tps, rt, D), bf16),
            pltpu.VMEM((nt, ts, rt), bf16),
            pltpu.VMEM((nt, rt, ts), bf16),
            pltpu.SMEM((nt * 2 * N_GROUPS,), jnp.int32),
        ],
    )
    return pl.pallas_call(
        functools.partial(_moe_kernel, ts=ts, nt=nt, tpr=tpr, tps=tps),
        grid_spec=grid_spec,
        out_shape=jax.ShapeDtypeStruct((n, D), f32),
        compiler_params=pltpu.CompilerParams(
            dimension_semantics=("arbitrary", "arbitrary"), vmem_limit_bytes=VMEM_LIMIT),
    )(cnt, h2, r3, x1, mod, g_post, wg, wu, wd)


def _count_table(cnt):
    return cnt[:, 0, :N_GROUPS].astype(jnp.int32).reshape(-1)


def kernel(x_prompt, x_sample, c_prompt, c_sample, state_pool, w_ada, b_ada, g_pre_mix, g_post_mix, g_pre_ffn, g_post_ffn, w_in, ln_v_g, ln_v_b, w_spatial, b_spatial, w_pool, pool_scale, w_out, w_router_grp, b_router_grp, w_router_exp, b_router_exp, w_exp_gate, w_exp_up, w_exp_down):
    depth = w_in.shape[0]
    assert depth == 1
    b, s, _ = x_prompt.shape
    n_s = x_sample.shape[0]
    l = 0

    c_all = jnp.concatenate([c_prompt, c_sample], axis=0)
    mod_p, mod_s, w_in_b, w_out_b = _mod_call(c_all, w_ada[l], b_ada[l], b, w_in[l], w_out[l])

    ws, bs = w_spatial[l], b_spatial[l]
    zeros = jnp.zeros((D,), f32)
    vecs = jnp.stack([
        g_pre_mix[l], g_post_mix[l], g_pre_ffn[l], g_post_ffn[l], ln_v_g[l], ln_v_b[l],
        pool_scale[l], zeros,
        jnp.repeat(ws[:, 0, 0], HEAD_DIM), jnp.repeat(bs[:, 0], HEAD_DIM),
        zeros, zeros, zeros, zeros, zeros, zeros])
    bias_full = jnp.repeat(bs.T, HEAD_DIM, axis=1)
    pad = LANES - N_EXPERTS - N_GROUPS
    w_r = jnp.concatenate([w_router_exp[l], w_router_grp[l], jnp.zeros((D, pad), f32)], axis=1)
    w_r_hi = w_r.astype(bf16)
    w_r_lo = (w_r - w_r_hi.astype(f32)).astype(bf16)
    w_r2 = jnp.concatenate([w_r_hi, w_r_lo], axis=1)
    b_r = jnp.concatenate([b_router_exp[l], b_router_grp[l], jnp.zeros((pad,), f32)])[None]

    state_t = jnp.transpose(state_pool[l], (1, 0, 2))
    (x1_p, h2_p, r3_p, cnt_p, plast, wg, wu, wd,
     x1_s, h2_s, r3_s, cnt_s, p_s, v_s) = _stage1_call(
        x_prompt, mod_p, vecs, w_in_b, ws, bias_full, w_pool[l], w_out_b, w_r2, b_r,
        w_exp_gate[l], w_exp_up[l], w_exp_down[l], x_sample.reshape(n_s, D), mod_s, state_t)

    g_post = g_post_ffn[l].reshape(1, D)
    y_p = _moe_call(
        _count_table(cnt_p), x1_p.reshape(b * s, D), h2_p.reshape(b * s, D),
        r3_p.reshape(b * s, LANES), mod_p, g_post, wg, wu, wd, TL, MOE_WINDOW // TL, s)
    y_s = _moe_call(
        _count_table(cnt_s[None]), x1_s, h2_s, r3_s, mod_s, g_post, wg, wu, wd, n_s, 1, 1)

    state_pool_prompt = plast[:, 1:][None]
    state_pool_sample = jnp.concatenate([state_pool[l][:, 1:], p_s[:, None, :]], axis=1)[None]
    chunk_v_sample = v_s.reshape(1, n_s, 1, D)
    return (y_p.reshape(b, s, D), y_s.reshape(n_s, 1, D), state_pool_prompt,
            state_pool_sample, chunk_v_sample)
```

```python
import functools

import jax
import jax.numpy as jnp
from jax import lax
from jax.experimental import pallas as pl
from jax.experimental.pallas import tpu as pltpu

D = 1024
CHUNK = 128
HEADS = 8
HEAD_DIM = 128
WINDOWS = (2, 4, 8, 16)
PW = 512
PG = 128
W_MAX = 16
N_GROUPS = 4
EPG = 8
N_EXPERTS = 32
D_EXPERT = 128
EPS = 1e-6
N_MOD = 6
LANES = 128
GROUP_LANE0 = 32
GIDX_LANE = 96

TL = 256
SUB = 2
MOE_WINDOW = 2048
ROW_ALIGN = 16
RB = 128
VMEM_LIMIT = 60 * 1024 * 1024

bf16 = jnp.bfloat16
f32 = jnp.float32


def _rms(x, g):
    ms = jnp.mean(x * x, axis=-1, keepdims=True)
    return x * lax.rsqrt(ms + EPS) * g


def _dot(a, b):
    return jnp.dot(a, b, preferred_element_type=f32)


_GELU_C = 2.0 * 0.7978845608028654


def _gelu(x):
    t = x * ((-_GELU_C) + (-_GELU_C * 0.044715) * (x * x))
    return x / (1.0 + jnp.exp(t))


def _expert_cast_specs(step_to_expert):
    e = step_to_expert
    col_block = pl.BlockSpec((D, D_EXPERT), lambda t: (e(t) // EPG, e(t) % EPG))
    in_specs = [
        pl.BlockSpec((1, D, D_EXPERT), lambda t: (e(t), 0, 0)),
        pl.BlockSpec((1, D, D_EXPERT), lambda t: (e(t), 0, 0)),
        pl.BlockSpec((1, D_EXPERT, D), lambda t: (e(t), 0, 0)),
    ]
    out_specs = [col_block, col_block, pl.BlockSpec((D_EXPERT, D), lambda t: (e(t), 0))]
    out_shapes = [
        jax.ShapeDtypeStruct((N_GROUPS * D, EPG * D_EXPERT), bf16),
        jax.ShapeDtypeStruct((N_GROUPS * D, EPG * D_EXPERT), bf16),
        jax.ShapeDtypeStruct((N_EXPERTS * D_EXPERT, D), bf16),
    ]
    return in_specs, out_specs, out_shapes


def _mod_kernel(cp_ref, cs_ref, w_ref, b_ref, w_in_ref, w_out_ref, op_ref, os_ref, w_in_o, w_out_o):
    w = w_ref[...].astype(bf16)
    for c_ref, o_ref in ((cp_ref, op_ref), (cs_ref, os_ref)):
        c = c_ref[...]
        o_ref[0] = _dot((c * jax.nn.sigmoid(c)).astype(bf16), w) + b_ref[0]
    w_in_o[...] = w_in_ref[...].astype(bf16)
    w_out_o[...] = w_out_ref[...].astype(bf16)


def _mod_call(c_p, c_s, w_ada, b_ada, w_in, w_out):
    nb, n_s = c_p.shape[0], c_s.shape[0]
    in_cols = w_in.shape[1] // N_MOD
    assert w_in.shape[1] == N_MOD * in_cols and in_cols % LANES == 0
    return pl.pallas_call(
        _mod_kernel,
        grid=(N_MOD,),
        in_specs=[
            pl.BlockSpec((nb, D), lambda j: (0, 0)),
            pl.BlockSpec((n_s, D), lambda j: (0, 0)),
            pl.BlockSpec((D, D), lambda j: (0, j)),
            pl.BlockSpec((1, 1, D), lambda j: (j, 0, 0)),
            pl.BlockSpec((D, in_cols), lambda j: (0, j)),
            pl.BlockSpec((D, D), lambda j: (0, 0)),
        ],
        out_specs=[pl.BlockSpec((1, nb, D), lambda j: (j, 0, 0)),
                   pl.BlockSpec((1, n_s, D), lambda j: (j, 0, 0)),
                   pl.BlockSpec((D, in_cols), lambda j: (0, j)),
                   pl.BlockSpec((D, D), lambda j: (0, 0))],
        out_shape=[jax.ShapeDtypeStruct((N_MOD, nb, D), f32),
                   jax.ShapeDtypeStruct((N_MOD, n_s, D), f32),
                   jax.ShapeDtypeStruct(w_in.shape, bf16),
                   jax.ShapeDtypeStruct(w_out.shape, bf16)],
        compiler_params=pltpu.CompilerParams(
            dimension_semantics=("arbitrary",), vmem_limit_bytes=VMEM_LIMIT),
    )(c_p, c_s, w_ada, b_ada.reshape(N_MOD, 1, D), w_in, w_out)


def _route(logits):
    t = logits.shape[0]
    lane = lax.broadcasted_iota(jnp.int32, (t, LANES), 1)
    lane_f = lane.astype(f32)
    neg = -jnp.inf
    big = 1e9
    gmask = (lane >= GROUP_LANE0) & (lane < GROUP_LANE0 + N_GROUPS)
    gl = jnp.where(gmask, logits, neg)
    gmax = jnp.max(gl, axis=-1, keepdims=True)
    g_idx = jnp.min(jnp.where(gl == gmax, lane_f - GROUP_LANE0, big), axis=-1, keepdims=True)
    sumexp = jnp.sum(jnp.where(gmask, jnp.exp(gl - gmax), 0.0), axis=-1, keepdims=True)
    p_g = 1.0 / sumexp
    lane_grp = (lane >> 3).astype(f32)
    emask = (lane < N_EXPERTS) & (lane_grp == g_idx)
    el = jnp.where(emask, logits, neg)
    m1 = jnp.max(el, axis=-1, keepdims=True)
    i1 = jnp.min(jnp.where(el == m1, lane_f, big), axis=-1, keepdims=True)
    el2 = jnp.where(lane_f == i1, neg, el)
    m2 = jnp.max(el2, axis=-1, keepdims=True)
    i2 = jnp.min(jnp.where(el2 == m2, lane_f, big), axis=-1, keepdims=True)
    e = jnp.exp(m2 - m1)
    w1 = p_g / (1.0 + e)
    w2 = w1 * e

    def split3(w):
        hi = w.astype(bf16).astype(f32)
        mid = (w - hi).astype(bf16).astype(f32)
        lo = w - hi - mid
        return hi, mid, lo

    r3 = jnp.where(lane == GIDX_LANE, g_idx, 0.0)
    for idx, w in ((i1, w1), (i2, w2)):
        for part, wp in enumerate(split3(w)):
            r3 = r3 + jnp.where(lane_f == idx + float(part * N_EXPERTS), wp, 0.0)
    counts = jnp.sum(jnp.where(lane_f == g_idx, 1.0, 0.0), axis=0, keepdims=True)
    return r3.astype(bf16), jnp.broadcast_to(counts, (8, LANES))


def _merge_project(u, ga, gb, mix, y_b, w_out_ref):
    y_a = u * mix
    merged = jax.nn.sigmoid(ga) * y_a + jax.nn.sigmoid(gb) * y_b
    return _dot(merged.astype(bf16), w_out_ref[...])


def _residual_route(x, y, mods, vec_ref, w_r_ref, b_r_ref):
    sh1, sc1, gt1, sh2, sc2, gt2 = mods
    x1 = x + gt1 * _rms(y, vec_ref[1:2])
    h2 = _rms(x1, vec_ref[2:3] * (1.0 + sc2)) + sh2
    h2_hi = h2.astype(bf16)
    h2_lo = (h2 - h2_hi.astype(f32)).astype(bf16)
    r = _dot(h2_hi, w_r_ref[...]) + _dot(h2_lo, w_r_ref[...])
    logits = r[:, :LANES] + r[:, LANES:] + b_r_ref[...]
    r3, counts = _route(logits)
    return x1, h2_hi, r3, counts


def _in_proj(x, mods, vec_ref, w_in_ref, after_first_dot=lambda: None):
    sh1, sc1 = mods[0], mods[1]
    h = _rms(x, vec_ref[0:1] * (1.0 + sc1)) + sh1
    hb = h.astype(bf16)
    zu = _dot(hb, w_in_ref[:, 0:D])
    after_first_dot()
    zv = _dot(hb, w_in_ref[:, D:2 * D])
    p = _dot(hb, w_in_ref[:, 2 * D:2 * D + PW])
    ga = _dot(hb, w_in_ref[:, 2 * D + PW:3 * D + PW])
    gb = _dot(hb, w_in_ref[:, 3 * D + PW:4 * D + PW])
    return zu, zv, p, ga, gb


def _activate(zu, zv, vec_ref):
    u = _gelu(zu)
    v = _gelu(zv)
    mu = jnp.mean(v, axis=-1, keepdims=True)
    vc = v - mu
    var = jnp.mean(vc * vc, axis=-1, keepdims=True)
    v = vc * lax.rsqrt(var + EPS) * vec_ref[4:5] + vec_ref[5:6]
    return u, v


def _pool_out(d_groups, vec_ref, w_pool_ref):
    parts = [_dot(d.astype(bf16), w_pool_ref[gi].astype(bf16)) for gi, d in enumerate(d_groups)]
    return jnp.concatenate(parts, axis=1) * vec_ref[6:7]


def _stage1_kernel(x_ref, xprev_ref, mod_ref, vec_ref, w_in_ref, w_sp_ref, bias_ref,
                   w_pool_ref, w_out_ref, w_r_ref, b_r_ref, eg_ref, eu_ref, ed_ref,
                   xs_ref, mods_ref, states_ref,
                   x1_ref, h2_ref, r3_ref, cnt_ref, plast_ref, wg_ref, wu_ref, wd_ref,
                   x1s_ref, h2s_ref, r3s_ref, cnts_ref, ps_ref, vs_ref,
                   pbuf, ybuf, *, tiles_per_seq):
    t = pl.program_id(0)
    last = pl.num_programs(0) - 1

    def mods_of(tile):
        b = tile // tiles_per_seq
        return [mod_ref[i, pl.ds(b, 1), :] for i in range(N_MOD)]

    def second_half(k):
        rows = slice(k * TL, (k + 1) * TL)
        tile = jnp.maximum((t - 1) * SUB + k, 0)
        x1, h2b, r3, counts = _residual_route(xprev_ref[rows, :], ybuf[k], mods_of(tile), vec_ref,
                                              w_r_ref, b_r_ref)
        x1_ref[rows, :] = x1
        h2_ref[rows, :] = h2b
        r3_ref[rows, :] = r3
        cnt_ref[k] = counts

    def cast_expert():
        wg_ref[...] = eg_ref[0].astype(bf16)
        wu_ref[...] = eu_ref[0].astype(bf16)
        wd_ref[...] = ed_ref[0].astype(bf16)

    def first_half(k, under_projection):
        tile = t * SUB + k
        s = tile % tiles_per_seq
        x = x_ref[k * TL:(k + 1) * TL, :]
        zu, zv, p, ga, gb = _in_proj(x, mods_of(tile), vec_ref, w_in_ref, under_projection)
        u, v = _activate(zu, zv, vec_ref)

        vb = v.astype(bf16)
        row = lax.broadcasted_iota(jnp.int32, (CHUNK, CHUNK), 0)
        col = lax.broadcasted_iota(jnp.int32, (CHUNK, CHUNK), 1)
        w_tril = [jnp.where(row >= col, w_sp_ref[hd], 0.0).astype(bf16) for hd in range(HEADS)]
        bias = bias_ref[...]
        chunks = []
        for c in range(TL // CHUNK):
            heads = [_dot(w_tril[hd],
                          vb[c * CHUNK:(c + 1) * CHUNK, hd * HEAD_DIM:(hd + 1) * HEAD_DIM])
                     for hd in range(HEADS)]
            chunks.append(jnp.concatenate(heads, axis=1) + bias)
        mix = jnp.concatenate(chunks, axis=0)

        carry = jnp.where(s == 0, 0.0, pbuf[...])
        ext = jnp.concatenate([carry, p], axis=0)
        pos = s * TL + lax.broadcasted_iota(jnp.int32, (TL, PG), 0)
        d_groups = []
        for gi, w in enumerate(WINDOWS):
            acc = ext[:, gi * PG:(gi + 1) * PG]
            shift = 1
            while shift < w:
                acc = acc + pltpu.roll(acc, shift, 0)
                shift *= 2
            cnt = jnp.minimum(pos + 1, w).astype(f32)
            d_groups.append(acc[W_MAX:] / cnt - p[:, gi * PG:(gi + 1) * PG])
        pbuf[...] = p[TL - W_MAX:]
        plast_ref[0] = p[TL - W_MAX:]
        y_b = _pool_out(d_groups, vec_ref, w_pool_ref)
        ybuf[k] = _merge_project(u, ga, gb, mix, y_b, w_out_ref)

    @pl.when(t == 0)
    def _():
        ybuf[...] = jnp.zeros_like(ybuf)
        pbuf[...] = jnp.zeros_like(pbuf)

    @pl.when(t < last)
    def _():
        for k in range(SUB):
            def under_projection(k=k):
                if k == 0:
                    cast_expert()
                second_half(k)
            first_half(k, under_projection)

    @pl.when(t == last)
    def _():
        for k in range(SUB):
            second_half(k)
        _sample_tokens(xs_ref, mods_ref, vec_ref, w_in_ref, states_ref, w_pool_ref, w_out_ref,
                       w_r_ref, b_r_ref, x1s_ref, h2s_ref, r3s_ref, cnts_ref, ps_ref, vs_ref)


def _sample_tokens(x_ref, mod_ref, vec_ref, w_in_ref, state_ref, w_pool_ref,
                   w_out_ref, w_r_ref, b_r_ref,
                   x1_ref, h2_ref, r3_ref, cnt_ref, p_ref, v_ref):
    x = x_ref[...]
    mods = [mod_ref[i] for i in range(N_MOD)]
    zu, zv, p, ga, gb = _in_proj(x, mods, vec_ref, w_in_ref)
    u, v = _activate(zu, zv, vec_ref)
    v_ref[...] = v
    p_ref[...] = p
    mix = v * vec_ref[8:9] + vec_ref[9:10]
    d_groups = []
    for gi, w in enumerate(WINDOWS):
        sl = slice(gi * PG, (gi + 1) * PG)
        acc = p[:, sl]
        for r in range(W_MAX - w, W_MAX - 1):
            acc = acc + state_ref[:, r, sl]
        d_groups.append(acc / float(w) - p[:, sl])
    y_b = _pool_out(d_groups, vec_ref, w_pool_ref)
    y = _merge_project(u, ga, gb, mix, y_b, w_out_ref)
    x1, h2b, r3, counts = _residual_route(x, y, mods, vec_ref, w_r_ref, b_r_ref)
    x1_ref[...] = x1
    h2_ref[...] = h2b
    r3_ref[...] = r3
    cnt_ref[...] = counts


def _const_spec(shape):
    nd = len(shape)
    return pl.BlockSpec(shape, lambda *_: (0,) * nd, pipeline_mode=pl.Buffered(1))


def _stage1_call(x, mod_p, vecs, w_in_b, w_sp, bias_full, w_pool_b, w_out_b, w_r, b_r,
                 w_gate, w_up, w_down, x_s, mod_s, state_t):
    b, s, _ = x.shape
    n_s = x_s.shape[0]
    const_out = lambda shape: pl.BlockSpec(shape, lambda t: (0,) * len(shape))
    ns = s // TL
    nt = b * ns
    steps = nt // SUB
    assert ns % SUB == 0
    assert steps >= N_EXPERTS
    x2 = x.reshape(b * s, D)
    rows = SUB * TL
    cur = lambda t: (jnp.minimum(t, steps - 1), 0)
    prev = lambda t: (jnp.maximum(t - 1, 0), 0)
    e_in, e_out, e_shapes = _expert_cast_specs(lambda t: jnp.minimum(t, N_EXPERTS - 1))
    return pl.pallas_call(
        functools.partial(_stage1_kernel, tiles_per_seq=ns),
        grid=(steps + 1,),
        in_specs=[
            pl.BlockSpec((rows, D), cur),
            pl.BlockSpec((rows, D), prev),
            _const_spec(mod_p.shape),
            _const_spec(vecs.shape),
            _const_spec(w_in_b.shape),
            _const_spec(w_sp.shape),
            _const_spec(bias_full.shape),
            _const_spec(w_pool_b.shape),
            _const_spec(w_out_b.shape),
            _const_spec(w_r.shape),
            _const_spec(b_r.shape),
        ] + e_in + [_const_spec(x_s.shape), _const_spec(mod_s.shape), _const_spec(state_t.shape)],
        out_specs=[
            pl.BlockSpec((rows, D), prev),
            pl.BlockSpec((rows, D), prev),
            pl.BlockSpec((rows, LANES), prev),
            pl.BlockSpec((SUB, 8, LANES), lambda t: (jnp.maximum(t - 1, 0), 0, 0)),
            pl.BlockSpec((1, W_MAX, PW),
                         lambda t: (jnp.minimum(t, steps - 1) * SUB // ns, 0, 0)),
        ] + e_out + [const_out((n_s, D)), const_out((n_s, D)), const_out((n_s, LANES)),
                     const_out((8, LANES)), const_out((n_s, PW)), const_out((n_s, D))],
        out_shape=[
            jax.ShapeDtypeStruct((b * s, D), f32),
            jax.ShapeDtypeStruct((b * s, D), bf16),
            jax.ShapeDtypeStruct((b * s, LANES), bf16),
            jax.ShapeDtypeStruct((nt, 8, LANES), f32),
            jax.ShapeDtypeStruct((b, W_MAX, PW), f32),
        ] + e_shapes + [
            jax.ShapeDtypeStruct((n_s, D), f32),
            jax.ShapeDtypeStruct((n_s, D), bf16),
            jax.ShapeDtypeStruct((n_s, LANES), bf16),
            jax.ShapeDtypeStruct((8, LANES), f32),
            jax.ShapeDtypeStruct((n_s, PW), f32),
            jax.ShapeDtypeStruct((n_s, D), f32),
        ],
        scratch_shapes=[pltpu.VMEM((W_MAX, PW), f32), pltpu.VMEM((SUB, TL, D), f32)],
        compiler_params=pltpu.CompilerParams(
            dimension_semantics=("arbitrary",), vmem_limit_bytes=VMEM_LIMIT),
    )(x2, x2, mod_p, vecs, w_in_b, w_sp, bias_full, w_pool_b, w_out_b, w_r, b_r,
      w_gate, w_up, w_down, x_s, mod_s, state_t)


def _moe_buffer_rows(ts, nt):
    worst = ts * nt + nt * N_GROUPS * (ROW_ALIGN - 1) + N_GROUPS * (RB - 1)
    return -(-worst // RB) * RB


def _moe_kernel(cnt_ref, h2_ref, r3_ref, x1_ref, mod_ref, g_ref, wg_ref, wu_ref, wd_ref, o_ref,
                hsbuf, hs_tile, ys_tile, pt_buf, pm_buf, tab,
                *, ts, nt, tpr, tps):
    w = pl.program_id(0)
    i = pl.program_id(1)
    rt = ts + LANES

    def copy_rows(src, src0, dst, dst0, nrows, ncols):
        def body(j, _):
            s0 = pl.multiple_of(src0 + j * ROW_ALIGN, ROW_ALIGN)
            d0 = pl.multiple_of(dst0 + j * ROW_ALIGN, ROW_ALIGN)
            dst[pl.ds(d0, ROW_ALIGN), :] = src[pl.ds(s0, ROW_ALIGN), :ncols]
            return 0
        lax.fori_loop(0, nrows // ROW_ALIGN, body, 0)

    @pl.when((w == 0) & (i == 0))
    def _():
        hsbuf[...] = jnp.zeros_like(hsbuf)
        ys_tile[...] = jnp.zeros_like(ys_tile)

    @pl.when(i == 0)
    def _sort_and_run_experts():
        def run_len(tile, g):
            c = cnt_ref[(w * nt + tile) * N_GROUPS + g]
            return ((c + (ROW_ALIGN - 1)) // ROW_ALIGN) * ROW_ALIGN

        lens = [[run_len(t, g) for g in range(N_GROUPS)] for t in range(nt)]
        region = [sum(lens[t][g] for t in range(nt)) for g in range(N_GROUPS)]
        region = [((r + (RB - 1)) // RB) * RB for r in region]
        base = [sum(region[:g]) for g in range(N_GROUPS)]
        offs = list(base)
        woff = []
        for t in range(nt):
            woff.append(list(offs))
            for g in range(N_GROUPS):
                tab[t * 2 * N_GROUPS + g] = lens[t][g]
                tab[t * 2 * N_GROUPS + N_GROUPS + g] = offs[g]
                offs[g] = offs[g] + lens[t][g]

        lane = lax.broadcasted_iota(jnp.int32, (ts, LANES), 1)
        lane_f = lane.astype(f32)
        r_i = lax.broadcasted_iota(jnp.int32, (ts, ts), 0)
        c_i = lax.broadcasted_iota(jnp.int32, (ts, ts), 1)
        ltri = jnp.where(r_i > c_i, 1.0, 0.0).astype(bf16)
        lane8 = lax.broadcasted_iota(jnp.int32, (8, LANES), 1)
        sel = jnp.where(lane8 == 0, float(ROW_ALIGN), jnp.where(lane8 == 1, 1.0, 0.0)).astype(bf16)
        rt_lane = lax.broadcasted_iota(jnp.int32, (ts, rt), 1).astype(f32)
        rt_sub = lax.broadcasted_iota(jnp.int32, (rt, ts), 0).astype(f32)

        tiles = range(nt)
        r3s = [r3_ref[t * ts:(t + 1) * ts, :] for t in tiles]
        gids = [jnp.sum(jnp.where(lane == GIDX_LANE, r3s[t].astype(f32), 0.0), axis=-1,
                        keepdims=True) for t in tiles]
        onehots = [jnp.where(lane_f == gids[t], 1.0, 0.0) for t in tiles]
        ranks = [_dot(ltri, onehots[t].astype(bf16)) for t in tiles]
        poss = []
        for t in tiles:
            seg = jnp.zeros((1, LANES), f32)
            start = 0
            for g in range(N_GROUPS):
                seg = seg + jnp.where(lane[0:1] == g, jnp.asarray(start, jnp.int32).astype(f32), 0.0)
                start = start + lens[t][g]
            poss.append(jnp.sum(onehots[t] * (ranks[t] + seg), axis=-1, keepdims=True))
        for t in tiles:
            pt_buf[t] = jnp.where(rt_lane == poss[t], 1.0, 0.0).astype(bf16)
        pos_rows = []
        for t in tiles:
            q = jnp.floor(poss[t] * (1.0 / ROW_ALIGN))
            digits = jnp.where(lane == 0, q, jnp.where(lane == 1, poss[t] - q * ROW_ALIGN, 0.0))
            pos_rows.append(lax.dot_general(sel, digits.astype(bf16), (((1,), (1,)), ((), ())),
                                            preferred_element_type=f32))
        for t in tiles:
            pm_buf[t] = jnp.where(rt_sub == pos_rows[t][0:1], 1.0, 0.0).astype(bf16)

        for t0 in range(0, nt, tps):
            for k in range(tps):
                t = t0 + k
                p_mat = pm_buf[t]
                hs_tile[k, :, :D] = _dot(p_mat, h2_ref[t * ts:(t + 1) * ts, :]).astype(bf16)
                hs_tile[k, :, D:] = _dot(p_mat, r3_ref[t * ts:(t + 1) * ts, :]).astype(bf16)
            for k in range(tps):
                t = t0 + k
                start = 0
                for g in range(N_GROUPS):
                    copy_rows(hs_tile.at[k], start, hsbuf, woff[t][g], lens[t][g], D + LANES)
                    start = start + lens[t][g]

        def expert_rows(g, r0, nrows):
            rows = hsbuf[pl.ds(r0, nrows), :D]
            gate = _dot(rows, wg_ref[g * D:(g + 1) * D, :])
            up = _dot(rows, wu_ref[g * D:(g + 1) * D, :])
            c3 = hsbuf[pl.ds(r0, nrows), D:].astype(f32)
            cw_lanes = c3 + pltpu.roll(c3, LANES - N_EXPERTS, 1) + pltpu.roll(c3, LANES - 2 * N_EXPERTS, 1)
            cw = jnp.concatenate(
                [jnp.broadcast_to(cw_lanes[:, g * EPG + j:g * EPG + j + 1], (nrows, D_EXPERT))
                 for j in range(EPG)], axis=1)
            act = gate * jax.nn.sigmoid(gate) * up * cw
            hsbuf[pl.ds(r0, nrows), :D] = _dot(
                act.astype(bf16), wd_ref[g * D:(g + 1) * D, :]).astype(bf16)

        for g in range(N_GROUPS):
            n_blocks = region[g] // RB

            def two_blocks(b, _, g=g):
                expert_rows(g, pl.multiple_of(base[g] + 2 * b * RB, RB), 2 * RB)
                return 0
            lax.fori_loop(0, n_blocks // 2, two_blocks, 0)

            @pl.when(n_blocks % 2 == 1)
            def _(g=g, n_blocks=n_blocks):
                expert_rows(g, pl.multiple_of(base[g] + (n_blocks - 1) * RB, RB), RB)

    tiles = [i * tps + k for k in range(tps)]
    for k, tile in enumerate(tiles):
        start = 0
        for g in range(N_GROUPS):
            ln = tab[tile * 2 * N_GROUPS + g]
            copy_rows(hsbuf, tab[tile * 2 * N_GROUPS + N_GROUPS + g], ys_tile.at[k], start, ln, D)
            start = start + ln
    fs = [_dot(pt_buf[tile], ys_tile[k]) for k, tile in enumerate(tiles)]
    for k, tile in enumerate(tiles):
        rows = slice(k * ts, (k + 1) * ts)
        tok0 = (w * nt + tile) * ts
        gt2_row = N_MOD - 1
        gt2 = (mod_ref[gt2_row, pl.ds(tok0, ts), :] if tpr == 1
               else mod_ref[gt2_row, pl.ds(tok0 // tpr, 1), :])
        o_ref[rows, :] = x1_ref[rows, :] + gt2 * _rms(fs[k], g_ref[...])


def _moe_call(cnt, x1, h2, r3, mod, g_post, wg, wu, wd, ts, nt, tpr):
    n = x1.shape[0]
    win = ts * nt
    rbuf = _moe_buffer_rows(ts, nt)
    rt = ts + LANES
    tps = 2 if nt % 2 == 0 else 1
    steps = nt // tps
    grid_spec = pltpu.PrefetchScalarGridSpec(
        num_scalar_prefetch=1,
        grid=(n // win, steps),
        in_specs=[
            pl.BlockSpec((win, D), lambda w, i, c: (w, 0)),
            pl.BlockSpec((win, LANES), lambda w, i, c: (w, 0)),
            pl.BlockSpec((tps * ts, D), lambda w, i, c: (w * steps + i, 0)),
            _const_spec(mod.shape),
            _const_spec(g_post.shape),
            _const_spec(wg.shape),
            _const_spec(wu.shape),
            _const_spec(wd.shape),
        ],
        out_specs=pl.BlockSpec((tps * ts, D), lambda w, i, c: (w * steps + i, 0)),
        scratch_shapes=[
            pltpu.VMEM((rbuf, D + LANES), bf16),
            pltpu.VMEM((tps, rt, D + LANES), bf16),
            pltpu.VMEM((tps, rt, D), bf16),
            pltpu.VMEM((nt, ts, rt), bf16),
            pltpu.VMEM((nt, rt, ts), bf16),
            pltpu.SMEM((nt * 2 * N_GROUPS,), jnp.int32),
        ],
    )
    return pl.pallas_call(
        functools.partial(_moe_kernel, ts=ts, nt=nt, tpr=tpr, tps=tps),
        grid_spec=grid_spec,
        out_shape=jax.ShapeDtypeStruct((n, D), f32),
        compiler_params=pltpu.CompilerParams(
            dimension_semantics=("arbitrary", "arbitrary"), vmem_limit_bytes=VMEM_LIMIT),
    )(cnt, h2, r3, x1, mod, g_post, wg, wu, wd)


def _count_table(cnt):
    return cnt[:, 0, :N_GROUPS].astype(jnp.int32).reshape(-1)


def kernel(x_prompt, x_sample, c_prompt, c_sample, state_pool, w_ada, b_ada, g_pre_mix, g_post_mix, g_pre_ffn, g_post_ffn, w_in, ln_v_g, ln_v_b, w_spatial, b_spatial, w_pool, pool_scale, w_out, w_router_grp, b_router_grp, w_router_exp, b_router_exp, w_exp_gate, w_exp_up, w_exp_down):
    depth = w_in.shape[0]
    assert depth == 1
    b, s, _ = x_prompt.shape
    n_s = x_sample.shape[0]
    l = 0

    mod_p, mod_s, w_in_b, w_out_b = _mod_call(
        c_prompt, c_sample, w_ada[l], b_ada[l], w_in[l], w_out[l])

    ws, bs = w_spatial[l], b_spatial[l]
    zeros = jnp.zeros((D,), f32)
    vecs = jnp.stack([
        g_pre_mix[l], g_post_mix[l], g_pre_ffn[l], g_post_ffn[l], ln_v_g[l], ln_v_b[l],
        pool_scale[l], zeros,
        jnp.repeat(ws[:, 0, 0], HEAD_DIM), jnp.repeat(bs[:, 0], HEAD_DIM),
        zeros, zeros, zeros, zeros, zeros, zeros])
    bias_full = jnp.repeat(bs.T, HEAD_DIM, axis=1)
    pad = LANES - N_EXPERTS - N_GROUPS
    w_r = jnp.concatenate([w_router_exp[l], w_router_grp[l], jnp.zeros((D, pad), f32)], axis=1)
    w_r_hi = w_r.astype(bf16)
    w_r_lo = (w_r - w_r_hi.astype(f32)).astype(bf16)
    w_r2 = jnp.concatenate([w_r_hi, w_r_lo], axis=1)
    b_r = jnp.concatenate([b_router_exp[l], b_router_grp[l], jnp.zeros((pad,), f32)])[None]

    state_t = state_pool[l]
    (x1_p, h2_p, r3_p, cnt_p, plast, wg, wu, wd,
     x1_s, h2_s, r3_s, cnt_s, p_s, v_s) = _stage1_call(
        x_prompt, mod_p, vecs, w_in_b, ws, bias_full, w_pool[l], w_out_b, w_r2, b_r,
        w_exp_gate[l], w_exp_up[l], w_exp_down[l], x_sample.reshape(n_s, D), mod_s, state_t)

    g_post = g_post_ffn[l].reshape(1, D)
    y_p = _moe_call(
        _count_table(cnt_p), x1_p.reshape(b * s, D), h2_p.reshape(b * s, D),
        r3_p.reshape(b * s, LANES), mod_p, g_post, wg, wu, wd, TL, MOE_WINDOW // TL, s)
    y_s = _moe_call(
        _count_table(cnt_s[None]), x1_s, h2_s, r3_s, mod_s, g_post, wg, wu, wd, n_s, 1, 1)

    state_pool_prompt = plast[:, 1:][None]
    state_pool_sample = jnp.concatenate([state_pool[l][:, 1:], p_s[:, None, :]], axis=1)[None]
    chunk_v_sample = v_s.reshape(1, n_s, 1, D)
    return (y_p.reshape(b, s, D), y_s.reshape(n_s, 1, D), state_pool_prompt,
            state_pool_sample, chunk_v_sample)
```

```python
import functools

import jax
import jax.numpy as jnp
from jax import lax
from jax.experimental import pallas as pl
from jax.experimental.pallas import tpu as pltpu

D = 1024
CHUNK = 128
HEADS = 8
HEAD_DIM = 128
WINDOWS = (2, 4, 8, 16)
PW = 512
PG = 128
W_MAX = 16
N_GROUPS = 4
EPG = 8
N_EXPERTS = 32
D_EXPERT = 128
EPS = 1e-6
N_MOD = 6
LANES = 128
GROUP_LANE0 = 32
GIDX_LANE = 96

TL = 256
SUB = 2
MOE_WINDOW = 2048
ROW_ALIGN = 16
RB = 128
VMEM_LIMIT = 60 * 1024 * 1024

bf16 = jnp.bfloat16
f32 = jnp.float32


def _rms(x, g):
    ms = jnp.mean(x * x, axis=-1, keepdims=True)
    return x * lax.rsqrt(ms + EPS) * g


def _dot(a, b):
    return jnp.dot(a, b, preferred_element_type=f32)


_GELU_C = 2.0 * 0.7978845608028654


def _gelu(x):
    t = x * ((-_GELU_C) + (-_GELU_C * 0.044715) * (x * x))
    return x / (1.0 + jnp.exp(t))


def _expert_cast_specs(step_to_expert):
    e = step_to_expert
    col_block = pl.BlockSpec((D, D_EXPERT), lambda t: (e(t) // EPG, e(t) % EPG))
    in_specs = [
        pl.BlockSpec((1, D, D_EXPERT), lambda t: (e(t), 0, 0)),
        pl.BlockSpec((1, D, D_EXPERT), lambda t: (e(t), 0, 0)),
        pl.BlockSpec((1, D_EXPERT, D), lambda t: (e(t), 0, 0)),
    ]
    out_specs = [col_block, col_block, pl.BlockSpec((D_EXPERT, D), lambda t: (e(t), 0))]
    out_shapes = [
        jax.ShapeDtypeStruct((N_GROUPS * D, EPG * D_EXPERT), bf16),
        jax.ShapeDtypeStruct((N_GROUPS * D, EPG * D_EXPERT), bf16),
        jax.ShapeDtypeStruct((N_EXPERTS * D_EXPERT, D), bf16),
    ]
    return in_specs, out_specs, out_shapes


def _mod_kernel(cp_ref, cs_ref, w_ref, b_ref, w_in_ref, w_out_ref, op_ref, os_ref, w_in_o, w_out_o):
    w = w_ref[...].astype(bf16)
    for c_ref, o_ref in ((cp_ref, op_ref), (cs_ref, os_ref)):
        c = c_ref[...]
        o_ref[0] = _dot((c * jax.nn.sigmoid(c)).astype(bf16), w) + b_ref[0]
    w_in_o[...] = w_in_ref[...].astype(bf16)
    w_out_o[...] = w_out_ref[...].astype(bf16)


def _mod_call(c_p, c_s, w_ada, b_ada, w_in, w_out):
    nb, n_s = c_p.shape[0], c_s.shape[0]
    in_cols = w_in.shape[1] // N_MOD
    assert w_in.shape[1] == N_MOD * in_cols and in_cols % LANES == 0
    return pl.pallas_call(
        _mod_kernel,
        grid=(N_MOD,),
        in_specs=[
            pl.BlockSpec((nb, D), lambda j: (0, 0)),
            pl.BlockSpec((n_s, D), lambda j: (0, 0)),
            pl.BlockSpec((D, D), lambda j: (0, j)),
            pl.BlockSpec((1, 1, D), lambda j: (j, 0, 0)),
            pl.BlockSpec((D, in_cols), lambda j: (0, j)),
            pl.BlockSpec((D, D), lambda j: (0, 0)),
        ],
        out_specs=[pl.BlockSpec((1, nb, D), lambda j: (j, 0, 0)),
                   pl.BlockSpec((1, n_s, D), lambda j: (j, 0, 0)),
                   pl.BlockSpec((D, in_cols), lambda j: (0, j)),
                   pl.BlockSpec((D, D), lambda j: (0, 0))],
        out_shape=[jax.ShapeDtypeStruct((N_MOD, nb, D), f32),
                   jax.ShapeDtypeStruct((N_MOD, n_s, D), f32),
                   jax.ShapeDtypeStruct(w_in.shape, bf16),
                   jax.ShapeDtypeStruct(w_out.shape, bf16)],
        compiler_params=pltpu.CompilerParams(
            dimension_semantics=("arbitrary",), vmem_limit_bytes=VMEM_LIMIT),
    )(c_p, c_s, w_ada, b_ada.reshape(N_MOD, 1, D), w_in, w_out)


def _route(logits):
    t = logits.shape[0]
    lane = lax.broadcasted_iota(jnp.int32, (t, LANES), 1)
    lane_f = lane.astype(f32)
    neg = -jnp.inf
    big = 1e9
    gmask = (lane >= GROUP_LANE0) & (lane < GROUP_LANE0 + N_GROUPS)
    gl = jnp.where(gmask, logits, neg)
    gmax = jnp.max(gl, axis=-1, keepdims=True)
    g_idx = jnp.min(jnp.where(gl == gmax, lane_f - GROUP_LANE0, big), axis=-1, keepdims=True)
    sumexp = jnp.sum(jnp.where(gmask, jnp.exp(gl - gmax), 0.0), axis=-1, keepdims=True)
    p_g = 1.0 / sumexp
    lane_grp = (lane >> 3).astype(f32)
    emask = (lane < N_EXPERTS) & (lane_grp == g_idx)
    el = jnp.where(emask, logits, neg)
    m1 = jnp.max(el, axis=-1, keepdims=True)
    i1 = jnp.min(jnp.where(el == m1, lane_f, big), axis=-1, keepdims=True)
    el2 = jnp.where(lane_f == i1, neg, el)
    m2 = jnp.max(el2, axis=-1, keepdims=True)
    i2 = jnp.min(jnp.where(el2 == m2, lane_f, big), axis=-1, keepdims=True)
    e = jnp.exp(m2 - m1)
    w1 = p_g / (1.0 + e)
    w2 = w1 * e

    def split3(w):
        hi = w.astype(bf16).astype(f32)
        mid = (w - hi).astype(bf16).astype(f32)
        lo = w - hi - mid
        return hi, mid, lo

    r3 = jnp.where(lane == GIDX_LANE, g_idx, 0.0)
    for idx, w in ((i1, w1), (i2, w2)):
        for part, wp in enumerate(split3(w)):
            r3 = r3 + jnp.where(lane_f == idx + float(part * N_EXPERTS), wp, 0.0)
    counts = jnp.sum(jnp.where(lane_f == g_idx, 1.0, 0.0), axis=0, keepdims=True)
    return r3.astype(bf16), jnp.broadcast_to(counts, (8, LANES))


def _merge_project(u, ga, gb, mix, y_b, w_out_ref):
    y_a = u * mix
    merged = jax.nn.sigmoid(ga) * y_a + jax.nn.sigmoid(gb) * y_b
    return _dot(merged.astype(bf16), w_out_ref[...])


def _residual_route(x, y, mods, vec_ref, w_r_ref, b_r_ref):
    sh1, sc1, gt1, sh2, sc2, gt2 = mods
    x1 = x + gt1 * _rms(y, vec_ref[1:2])
    h2 = _rms(x1, vec_ref[2:3] * (1.0 + sc2)) + sh2
    h2_hi = h2.astype(bf16)
    h2_lo = (h2 - h2_hi.astype(f32)).astype(bf16)
    r = _dot(h2_hi, w_r_ref[...]) + _dot(h2_lo, w_r_ref[...])
    logits = r[:, :LANES] + r[:, LANES:] + b_r_ref[...]
    r3, counts = _route(logits)
    return x1, h2_hi, r3, counts


def _in_proj(x, mods, vec_ref, w_in_ref, after_first_dot=lambda: None):
    sh1, sc1 = mods[0], mods[1]
    h = _rms(x, vec_ref[0:1] * (1.0 + sc1)) + sh1
    hb = h.astype(bf16)
    zu = _dot(hb, w_in_ref[:, 0:D])
    after_first_dot()
    zv = _dot(hb, w_in_ref[:, D:2 * D])
    p = _dot(hb, w_in_ref[:, 2 * D:2 * D + PW])
    ga = _dot(hb, w_in_ref[:, 2 * D + PW:3 * D + PW])
    gb = _dot(hb, w_in_ref[:, 3 * D + PW:4 * D + PW])
    return zu, zv, p, ga, gb


def _activate(zu, zv, vec_ref):
    u = _gelu(zu)
    v = _gelu(zv)
    mu = jnp.mean(v, axis=-1, keepdims=True)
    vc = v - mu
    var = jnp.mean(vc * vc, axis=-1, keepdims=True)
    v = vc * lax.rsqrt(var + EPS) * vec_ref[4:5] + vec_ref[5:6]
    return u, v


def _pool_out(d_groups, vec_ref, w_pool_ref):
    parts = [_dot(d.astype(bf16), w_pool_ref[gi].astype(bf16)) for gi, d in enumerate(d_groups)]
    return jnp.concatenate(parts, axis=1) * vec_ref[6:7]


def _stage1_kernel(x_ref, xprev_ref, mod_ref, vec_ref, w_in_ref, w_sp_ref, bias_ref,
                   w_pool_ref, w_out_ref, w_r_ref, b_r_ref, eg_ref, eu_ref, ed_ref,
                   xs_ref, mods_ref, states_ref,
                   x1_ref, h2_ref, r3_ref, cnt_ref, plast_ref, wg_ref, wu_ref, wd_ref,
                   x1s_ref, h2s_ref, r3s_ref, cnts_ref, ps_ref, vs_ref,
                   pbuf, ybuf, *, tiles_per_seq):
    t = pl.program_id(0)
    last = pl.num_programs(0) - 1

    def mods_of(tile):
        b = tile // tiles_per_seq
        return [mod_ref[i, pl.ds(b, 1), :] for i in range(N_MOD)]

    def second_half(k):
        rows = slice(k * TL, (k + 1) * TL)
        tile = jnp.maximum((t - 1) * SUB + k, 0)
        x1, h2b, r3, counts = _residual_route(xprev_ref[rows, :], ybuf[k], mods_of(tile), vec_ref,
                                              w_r_ref, b_r_ref)
        x1_ref[rows, :] = x1
        h2_ref[rows, :] = h2b
        r3_ref[rows, :] = r3
        cnt_ref[k] = counts

    def cast_expert():
        wg_ref[...] = eg_ref[0].astype(bf16)
        wu_ref[...] = eu_ref[0].astype(bf16)
        wd_ref[...] = ed_ref[0].astype(bf16)

    def first_half(k, under_projection):
        tile = t * SUB + k
        s = tile % tiles_per_seq
        x = x_ref[k * TL:(k + 1) * TL, :]
        zu, zv, p, ga, gb = _in_proj(x, mods_of(tile), vec_ref, w_in_ref, under_projection)
        u, v = _activate(zu, zv, vec_ref)

        vb = v.astype(bf16)
        row = lax.broadcasted_iota(jnp.int32, (CHUNK, CHUNK), 0)
        col = lax.broadcasted_iota(jnp.int32, (CHUNK, CHUNK), 1)
        w_tril = [jnp.where(row >= col, w_sp_ref[hd], 0.0).astype(bf16) for hd in range(HEADS)]
        bias = bias_ref[...]
        chunks = []
        for c in range(TL // CHUNK):
            heads = [_dot(w_tril[hd],
                          vb[c * CHUNK:(c + 1) * CHUNK, hd * HEAD_DIM:(hd + 1) * HEAD_DIM])
                     for hd in range(HEADS)]
            chunks.append(jnp.concatenate(heads, axis=1) + bias)
        mix = jnp.concatenate(chunks, axis=0)

        carry = jnp.where(s == 0, 0.0, pbuf[...])
        ext = jnp.concatenate([carry, p], axis=0)
        pos = s * TL + lax.broadcasted_iota(jnp.int32, (TL, PG), 0)
        d_groups = []
        for gi, w in enumerate(WINDOWS):
            acc = ext[:, gi * PG:(gi + 1) * PG]
            shift = 1
            while shift < w:
                acc = acc + pltpu.roll(acc, shift, 0)
                shift *= 2
            cnt = jnp.minimum(pos + 1, w).astype(f32)
            d_groups.append(acc[W_MAX:] / cnt - p[:, gi * PG:(gi + 1) * PG])
        pbuf[...] = p[TL - W_MAX:]
        plast_ref[0] = p[TL - W_MAX:]
        y_b = _pool_out(d_groups, vec_ref, w_pool_ref)
        ybuf[k] = _merge_project(u, ga, gb, mix, y_b, w_out_ref)

    @pl.when(t == 0)
    def _():
        ybuf[...] = jnp.zeros_like(ybuf)
        pbuf[...] = jnp.zeros_like(pbuf)

    @pl.when(t < last)
    def _():
        for k in range(SUB):
            def under_projection(k=k):
                if k == 0:
                    cast_expert()
                second_half(k)
            first_half(k, under_projection)

    @pl.when(t == last)
    def _():
        for k in range(SUB):
            second_half(k)
        _sample_tokens(xs_ref, mods_ref, vec_ref, w_in_ref, states_ref, w_pool_ref, w_out_ref,
                       w_r_ref, b_r_ref, x1s_ref, h2s_ref, r3s_ref, cnts_ref, ps_ref, vs_ref)


def _sample_tokens(x_ref, mod_ref, vec_ref, w_in_ref, state_ref, w_pool_ref,
                   w_out_ref, w_r_ref, b_r_ref,
                   x1_ref, h2_ref, r3_ref, cnt_ref, p_ref, v_ref):
    x = x_ref[...]
    mods = [mod_ref[i] for i in range(N_MOD)]
    zu, zv, p, ga, gb = _in_proj(x, mods, vec_ref, w_in_ref)
    u, v = _activate(zu, zv, vec_ref)
    v_ref[...] = v
    p_ref[...] = p
    mix = v * vec_ref[8:9] + vec_ref[9:10]
    d_groups = []
    for gi, w in enumerate(WINDOWS):
        sl = slice(gi * PG, (gi + 1) * PG)
        acc = p[:, sl]
        for r in range(W_MAX - w, W_MAX - 1):
            acc = acc + state_ref[r][:, sl]
        d_groups.append(acc / float(w) - p[:, sl])
    y_b = _pool_out(d_groups, vec_ref, w_pool_ref)
    y = _merge_project(u, ga, gb, mix, y_b, w_out_ref)
    x1, h2b, r3, counts = _residual_route(x, y, mods, vec_ref, w_r_ref, b_r_ref)
    x1_ref[...] = x1
    h2_ref[...] = h2b
    r3_ref[...] = r3
    cnt_ref[...] = counts


def _const_spec(shape):
    nd = len(shape)
    return pl.BlockSpec(shape, lambda *_: (0,) * nd, pipeline_mode=pl.Buffered(1))


def _stage1_call(x, mod_p, vecs, w_in_b, w_sp, bias_full, w_pool_b, w_out_b, w_r, b_r,
                 w_gate, w_up, w_down, x_s, mod_s, state_t):
    b, s, _ = x.shape
    n_s = x_s.shape[0]
    const_out = lambda shape: pl.BlockSpec(shape, lambda t: (0,) * len(shape))
    ns = s // TL
    nt = b * ns
    steps = nt // SUB
    assert ns % SUB == 0
    assert steps >= N_EXPERTS
    x2 = x.reshape(b * s, D)
    rows = SUB * TL
    cur = lambda t: (jnp.minimum(t, steps - 1), 0)
    prev = lambda t: (jnp.maximum(t - 1, 0), 0)
    e_in, e_out, e_shapes = _expert_cast_specs(lambda t: jnp.minimum(t, N_EXPERTS - 1))
    return pl.pallas_call(
        functools.partial(_stage1_kernel, tiles_per_seq=ns),
        grid=(steps + 1,),
        in_specs=[
            pl.BlockSpec((rows, D), cur),
            pl.BlockSpec((rows, D), prev),
            _const_spec(mod_p.shape),
            _const_spec(vecs.shape),
            _const_spec(w_in_b.shape),
            _const_spec(w_sp.shape),
            _const_spec(bias_full.shape),
            _const_spec(w_pool_b.shape),
            _const_spec(w_out_b.shape),
            _const_spec(w_r.shape),
            _const_spec(b_r.shape),
        ] + e_in + [_const_spec(x_s.shape), _const_spec(mod_s.shape), _const_spec(state_t.shape)],
        out_specs=[
            pl.BlockSpec((rows, D), prev),
            pl.BlockSpec((rows, D), prev),
            pl.BlockSpec((rows, LANES), prev),
            pl.BlockSpec((SUB, 8, LANES), lambda t: (jnp.maximum(t - 1, 0), 0, 0)),
            pl.BlockSpec((1, W_MAX, PW),
                         lambda t: (jnp.minimum(t, steps - 1) * SUB // ns, 0, 0)),
        ] + e_out + [const_out((n_s, D)), const_out((n_s, D)), const_out((n_s, LANES)),
                     const_out((8, LANES)), const_out((n_s, PW)), const_out((n_s, D))],
        out_shape=[
            jax.ShapeDtypeStruct((b * s, D), f32),
            jax.ShapeDtypeStruct((b * s, D), bf16),
            jax.ShapeDtypeStruct((b * s, LANES), bf16),
            jax.ShapeDtypeStruct((nt, 8, LANES), f32),
            jax.ShapeDtypeStruct((b, W_MAX, PW), f32),
        ] + e_shapes + [
            jax.ShapeDtypeStruct((n_s, D), f32),
            jax.ShapeDtypeStruct((n_s, D), bf16),
            jax.ShapeDtypeStruct((n_s, LANES), bf16),
            jax.ShapeDtypeStruct((8, LANES), f32),
            jax.ShapeDtypeStruct((n_s, PW), f32),
            jax.ShapeDtypeStruct((n_s, D), f32),
        ],
        scratch_shapes=[pltpu.VMEM((W_MAX, PW), f32), pltpu.VMEM((SUB, TL, D), f32)],
        compiler_params=pltpu.CompilerParams(
            dimension_semantics=("arbitrary",), vmem_limit_bytes=VMEM_LIMIT),
    )(x2, x2, mod_p, vecs, w_in_b, w_sp, bias_full, w_pool_b, w_out_b, w_r, b_r,
      w_gate, w_up, w_down, x_s, mod_s, state_t)


def _moe_buffer_rows(ts, nt):
    worst = ts * nt + nt * N_GROUPS * (ROW_ALIGN - 1) + N_GROUPS * (RB - 1)
    return -(-worst // RB) * RB


def _moe_kernel(cnt_ref, h2_ref, r3_ref, x1_ref, mod_ref, g_ref, wg_ref, wu_ref, wd_ref, o_ref,
                hsbuf, hs_tile, ys_tile, pt_buf, pm_buf, tab,
                *, ts, nt, tpr, tps):
    w = pl.program_id(0)
    i = pl.program_id(1)
    rt = ts + LANES

    def copy_rows(src, src0, dst, dst0, nrows, ncols):
        def body(j, _):
            s0 = pl.multiple_of(src0 + j * ROW_ALIGN, ROW_ALIGN)
            d0 = pl.multiple_of(dst0 + j * ROW_ALIGN, ROW_ALIGN)
            dst[pl.ds(d0, ROW_ALIGN), :] = src[pl.ds(s0, ROW_ALIGN), :ncols]
            return 0
        lax.fori_loop(0, nrows // ROW_ALIGN, body, 0)

    @pl.when((w == 0) & (i == 0))
    def _():
        hsbuf[...] = jnp.zeros_like(hsbuf)
        ys_tile[...] = jnp.zeros_like(ys_tile)

    @pl.when(i == 0)
    def _sort_and_run_experts():
        def run_len(tile, g):
            c = cnt_ref[(w * nt + tile) * N_GROUPS + g]
            return ((c + (ROW_ALIGN - 1)) // ROW_ALIGN) * ROW_ALIGN

        lens = [[run_len(t, g) for g in range(N_GROUPS)] for t in range(nt)]
        region = [sum(lens[t][g] for t in range(nt)) for g in range(N_GROUPS)]
        region = [((r + (RB - 1)) // RB) * RB for r in region]
        base = [sum(region[:g]) for g in range(N_GROUPS)]
        offs = list(base)
        woff = []
        for t in range(nt):
            woff.append(list(offs))
            for g in range(N_GROUPS):
                tab[t * 2 * N_GROUPS + g] = lens[t][g]
                tab[t * 2 * N_GROUPS + N_GROUPS + g] = offs[g]
                offs[g] = offs[g] + lens[t][g]

        lane = lax.broadcasted_iota(jnp.int32, (ts, LANES), 1)
        lane_f = lane.astype(f32)
        r_i = lax.broadcasted_iota(jnp.int32, (ts, ts), 0)
        c_i = lax.broadcasted_iota(jnp.int32, (ts, ts), 1)
        ltri = jnp.where(r_i > c_i, 1.0, 0.0).astype(bf16)
        lane8 = lax.broadcasted_iota(jnp.int32, (8, LANES), 1)
        sel = jnp.where(lane8 == 0, float(ROW_ALIGN), jnp.where(lane8 == 1, 1.0, 0.0)).astype(bf16)
        rt_lane = lax.broadcasted_iota(jnp.int32, (ts, rt), 1).astype(f32)
        rt_sub = lax.broadcasted_iota(jnp.int32, (rt, ts), 0).astype(f32)

        tiles = range(nt)
        r3s = [r3_ref[t * ts:(t + 1) * ts, :] for t in tiles]
        gids = [jnp.sum(jnp.where(lane == GIDX_LANE, r3s[t].astype(f32), 0.0), axis=-1,
                        keepdims=True) for t in tiles]
        onehots = [jnp.where(lane_f == gids[t], 1.0, 0.0) for t in tiles]
        ranks = [_dot(ltri, onehots[t].astype(bf16)) for t in tiles]
        poss = []
        for t in tiles:
            seg = jnp.zeros((1, LANES), f32)
            start = 0
            for g in range(N_GROUPS):
                seg = seg + jnp.where(lane[0:1] == g, jnp.asarray(start, jnp.int32).astype(f32), 0.0)
                start = start + lens[t][g]
            poss.append(jnp.sum(onehots[t] * (ranks[t] + seg), axis=-1, keepdims=True))
        for t in tiles:
            pt_buf[t] = jnp.where(rt_lane == poss[t], 1.0, 0.0).astype(bf16)
        pos_rows = []
        for t in tiles:
            q = jnp.floor(poss[t] * (1.0 / ROW_ALIGN))
            digits = jnp.where(lane == 0, q, jnp.where(lane == 1, poss[t] - q * ROW_ALIGN, 0.0))
            pos_rows.append(lax.dot_general(sel, digits.astype(bf16), (((1,), (1,)), ((), ())),
                                            preferred_element_type=f32))
        for t in tiles:
            pm_buf[t] = jnp.where(rt_sub == pos_rows[t][0:1], 1.0, 0.0).astype(bf16)

        for t0 in range(0, nt, tps):
            for k in range(tps):
                t = t0 + k
                p_mat = pm_buf[t]
                hs_tile[k, :, :D] = _dot(p_mat, h2_ref[t * ts:(t + 1) * ts, :]).astype(bf16)
                hs_tile[k, :, D:] = _dot(p_mat, r3_ref[t * ts:(t + 1) * ts, :]).astype(bf16)
            for k in range(tps):
                t = t0 + k
                start = 0
                for g in range(N_GROUPS):
                    copy_rows(hs_tile.at[k], start, hsbuf, woff[t][g], lens[t][g], D + LANES)
                    start = start + lens[t][g]

        def expert_rows(g, r0, nrows):
            rows = hsbuf[pl.ds(r0, nrows), :D]
            gate = _dot(rows, wg_ref[g * D:(g + 1) * D, :])
            up = _dot(rows, wu_ref[g * D:(g + 1) * D, :])
            c3 = hsbuf[pl.ds(r0, nrows), D:].astype(f32)
            cw_lanes = c3 + pltpu.roll(c3, LANES - N_EXPERTS, 1) + pltpu.roll(c3, LANES - 2 * N_EXPERTS, 1)
            cw = jnp.concatenate(
                [jnp.broadcast_to(cw_lanes[:, g * EPG + j:g * EPG + j + 1], (nrows, D_EXPERT))
                 for j in range(EPG)], axis=1)
            act = gate * jax.nn.sigmoid(gate) * up * cw
            hsbuf[pl.ds(r0, nrows), :D] = _dot(
                act.astype(bf16), wd_ref[g * D:(g + 1) * D, :]).astype(bf16)

        for g in range(N_GROUPS):
            n_blocks = region[g] // RB

            def two_blocks(b, _, g=g):
                expert_rows(g, pl.multiple_of(base[g] + 2 * b * RB, RB), 2 * RB)
                return 0
            lax.fori_loop(0, n_blocks // 2, two_blocks, 0)

            @pl.when(n_blocks % 2 == 1)
            def _(g=g, n_blocks=n_blocks):
                expert_rows(g, pl.multiple_of(base[g] + (n_blocks - 1) * RB, RB), RB)

    tiles = [i * tps + k for k in range(tps)]
    for k, tile in enumerate(tiles):
        start = 0
        for g in range(N_GROUPS):
            ln = tab[tile * 2 * N_GROUPS + g]
            copy_rows(hsbuf, tab[tile * 2 * N_GROUPS + N_GROUPS + g], ys_tile.at[k], start, ln, D)
            start = start + ln
    fs = [_dot(pt_buf[tile], ys_tile[k]) for k, tile in enumerate(tiles)]
    for k, tile in enumerate(tiles):
        rows = slice(k * ts, (k + 1) * ts)
        tok0 = (w * nt + tile) * ts
        gt2_row = N_MOD - 1
        gt2 = (mod_ref[gt2_row, pl.ds(tok0, ts), :] if tpr == 1
               else mod_ref[gt2_row, pl.ds(tok0 // tpr, 1), :])
        o_ref[rows, :] = x1_ref[rows, :] + gt2 * _rms(fs[k], g_ref[...])


def _moe_call(cnt, x1, h2, r3, mod, g_post, wg, wu, wd, ts, nt, tpr):
    n = x1.shape[0]
    win = ts * nt
    rbuf = _moe_buffer_rows(ts, nt)
    rt = ts + LANES
    tps = 2 if nt % 2 == 0 else 1
    steps = nt // tps
    grid_spec = pltpu.PrefetchScalarGridSpec(
        num_scalar_prefetch=1,
        grid=(n // win, steps),
        in_specs=[
            pl.BlockSpec((win, D), lambda w, i, c: (w, 0)),
            pl.BlockSpec((win, LANES), lambda w, i, c: (w, 0)),
            pl.BlockSpec((tps * ts, D), lambda w, i, c: (w * steps + i, 0)),
            _const_spec(mod.shape),
            _const_spec(g_post.shape),
            _const_spec(wg.shape),
            _const_spec(wu.shape),
            _const_spec(wd.shape),
        ],
        out_specs=pl.BlockSpec((tps * ts, D), lambda w, i, c: (w * steps + i, 0)),
        scratch_shapes=[
            pltpu.VMEM((rbuf, D + LANES), bf16),
            pltpu.VMEM((tps, rt, D + LANES), bf16),
            pltpu.VMEM((tps, rt, D), bf16),
            pltpu.VMEM((nt, ts, rt), bf16),
            pltpu.VMEM((nt, rt, ts), bf16),
            pltpu.SMEM((nt * 2 * N_GROUPS,), jnp.int32),
        ],
    )
    return pl.pallas_call(
        functools.partial(_moe_kernel, ts=ts, nt=nt, tpr=tpr, tps=tps),
        grid_spec=grid_spec,
        out_shape=jax.ShapeDtypeStruct((n, D), f32),
        compiler_params=pltpu.CompilerParams(
            dimension_semantics=("arbitrary", "arbitrary"), vmem_limit_bytes=VMEM_LIMIT),
    )(cnt, h2, r3, x1, mod, g_post, wg, wu, wd)


def _count_table(cnt):
    return cnt[:, 0, :N_GROUPS].astype(jnp.int32).reshape(-1)


def kernel(x_prompt, x_sample, c_prompt, c_sample, state_pool, w_ada, b_ada, g_pre_mix, g_post_mix, g_pre_ffn, g_post_ffn, w_in, ln_v_g, ln_v_b, w_spatial, b_spatial, w_pool, pool_scale, w_out, w_router_grp, b_router_grp, w_router_exp, b_router_exp, w_exp_gate, w_exp_up, w_exp_down):
    depth = w_in.shape[0]
    assert depth == 1
    b, s, _ = x_prompt.shape
    n_s = x_sample.shape[0]
    l = 0

    mod_p, mod_s, w_in_b, w_out_b = _mod_call(
        c_prompt, c_sample, w_ada[l], b_ada[l], w_in[l], w_out[l])

    ws, bs = w_spatial[l], b_spatial[l]
    zeros = jnp.zeros((D,), f32)
    vecs = jnp.stack([
        g_pre_mix[l], g_post_mix[l], g_pre_ffn[l], g_post_ffn[l], ln_v_g[l], ln_v_b[l],
        pool_scale[l], zeros,
        jnp.repeat(ws[:, 0, 0], HEAD_DIM), jnp.repeat(bs[:, 0], HEAD_DIM),
        zeros, zeros, zeros, zeros, zeros, zeros])
    bias_full = jnp.repeat(bs.T, HEAD_DIM, axis=1)
    pad = LANES - N_EXPERTS - N_GROUPS
    w_r = jnp.concatenate([w_router_exp[l], w_router_grp[l], jnp.zeros((D, pad), f32)], axis=1)
    w_r_hi = w_r.astype(bf16)
    w_r_lo = (w_r - w_r_hi.astype(f32)).astype(bf16)
    w_r2 = jnp.concatenate([w_r_hi, w_r_lo], axis=1)
    b_r = jnp.concatenate([b_router_exp[l], b_router_grp[l], jnp.zeros((pad,), f32)])[None]

    state_t = jnp.transpose(state_pool[l], (1, 0, 2))
    (x1_p, h2_p, r3_p, cnt_p, plast, wg, wu, wd,
     x1_s, h2_s, r3_s, cnt_s, p_s, v_s) = _stage1_call(
        x_prompt, mod_p, vecs, w_in_b, ws, bias_full, w_pool[l], w_out_b, w_r2, b_r,
        w_exp_gate[l], w_exp_up[l], w_exp_down[l], x_sample.reshape(n_s, D), mod_s, state_t)

    g_post = g_post_ffn[l].reshape(1, D)
    y_p = _moe_call(
        _count_table(cnt_p), x1_p.reshape(b * s, D), h2_p.reshape(b * s, D),
        r3_p.reshape(b * s, LANES), mod_p, g_post, wg, wu, wd, TL, MOE_WINDOW // TL, s)
    y_s = _moe_call(
        _count_table(cnt_s[None]), x1_s, h2_s, r3_s, mod_s, g_post, wg, wu, wd, n_s, 1, 1)

    state_pool_prompt = plast[:, 1:][None]
    state_pool_sample = jnp.concatenate([state_pool[l][:, 1:], p_s[:, None, :]], axis=1)[None]
    chunk_v_sample = v_s.reshape(1, n_s, 1, D)
    return (y_p.reshape(b, s, D), y_s.reshape(n_s, 1, D), state_pool_prompt,
            state_pool_sample, chunk_v_sample)
```

```python
import functools

import jax
import jax.numpy as jnp
from jax import lax
from jax.experimental import pallas as pl
from jax.experimental.pallas import tpu as pltpu

D = 1024
CHUNK = 128
HEADS = 8
HEAD_DIM = 128
WINDOWS = (2, 4, 8, 16)
PW = 512
PG = 128
W_MAX = 16
N_GROUPS = 4
EPG = 8
N_EXPERTS = 32
D_EXPERT = 128
EPS = 1e-6
N_MOD = 6
LANES = 128
GROUP_LANE0 = 32
GIDX_LANE = 96

TL = 256
SUB = 2
MOE_WINDOW = 2048
ROW_ALIGN = 16
RB = 128
VMEM_LIMIT = 60 * 1024 * 1024

bf16 = jnp.bfloat16
f32 = jnp.float32


def _rms(x, g):
    ms = jnp.mean(x * x, axis=-1, keepdims=True)
    return x * lax.rsqrt(ms + EPS) * g


def _dot(a, b):
    return jnp.dot(a, b, preferred_element_type=f32)


_GELU_C = 2.0 * 0.7978845608028654


def _gelu(x):
    t = x * ((-_GELU_C) + (-_GELU_C * 0.044715) * (x * x))
    return x / (1.0 + jnp.exp(t))


def _expert_cast_specs(step_to_expert):
    e = step_to_expert
    col_block = pl.BlockSpec((D, D_EXPERT), lambda t: (e(t) // EPG, e(t) % EPG))
    in_specs = [
        pl.BlockSpec((1, D, D_EXPERT), lambda t: (e(t), 0, 0)),
        pl.BlockSpec((1, D, D_EXPERT), lambda t: (e(t), 0, 0)),
        pl.BlockSpec((1, D_EXPERT, D), lambda t: (e(t), 0, 0)),
    ]
    out_specs = [col_block, col_block, pl.BlockSpec((D_EXPERT, D), lambda t: (e(t), 0))]
    out_shapes = [
        jax.ShapeDtypeStruct((N_GROUPS * D, EPG * D_EXPERT), bf16),
        jax.ShapeDtypeStruct((N_GROUPS * D, EPG * D_EXPERT), bf16),
        jax.ShapeDtypeStruct((N_EXPERTS * D_EXPERT, D), bf16),
    ]
    return in_specs, out_specs, out_shapes


def _mod_kernel(cp_ref, cs_ref, w_ref, b_ref, w_in_ref, w_out_ref, op_ref, os_ref, w_in_o, w_out_o):
    w = w_ref[...].astype(bf16)
    for c_ref, o_ref in ((cp_ref, op_ref), (cs_ref, os_ref)):
        c = c_ref[...]
        o_ref[0] = _dot((c * jax.nn.sigmoid(c)).astype(bf16), w) + b_ref[0]
    w_in_o[...] = w_in_ref[...].astype(bf16)
    w_out_o[...] = w_out_ref[...].astype(bf16)


def _mod_call(c_p, c_s, w_ada, b_ada, w_in, w_out):
    nb, n_s = c_p.shape[0], c_s.shape[0]
    in_cols = w_in.shape[1] // N_MOD
    assert w_in.shape[1] == N_MOD * in_cols and in_cols % LANES == 0
    return pl.pallas_call(
        _mod_kernel,
        grid=(N_MOD,),
        in_specs=[
            pl.BlockSpec((nb, D), lambda j: (0, 0)),
            pl.BlockSpec((n_s, D), lambda j: (0, 0)),
            pl.BlockSpec((D, D), lambda j: (0, j)),
            pl.BlockSpec((1, 1, D), lambda j: (j, 0, 0)),
            pl.BlockSpec((D, in_cols), lambda j: (0, j)),
            pl.BlockSpec((D, D), lambda j: (0, 0)),
        ],
        out_specs=[pl.BlockSpec((1, nb, D), lambda j: (j, 0, 0)),
                   pl.BlockSpec((1, n_s, D), lambda j: (j, 0, 0)),
                   pl.BlockSpec((D, in_cols), lambda j: (0, j)),
                   pl.BlockSpec((D, D), lambda j: (0, 0))],
        out_shape=[jax.ShapeDtypeStruct((N_MOD, nb, D), f32),
                   jax.ShapeDtypeStruct((N_MOD, n_s, D), f32),
                   jax.ShapeDtypeStruct(w_in.shape, bf16),
                   jax.ShapeDtypeStruct(w_out.shape, bf16)],
        compiler_params=pltpu.CompilerParams(
            dimension_semantics=("arbitrary",), vmem_limit_bytes=VMEM_LIMIT),
    )(c_p, c_s, w_ada, b_ada.reshape(N_MOD, 1, D), w_in, w_out)


def _route(logits):
    t = logits.shape[0]
    lane = lax.broadcasted_iota(jnp.int32, (t, LANES), 1)
    lane_f = lane.astype(f32)
    neg = -jnp.inf
    big = 1e9
    gmask = (lane >= GROUP_LANE0) & (lane < GROUP_LANE0 + N_GROUPS)
    gl = jnp.where(gmask, logits, neg)
    gmax = jnp.max(gl, axis=-1, keepdims=True)
    g_idx = jnp.min(jnp.where(gl == gmax, lane_f - GROUP_LANE0, big), axis=-1, keepdims=True)
    sumexp = jnp.sum(jnp.where(gmask, jnp.exp(gl - gmax), 0.0), axis=-1, keepdims=True)
    p_g = 1.0 / sumexp
    lane_grp = (lane >> 3).astype(f32)
    emask = (lane < N_EXPERTS) & (lane_grp == g_idx)
    el = jnp.where(emask, logits, neg)
    m1 = jnp.max(el, axis=-1, keepdims=True)
    i1 = jnp.min(jnp.where(el == m1, lane_f, big), axis=-1, keepdims=True)
    el2 = jnp.where(lane_f == i1, neg, el)
    m2 = jnp.max(el2, axis=-1, keepdims=True)
    i2 = jnp.min(jnp.where(el2 == m2, lane_f, big), axis=-1, keepdims=True)
    e = jnp.exp(m2 - m1)
    w1 = p_g / (1.0 + e)
    w2 = w1 * e

    def split3(w):
        hi = w.astype(bf16).astype(f32)
        mid = (w - hi).astype(bf16).astype(f32)
        lo = w - hi - mid
        return hi, mid, lo

    r3 = jnp.where(lane == GIDX_LANE, g_idx, 0.0)
    for idx, w in ((i1, w1), (i2, w2)):
        for part, wp in enumerate(split3(w)):
            r3 = r3 + jnp.where(lane_f == idx + float(part * N_EXPERTS), wp, 0.0)
    counts = jnp.sum(jnp.where(lane_f == g_idx, 1.0, 0.0), axis=0, keepdims=True)
    return r3.astype(bf16), jnp.broadcast_to(counts, (8, LANES))


def _merge_project(u, ga, gb, mix, y_b, w_out_ref):
    y_a = u * mix
    merged = jax.nn.sigmoid(ga) * y_a + jax.nn.sigmoid(gb) * y_b
    return _dot(merged.astype(bf16), w_out_ref[...])


def _residual_route(x, y, mods, vec_ref, w_r_ref, b_r_ref):
    sh1, sc1, gt1, sh2, sc2, gt2 = mods
    x1 = x + gt1 * _rms(y, vec_ref[1:2])
    h2 = _rms(x1, vec_ref[2:3] * (1.0 + sc2)) + sh2
    h2_hi = h2.astype(bf16)
    h2_lo = (h2 - h2_hi.astype(f32)).astype(bf16)
    r = _dot(h2_hi, w_r_ref[...]) + _dot(h2_lo, w_r_ref[...])
    logits = r[:, :LANES] + r[:, LANES:] + b_r_ref[...]
    r3, counts = _route(logits)
    return x1, h2_hi, r3, counts


def _in_proj(x, mods, vec_ref, w_in_ref, after_first_dot=lambda: None):
    sh1, sc1 = mods[0], mods[1]
    h = _rms(x, vec_ref[0:1] * (1.0 + sc1)) + sh1
    hb = h.astype(bf16)
    zu = _dot(hb, w_in_ref[:, 0:D])
    after_first_dot()
    zv = _dot(hb, w_in_ref[:, D:2 * D])
    p = _dot(hb, w_in_ref[:, 2 * D:2 * D + PW])
    ga = _dot(hb, w_in_ref[:, 2 * D + PW:3 * D + PW])
    gb = _dot(hb, w_in_ref[:, 3 * D + PW:4 * D + PW])
    return zu, zv, p, ga, gb


def _activate(zu, zv, vec_ref):
    u = _gelu(zu)
    v = _gelu(zv)
    mu = jnp.mean(v, axis=-1, keepdims=True)
    vc = v - mu
    var = jnp.mean(vc * vc, axis=-1, keepdims=True)
    v = vc * lax.rsqrt(var + EPS) * vec_ref[4:5] + vec_ref[5:6]
    return u, v


def _pool_out(d_groups, vec_ref, w_pool_ref):
    parts = [_dot(d.astype(bf16), w_pool_ref[gi].astype(bf16)) for gi, d in enumerate(d_groups)]
    return jnp.concatenate(parts, axis=1) * vec_ref[6:7]


def _stage1_kernel(x_ref, xprev_ref, mod_ref, vec_ref, w_in_ref, w_sp_ref, bias_ref,
                   w_pool_ref, w_out_ref, w_r_ref, b_r_ref, eg_ref, eu_ref, ed_ref,
                   xs_ref, mods_ref, states_ref,
                   x1_ref, h2_ref, r3_ref, cnt_ref, plast_ref, wg_ref, wu_ref, wd_ref,
                   x1s_ref, h2s_ref, r3s_ref, cnts_ref, ps_ref, vs_ref,
                   pbuf, ybuf, *, tiles_per_seq):
    t = pl.program_id(0)
    last = pl.num_programs(0) - 1

    def mods_of(tile):
        b = tile // tiles_per_seq
        return [mod_ref[i, pl.ds(b, 1), :] for i in range(N_MOD)]

    def second_half(k):
        rows = slice(k * TL, (k + 1) * TL)
        tile = jnp.maximum((t - 1) * SUB + k, 0)
        x1, h2b, r3, counts = _residual_route(xprev_ref[rows, :], ybuf[k], mods_of(tile), vec_ref,
                                              w_r_ref, b_r_ref)
        x1_ref[rows, :] = x1
        h2_ref[rows, :] = h2b
        r3_ref[rows, :] = r3
        cnt_ref[k] = counts

    def cast_expert():
        wg_ref[...] = eg_ref[0].astype(bf16)
        wu_ref[...] = eu_ref[0].astype(bf16)
        wd_ref[...] = ed_ref[0].astype(bf16)

    def first_half(k, under_projection):
        tile = t * SUB + k
        s = tile % tiles_per_seq
        x = x_ref[k * TL:(k + 1) * TL, :]
        zu, zv, p, ga, gb = _in_proj(x, mods_of(tile), vec_ref, w_in_ref, under_projection)
        u, v = _activate(zu, zv, vec_ref)

        vb = v.astype(bf16)
        row = lax.broadcasted_iota(jnp.int32, (CHUNK, CHUNK), 0)
        col = lax.broadcasted_iota(jnp.int32, (CHUNK, CHUNK), 1)
        w_tril = [jnp.where(row >= col, w_sp_ref[hd], 0.0).astype(bf16) for hd in range(HEADS)]
        bias = bias_ref[...]
        chunks = []
        for c in range(TL // CHUNK):
            heads = [_dot(w_tril[hd],
                          vb[c * CHUNK:(c + 1) * CHUNK, hd * HEAD_DIM:(hd + 1) * HEAD_DIM])
                     for hd in range(HEADS)]
            chunks.append(jnp.concatenate(heads, axis=1) + bias)
        mix = jnp.concatenate(chunks, axis=0)

        carry = jnp.where(s == 0, 0.0, pbuf[...])
        ext = jnp.concatenate([carry, p], axis=0)
        pos = s * TL + lax.broadcasted_iota(jnp.int32, (TL, PG), 0)
        d_groups = []
        for gi, w in enumerate(WINDOWS):
            acc = ext[:, gi * PG:(gi + 1) * PG]
            shift = 1
            while shift < w:
                acc = acc + pltpu.roll(acc, shift, 0)
                shift *= 2
            cnt = jnp.minimum(pos + 1, w).astype(f32)
            d_groups.append(acc[W_MAX:] / cnt - p[:, gi * PG:(gi + 1) * PG])
        pbuf[...] = p[TL - W_MAX:]
        plast_ref[0] = p[TL - W_MAX:]
        y_b = _pool_out(d_groups, vec_ref, w_pool_ref)
        ybuf[k] = _merge_project(u, ga, gb, mix, y_b, w_out_ref)

    @pl.when(t == 0)
    def _():
        ybuf[...] = jnp.zeros_like(ybuf)
        pbuf[...] = jnp.zeros_like(pbuf)

    @pl.when(t < last)
    def _():
        for k in range(SUB):
            def under_projection(k=k):
                if k == 0:
                    cast_expert()
                second_half(k)
            first_half(k, under_projection)

    @pl.when(t == last)
    def _():
        for k in range(SUB):
            second_half(k)
        _sample_tokens(xs_ref, mods_ref, vec_ref, w_in_ref, states_ref, w_pool_ref, w_out_ref,
                       w_r_ref, b_r_ref, x1s_ref, h2s_ref, r3s_ref, cnts_ref, ps_ref, vs_ref)


def _sample_tokens(x_ref, mod_ref, vec_ref, w_in_ref, state_ref, w_pool_ref,
                   w_out_ref, w_r_ref, b_r_ref,
                   x1_ref, h2_ref, r3_ref, cnt_ref, p_ref, v_ref):
    x = x_ref[...]
    mods = [mod_ref[i] for i in range(N_MOD)]
    zu, zv, p, ga, gb = _in_proj(x, mods, vec_ref, w_in_ref)
    u, v = _activate(zu, zv, vec_ref)
    v_ref[...] = v
    p_ref[...] = p
    mix = v * vec_ref[8:9] + vec_ref[9:10]
    d_groups = []
    for gi, w in enumerate(WINDOWS):
        sl = slice(gi * PG, (gi + 1) * PG)
        acc = p[:, sl]
        for r in range(W_MAX - w, W_MAX - 1):
            acc = acc + state_ref[r][:, sl]
        d_groups.append(acc / float(w) - p[:, sl])
    y_b = _pool_out(d_groups, vec_ref, w_pool_ref)
    y = _merge_project(u, ga, gb, mix, y_b, w_out_ref)
    x1, h2b, r3, counts = _residual_route(x, y, mods, vec_ref, w_r_ref, b_r_ref)
    x1_ref[...] = x1
    h2_ref[...] = h2b
    r3_ref[...] = r3
    cnt_ref[...] = counts


def _const_spec(shape):
    nd = len(shape)
    return pl.BlockSpec(shape, lambda *_: (0,) * nd, pipeline_mode=pl.Buffered(1))


def _stage1_call(x, mod_p, vecs, w_in_b, w_sp, bias_full, w_pool_b, w_out_b, w_r, b_r,
                 w_gate, w_up, w_down, x_s, mod_s, state_t):
    b, s, _ = x.shape
    n_s = x_s.shape[0]
    const_out = lambda shape: pl.BlockSpec(shape, lambda t: (0,) * len(shape))
    ns = s // TL
    nt = b * ns
    steps = nt // SUB
    assert ns % SUB == 0
    assert steps >= N_EXPERTS
    x2 = x.reshape(b * s, D)
    rows = SUB * TL
    cur = lambda t: (jnp.minimum(t, steps - 1), 0)
    prev = lambda t: (jnp.maximum(t - 1, 0), 0)
    e_in, e_out, e_shapes = _expert_cast_specs(lambda t: jnp.minimum(t, N_EXPERTS - 1))
    return pl.pallas_call(
        functools.partial(_stage1_kernel, tiles_per_seq=ns),
        grid=(steps + 1,),
        in_specs=[
            pl.BlockSpec((rows, D), cur),
            pl.BlockSpec((rows, D), prev),
            _const_spec(mod_p.shape),
            _const_spec(vecs.shape),
            _const_spec(w_in_b.shape),
            _const_spec(w_sp.shape),
            _const_spec(bias_full.shape),
            _const_spec(w_pool_b.shape),
            _const_spec(w_out_b.shape),
            _const_spec(w_r.shape),
            _const_spec(b_r.shape),
        ] + e_in + [_const_spec(x_s.shape), _const_spec(mod_s.shape), _const_spec(state_t.shape)],
        out_specs=[
            pl.BlockSpec((rows, D), prev),
            pl.BlockSpec((rows, D), prev),
            pl.BlockSpec((rows, LANES), prev),
            pl.BlockSpec((SUB, 8, LANES), lambda t: (jnp.maximum(t - 1, 0), 0, 0)),
            pl.BlockSpec((1, W_MAX, PW),
                         lambda t: (jnp.minimum(t, steps - 1) * SUB // ns, 0, 0)),
        ] + e_out + [const_out((n_s, D)), const_out((n_s, D)), const_out((n_s, LANES)),
                     const_out((8, LANES)), const_out((n_s, PW)), const_out((n_s, D))],
        out_shape=[
            jax.ShapeDtypeStruct((b * s, D), f32),
            jax.ShapeDtypeStruct((b * s, D), bf16),
            jax.ShapeDtypeStruct((b * s, LANES), bf16),
            jax.ShapeDtypeStruct((nt, 8, LANES), f32),
            jax.ShapeDtypeStruct((b, W_MAX, PW), f32),
        ] + e_shapes + [
            jax.ShapeDtypeStruct((n_s, D), f32),
            jax.ShapeDtypeStruct((n_s, D), bf16),
            jax.ShapeDtypeStruct((n_s, LANES), bf16),
            jax.ShapeDtypeStruct((8, LANES), f32),
            jax.ShapeDtypeStruct((n_s, PW), f32),
            jax.ShapeDtypeStruct((n_s, D), f32),
        ],
        scratch_shapes=[pltpu.VMEM((W_MAX, PW), f32), pltpu.VMEM((SUB, TL, D), f32)],
        compiler_params=pltpu.CompilerParams(
            dimension_semantics=("arbitrary",), vmem_limit_bytes=VMEM_LIMIT),
    )(x2, x2, mod_p, vecs, w_in_b, w_sp, bias_full, w_pool_b, w_out_b, w_r, b_r,
      w_gate, w_up, w_down, x_s, mod_s, state_t)


def _moe_buffer_rows(ts, nt):
    worst = ts * nt + nt * N_GROUPS * (ROW_ALIGN - 1) + N_GROUPS * (RB - 1)
    return -(-worst // RB) * RB


def _moe_kernel(cnt_ref, h2_ref, r3_ref, x1_ref, mod_ref, g_ref, wg_ref, wu_ref, wd_ref, o_ref,
                hsbuf, hs_tile, ys_tile, pt_buf, pm_buf, tab,
                *, ts, nt, tpr, tps):
    w = pl.program_id(0)
    i = pl.program_id(1)
    rt = ts + LANES

    def copy_runs(tile_buf, run_lens, run_offs, ncols, to_window):
        starts = [0]
        for g in range(N_GROUPS):
            starts.append(starts[-1] + run_lens[g])

        def body(j, _):
            r = j * ROW_ALIGN
            shift = run_offs[0]
            for g in range(1, N_GROUPS):
                shift = jnp.where(r >= starts[g], run_offs[g] - starts[g], shift)
            t0 = pl.multiple_of(r, ROW_ALIGN)
            w0 = pl.multiple_of(r + shift, ROW_ALIGN)
            if to_window:
                hsbuf[pl.ds(w0, ROW_ALIGN), :] = tile_buf[pl.ds(t0, ROW_ALIGN), :ncols]
            else:
                tile_buf[pl.ds(t0, ROW_ALIGN), :] = hsbuf[pl.ds(w0, ROW_ALIGN), :ncols]
            return 0
        lax.fori_loop(0, starts[N_GROUPS] // ROW_ALIGN, body, 0)

    @pl.when((w == 0) & (i == 0))
    def _():
        hsbuf[...] = jnp.zeros_like(hsbuf)
        ys_tile[...] = jnp.zeros_like(ys_tile)

    @pl.when(i == 0)
    def _sort_and_run_experts():
        def run_len(tile, g):
            c = cnt_ref[(w * nt + tile) * N_GROUPS + g]
            return ((c + (ROW_ALIGN - 1)) // ROW_ALIGN) * ROW_ALIGN

        lens = [[run_len(t, g) for g in range(N_GROUPS)] for t in range(nt)]
        region = [sum(lens[t][g] for t in range(nt)) for g in range(N_GROUPS)]
        region = [((r + (RB - 1)) // RB) * RB for r in region]
        base = [sum(region[:g]) for g in range(N_GROUPS)]
        offs = list(base)
        woff = []
        for t in range(nt):
            woff.append(list(offs))
            for g in range(N_GROUPS):
                tab[t * 2 * N_GROUPS + g] = lens[t][g]
                tab[t * 2 * N_GROUPS + N_GROUPS + g] = offs[g]
                offs[g] = offs[g] + lens[t][g]

        lane = lax.broadcasted_iota(jnp.int32, (ts, LANES), 1)
        lane_f = lane.astype(f32)
        r_i = lax.broadcasted_iota(jnp.int32, (ts, ts), 0)
        c_i = lax.broadcasted_iota(jnp.int32, (ts, ts), 1)
        ltri = jnp.where(r_i > c_i, 1.0, 0.0).astype(bf16)
        lane8 = lax.broadcasted_iota(jnp.int32, (8, LANES), 1)
        sel = jnp.where(lane8 == 0, float(ROW_ALIGN), jnp.where(lane8 == 1, 1.0, 0.0)).astype(bf16)
        rt_lane = lax.broadcasted_iota(jnp.int32, (ts, rt), 1).astype(f32)
        rt_sub = lax.broadcasted_iota(jnp.int32, (rt, ts), 0).astype(f32)

        tiles = range(nt)
        r3s = [r3_ref[t * ts:(t + 1) * ts, :] for t in tiles]
        gids = [jnp.sum(jnp.where(lane == GIDX_LANE, r3s[t].astype(f32), 0.0), axis=-1,
                        keepdims=True) for t in tiles]
        onehots = [jnp.where(lane_f == gids[t], 1.0, 0.0) for t in tiles]
        ranks = [_dot(ltri, onehots[t].astype(bf16)) for t in tiles]
        poss = []
        for t in tiles:
            seg = jnp.zeros((1, LANES), f32)
            start = 0
            for g in range(N_GROUPS):
                seg = seg + jnp.where(lane[0:1] == g, jnp.asarray(start, jnp.int32).astype(f32), 0.0)
                start = start + lens[t][g]
            poss.append(jnp.sum(onehots[t] * (ranks[t] + seg), axis=-1, keepdims=True))
        for t in tiles:
            pt_buf[t] = jnp.where(rt_lane == poss[t], 1.0, 0.0).astype(bf16)
        pos_rows = []
        for t in tiles:
            q = jnp.floor(poss[t] * (1.0 / ROW_ALIGN))
            digits = jnp.where(lane == 0, q, jnp.where(lane == 1, poss[t] - q * ROW_ALIGN, 0.0))
            pos_rows.append(lax.dot_general(sel, digits.astype(bf16), (((1,), (1,)), ((), ())),
                                            preferred_element_type=f32))
        for t in tiles:
            pm_buf[t] = jnp.where(rt_sub == pos_rows[t][0:1], 1.0, 0.0).astype(bf16)

        for t0 in range(0, nt, tps):
            for k in range(tps):
                t = t0 + k
                p_mat = pm_buf[t]
                hs_tile[k, :, :D] = _dot(p_mat, h2_ref[t * ts:(t + 1) * ts, :]).astype(bf16)
                hs_tile[k, :, D:] = _dot(p_mat, r3_ref[t * ts:(t + 1) * ts, :]).astype(bf16)
            for k in range(tps):
                t = t0 + k
                copy_runs(hs_tile.at[k], lens[t], woff[t], D + LANES, to_window=True)

        def expert_rows(g, r0, nrows):
            rows = hsbuf[pl.ds(r0, nrows), :D]
            gate = _dot(rows, wg_ref[g * D:(g + 1) * D, :])
            up = _dot(rows, wu_ref[g * D:(g + 1) * D, :])
            c3 = hsbuf[pl.ds(r0, nrows), D:].astype(f32)
            cw_lanes = c3 + pltpu.roll(c3, LANES - N_EXPERTS, 1) + pltpu.roll(c3, LANES - 2 * N_EXPERTS, 1)
            cw = jnp.concatenate(
                [jnp.broadcast_to(cw_lanes[:, g * EPG + j:g * EPG + j + 1], (nrows, D_EXPERT))
                 for j in range(EPG)], axis=1)
            act = gate * jax.nn.sigmoid(gate) * up * cw
            hsbuf[pl.ds(r0, nrows), :D] = _dot(
                act.astype(bf16), wd_ref[g * D:(g + 1) * D, :]).astype(bf16)

        for g in range(N_GROUPS):
            n_blocks = region[g] // RB

            def two_blocks(b, _, g=g):
                expert_rows(g, pl.multiple_of(base[g] + 2 * b * RB, RB), 2 * RB)
                return 0
            lax.fori_loop(0, n_blocks // 2, two_blocks, 0)

            @pl.when(n_blocks % 2 == 1)
            def _(g=g, n_blocks=n_blocks):
                expert_rows(g, pl.multiple_of(base[g] + (n_blocks - 1) * RB, RB), RB)

    tiles = [i * tps + k for k in range(tps)]
    for k, tile in enumerate(tiles):
        run_lens = [tab[tile * 2 * N_GROUPS + g] for g in range(N_GROUPS)]
        run_offs = [tab[tile * 2 * N_GROUPS + N_GROUPS + g] for g in range(N_GROUPS)]
        copy_runs(ys_tile.at[k], run_lens, run_offs, D, to_window=False)
    fs = [_dot(pt_buf[tile], ys_tile[k]) for k, tile in enumerate(tiles)]
    for k, tile in enumerate(tiles):
        rows = slice(k * ts, (k + 1) * ts)
        tok0 = (w * nt + tile) * ts
        gt2_row = N_MOD - 1
        gt2 = (mod_ref[gt2_row, pl.ds(tok0, ts), :] if tpr == 1
               else mod_ref[gt2_row, pl.ds(tok0 // tpr, 1), :])
        o_ref[rows, :] = x1_ref[rows, :] + gt2 * _rms(fs[k], g_ref[...])


def _moe_call(cnt, x1, h2, r3, mod, g_post, wg, wu, wd, ts, nt, tpr):
    n = x1.shape[0]
    win = ts * nt
    rbuf = _moe_buffer_rows(ts, nt)
    rt = ts + LANES
    tps = 2 if nt % 2 == 0 else 1
    steps = nt // tps
    grid_spec = pltpu.PrefetchScalarGridSpec(
        num_scalar_prefetch=1,
        grid=(n // win, steps),
        in_specs=[
            pl.BlockSpec((win, D), lambda w, i, c: (w, 0)),
            pl.BlockSpec((win, LANES), lambda w, i, c: (w, 0)),
            pl.BlockSpec((tps * ts, D), lambda w, i, c: (w * steps + i, 0)),
            _const_spec(mod.shape),
            _const_spec(g_post.shape),
            _const_spec(wg.shape),
            _const_spec(wu.shape),
            _const_spec(wd.shape),
        ],
        out_specs=pl.BlockSpec((tps * ts, D), lambda w, i, c: (w * steps + i, 0)),
        scratch_shapes=[
            pltpu.VMEM((rbuf, D + LANES), bf16),
            pltpu.VMEM((tps, rt, D + LANES), bf16),
            pltpu.VMEM((tps, rt, D), bf16),
            pltpu.VMEM((nt, ts, rt), bf16),
            pltpu.VMEM((nt, rt, ts), bf16),
            pltpu.SMEM((nt * 2 * N_GROUPS,), jnp.int32),
        ],
    )
    return pl.pallas_call(
        functools.partial(_moe_kernel, ts=ts, nt=nt, tpr=tpr, tps=tps),
        grid_spec=grid_spec,
        out_shape=jax.ShapeDtypeStruct((n, D), f32),
        compiler_params=pltpu.CompilerParams(
            dimension_semantics=("arbitrary", "arbitrary"), vmem_limit_bytes=VMEM_LIMIT),
    )(cnt, h2, r3, x1, mod, g_post, wg, wu, wd)


def _count_table(cnt):
    return cnt[:, 0, :N_GROUPS].astype(jnp.int32).reshape(-1)


def kernel(x_prompt, x_sample, c_prompt, c_sample, state_pool, w_ada, b_ada, g_pre_mix, g_post_mix, g_pre_ffn, g_post_ffn, w_in, ln_v_g, ln_v_b, w_spatial, b_spatial, w_pool, pool_scale, w_out, w_router_grp, b_router_grp, w_router_exp, b_router_exp, w_exp_gate, w_exp_up, w_exp_down):
    depth = w_in.shape[0]
    assert depth == 1
    b, s, _ = x_prompt.shape
    n_s = x_sample.shape[0]
    l = 0

    mod_p, mod_s, w_in_b, w_out_b = _mod_call(
        c_prompt, c_sample, w_ada[l], b_ada[l], w_in[l], w_out[l])

    ws, bs = w_spatial[l], b_spatial[l]
    zeros = jnp.zeros((D,), f32)
    vecs = jnp.stack([
        g_pre_mix[l], g_post_mix[l], g_pre_ffn[l], g_post_ffn[l], ln_v_g[l], ln_v_b[l],
        pool_scale[l], zeros,
        jnp.repeat(ws[:, 0, 0], HEAD_DIM), jnp.repeat(bs[:, 0], HEAD_DIM),
        zeros, zeros, zeros, zeros, zeros, zeros])
    bias_full = jnp.repeat(bs.T, HEAD_DIM, axis=1)
    pad = LANES - N_EXPERTS - N_GROUPS
    w_r = jnp.concatenate([w_router_exp[l], w_router_grp[l], jnp.zeros((D, pad), f32)], axis=1)
    w_r_hi = w_r.astype(bf16)
    w_r_lo = (w_r - w_r_hi.astype(f32)).astype(bf16)
    w_r2 = jnp.concatenate([w_r_hi, w_r_lo], axis=1)
    b_r = jnp.concatenate([b_router_exp[l], b_router_grp[l], jnp.zeros((pad,), f32)])[None]

    state_t = jnp.transpose(state_pool[l], (1, 0, 2))
    (x1_p, h2_p, r3_p, cnt_p, plast, wg, wu, wd,
     x1_s, h2_s, r3_s, cnt_s, p_s, v_s) = _stage1_call(
        x_prompt, mod_p, vecs, w_in_b, ws, bias_full, w_pool[l], w_out_b, w_r2, b_r,
        w_exp_gate[l], w_exp_up[l], w_exp_down[l], x_sample.reshape(n_s, D), mod_s, state_t)

    g_post = g_post_ffn[l].reshape(1, D)
    y_p = _moe_call(
        _count_table(cnt_p), x1_p.reshape(b * s, D), h2_p.reshape(b * s, D),
        r3_p.reshape(b * s, LANES), mod_p, g_post, wg, wu, wd, TL, MOE_WINDOW // TL, s)
    y_s = _moe_call(
        _count_table(cnt_s[None]), x1_s, h2_s, r3_s, mod_s, g_post, wg, wu, wd, n_s, 1, 1)

    state_pool_prompt = plast[:, 1:][None]
    state_pool_sample = jnp.concatenate([state_pool[l][:, 1:], p_s[:, None, :]], axis=1)[None]
    chunk_v_sample = v_s.reshape(1, n_s, 1, D)
    return (y_p.reshape(b, s, D), y_s.reshape(n_s, 1, D), state_pool_prompt,
            state_pool_sample, chunk_v_sample)
```

```python
import functools

import jax
import jax.numpy as jnp
from jax import lax
from jax.experimental import pallas as pl
from jax.experimental.pallas import tpu as pltpu

D = 1024
CHUNK = 128
HEADS = 8
HEAD_DIM = 128
WINDOWS = (2, 4, 8, 16)
PW = 512
PG = 128
W_MAX = 16
N_GROUPS = 4
EPG = 8
N_EXPERTS = 32
D_EXPERT = 128
EPS = 1e-6
N_MOD = 6
LANES = 128
GROUP_LANE0 = 32
GIDX_LANE = 96

TL = 256
SUB = 2
MOE_WINDOW = 2048
ROW_ALIGN = 16
RB = 128
VMEM_LIMIT = 60 * 1024 * 1024

bf16 = jnp.bfloat16
f32 = jnp.float32


def _rms(x, g):
    ms = jnp.mean(x * x, axis=-1, keepdims=True)
    return x * lax.rsqrt(ms + EPS) * g


def _dot(a, b):
    return jnp.dot(a, b, preferred_element_type=f32)


_GELU_C = 2.0 * 0.7978845608028654


def _gelu(x):
    t = x * ((-_GELU_C) + (-_GELU_C * 0.044715) * (x * x))
    return x / (1.0 + jnp.exp(t))


def _expert_cast_specs(step_to_expert):
    e = step_to_expert
    col_block = pl.BlockSpec((D, D_EXPERT), lambda t: (e(t) // EPG, e(t) % EPG))
    in_specs = [
        pl.BlockSpec((1, D, D_EXPERT), lambda t: (e(t), 0, 0)),
        pl.BlockSpec((1, D, D_EXPERT), lambda t: (e(t), 0, 0)),
        pl.BlockSpec((1, D_EXPERT, D), lambda t: (e(t), 0, 0)),
    ]
    out_specs = [col_block, col_block, pl.BlockSpec((D_EXPERT, D), lambda t: (e(t), 0))]
    out_shapes = [
        jax.ShapeDtypeStruct((N_GROUPS * D, EPG * D_EXPERT), bf16),
        jax.ShapeDtypeStruct((N_GROUPS * D, EPG * D_EXPERT), bf16),
        jax.ShapeDtypeStruct((N_EXPERTS * D_EXPERT, D), bf16),
    ]
    return in_specs, out_specs, out_shapes


def _mod_kernel(cp_ref, cs_ref, w_ref, b_ref, w_in_ref, w_out_ref, op_ref, os_ref, w_in_o, w_out_o):
    w = w_ref[...].astype(bf16)
    for c_ref, o_ref in ((cp_ref, op_ref), (cs_ref, os_ref)):
        c = c_ref[...]
        o_ref[0] = _dot((c * jax.nn.sigmoid(c)).astype(bf16), w) + b_ref[0]
    w_in_o[...] = w_in_ref[...].astype(bf16)
    w_out_o[...] = w_out_ref[...].astype(bf16)


def _mod_call(c_p, c_s, w_ada, b_ada, w_in, w_out):
    nb, n_s = c_p.shape[0], c_s.shape[0]
    in_cols = w_in.shape[1] // N_MOD
    assert w_in.shape[1] == N_MOD * in_cols and in_cols % LANES == 0
    return pl.pallas_call(
        _mod_kernel,
        grid=(N_MOD,),
        in_specs=[
            pl.BlockSpec((nb, D), lambda j: (0, 0)),
            pl.BlockSpec((n_s, D), lambda j: (0, 0)),
            pl.BlockSpec((D, D), lambda j: (0, j)),
            pl.BlockSpec((1, 1, D), lambda j: (j, 0, 0)),
            pl.BlockSpec((D, in_cols), lambda j: (0, j)),
            pl.BlockSpec((D, D), lambda j: (0, 0)),
        ],
        out_specs=[pl.BlockSpec((1, nb, D), lambda j: (j, 0, 0)),
                   pl.BlockSpec((1, n_s, D), lambda j: (j, 0, 0)),
                   pl.BlockSpec((D, in_cols), lambda j: (0, j)),
                   pl.BlockSpec((D, D), lambda j: (0, 0))],
        out_shape=[jax.ShapeDtypeStruct((N_MOD, nb, D), f32),
                   jax.ShapeDtypeStruct((N_MOD, n_s, D), f32),
                   jax.ShapeDtypeStruct(w_in.shape, bf16),
                   jax.ShapeDtypeStruct(w_out.shape, bf16)],
        compiler_params=pltpu.CompilerParams(
            dimension_semantics=("arbitrary",), vmem_limit_bytes=VMEM_LIMIT),
    )(c_p, c_s, w_ada, b_ada.reshape(N_MOD, 1, D), w_in, w_out)


def _route(logits):
    t = logits.shape[0]
    lane = lax.broadcasted_iota(jnp.int32, (t, LANES), 1)
    lane_f = lane.astype(f32)
    neg = -jnp.inf
    big = 1e9
    gmask = (lane >= GROUP_LANE0) & (lane < GROUP_LANE0 + N_GROUPS)
    gl = jnp.where(gmask, logits, neg)
    gmax = jnp.max(gl, axis=-1, keepdims=True)
    g_idx = jnp.min(jnp.where(gl == gmax, lane_f - GROUP_LANE0, big), axis=-1, keepdims=True)
    sumexp = jnp.sum(jnp.where(gmask, jnp.exp(gl - gmax), 0.0), axis=-1, keepdims=True)
    p_g = 1.0 / sumexp
    lane_grp = (lane >> 3).astype(f32)
    emask = (lane < N_EXPERTS) & (lane_grp == g_idx)
    el = jnp.where(emask, logits, neg)
    m1 = jnp.max(el, axis=-1, keepdims=True)
    i1 = jnp.min(jnp.where(el == m1, lane_f, big), axis=-1, keepdims=True)
    el2 = jnp.where(lane_f == i1, neg, el)
    m2 = jnp.max(el2, axis=-1, keepdims=True)
    i2 = jnp.min(jnp.where(el2 == m2, lane_f, big), axis=-1, keepdims=True)
    e = jnp.exp(m2 - m1)
    w1 = p_g / (1.0 + e)
    w2 = w1 * e

    def split3(w):
        hi = w.astype(bf16).astype(f32)
        mid = (w - hi).astype(bf16).astype(f32)
        lo = w - hi - mid
        return hi, mid, lo

    r3 = jnp.where(lane == GIDX_LANE, g_idx, 0.0)
    for idx, w in ((i1, w1), (i2, w2)):
        for part, wp in enumerate(split3(w)):
            r3 = r3 + jnp.where(lane_f == idx + float(part * N_EXPERTS), wp, 0.0)
    counts = jnp.sum(jnp.where(lane_f == g_idx, 1.0, 0.0), axis=0, keepdims=True)
    return r3.astype(bf16), jnp.broadcast_to(counts, (8, LANES))


def _merge_project(u, ga, gb, mix, y_b, w_out_ref):
    y_a = u * mix
    merged = jax.nn.sigmoid(ga) * y_a + jax.nn.sigmoid(gb) * y_b
    return _dot(merged.astype(bf16), w_out_ref[...])


def _residual_route(x, y, mods, vec_ref, w_r_ref, b_r_ref):
    sh1, sc1, gt1, sh2, sc2, gt2 = mods
    x1 = x + gt1 * _rms(y, vec_ref[1:2])
    h2 = _rms(x1, vec_ref[2:3] * (1.0 + sc2)) + sh2
    h2_hi = h2.astype(bf16)
    h2_lo = (h2 - h2_hi.astype(f32)).astype(bf16)
    r = _dot(h2_hi, w_r_ref[...]) + _dot(h2_lo, w_r_ref[...])
    logits = r[:, :LANES] + r[:, LANES:] + b_r_ref[...]
    r3, counts = _route(logits)
    return x1, h2_hi, r3, counts


def _in_proj(x, mods, vec_ref, w_in_ref, after_first_dot=lambda: None):
    sh1, sc1 = mods[0], mods[1]
    h = _rms(x, vec_ref[0:1] * (1.0 + sc1)) + sh1
    hb = h.astype(bf16)
    zu = _dot(hb, w_in_ref[:, 0:D])
    after_first_dot()
    zv = _dot(hb, w_in_ref[:, D:2 * D])
    p = _dot(hb, w_in_ref[:, 2 * D:2 * D + PW])
    ga = _dot(hb, w_in_ref[:, 2 * D + PW:3 * D + PW])
    gb = _dot(hb, w_in_ref[:, 3 * D + PW:4 * D + PW])
    return zu, zv, p, ga, gb


def _activate(zu, zv, vec_ref):
    u = _gelu(zu)
    v = _gelu(zv)
    mu = jnp.mean(v, axis=-1, keepdims=True)
    vc = v - mu
    var = jnp.mean(vc * vc, axis=-1, keepdims=True)
    v = vc * lax.rsqrt(var + EPS) * vec_ref[4:5] + vec_ref[5:6]
    return u, v


def _pool_out(d_groups, vec_ref, w_pool_ref):
    parts = [_dot(d.astype(bf16), w_pool_ref[gi].astype(bf16)) for gi, d in enumerate(d_groups)]
    return jnp.concatenate(parts, axis=1) * vec_ref[6:7]


def _stage1_kernel(x_ref, xprev_ref, mod_ref, vec_ref, w_in_ref, w_sp_ref, bias_ref,
                   w_pool_ref, w_out_ref, w_r_ref, b_r_ref, eg_ref, eu_ref, ed_ref,
                   xs_ref, mods_ref, states_ref,
                   x1_ref, h2_ref, r3_ref, cnt_ref, plast_ref, wg_ref, wu_ref, wd_ref,
                   x1s_ref, h2s_ref, r3s_ref, cnts_ref, ps_ref, vs_ref,
                   pbuf, ybuf, *, tiles_per_seq):
    t = pl.program_id(0)
    last = pl.num_programs(0) - 1

    def mods_of(tile):
        b = tile // tiles_per_seq
        return [mod_ref[i, pl.ds(b, 1), :] for i in range(N_MOD)]

    def second_half(k):
        rows = slice(k * TL, (k + 1) * TL)
        tile = jnp.maximum((t - 1) * SUB + k, 0)
        x1, h2b, r3, counts = _residual_route(xprev_ref[rows, :], ybuf[k], mods_of(tile), vec_ref,
                                              w_r_ref, b_r_ref)
        x1_ref[rows, :] = x1
        h2_ref[rows, :] = h2b
        r3_ref[rows, :] = r3
        cnt_ref[k] = counts

    def cast_expert():
        wg_ref[...] = eg_ref[0].astype(bf16)
        wu_ref[...] = eu_ref[0].astype(bf16)
        wd_ref[...] = ed_ref[0].astype(bf16)

    def first_half(k, under_projection):
        tile = t * SUB + k
        s = tile % tiles_per_seq
        x = x_ref[k * TL:(k + 1) * TL, :]
        zu, zv, p, ga, gb = _in_proj(x, mods_of(tile), vec_ref, w_in_ref, under_projection)
        u, v = _activate(zu, zv, vec_ref)

        vb = v.astype(bf16)
        row = lax.broadcasted_iota(jnp.int32, (CHUNK, CHUNK), 0)
        col = lax.broadcasted_iota(jnp.int32, (CHUNK, CHUNK), 1)
        w_tril = [jnp.where(row >= col, w_sp_ref[hd], 0.0).astype(bf16) for hd in range(HEADS)]
        bias = bias_ref[...]
        chunks = []
        for c in range(TL // CHUNK):
            heads = [_dot(w_tril[hd],
                          vb[c * CHUNK:(c + 1) * CHUNK, hd * HEAD_DIM:(hd + 1) * HEAD_DIM])
                     for hd in range(HEADS)]
            chunks.append(jnp.concatenate(heads, axis=1) + bias)
        mix = jnp.concatenate(chunks, axis=0)

        carry = jnp.where(s == 0, 0.0, pbuf[...])
        ext = jnp.concatenate([carry, p], axis=0)
        pos = s * TL + lax.broadcasted_iota(jnp.int32, (TL, PG), 0)
        d_groups = []
        for gi, w in enumerate(WINDOWS):
            acc = ext[:, gi * PG:(gi + 1) * PG]
            shift = 1
            while shift < w:
                acc = acc + pltpu.roll(acc, shift, 0)
                shift *= 2
            cnt = jnp.minimum(pos + 1, w).astype(f32)
            d_groups.append(acc[W_MAX:] / cnt - p[:, gi * PG:(gi + 1) * PG])
        pbuf[...] = p[TL - W_MAX:]
        plast_ref[0] = p[TL - W_MAX:]
        y_b = _pool_out(d_groups, vec_ref, w_pool_ref)
        ybuf[k] = _merge_project(u, ga, gb, mix, y_b, w_out_ref)

    @pl.when(t == 0)
    def _():
        ybuf[...] = jnp.zeros_like(ybuf)
        pbuf[...] = jnp.zeros_like(pbuf)

    @pl.when(t < last)
    def _():
        for k in range(SUB):
            def under_projection(k=k):
                if k == 0:
                    cast_expert()
                second_half(k)
            first_half(k, under_projection)

    @pl.when(t == last)
    def _():
        for k in range(SUB):
            second_half(k)
        _sample_tokens(xs_ref, mods_ref, vec_ref, w_in_ref, states_ref, w_pool_ref, w_out_ref,
                       w_r_ref, b_r_ref, x1s_ref, h2s_ref, r3s_ref, cnts_ref, ps_ref, vs_ref)


def _sample_tokens(x_ref, mod_ref, vec_ref, w_in_ref, state_ref, w_pool_ref,
                   w_out_ref, w_r_ref, b_r_ref,
                   x1_ref, h2_ref, r3_ref, cnt_ref, p_ref, v_ref):
    x = x_ref[...]
    mods = [mod_ref[i] for i in range(N_MOD)]
    zu, zv, p, ga, gb = _in_proj(x, mods, vec_ref, w_in_ref)
    u, v = _activate(zu, zv, vec_ref)
    v_ref[...] = v
    p_ref[...] = p
    mix = v * vec_ref[8:9] + vec_ref[9:10]
    d_groups = []
    for gi, w in enumerate(WINDOWS):
        sl = slice(gi * PG, (gi + 1) * PG)
        acc = p[:, sl]
        for r in range(W_MAX - w, W_MAX - 1):
            acc = acc + state_ref[r][:, sl]
        d_groups.append(acc / float(w) - p[:, sl])
    y_b = _pool_out(d_groups, vec_ref, w_pool_ref)
    y = _merge_project(u, ga, gb, mix, y_b, w_out_ref)
    x1, h2b, r3, counts = _residual_route(x, y, mods, vec_ref, w_r_ref, b_r_ref)
    x1_ref[...] = x1
    h2_ref[...] = h2b
    r3_ref[...] = r3
    cnt_ref[...] = counts


def _const_spec(shape):
    nd = len(shape)
    return pl.BlockSpec(shape, lambda *_: (0,) * nd, pipeline_mode=pl.Buffered(1))


def _stage1_call(x, mod_p, vecs, w_in_b, w_sp, bias_full, w_pool_b, w_out_b, w_r, b_r,
                 w_gate, w_up, w_down, x_s, mod_s, state_t):
    b, s, _ = x.shape
    n_s = x_s.shape[0]
    const_out = lambda shape: pl.BlockSpec(shape, lambda t: (0,) * len(shape))
    ns = s // TL
    nt = b * ns
    steps = nt // SUB
    assert ns % SUB == 0
    assert steps >= N_EXPERTS
    x2 = x.reshape(b * s, D)
    rows = SUB * TL
    cur = lambda t: (jnp.minimum(t, steps - 1), 0)
    prev = lambda t: (jnp.maximum(t - 1, 0), 0)
    e_in, e_out, e_shapes = _expert_cast_specs(lambda t: jnp.minimum(t, N_EXPERTS - 1))
    return pl.pallas_call(
        functools.partial(_stage1_kernel, tiles_per_seq=ns),
        grid=(steps + 1,),
        in_specs=[
            pl.BlockSpec((rows, D), cur),
            pl.BlockSpec((rows, D), prev),
            _const_spec(mod_p.shape),
            _const_spec(vecs.shape),
            _const_spec(w_in_b.shape),
            _const_spec(w_sp.shape),
            _const_spec(bias_full.shape),
            _const_spec(w_pool_b.shape),
            _const_spec(w_out_b.shape),
            _const_spec(w_r.shape),
            _const_spec(b_r.shape),
        ] + e_in + [_const_spec(x_s.shape), _const_spec(mod_s.shape), _const_spec(state_t.shape)],
        out_specs=[
            pl.BlockSpec((rows, D), prev),
            pl.BlockSpec((rows, D), prev),
            pl.BlockSpec((rows, LANES), prev),
            pl.BlockSpec((SUB, 8, LANES), lambda t: (jnp.maximum(t - 1, 0), 0, 0)),
            pl.BlockSpec((1, W_MAX, PW),
                         lambda t: (jnp.minimum(t, steps - 1) * SUB // ns, 0, 0)),
        ] + e_out + [const_out((n_s, D)), const_out((n_s, D)), const_out((n_s, LANES)),
                     const_out((8, LANES)), const_out((n_s, PW)), const_out((n_s, D))],
        out_shape=[
            jax.ShapeDtypeStruct((b * s, D), f32),
            jax.ShapeDtypeStruct((b * s, D), bf16),
            jax.ShapeDtypeStruct((b * s, LANES), bf16),
            jax.ShapeDtypeStruct((nt, 8, LANES), f32),
            jax.ShapeDtypeStruct((b, W_MAX, PW), f32),
        ] + e_shapes + [
            jax.ShapeDtypeStruct((n_s, D), f32),
            jax.ShapeDtypeStruct((n_s, D), bf16),
            jax.ShapeDtypeStruct((n_s, LANES), bf16),
            jax.ShapeDtypeStruct((8, LANES), f32),
            jax.ShapeDtypeStruct((n_s, PW), f32),
            jax.ShapeDtypeStruct((n_s, D), f32),
        ],
        scratch_shapes=[pltpu.VMEM((W_MAX, PW), f32), pltpu.VMEM((SUB, TL, D), f32)],
        compiler_params=pltpu.CompilerParams(
            dimension_semantics=("arbitrary",), vmem_limit_bytes=VMEM_LIMIT),
    )(x2, x2, mod_p, vecs, w_in_b, w_sp, bias_full, w_pool_b, w_out_b, w_r, b_r,
      w_gate, w_up, w_down, x_s, mod_s, state_t)


def _moe_buffer_rows(ts, nt):
    worst = ts * nt + nt * N_GROUPS * (ROW_ALIGN - 1) + N_GROUPS * (RB - 1)
    return -(-worst // RB) * RB


def _moe_kernel(cnt_ref, h2_ref, r3_ref, x1_ref, mod_ref, g_ref, wg_ref, wu_ref, wd_ref, o_ref,
                hsbuf, hs_tile, ys_tile, pt_buf, pm_buf, tab,
                *, ts, nt, tpr, tps):
    w = pl.program_id(0)
    i = pl.program_id(1)
    rt = ts + LANES

    def copy_rows(src, src0, dst, dst0, nrows, ncols):
        def body(j, _):
            s0 = pl.multiple_of(src0 + j * ROW_ALIGN, ROW_ALIGN)
            d0 = pl.multiple_of(dst0 + j * ROW_ALIGN, ROW_ALIGN)
            dst[pl.ds(d0, ROW_ALIGN), :] = src[pl.ds(s0, ROW_ALIGN), :ncols]
            return 0
        lax.fori_loop(0, nrows // ROW_ALIGN, body, 0)

    @pl.when((w == 0) & (i == 0))
    def _():
        hsbuf[...] = jnp.zeros_like(hsbuf)
        ys_tile[...] = jnp.zeros_like(ys_tile)

    @pl.when(i == 0)
    def _sort_and_run_experts():
        def run_len(tile, g):
            c = cnt_ref[(w * nt + tile) * N_GROUPS + g]
            return ((c + (ROW_ALIGN - 1)) // ROW_ALIGN) * ROW_ALIGN

        lens = [[run_len(t, g) for g in range(N_GROUPS)] for t in range(nt)]
        region = [sum(lens[t][g] for t in range(nt)) for g in range(N_GROUPS)]
        region = [((r + (RB - 1)) // RB) * RB for r in region]
        base = [sum(region[:g]) for g in range(N_GROUPS)]
        offs = list(base)
        woff = []
        for t in range(nt):
            woff.append(list(offs))
            for g in range(N_GROUPS):
                tab[t * 2 * N_GROUPS + g] = lens[t][g]
                tab[t * 2 * N_GROUPS + N_GROUPS + g] = offs[g]
                offs[g] = offs[g] + lens[t][g]

        lane = lax.broadcasted_iota(jnp.int32, (ts, LANES), 1)
        lane_f = lane.astype(f32)
        r_i = lax.broadcasted_iota(jnp.int32, (ts, ts), 0)
        c_i = lax.broadcasted_iota(jnp.int32, (ts, ts), 1)
        ltri = jnp.where(r_i > c_i, 1.0, 0.0).astype(bf16)
        lane8 = lax.broadcasted_iota(jnp.int32, (8, LANES), 1)
        sel = jnp.where(lane8 == 0, float(ROW_ALIGN), jnp.where(lane8 == 1, 1.0, 0.0)).astype(bf16)
        rt_lane = lax.broadcasted_iota(jnp.int32, (ts, rt), 1).astype(f32)
        rt_sub = lax.broadcasted_iota(jnp.int32, (rt, ts), 0).astype(f32)

        tiles = range(nt)
        r3s = [r3_ref[t * ts:(t + 1) * ts, :] for t in tiles]
        gids = [jnp.sum(jnp.where(lane == GIDX_LANE, r3s[t].astype(f32), 0.0), axis=-1,
                        keepdims=True) for t in tiles]
        onehots = [jnp.where(lane_f == gids[t], 1.0, 0.0) for t in tiles]
        ranks = [_dot(ltri, onehots[t].astype(bf16)) for t in tiles]
        poss = []
        for t in tiles:
            seg = jnp.zeros((1, LANES), f32)
            start = 0
            for g in range(N_GROUPS):
                seg = seg + jnp.where(lane[0:1] == g, jnp.asarray(start, jnp.int32).astype(f32), 0.0)
                start = start + lens[t][g]
            poss.append(jnp.sum(onehots[t] * (ranks[t] + seg), axis=-1, keepdims=True))
        for t in tiles:
            pt_buf[t] = jnp.where(rt_lane == poss[t], 1.0, 0.0).astype(bf16)
        pos_rows = []
        for t in tiles:
            q = jnp.floor(poss[t] * (1.0 / ROW_ALIGN))
            digits = jnp.where(lane == 0, q, jnp.where(lane == 1, poss[t] - q * ROW_ALIGN, 0.0))
            pos_rows.append(lax.dot_general(sel, digits.astype(bf16), (((1,), (1,)), ((), ())),
                                            preferred_element_type=f32))
        for t in tiles:
            pm_buf[t] = jnp.where(rt_sub == pos_rows[t][0:1], 1.0, 0.0).astype(bf16)

        for t0 in range(0, nt, tps):
            for k in range(tps):
                t = t0 + k
                p_mat = pm_buf[t]
                hs_tile[k, :, :D] = _dot(p_mat, h2_ref[t * ts:(t + 1) * ts, :]).astype(bf16)
                hs_tile[k, :, D:] = _dot(p_mat, r3_ref[t * ts:(t + 1) * ts, :]).astype(bf16)
            for k in range(tps):
                t = t0 + k
                start = 0
                for g in range(N_GROUPS):
                    copy_rows(hs_tile.at[k], start, hsbuf, woff[t][g], lens[t][g], D + LANES)
                    start = start + lens[t][g]

        def expert_rows(g, r0, nrows):
            rows = hsbuf[pl.ds(r0, nrows), :D]
            gate = _dot(rows, wg_ref[g * D:(g + 1) * D, :])
            up = _dot(rows, wu_ref[g * D:(g + 1) * D, :])
            c3 = hsbuf[pl.ds(r0, nrows), D:].astype(f32)
            cw_lanes = c3 + pltpu.roll(c3, LANES - N_EXPERTS, 1) + pltpu.roll(c3, LANES - 2 * N_EXPERTS, 1)
            cw = jnp.concatenate(
                [jnp.broadcast_to(cw_lanes[:, g * EPG + j:g * EPG + j + 1], (nrows, D_EXPERT))
                 for j in range(EPG)], axis=1)
            act = gate * jax.nn.sigmoid(gate) * up * cw
            hsbuf[pl.ds(r0, nrows), :D] = _dot(
                act.astype(bf16), wd_ref[g * D:(g + 1) * D, :]).astype(bf16)

        for g in range(N_GROUPS):
            n_blocks = region[g] // RB

            def three_blocks(b, _, g=g):
                expert_rows(g, pl.multiple_of(base[g] + 3 * b * RB, RB), 3 * RB)
                return 0
            lax.fori_loop(0, n_blocks // 3, three_blocks, 0)
            done = (n_blocks // 3) * 3

            @pl.when(n_blocks - done == 2)
            def _(g=g, done=done):
                expert_rows(g, pl.multiple_of(base[g] + done * RB, RB), 2 * RB)

            @pl.when(n_blocks - done == 1)
            def _(g=g, done=done):
                expert_rows(g, pl.multiple_of(base[g] + done * RB, RB), RB)

    tiles = [i * tps + k for k in range(tps)]
    for k, tile in enumerate(tiles):
        start = 0
        for g in range(N_GROUPS):
            ln = tab[tile * 2 * N_GROUPS + g]
            copy_rows(hsbuf, tab[tile * 2 * N_GROUPS + N_GROUPS + g], ys_tile.at[k], start, ln, D)
            start = start + ln
    fs = [_dot(pt_buf[tile], ys_tile[k]) for k, tile in enumerate(tiles)]
    for k, tile in enumerate(tiles):
        rows = slice(k * ts, (k + 1) * ts)
        tok0 = (w * nt + tile) * ts
        gt2_row = N_MOD - 1
        gt2 = (mod_ref[gt2_row, pl.ds(tok0, ts), :] if tpr == 1
               else mod_ref[gt2_row, pl.ds(tok0 // tpr, 1), :])
        o_ref[rows, :] = x1_ref[rows, :] + gt2 * _rms(fs[k], g_ref[...])


def _moe_call(cnt, x1, h2, r3, mod, g_post, wg, wu, wd, ts, nt, tpr):
    n = x1.shape[0]
    win = ts * nt
    rbuf = _moe_buffer_rows(ts, nt)
    rt = ts + LANES
    tps = 2 if nt % 2 == 0 else 1
    steps = nt // tps
    grid_spec = pltpu.PrefetchScalarGridSpec(
        num_scalar_prefetch=1,
        grid=(n // win, steps),
        in_specs=[
            pl.BlockSpec((win, D), lambda w, i, c: (w, 0)),
            pl.BlockSpec((win, LANES), lambda w, i, c: (w, 0)),
            pl.BlockSpec((tps * ts, D), lambda w, i, c: (w * steps + i, 0)),
            _const_spec(mod.shape),
            _const_spec(g_post.shape),
            _const_spec(wg.shape),
            _const_spec(wu.shape),
            _const_spec(wd.shape),
        ],
        out_specs=pl.BlockSpec((tps * ts, D), lambda w, i, c: (w * steps + i, 0)),
        scratch_shapes=[
            pltpu.VMEM((rbuf, D + LANES), bf16),
            pltpu.VMEM((tps, rt, D + LANES), bf16),
            pltpu.VMEM((tps, rt, D), bf16),
            pltpu.VMEM((nt, ts, rt), bf16),
            pltpu.VMEM((nt, rt, ts), bf16),
            pltpu.SMEM((nt * 2 * N_GROUPS,), jnp.int32),
        ],
    )
    return pl.pallas_call(
        functools.partial(_moe_kernel, ts=ts, nt=nt, tpr=tpr, tps=tps),
        grid_spec=grid_spec,
        out_shape=jax.ShapeDtypeStruct((n, D), f32),
        compiler_params=pltpu.CompilerParams(
            dimension_semantics=("arbitrary", "arbitrary"), vmem_limit_bytes=VMEM_LIMIT),
    )(cnt, h2, r3, x1, mod, g_post, wg, wu, wd)


def _count_table(cnt):
    return cnt[:, 0, :N_GROUPS].astype(jnp.int32).reshape(-1)


def kernel(x_prompt, x_sample, c_prompt, c_sample, state_pool, w_ada, b_ada, g_pre_mix, g_post_mix, g_pre_ffn, g_post_ffn, w_in, ln_v_g, ln_v_b, w_spatial, b_spatial, w_pool, pool_scale, w_out, w_router_grp, b_router_grp, w_router_exp, b_router_exp, w_exp_gate, w_exp_up, w_exp_down):
    depth = w_in.shape[0]
    assert depth == 1
    b, s, _ = x_prompt.shape
    n_s = x_sample.shape[0]
    l = 0

    mod_p, mod_s, w_in_b, w_out_b = _mod_call(
        c_prompt, c_sample, w_ada[l], b_ada[l], w_in[l], w_out[l])

    ws, bs = w_spatial[l], b_spatial[l]
    zeros = jnp.zeros((D,), f32)
    vecs = jnp.stack([
        g_pre_mix[l], g_post_mix[l], g_pre_ffn[l], g_post_ffn[l], ln_v_g[l], ln_v_b[l],
        pool_scale[l], zeros,
        jnp.repeat(ws[:, 0, 0], HEAD_DIM), jnp.repeat(bs[:, 0], HEAD_DIM),
        zeros, zeros, zeros, zeros, zeros, zeros])
    bias_full = jnp.repeat(bs.T, HEAD_DIM, axis=1)
    pad = LANES - N_EXPERTS - N_GROUPS
    w_r = jnp.concatenate([w_router_exp[l], w_router_grp[l], jnp.zeros((D, pad), f32)], axis=1)
    w_r_hi = w_r.astype(bf16)
    w_r_lo = (w_r - w_r_hi.astype(f32)).astype(bf16)
    w_r2 = jnp.concatenate([w_r_hi, w_r_lo], axis=1)
    b_r = jnp.concatenate([b_router_exp[l], b_router_grp[l], jnp.zeros((pad,), f32)])[None]

    state_t = jnp.transpose(state_pool[l], (1, 0, 2))
    (x1_p, h2_p, r3_p, cnt_p, plast, wg, wu, wd,
     x1_s, h2_s, r3_s, cnt_s, p_s, v_s) = _stage1_call(
        x_prompt, mod_p, vecs, w_in_b, ws, bias_full, w_pool[l], w_out_b, w_r2, b_r,
        w_exp_gate[l], w_exp_up[l], w_exp_down[l], x_sample.reshape(n_s, D), mod_s, state_t)

    g_post = g_post_ffn[l].reshape(1, D)
    y_p = _moe_call(
        _count_table(cnt_p), x1_p.reshape(b * s, D), h2_p.reshape(b * s, D),
        r3_p.reshape(b * s, LANES), mod_p, g_post, wg, wu, wd, TL, MOE_WINDOW // TL, s)
    y_s = _moe_call(
        _count_table(cnt_s[None]), x1_s, h2_s, r3_s, mod_s, g_post, wg, wu, wd, n_s, 1, 1)

    state_pool_prompt = plast[:, 1:][None]
    state_pool_sample = jnp.concatenate([state_pool[l][:, 1:], p_s[:, None, :]], axis=1)[None]
    chunk_v_sample = v_s.reshape(1, n_s, 1, D)
    return (y_p.reshape(b, s, D), y_s.reshape(n_s, 1, D), state_pool_prompt,
            state_pool_sample, chunk_v_sample)
```

```python
import functools

import jax
import jax.numpy as jnp
from jax import lax
from jax.experimental import pallas as pl
from jax.experimental.pallas import tpu as pltpu

D = 1024
CHUNK = 128
HEADS = 8
HEAD_DIM = 128
WINDOWS = (2, 4, 8, 16)
PW = 512
PG = 128
W_MAX = 16
N_GROUPS = 4
EPG = 8
N_EXPERTS = 32
D_EXPERT = 128
EPS = 1e-6
N_MOD = 6
LANES = 128
GROUP_LANE0 = 32
GIDX_LANE = 96

TL = 256
SUB = 2
MOE_WINDOW = 2048
ROW_ALIGN = 16
RB = 128
VMEM_LIMIT = 60 * 1024 * 1024

bf16 = jnp.bfloat16
f32 = jnp.float32


def _rms(x, g):
    ms = jnp.mean(x * x, axis=-1, keepdims=True)
    return x * lax.rsqrt(ms + EPS) * g


def _dot(a, b):
    return jnp.dot(a, b, preferred_element_type=f32)


_GELU_C = 2.0 * 0.7978845608028654


def _gelu(x):
    t = x * ((-_GELU_C) + (-_GELU_C * 0.044715) * (x * x))
    return x / (1.0 + jnp.exp(t))


def _expert_cast_specs(step_to_expert):
    e = step_to_expert
    col_block = pl.BlockSpec((D, D_EXPERT), lambda t: (e(t) // EPG, e(t) % EPG))
    in_specs = [
        pl.BlockSpec((1, D, D_EXPERT), lambda t: (e(t), 0, 0)),
        pl.BlockSpec((1, D, D_EXPERT), lambda t: (e(t), 0, 0)),
        pl.BlockSpec((1, D_EXPERT, D), lambda t: (e(t), 0, 0)),
    ]
    out_specs = [col_block, col_block, pl.BlockSpec((D_EXPERT, D), lambda t: (e(t), 0))]
    out_shapes = [
        jax.ShapeDtypeStruct((N_GROUPS * D, EPG * D_EXPERT), bf16),
        jax.ShapeDtypeStruct((N_GROUPS * D, EPG * D_EXPERT), bf16),
        jax.ShapeDtypeStruct((N_EXPERTS * D_EXPERT, D), bf16),
    ]
    return in_specs, out_specs, out_shapes


def _mod_kernel(cp_ref, cs_ref, w_ref, b_ref, w_in_ref, w_out_ref, op_ref, os_ref, w_in_o, w_out_o):
    w = w_ref[...].astype(bf16)
    for c_ref, o_ref in ((cp_ref, op_ref), (cs_ref, os_ref)):
        c = c_ref[...]
        o_ref[0] = _dot((c * jax.nn.sigmoid(c)).astype(bf16), w) + b_ref[0]
    w_in_o[...] = w_in_ref[...].astype(bf16)
    w_out_o[...] = w_out_ref[...].astype(bf16)


def _mod_call(c_p, c_s, w_ada, b_ada, w_in, w_out):
    nb, n_s = c_p.shape[0], c_s.shape[0]
    in_cols = w_in.shape[1] // N_MOD
    assert w_in.shape[1] == N_MOD * in_cols and in_cols % LANES == 0
    return pl.pallas_call(
        _mod_kernel,
        grid=(N_MOD,),
        in_specs=[
            pl.BlockSpec((nb, D), lambda j: (0, 0)),
            pl.BlockSpec((n_s, D), lambda j: (0, 0)),
            pl.BlockSpec((D, D), lambda j: (0, j)),
            pl.BlockSpec((1, 1, D), lambda j: (j, 0, 0)),
            pl.BlockSpec((D, in_cols), lambda j: (0, j)),
            pl.BlockSpec((D, D), lambda j: (0, 0)),
        ],
        out_specs=[pl.BlockSpec((1, nb, D), lambda j: (j, 0, 0)),
                   pl.BlockSpec((1, n_s, D), lambda j: (j, 0, 0)),
                   pl.BlockSpec((D, in_cols), lambda j: (0, j)),
                   pl.BlockSpec((D, D), lambda j: (0, 0))],
        out_shape=[jax.ShapeDtypeStruct((N_MOD, nb, D), f32),
                   jax.ShapeDtypeStruct((N_MOD, n_s, D), f32),
                   jax.ShapeDtypeStruct(w_in.shape, bf16),
                   jax.ShapeDtypeStruct(w_out.shape, bf16)],
        compiler_params=pltpu.CompilerParams(
            dimension_semantics=("arbitrary",), vmem_limit_bytes=VMEM_LIMIT),
    )(c_p, c_s, w_ada, b_ada.reshape(N_MOD, 1, D), w_in, w_out)


def _route(logits):
    t = logits.shape[0]
    lane = lax.broadcasted_iota(jnp.int32, (t, LANES), 1)
    lane_f = lane.astype(f32)
    neg = -jnp.inf
    big = 1e9
    gmask = (lane >= GROUP_LANE0) & (lane < GROUP_LANE0 + N_GROUPS)
    gl = jnp.where(gmask, logits, neg)
    gmax = jnp.max(gl, axis=-1, keepdims=True)
    g_idx = jnp.min(jnp.where(gl == gmax, lane_f - GROUP_LANE0, big), axis=-1, keepdims=True)
    sumexp = jnp.sum(jnp.where(gmask, jnp.exp(gl - gmax), 0.0), axis=-1, keepdims=True)
    p_g = 1.0 / sumexp
    lane_grp = (lane >> 3).astype(f32)
    emask = (lane < N_EXPERTS) & (lane_grp == g_idx)
    el = jnp.where(emask, logits, neg)
    m1 = jnp.max(el, axis=-1, keepdims=True)
    i1 = jnp.min(jnp.where(el == m1, lane_f, big), axis=-1, keepdims=True)
    el2 = jnp.where(lane_f == i1, neg, el)
    m2 = jnp.max(el2, axis=-1, keepdims=True)
    i2 = jnp.min(jnp.where(el2 == m2, lane_f, big), axis=-1, keepdims=True)
    e = jnp.exp(m2 - m1)
    w1 = p_g / (1.0 + e)
    w2 = w1 * e

    def split3(w):
        hi = w.astype(bf16).astype(f32)
        mid = (w - hi).astype(bf16).astype(f32)
        lo = w - hi - mid
        return hi, mid, lo

    r3 = jnp.where(lane == GIDX_LANE, g_idx, 0.0)
    for idx, w in ((i1, w1), (i2, w2)):
        for part, wp in enumerate(split3(w)):
            r3 = r3 + jnp.where(lane_f == idx + float(part * N_EXPERTS), wp, 0.0)
    counts = jnp.sum(jnp.where(lane_f == g_idx, 1.0, 0.0), axis=0, keepdims=True)
    return r3.astype(bf16), jnp.broadcast_to(counts, (8, LANES))


def _merge_project(u, ga, gb, mix, y_b, w_out_ref):
    y_a = u * mix
    merged = jax.nn.sigmoid(ga) * y_a + jax.nn.sigmoid(gb) * y_b
    return _dot(merged.astype(bf16), w_out_ref[...])


def _residual_route(x, y, mods, vec_ref, w_r_ref, b_r_ref):
    sh1, sc1, gt1, sh2, sc2, gt2 = mods
    x1 = x + gt1 * _rms(y, vec_ref[1:2])
    h2 = _rms(x1, vec_ref[2:3] * (1.0 + sc2)) + sh2
    h2_hi = h2.astype(bf16)
    h2_lo = (h2 - h2_hi.astype(f32)).astype(bf16)
    r = _dot(h2_hi, w_r_ref[...]) + _dot(h2_lo, w_r_ref[...])
    logits = r[:, :LANES] + r[:, LANES:] + b_r_ref[...]
    r3, counts = _route(logits)
    return x1, h2_hi, r3, counts


def _in_proj(x, mods, vec_ref, w_in_ref, after_first_dot=lambda: None):
    sh1, sc1 = mods[0], mods[1]
    h = _rms(x, vec_ref[0:1] * (1.0 + sc1)) + sh1
    hb = h.astype(bf16)
    zu = _dot(hb, w_in_ref[:, 0:D])
    after_first_dot()
    zv = _dot(hb, w_in_ref[:, D:2 * D])
    p = _dot(hb, w_in_ref[:, 2 * D:2 * D + PW])
    ga = _dot(hb, w_in_ref[:, 2 * D + PW:3 * D + PW])
    gb = _dot(hb, w_in_ref[:, 3 * D + PW:4 * D + PW])
    return zu, zv, p, ga, gb


def _activate(zu, zv, vec_ref):
    u = _gelu(zu)
    v = _gelu(zv)
    mu = jnp.mean(v, axis=-1, keepdims=True)
    vc = v - mu
    var = jnp.mean(vc * vc, axis=-1, keepdims=True)
    v = vc * lax.rsqrt(var + EPS) * vec_ref[4:5] + vec_ref[5:6]
    return u, v


def _pool_out(d_groups, vec_ref, w_pool_ref):
    parts = [_dot(d.astype(bf16), w_pool_ref[gi].astype(bf16)) for gi, d in enumerate(d_groups)]
    return jnp.concatenate(parts, axis=1) * vec_ref[6:7]


def _stage1_kernel(x_ref, xprev_ref, mod_ref, vec_ref, w_in_ref, w_sp_ref, bias_ref,
                   w_pool_ref, w_out_ref, w_r_ref, b_r_ref, eg_ref, eu_ref, ed_ref,
                   xs_ref, mods_ref, states_ref,
                   x1_ref, h2_ref, r3_ref, cnt_ref, plast_ref, wg_ref, wu_ref, wd_ref,
                   x1s_ref, h2s_ref, r3s_ref, cnts_ref, ps_ref, vs_ref,
                   pbuf, ybuf, *, tiles_per_seq):
    t = pl.program_id(0)
    last = pl.num_programs(0) - 1

    def mods_of(tile):
        b = tile // tiles_per_seq
        return [mod_ref[i, pl.ds(b, 1), :] for i in range(N_MOD)]

    def second_half(k):
        rows = slice(k * TL, (k + 1) * TL)
        tile = jnp.maximum((t - 1) * SUB + k, 0)
        x1, h2b, r3, counts = _residual_route(xprev_ref[rows, :], ybuf[k], mods_of(tile), vec_ref,
                                              w_r_ref, b_r_ref)
        x1_ref[rows, :] = x1
        h2_ref[rows, :] = h2b
        r3_ref[rows, :] = r3
        cnt_ref[k] = counts

    def cast_expert():
        wg_ref[...] = eg_ref[0].astype(bf16)
        wu_ref[...] = eu_ref[0].astype(bf16)
        wd_ref[...] = ed_ref[0].astype(bf16)

    def first_half(k, under_projection):
        tile = t * SUB + k
        s = tile % tiles_per_seq
        x = x_ref[k * TL:(k + 1) * TL, :]
        zu, zv, p, ga, gb = _in_proj(x, mods_of(tile), vec_ref, w_in_ref, under_projection)
        u, v = _activate(zu, zv, vec_ref)

        vb = v.astype(bf16)
        row = lax.broadcasted_iota(jnp.int32, (CHUNK, CHUNK), 0)
        col = lax.broadcasted_iota(jnp.int32, (CHUNK, CHUNK), 1)
        w_tril = [jnp.where(row >= col, w_sp_ref[hd], 0.0).astype(bf16) for hd in range(HEADS)]
        bias = bias_ref[...]
        chunks = []
        for c in range(TL // CHUNK):
            heads = [_dot(w_tril[hd],
                          vb[c * CHUNK:(c + 1) * CHUNK, hd * HEAD_DIM:(hd + 1) * HEAD_DIM])
                     for hd in range(HEADS)]
            chunks.append(jnp.concatenate(heads, axis=1) + bias)
        mix = jnp.concatenate(chunks, axis=0)

        carry = jnp.where(s == 0, 0.0, pbuf[...])
        ext = jnp.concatenate([carry, p], axis=0)
        pos = s * TL + lax.broadcasted_iota(jnp.int32, (TL, PG), 0)
        d_groups = []
        for gi, w in enumerate(WINDOWS):
            acc = ext[:, gi * PG:(gi + 1) * PG]
            shift = 1
            while shift < w:
                acc = acc + pltpu.roll(acc, shift, 0)
                shift *= 2
            cnt = jnp.minimum(pos + 1, w).astype(f32)
            d_groups.append(acc[W_MAX:] / cnt - p[:, gi * PG:(gi + 1) * PG])
        pbuf[...] = p[TL - W_MAX:]
        plast_ref[0] = p[TL - W_MAX:]
        y_b = _pool_out(d_groups, vec_ref, w_pool_ref)
        ybuf[k] = _merge_project(u, ga, gb, mix, y_b, w_out_ref)

    @pl.when(t == 0)
    def _():
        ybuf[...] = jnp.zeros_like(ybuf)
        pbuf[...] = jnp.zeros_like(pbuf)

    @pl.when(t < last)
    def _():
        for k in range(SUB):
            def under_projection(k=k):
                if k == 0:
                    cast_expert()
                second_half(k)
            first_half(k, under_projection)

    @pl.when(t == last)
    def _():
        for k in range(SUB):
            second_half(k)
        _sample_tokens(xs_ref, mods_ref, vec_ref, w_in_ref, states_ref, w_pool_ref, w_out_ref,
                       w_r_ref, b_r_ref, x1s_ref, h2s_ref, r3s_ref, cnts_ref, ps_ref, vs_ref)


def _sample_tokens(x_ref, mod_ref, vec_ref, w_in_ref, state_ref, w_pool_ref,
                   w_out_ref, w_r_ref, b_r_ref,
                   x1_ref, h2_ref, r3_ref, cnt_ref, p_ref, v_ref):
    x = x_ref[...]
    mods = [mod_ref[i] for i in range(N_MOD)]
    zu, zv, p, ga, gb = _in_proj(x, mods, vec_ref, w_in_ref)
    u, v = _activate(zu, zv, vec_ref)
    v_ref[...] = v
    p_ref[...] = p
    mix = v * vec_ref[8:9] + vec_ref[9:10]
    d_groups = []
    for gi, w in enumerate(WINDOWS):
        sl = slice(gi * PG, (gi + 1) * PG)
        acc = p[:, sl]
        for r in range(W_MAX - w, W_MAX - 1):
            acc = acc + state_ref[r][:, sl]
        d_groups.append(acc / float(w) - p[:, sl])
    y_b = _pool_out(d_groups, vec_ref, w_pool_ref)
    y = _merge_project(u, ga, gb, mix, y_b, w_out_ref)
    x1, h2b, r3, counts = _residual_route(x, y, mods, vec_ref, w_r_ref, b_r_ref)
    x1_ref[...] = x1
    h2_ref[...] = h2b
    r3_ref[...] = r3
    cnt_ref[...] = counts


def _const_spec(shape):
    nd = len(shape)
    return pl.BlockSpec(shape, lambda *_: (0,) * nd, pipeline_mode=pl.Buffered(1))


def _stage1_call(x, mod_p, vecs, w_in_b, w_sp, bias_full, w_pool_b, w_out_b, w_r, b_r,
                 w_gate, w_up, w_down, x_s, mod_s, state_t):
    b, s, _ = x.shape
    n_s = x_s.shape[0]
    const_out = lambda shape: pl.BlockSpec(shape, lambda t: (0,) * len(shape))
    ns = s // TL
    nt = b * ns
    steps = nt // SUB
    assert ns % SUB == 0
    assert steps >= N_EXPERTS
    x2 = x.reshape(b * s, D)
    rows = SUB * TL
    cur = lambda t: (jnp.minimum(t, steps - 1), 0)
    prev = lambda t: (jnp.maximum(t - 1, 0), 0)
    e_in, e_out, e_shapes = _expert_cast_specs(lambda t: jnp.minimum(t, N_EXPERTS - 1))
    return pl.pallas_call(
        functools.partial(_stage1_kernel, tiles_per_seq=ns),
        grid=(steps + 1,),
        in_specs=[
            pl.BlockSpec((rows, D), cur),
            pl.BlockSpec((rows, D), prev),
            _const_spec(mod_p.shape),
            _const_spec(vecs.shape),
            _const_spec(w_in_b.shape),
            _const_spec(w_sp.shape),
            _const_spec(bias_full.shape),
            _const_spec(w_pool_b.shape),
            _const_spec(w_out_b.shape),
            _const_spec(w_r.shape),
            _const_spec(b_r.shape),
        ] + e_in + [_const_spec(x_s.shape), _const_spec(mod_s.shape), _const_spec(state_t.shape)],
        out_specs=[
            pl.BlockSpec((rows, D), prev),
            pl.BlockSpec((rows, D), prev),
            pl.BlockSpec((rows, LANES), prev),
            pl.BlockSpec((SUB, 8, LANES), lambda t: (jnp.maximum(t - 1, 0), 0, 0)),
            pl.BlockSpec((1, W_MAX, PW),
                         lambda t: (jnp.minimum(t, steps - 1) * SUB // ns, 0, 0)),
        ] + e_out + [const_out((n_s, D)), const_out((n_s, D)), const_out((n_s, LANES)),
                     const_out((8, LANES)), const_out((n_s, PW)), const_out((n_s, D))],
        out_shape=[
            jax.ShapeDtypeStruct((b * s, D), f32),
            jax.ShapeDtypeStruct((b * s, D), bf16),
            jax.ShapeDtypeStruct((b * s, LANES), bf16),
            jax.ShapeDtypeStruct((nt, 8, LANES), f32),
            jax.ShapeDtypeStruct((b, W_MAX, PW), f32),
        ] + e_shapes + [
            jax.ShapeDtypeStruct((n_s, D), f32),
            jax.ShapeDtypeStruct((n_s, D), bf16),
            jax.ShapeDtypeStruct((n_s, LANES), bf16),
            jax.ShapeDtypeStruct((8, LANES), f32),
            jax.ShapeDtypeStruct((n_s, PW), f32),
            jax.ShapeDtypeStruct((n_s, D), f32),
        ],
        scratch_shapes=[pltpu.VMEM((W_MAX, PW), f32), pltpu.VMEM((SUB, TL, D), f32)],
        compiler_params=pltpu.CompilerParams(
            dimension_semantics=("arbitrary",), vmem_limit_bytes=VMEM_LIMIT),
    )(x2, x2, mod_p, vecs, w_in_b, w_sp, bias_full, w_pool_b, w_out_b, w_r, b_r,
      w_gate, w_up, w_down, x_s, mod_s, state_t)


def _moe_buffer_rows(ts, nt):
    worst = ts * nt + nt * N_GROUPS * (ROW_ALIGN - 1) + N_GROUPS * (RB - 1)
    return -(-worst // RB) * RB


def _moe_kernel(cnt_ref, h2_ref, r3_ref, x1_ref, mod_ref, g_ref, wg_ref, wu_ref, wd_ref, o_ref,
                hsbuf, hs_tile, ys_tile, pt_buf, tab,
                *, ts, nt, tpr, tps):
    w = pl.program_id(0)
    i = pl.program_id(1)
    rt = ts + LANES

    def copy_rows(src, src0, dst, dst0, nrows, ncols):
        def body(j, _):
            s0 = pl.multiple_of(src0 + j * ROW_ALIGN, ROW_ALIGN)
            d0 = pl.multiple_of(dst0 + j * ROW_ALIGN, ROW_ALIGN)
            dst[pl.ds(d0, ROW_ALIGN), :] = src[pl.ds(s0, ROW_ALIGN), :ncols]
            return 0
        lax.fori_loop(0, nrows // ROW_ALIGN, body, 0)

    @pl.when((w == 0) & (i == 0))
    def _():
        hsbuf[...] = jnp.zeros_like(hsbuf)
        ys_tile[...] = jnp.zeros_like(ys_tile)

    @pl.when(i == 0)
    def _sort_and_run_experts():
        def run_len(tile, g):
            c = cnt_ref[(w * nt + tile) * N_GROUPS + g]
            return ((c + (ROW_ALIGN - 1)) // ROW_ALIGN) * ROW_ALIGN

        lens = [[run_len(t, g) for g in range(N_GROUPS)] for t in range(nt)]
        region = [sum(lens[t][g] for t in range(nt)) for g in range(N_GROUPS)]
        region = [((r + (RB - 1)) // RB) * RB for r in region]
        base = [sum(region[:g]) for g in range(N_GROUPS)]
        offs = list(base)
        woff = []
        for t in range(nt):
            woff.append(list(offs))
            for g in range(N_GROUPS):
                tab[t * 2 * N_GROUPS + g] = lens[t][g]
                tab[t * 2 * N_GROUPS + N_GROUPS + g] = offs[g]
                offs[g] = offs[g] + lens[t][g]

        lane = lax.broadcasted_iota(jnp.int32, (ts, LANES), 1)
        lane_f = lane.astype(f32)
        r_i = lax.broadcasted_iota(jnp.int32, (ts, ts), 0)
        c_i = lax.broadcasted_iota(jnp.int32, (ts, ts), 1)
        ltri = jnp.where(r_i > c_i, 1.0, 0.0).astype(bf16)
        rt_lane = lax.broadcasted_iota(jnp.int32, (ts, rt), 1).astype(f32)

        tiles = range(nt)
        r3s = [r3_ref[t * ts:(t + 1) * ts, :] for t in tiles]
        gids = [jnp.sum(jnp.where(lane == GIDX_LANE, r3s[t].astype(f32), 0.0), axis=-1,
                        keepdims=True) for t in tiles]
        onehots = [jnp.where(lane_f == gids[t], 1.0, 0.0) for t in tiles]
        ranks = [_dot(ltri, onehots[t].astype(bf16)) for t in tiles]
        poss = []
        for t in tiles:
            seg = jnp.zeros((1, LANES), f32)
            start = 0
            for g in range(N_GROUPS):
                seg = seg + jnp.where(lane[0:1] == g, jnp.asarray(start, jnp.int32).astype(f32), 0.0)
                start = start + lens[t][g]
            poss.append(jnp.sum(onehots[t] * (ranks[t] + seg), axis=-1, keepdims=True))
        for t in tiles:
            pt_buf[t] = jnp.where(rt_lane == poss[t], 1.0, 0.0).astype(bf16)
        def sort_rows(t, x):
            return lax.dot_general(pt_buf[t], x, (((0,), (0,)), ((), ())),
                                   preferred_element_type=f32).astype(bf16)

        for t0 in range(0, nt, tps):
            for k in range(tps):
                t = t0 + k
                hs_tile[k, :, :D] = sort_rows(t, h2_ref[t * ts:(t + 1) * ts, :])
                hs_tile[k, :, D:] = sort_rows(t, r3_ref[t * ts:(t + 1) * ts, :])
            for k in range(tps):
                t = t0 + k
                start = 0
                for g in range(N_GROUPS):
                    copy_rows(hs_tile.at[k], start, hsbuf, woff[t][g], lens[t][g], D + LANES)
                    start = start + lens[t][g]

        def expert_rows(g, r0, nrows):
            rows = hsbuf[pl.ds(r0, nrows), :D]
            gate = _dot(rows, wg_ref[g * D:(g + 1) * D, :])
            up = _dot(rows, wu_ref[g * D:(g + 1) * D, :])
            c3 = hsbuf[pl.ds(r0, nrows), D:].astype(f32)
            cw_lanes = c3 + pltpu.roll(c3, LANES - N_EXPERTS, 1) + pltpu.roll(c3, LANES - 2 * N_EXPERTS, 1)
            cw = jnp.concatenate(
                [jnp.broadcast_to(cw_lanes[:, g * EPG + j:g * EPG + j + 1], (nrows, D_EXPERT))
                 for j in range(EPG)], axis=1)
            act = gate * jax.nn.sigmoid(gate) * up * cw
            hsbuf[pl.ds(r0, nrows), :D] = _dot(
                act.astype(bf16), wd_ref[g * D:(g + 1) * D, :]).astype(bf16)

        for g in range(N_GROUPS):
            n_blocks = region[g] // RB

            def three_blocks(b, _, g=g):
                expert_rows(g, pl.multiple_of(base[g] + 3 * b * RB, RB), 3 * RB)
                return 0
            lax.fori_loop(0, n_blocks // 3, three_blocks, 0)
            done = (n_blocks // 3) * 3

            @pl.when(n_blocks - done == 2)
            def _(g=g, done=done):
                expert_rows(g, pl.multiple_of(base[g] + done * RB, RB), 2 * RB)

            @pl.when(n_blocks - done == 1)
            def _(g=g, done=done):
                expert_rows(g, pl.multiple_of(base[g] + done * RB, RB), RB)

    tiles = [i * tps + k for k in range(tps)]
    for k, tile in enumerate(tiles):
        start = 0
        for g in range(N_GROUPS):
            ln = tab[tile * 2 * N_GROUPS + g]
            copy_rows(hsbuf, tab[tile * 2 * N_GROUPS + N_GROUPS + g], ys_tile.at[k], start, ln, D)
            start = start + ln
    fs = [_dot(pt_buf[tile], ys_tile[k]) for k, tile in enumerate(tiles)]
    for k, tile in enumerate(tiles):
        rows = slice(k * ts, (k + 1) * ts)
        tok0 = (w * nt + tile) * ts
        gt2_row = N_MOD - 1
        gt2 = (mod_ref[gt2_row, pl.ds(tok0, ts), :] if tpr == 1
               else mod_ref[gt2_row, pl.ds(tok0 // tpr, 1), :])
        o_ref[rows, :] = x1_ref[rows, :] + gt2 * _rms(fs[k], g_ref[...])


def _moe_call(cnt, x1, h2, r3, mod, g_post, wg, wu, wd, ts, nt, tpr):
    n = x1.shape[0]
    win = ts * nt
    rbuf = _moe_buffer_rows(ts, nt)
    rt = ts + LANES
    tps = 2 if nt % 2 == 0 else 1
    steps = nt // tps
    grid_spec = pltpu.PrefetchScalarGridSpec(
        num_scalar_prefetch=1,
        grid=(n // win, steps),
        in_specs=[
            pl.BlockSpec((win, D), lambda w, i, c: (w, 0)),
            pl.BlockSpec((win, LANES), lambda w, i, c: (w, 0)),
            pl.BlockSpec((tps * ts, D), lambda w, i, c: (w * steps + i, 0)),
            _const_spec(mod.shape),
            _const_spec(g_post.shape),
            _const_spec(wg.shape),
            _const_spec(wu.shape),
            _const_spec(wd.shape),
        ],
        out_specs=pl.BlockSpec((tps * ts, D), lambda w, i, c: (w * steps + i, 0)),
        scratch_shapes=[
            pltpu.VMEM((rbuf, D + LANES), bf16),
            pltpu.VMEM((tps, rt, D + LANES), bf16),
            pltpu.VMEM((tps, rt, D), bf16),
            pltpu.VMEM((nt, ts, rt), bf16),
            pltpu.SMEM((nt * 2 * N_GROUPS,), jnp.int32),
        ],
    )
    return pl.pallas_call(
        functools.partial(_moe_kernel, ts=ts, nt=nt, tpr=tpr, tps=tps),
        grid_spec=grid_spec,
        out_shape=jax.ShapeDtypeStruct((n, D), f32),
        compiler_params=pltpu.CompilerParams(
            dimension_semantics=("arbitrary", "arbitrary"), vmem_limit_bytes=VMEM_LIMIT),
    )(cnt, h2, r3, x1, mod, g_post, wg, wu, wd)


def _count_table(cnt):
    return cnt[:, 0, :N_GROUPS].astype(jnp.int32).reshape(-1)


def kernel(x_prompt, x_sample, c_prompt, c_sample, state_pool, w_ada, b_ada, g_pre_mix, g_post_mix, g_pre_ffn, g_post_ffn, w_in, ln_v_g, ln_v_b, w_spatial, b_spatial, w_pool, pool_scale, w_out, w_router_grp, b_router_grp, w_router_exp, b_router_exp, w_exp_gate, w_exp_up, w_exp_down):
    depth = w_in.shape[0]
    assert depth == 1
    b, s, _ = x_prompt.shape
    n_s = x_sample.shape[0]
    l = 0

    mod_p, mod_s, w_in_b, w_out_b = _mod_call(
        c_prompt, c_sample, w_ada[l], b_ada[l], w_in[l], w_out[l])

    ws, bs = w_spatial[l], b_spatial[l]
    zeros = jnp.zeros((D,), f32)
    vecs = jnp.stack([
        g_pre_mix[l], g_post_mix[l], g_pre_ffn[l], g_post_ffn[l], ln_v_g[l], ln_v_b[l],
        pool_scale[l], zeros,
        jnp.repeat(ws[:, 0, 0], HEAD_DIM), jnp.repeat(bs[:, 0], HEAD_DIM),
        zeros, zeros, zeros, zeros, zeros, zeros])
    bias_full = jnp.repeat(bs.T, HEAD_DIM, axis=1)
    pad = LANES - N_EXPERTS - N_GROUPS
    w_r = jnp.concatenate([w_router_exp[l], w_router_grp[l], jnp.zeros((D, pad), f32)], axis=1)
    w_r_hi = w_r.astype(bf16)
    w_r_lo = (w_r - w_r_hi.astype(f32)).astype(bf16)
    w_r2 = jnp.concatenate([w_r_hi, w_r_lo], axis=1)
    b_r = jnp.concatenate([b_router_exp[l], b_router_grp[l], jnp.zeros((pad,), f32)])[None]

    state_t = jnp.transpose(state_pool[l], (1, 0, 2))
    (x1_p, h2_p, r3_p, cnt_p, plast, wg, wu, wd,
     x1_s, h2_s, r3_s, cnt_s, p_s, v_s) = _stage1_call(
        x_prompt, mod_p, vecs, w_in_b, ws, bias_full, w_pool[l], w_out_b, w_r2, b_r,
        w_exp_gate[l], w_exp_up[l], w_exp_down[l], x_sample.reshape(n_s, D), mod_s, state_t)

    g_post = g_post_ffn[l].reshape(1, D)
    y_p = _moe_call(
        _count_table(cnt_p), x1_p.reshape(b * s, D), h2_p.reshape(b * s, D),
        r3_p.reshape(b * s, LANES), mod_p, g_post, wg, wu, wd, TL, MOE_WINDOW // TL, s)
    y_s = _moe_call(
        _count_table(cnt_s[None]), x1_s, h2_s, r3_s, mod_s, g_post, wg, wu, wd, n_s, 1, 1)

    state_pool_prompt = plast[:, 1:][None]
    state_pool_sample = jnp.concatenate([state_pool[l][:, 1:], p_s[:, None, :]], axis=1)[None]
    chunk_v_sample = v_s.reshape(1, n_s, 1, D)
    return (y_p.reshape(b, s, D), y_s.reshape(n_s, 1, D), state_pool_prompt,
            state_pool_sample, chunk_v_sample)
```

```python
import functools

import jax
import jax.numpy as jnp
from jax import lax
from jax.experimental import pallas as pl
from jax.experimental.pallas import tpu as pltpu

D = 1024
CHUNK = 128
HEADS = 8
HEAD_DIM = 128
WINDOWS = (2, 4, 8, 16)
PW = 512
PG = 128
W_MAX = 16
N_GROUPS = 4
EPG = 8
N_EXPERTS = 32
D_EXPERT = 128
EPS = 1e-6
N_MOD = 6
LANES = 128
GROUP_LANE0 = 32
GIDX_LANE = 96

TL = 256
SUB = 2
MOE_WINDOW = 2048
ROW_ALIGN = 16
RB = 128
VMEM_LIMIT = 60 * 1024 * 1024

bf16 = jnp.bfloat16
f32 = jnp.float32


def _rms(x, g):
    ms = jnp.mean(x * x, axis=-1, keepdims=True)
    return x * lax.rsqrt(ms + EPS) * g


def _dot(a, b):
    return jnp.dot(a, b, preferred_element_type=f32)


_GELU_C = 2.0 * 0.7978845608028654


def _gelu(x):
    t = x * ((-_GELU_C) + (-_GELU_C * 0.044715) * (x * x))
    return x / (1.0 + jnp.exp(t))


def _expert_cast_specs(step_to_expert):
    e = step_to_expert
    col_block = pl.BlockSpec((D, D_EXPERT), lambda t: (e(t) // EPG, e(t) % EPG))
    in_specs = [
        pl.BlockSpec((1, D, D_EXPERT), lambda t: (e(t), 0, 0)),
        pl.BlockSpec((1, D, D_EXPERT), lambda t: (e(t), 0, 0)),
        pl.BlockSpec((1, D_EXPERT, D), lambda t: (e(t), 0, 0)),
    ]
    out_specs = [col_block, col_block, pl.BlockSpec((D_EXPERT, D), lambda t: (e(t), 0))]
    out_shapes = [
        jax.ShapeDtypeStruct((N_GROUPS * D, EPG * D_EXPERT), bf16),
        jax.ShapeDtypeStruct((N_GROUPS * D, EPG * D_EXPERT), bf16),
        jax.ShapeDtypeStruct((N_EXPERTS * D_EXPERT, D), bf16),
    ]
    return in_specs, out_specs, out_shapes


def _mod_kernel(cp_ref, cs_ref, w_ref, b_ref, w_in_ref, w_out_ref, op_ref, os_ref, w_in_o, w_out_o):
    w = w_ref[...].astype(bf16)
    for c_ref, o_ref in ((cp_ref, op_ref), (cs_ref, os_ref)):
        c = c_ref[...]
        o_ref[0] = _dot((c * jax.nn.sigmoid(c)).astype(bf16), w) + b_ref[0]
    w_in_o[...] = w_in_ref[...].astype(bf16)
    w_out_o[...] = w_out_ref[...].astype(bf16)


def _mod_call(c_p, c_s, w_ada, b_ada, w_in, w_out):
    nb, n_s = c_p.shape[0], c_s.shape[0]
    in_cols = w_in.shape[1] // N_MOD
    assert w_in.shape[1] == N_MOD * in_cols and in_cols % LANES == 0
    return pl.pallas_call(
        _mod_kernel,
        grid=(N_MOD,),
        in_specs=[
            pl.BlockSpec((nb, D), lambda j: (0, 0)),
            pl.BlockSpec((n_s, D), lambda j: (0, 0)),
            pl.BlockSpec((D, D), lambda j: (0, j)),
            pl.BlockSpec((1, 1, D), lambda j: (j, 0, 0)),
            pl.BlockSpec((D, in_cols), lambda j: (0, j)),
            pl.BlockSpec((D, D), lambda j: (0, 0)),
        ],
        out_specs=[pl.BlockSpec((1, nb, D), lambda j: (j, 0, 0)),
                   pl.BlockSpec((1, n_s, D), lambda j: (j, 0, 0)),
                   pl.BlockSpec((D, in_cols), lambda j: (0, j)),
                   pl.BlockSpec((D, D), lambda j: (0, 0))],
        out_shape=[jax.ShapeDtypeStruct((N_MOD, nb, D), f32),
                   jax.ShapeDtypeStruct((N_MOD, n_s, D), f32),
                   jax.ShapeDtypeStruct(w_in.shape, bf16),
                   jax.ShapeDtypeStruct(w_out.shape, bf16)],
        compiler_params=pltpu.CompilerParams(
            dimension_semantics=("arbitrary",), vmem_limit_bytes=VMEM_LIMIT),
    )(c_p, c_s, w_ada, b_ada.reshape(N_MOD, 1, D), w_in, w_out)


def _route(logits):
    t = logits.shape[0]
    lane = lax.broadcasted_iota(jnp.int32, (t, LANES), 1)
    lane_f = lane.astype(f32)
    neg = -jnp.inf
    big = 1e9
    gmask = (lane >= GROUP_LANE0) & (lane < GROUP_LANE0 + N_GROUPS)
    gl = jnp.where(gmask, logits, neg)
    gmax = jnp.max(gl, axis=-1, keepdims=True)
    g_idx = jnp.min(jnp.where(gl == gmax, lane_f - GROUP_LANE0, big), axis=-1, keepdims=True)
    sumexp = jnp.sum(jnp.where(gmask, jnp.exp(gl - gmax), 0.0), axis=-1, keepdims=True)
    p_g = 1.0 / sumexp
    lane_grp = (lane >> 3).astype(f32)
    emask = (lane < N_EXPERTS) & (lane_grp == g_idx)
    el = jnp.where(emask, logits, neg)
    m1 = jnp.max(el, axis=-1, keepdims=True)
    i1 = jnp.min(jnp.where(el == m1, lane_f, big), axis=-1, keepdims=True)
    el2 = jnp.where(lane_f == i1, neg, el)
    m2 = jnp.max(el2, axis=-1, keepdims=True)
    i2 = jnp.min(jnp.where(el2 == m2, lane_f, big), axis=-1, keepdims=True)
    e = jnp.exp(m2 - m1)
    w1 = p_g / (1.0 + e)
    w2 = w1 * e

    def split3(w):
        hi = w.astype(bf16).astype(f32)
        mid = (w - hi).astype(bf16).astype(f32)
        lo = w - hi - mid
        return hi, mid, lo

    r3 = jnp.where(lane == GIDX_LANE, g_idx, 0.0)
    for idx, w in ((i1, w1), (i2, w2)):
        for part, wp in enumerate(split3(w)):
            r3 = r3 + jnp.where(lane_f == idx + float(part * N_EXPERTS), wp, 0.0)
    counts = jnp.sum(jnp.where(lane_f == g_idx, 1.0, 0.0), axis=0, keepdims=True)
    return r3.astype(bf16), jnp.broadcast_to(counts, (8, LANES))


def _merge_project(u, ga, gb, mix, y_b, w_out_ref):
    y_a = u * mix
    merged = jax.nn.sigmoid(ga) * y_a + jax.nn.sigmoid(gb) * y_b
    return _dot(merged.astype(bf16), w_out_ref[...])


def _residual_route(x, y, mods, vec_ref, w_r_ref, b_r_ref):
    sh1, sc1, gt1, sh2, sc2, gt2 = mods
    x1 = x + gt1 * _rms(y, vec_ref[1:2])
    h2 = _rms(x1, vec_ref[2:3] * (1.0 + sc2)) + sh2
    h2_hi = h2.astype(bf16)
    h2_lo = (h2 - h2_hi.astype(f32)).astype(bf16)
    r = _dot(h2_hi, w_r_ref[...]) + _dot(h2_lo, w_r_ref[...])
    logits = r[:, :LANES] + r[:, LANES:] + b_r_ref[...]
    r3, counts = _route(logits)
    return x1, h2_hi, r3, counts


def _in_proj(x, mods, vec_ref, w_in_ref, after_first_dot=lambda: None):
    sh1, sc1 = mods[0], mods[1]
    h = _rms(x, vec_ref[0:1] * (1.0 + sc1)) + sh1
    hb = h.astype(bf16)
    zu = _dot(hb, w_in_ref[:, 0:D])
    after_first_dot()
    zv = _dot(hb, w_in_ref[:, D:2 * D])
    p = _dot(hb, w_in_ref[:, 2 * D:2 * D + PW])
    ga = _dot(hb, w_in_ref[:, 2 * D + PW:3 * D + PW])
    gb = _dot(hb, w_in_ref[:, 3 * D + PW:4 * D + PW])
    return zu, zv, p, ga, gb


def _activate(zu, zv, vec_ref):
    u = _gelu(zu)
    v = _gelu(zv)
    mu = jnp.mean(v, axis=-1, keepdims=True)
    vc = v - mu
    var = jnp.mean(vc * vc, axis=-1, keepdims=True)
    v = vc * lax.rsqrt(var + EPS) * vec_ref[4:5] + vec_ref[5:6]
    return u, v


def _pool_out(d_groups, vec_ref, w_pool_ref):
    parts = [_dot(d.astype(bf16), w_pool_ref[gi].astype(bf16)) for gi, d in enumerate(d_groups)]
    return jnp.concatenate(parts, axis=1) * vec_ref[6:7]


def _stage1_kernel(x_ref, xprev_ref, mod_ref, vec_ref, w_in_ref, w_sp_ref, bias_ref,
                   w_pool_ref, w_out_ref, w_r_ref, b_r_ref, eg_ref, eu_ref, ed_ref,
                   xs_ref, mods_ref, states_ref,
                   x1_ref, h2_ref, r3_ref, cnt_ref, plast_ref, wg_ref, wu_ref, wd_ref,
                   x1s_ref, h2s_ref, r3s_ref, cnts_ref, ps_ref, vs_ref,
                   pbuf, ybuf, *, tiles_per_seq):
    t = pl.program_id(0)
    last = pl.num_programs(0) - 1

    def mods_of(tile):
        b = tile // tiles_per_seq
        return [mod_ref[i, pl.ds(b, 1), :] for i in range(N_MOD)]

    def second_half(k):
        rows = slice(k * TL, (k + 1) * TL)
        tile = jnp.maximum((t - 1) * SUB + k, 0)
        x1, h2b, r3, counts = _residual_route(xprev_ref[rows, :], ybuf[k], mods_of(tile), vec_ref,
                                              w_r_ref, b_r_ref)
        x1_ref[rows, :] = x1
        h2_ref[rows, :] = h2b
        r3_ref[rows, :] = r3
        cnt_ref[k] = counts

    def cast_expert():
        wg_ref[...] = eg_ref[0].astype(bf16)
        wu_ref[...] = eu_ref[0].astype(bf16)
        wd_ref[...] = ed_ref[0].astype(bf16)

    def first_half(k, under_projection):
        tile = t * SUB + k
        s = tile % tiles_per_seq
        x = x_ref[k * TL:(k + 1) * TL, :]
        zu, zv, p, ga, gb = _in_proj(x, mods_of(tile), vec_ref, w_in_ref, under_projection)
        u, v = _activate(zu, zv, vec_ref)

        vb = v.astype(bf16)
        row = lax.broadcasted_iota(jnp.int32, (CHUNK, CHUNK), 0)
        col = lax.broadcasted_iota(jnp.int32, (CHUNK, CHUNK), 1)
        w_tril = [jnp.where(row >= col, w_sp_ref[hd], 0.0).astype(bf16) for hd in range(HEADS)]
        bias = bias_ref[...]
        chunks = []
        for c in range(TL // CHUNK):
            heads = [_dot(w_tril[hd],
                          vb[c * CHUNK:(c + 1) * CHUNK, hd * HEAD_DIM:(hd + 1) * HEAD_DIM])
                     for hd in range(HEADS)]
            chunks.append(jnp.concatenate(heads, axis=1) + bias)
        mix = jnp.concatenate(chunks, axis=0)

        carry = jnp.where(s == 0, 0.0, pbuf[...])
        ext = jnp.concatenate([carry, p], axis=0)
        pos = s * TL + lax.broadcasted_iota(jnp.int32, (TL, PG), 0)
        d_groups = []
        for gi, w in enumerate(WINDOWS):
            acc = ext[:, gi * PG:(gi + 1) * PG]
            shift = 1
            while shift < w:
                acc = acc + pltpu.roll(acc, shift, 0)
                shift *= 2
            cnt = jnp.minimum(pos + 1, w).astype(f32)
            d_groups.append(acc[W_MAX:] / cnt - p[:, gi * PG:(gi + 1) * PG])
        pbuf[...] = p[TL - W_MAX:]
        plast_ref[0] = p[TL - W_MAX:]
        y_b = _pool_out(d_groups, vec_ref, w_pool_ref)
        ybuf[k] = _merge_project(u, ga, gb, mix, y_b, w_out_ref)

    @pl.when(t == 0)
    def _():
        ybuf[...] = jnp.zeros_like(ybuf)
        pbuf[...] = jnp.zeros_like(pbuf)

    @pl.when(t < last)
    def _():
        for k in range(SUB):
            def under_projection(k=k):
                if k == 0:
                    cast_expert()
                second_half(k)
            first_half(k, under_projection)

    @pl.when(t == last)
    def _():
        for k in range(SUB):
            second_half(k)
        _sample_tokens(xs_ref, mods_ref, vec_ref, w_in_ref, states_ref, w_pool_ref, w_out_ref,
                       w_r_ref, b_r_ref, x1s_ref, h2s_ref, r3s_ref, cnts_ref, ps_ref, vs_ref)


def _sample_tokens(x_ref, mod_ref, vec_ref, w_in_ref, state_ref, w_pool_ref,
                   w_out_ref, w_r_ref, b_r_ref,
                   x1_ref, h2_ref, r3_ref, cnt_ref, p_ref, v_ref):
    x = x_ref[...]
    mods = [mod_ref[i] for i in range(N_MOD)]
    zu, zv, p, ga, gb = _in_proj(x, mods, vec_ref, w_in_ref)
    u, v = _activate(zu, zv, vec_ref)
    v_ref[...] = v
    p_ref[...] = p
    mix = v * vec_ref[8:9] + vec_ref[9:10]
    d_groups = []
    for gi, w in enumerate(WINDOWS):
        sl = slice(gi * PG, (gi + 1) * PG)
        acc = p[:, sl]
        for r in range(W_MAX - w, W_MAX - 1):
            acc = acc + state_ref[r][:, sl]
        d_groups.append(acc / float(w) - p[:, sl])
    y_b = _pool_out(d_groups, vec_ref, w_pool_ref)
    y = _merge_project(u, ga, gb, mix, y_b, w_out_ref)
    x1, h2b, r3, counts = _residual_route(x, y, mods, vec_ref, w_r_ref, b_r_ref)
    x1_ref[...] = x1
    h2_ref[...] = h2b
    r3_ref[...] = r3
    cnt_ref[...] = counts


def _const_spec(shape):
    nd = len(shape)
    return pl.BlockSpec(shape, lambda *_: (0,) * nd, pipeline_mode=pl.Buffered(1))


def _stage1_call(x, mod_p, vecs, w_in_b, w_sp, bias_full, w_pool_b, w_out_b, w_r, b_r,
                 w_gate, w_up, w_down, x_s, mod_s, state_t):
    b, s, _ = x.shape
    n_s = x_s.shape[0]
    const_out = lambda shape: pl.BlockSpec(shape, lambda t: (0,) * len(shape))
    ns = s // TL
    nt = b * ns
    steps = nt // SUB
    assert ns % SUB == 0
    assert steps >= N_EXPERTS
    x2 = x.reshape(b * s, D)
    rows = SUB * TL
    cur = lambda t: (jnp.minimum(t, steps - 1), 0)
    prev = lambda t: (jnp.maximum(t - 1, 0), 0)
    e_in, e_out, e_shapes = _expert_cast_specs(lambda t: jnp.minimum(t, N_EXPERTS - 1))
    return pl.pallas_call(
        functools.partial(_stage1_kernel, tiles_per_seq=ns),
        grid=(steps + 1,),
        in_specs=[
            pl.BlockSpec((rows, D), cur),
            pl.BlockSpec((rows, D), prev),
            _const_spec(mod_p.shape),
            _const_spec(vecs.shape),
            _const_spec(w_in_b.shape),
            _const_spec(w_sp.shape),
            _const_spec(bias_full.shape),
            _const_spec(w_pool_b.shape),
            _const_spec(w_out_b.shape),
            _const_spec(w_r.shape),
            _const_spec(b_r.shape),
        ] + e_in + [_const_spec(x_s.shape), _const_spec(mod_s.shape), _const_spec(state_t.shape)],
        out_specs=[
            pl.BlockSpec((rows, D), prev),
            pl.BlockSpec((rows, D), prev),
            pl.BlockSpec((rows, LANES), prev),
            pl.BlockSpec((SUB, 8, LANES), lambda t: (jnp.maximum(t - 1, 0), 0, 0)),
            pl.BlockSpec((1, W_MAX, PW),
                         lambda t: (jnp.minimum(t, steps - 1) * SUB // ns, 0, 0)),
        ] + e_out + [const_out((n_s, D)), const_out((n_s, D)), const_out((n_s, LANES)),
                     const_out((8, LANES)), const_out((n_s, PW)), const_out((n_s, D))],
        out_shape=[
            jax.ShapeDtypeStruct((b * s, D), f32),
            jax.ShapeDtypeStruct((b * s, D), bf16),
            jax.ShapeDtypeStruct((b * s, LANES), bf16),
            jax.ShapeDtypeStruct((nt, 8, LANES), f32),
            jax.ShapeDtypeStruct((b, W_MAX, PW), f32),
        ] + e_shapes + [
            jax.ShapeDtypeStruct((n_s, D), f32),
            jax.ShapeDtypeStruct((n_s, D), bf16),
            jax.ShapeDtypeStruct((n_s, LANES), bf16),
            jax.ShapeDtypeStruct((8, LANES), f32),
            jax.ShapeDtypeStruct((n_s, PW), f32),
            jax.ShapeDtypeStruct((n_s, D), f32),
        ],
        scratch_shapes=[pltpu.VMEM((W_MAX, PW), f32), pltpu.VMEM((SUB, TL, D), f32)],
        compiler_params=pltpu.CompilerParams(
            dimension_semantics=("arbitrary",), vmem_limit_bytes=VMEM_LIMIT),
    )(x2, x2, mod_p, vecs, w_in_b, w_sp, bias_full, w_pool_b, w_out_b, w_r, b_r,
      w_gate, w_up, w_down, x_s, mod_s, state_t)


def _moe_buffer_rows(ts, nt):
    worst = ts * nt + nt * N_GROUPS * (ROW_ALIGN - 1) + N_GROUPS * (RB - 1)
    return -(-worst // RB) * RB


def _moe_kernel(cnt_ref, h2_ref, r3_ref, x1_ref, mod_ref, g_ref, wg_ref, wu_ref, wd_ref, o_ref,
                hsbuf, hs_tile, ys_tile, pt_buf, tab,
                *, ts, nt, tpr, tps):
    w = pl.program_id(0)
    i = pl.program_id(1)
    rt = ts + LANES

    def copy_rows(src, src0, dst, dst0, nrows, ncols):
        def body(j, _):
            s0 = pl.multiple_of(src0 + j * ROW_ALIGN, ROW_ALIGN)
            d0 = pl.multiple_of(dst0 + j * ROW_ALIGN, ROW_ALIGN)
            dst[pl.ds(d0, ROW_ALIGN), :] = src[pl.ds(s0, ROW_ALIGN), :ncols]
            return 0
        lax.fori_loop(0, nrows // ROW_ALIGN, body, 0)

    @pl.when((w == 0) & (i == 0))
    def _():
        hsbuf[...] = jnp.zeros_like(hsbuf)
        ys_tile[...] = jnp.zeros_like(ys_tile)

    @pl.when(i == 0)
    def _sort_and_run_experts():
        def run_len(tile, g):
            c = cnt_ref[(w * nt + tile) * N_GROUPS + g]
            return ((c + (ROW_ALIGN - 1)) // ROW_ALIGN) * ROW_ALIGN

        lens = [[run_len(t, g) for g in range(N_GROUPS)] for t in range(nt)]
        region = [sum(lens[t][g] for t in range(nt)) for g in range(N_GROUPS)]
        region = [((r + (RB - 1)) // RB) * RB for r in region]
        base = [sum(region[:g]) for g in range(N_GROUPS)]
        offs = list(base)
        woff = []
        for t in range(nt):
            woff.append(list(offs))
            for g in range(N_GROUPS):
                tab[t * 2 * N_GROUPS + g] = lens[t][g]
                tab[t * 2 * N_GROUPS + N_GROUPS + g] = offs[g]
                offs[g] = offs[g] + lens[t][g]

        lane = lax.broadcasted_iota(jnp.int32, (ts, LANES), 1)
        lane_f = lane.astype(f32)
        r_i = lax.broadcasted_iota(jnp.int32, (ts, ts), 0)
        c_i = lax.broadcasted_iota(jnp.int32, (ts, ts), 1)
        ltri = jnp.where(r_i > c_i, 1.0, 0.0).astype(bf16)
        rt_lane = lax.broadcasted_iota(jnp.int32, (ts, rt), 1).astype(f32)

        tiles = range(nt)
        r3s = [r3_ref[t * ts:(t + 1) * ts, :] for t in tiles]
        gids = [jnp.sum(jnp.where(lane == GIDX_LANE, r3s[t].astype(f32), 0.0), axis=-1,
                        keepdims=True) for t in tiles]
        onehots = [jnp.where(lane_f == gids[t], 1.0, 0.0) for t in tiles]
        ranks = [_dot(ltri, onehots[t].astype(bf16)) for t in tiles]
        poss = []
        for t in tiles:
            seg = jnp.zeros((1, LANES), f32)
            start = 0
            for g in range(N_GROUPS):
                seg = seg + jnp.where(lane[0:1] == g, jnp.asarray(start, jnp.int32).astype(f32), 0.0)
                start = start + lens[t][g]
            poss.append(jnp.sum(onehots[t] * (ranks[t] + seg), axis=-1, keepdims=True))
        for t in tiles:
            pt_buf[t] = jnp.where(rt_lane == poss[t], 1.0, 0.0).astype(bf16)
        def sort_rows(t, x):
            return lax.dot_general(pt_buf[t], x, (((0,), (0,)), ((), ())),
                                   preferred_element_type=f32).astype(bf16)

        per_pass = hs_tile.shape[0]
        for t0 in range(0, nt, per_pass):
            for k in range(per_pass):
                t = t0 + k
                hs_tile[k, :, :D] = sort_rows(t, h2_ref[t * ts:(t + 1) * ts, :])
                hs_tile[k, :, D:] = sort_rows(t, r3_ref[t * ts:(t + 1) * ts, :])
            for k in range(per_pass):
                t = t0 + k
                start = 0
                for g in range(N_GROUPS):
                    copy_rows(hs_tile.at[k], start, hsbuf, woff[t][g], lens[t][g], D + LANES)
                    start = start + lens[t][g]

        def expert_rows(g, r0, nrows):
            rows = hsbuf[pl.ds(r0, nrows), :D]
            gate = _dot(rows, wg_ref[g * D:(g + 1) * D, :])
            up = _dot(rows, wu_ref[g * D:(g + 1) * D, :])
            c3 = hsbuf[pl.ds(r0, nrows), D:].astype(f32)
            cw_lanes = c3 + pltpu.roll(c3, LANES - N_EXPERTS, 1) + pltpu.roll(c3, LANES - 2 * N_EXPERTS, 1)
            cw = jnp.concatenate(
                [jnp.broadcast_to(cw_lanes[:, g * EPG + j:g * EPG + j + 1], (nrows, D_EXPERT))
                 for j in range(EPG)], axis=1)
            act = gate * jax.nn.sigmoid(gate) * up * cw
            hsbuf[pl.ds(r0, nrows), :D] = _dot(
                act.astype(bf16), wd_ref[g * D:(g + 1) * D, :]).astype(bf16)

        for g in range(N_GROUPS):
            n_blocks = region[g] // RB

            def three_blocks(b, _, g=g):
                expert_rows(g, pl.multiple_of(base[g] + 3 * b * RB, RB), 3 * RB)
                return 0
            lax.fori_loop(0, n_blocks // 3, three_blocks, 0)
            done = (n_blocks // 3) * 3

            @pl.when(n_blocks - done == 2)
            def _(g=g, done=done):
                expert_rows(g, pl.multiple_of(base[g] + done * RB, RB), 2 * RB)

            @pl.when(n_blocks - done == 1)
            def _(g=g, done=done):
                expert_rows(g, pl.multiple_of(base[g] + done * RB, RB), RB)

    tiles = [i * tps + k for k in range(tps)]
    for k, tile in enumerate(tiles):
        start = 0
        for g in range(N_GROUPS):
            ln = tab[tile * 2 * N_GROUPS + g]
            copy_rows(hsbuf, tab[tile * 2 * N_GROUPS + N_GROUPS + g], ys_tile.at[k], start, ln, D)
            start = start + ln
    fs = [_dot(pt_buf[tile], ys_tile[k]) for k, tile in enumerate(tiles)]
    for k, tile in enumerate(tiles):
        rows = slice(k * ts, (k + 1) * ts)
        tok0 = (w * nt + tile) * ts
        gt2_row = N_MOD - 1
        gt2 = (mod_ref[gt2_row, pl.ds(tok0, ts), :] if tpr == 1
               else mod_ref[gt2_row, pl.ds(tok0 // tpr, 1), :])
        o_ref[rows, :] = x1_ref[rows, :] + gt2 * _rms(fs[k], g_ref[...])


def _moe_call(cnt, x1, h2, r3, mod, g_post, wg, wu, wd, ts, nt, tpr):
    n = x1.shape[0]
    win = ts * nt
    rbuf = _moe_buffer_rows(ts, nt)
    rt = ts + LANES
    tps = 2 if nt % 2 == 0 else 1
    steps = nt // tps
    grid_spec = pltpu.PrefetchScalarGridSpec(
        num_scalar_prefetch=1,
        grid=(n // win, steps),
        in_specs=[
            pl.BlockSpec((win, D), lambda w, i, c: (w, 0)),
            pl.BlockSpec((win, LANES), lambda w, i, c: (w, 0)),
            pl.BlockSpec((tps * ts, D), lambda w, i, c: (w * steps + i, 0)),
            _const_spec(mod.shape),
            _const_spec(g_post.shape),
            _const_spec(wg.shape),
            _const_spec(wu.shape),
            _const_spec(wd.shape),
        ],
        out_specs=pl.BlockSpec((tps * ts, D), lambda w, i, c: (w * steps + i, 0)),
        scratch_shapes=[
            pltpu.VMEM((rbuf, D + LANES), bf16),
            pltpu.VMEM((4 if nt % 4 == 0 else tps, rt, D + LANES), bf16),
            pltpu.VMEM((tps, rt, D), bf16),
            pltpu.VMEM((nt, ts, rt), bf16),
            pltpu.SMEM((nt * 2 * N_GROUPS,), jnp.int32),
        ],
    )
    return pl.pallas_call(
        functools.partial(_moe_kernel, ts=ts, nt=nt, tpr=tpr, tps=tps),
        grid_spec=grid_spec,
        out_shape=jax.ShapeDtypeStruct((n, D), f32),
        compiler_params=pltpu.CompilerParams(
            dimension_semantics=("arbitrary", "arbitrary"), vmem_limit_bytes=VMEM_LIMIT),
    )(cnt, h2, r3, x1, mod, g_post, wg, wu, wd)


def _count_table(cnt):
    return cnt[:, 0, :N_GROUPS].astype(jnp.int32).reshape(-1)


def kernel(x_prompt, x_sample, c_prompt, c_sample, state_pool, w_ada, b_ada, g_pre_mix, g_post_mix, g_pre_ffn, g_post_ffn, w_in, ln_v_g, ln_v_b, w_spatial, b_spatial, w_pool, pool_scale, w_out, w_router_grp, b_router_grp, w_router_exp, b_router_exp, w_exp_gate, w_exp_up, w_exp_down):
    depth = w_in.shape[0]
    assert depth == 1
    b, s, _ = x_prompt.shape
    n_s = x_sample.shape[0]
    l = 0

    mod_p, mod_s, w_in_b, w_out_b = _mod_call(
        c_prompt, c_sample, w_ada[l], b_ada[l], w_in[l], w_out[l])

    ws, bs = w_spatial[l], b_spatial[l]
    zeros = jnp.zeros((D,), f32)
    vecs = jnp.stack([
        g_pre_mix[l], g_post_mix[l], g_pre_ffn[l], g_post_ffn[l], ln_v_g[l], ln_v_b[l],
        pool_scale[l], zeros,
        jnp.repeat(ws[:, 0, 0], HEAD_DIM), jnp.repeat(bs[:, 0], HEAD_DIM),
        zeros, zeros, zeros, zeros, zeros, zeros])
    bias_full = jnp.repeat(bs.T, HEAD_DIM, axis=1)
    pad = LANES - N_EXPERTS - N_GROUPS
    w_r = jnp.concatenate([w_router_exp[l], w_router_grp[l], jnp.zeros((D, pad), f32)], axis=1)
    w_r_hi = w_r.astype(bf16)
    w_r_lo = (w_r - w_r_hi.astype(f32)).astype(bf16)
    w_r2 = jnp.concatenate([w_r_hi, w_r_lo], axis=1)
    b_r = jnp.concatenate([b_router_exp[l], b_router_grp[l], jnp.zeros((pad,), f32)])[None]

    state_t = jnp.transpose(state_pool[l], (1, 0, 2))
    (x1_p, h2_p, r3_p, cnt_p, plast, wg, wu, wd,
     x1_s, h2_s, r3_s, cnt_s, p_s, v_s) = _stage1_call(
        x_prompt, mod_p, vecs, w_in_b, ws, bias_full, w_pool[l], w_out_b, w_r2, b_r,
        w_exp_gate[l], w_exp_up[l], w_exp_down[l], x_sample.reshape(n_s, D), mod_s, state_t)

    g_post = g_post_ffn[l].reshape(1, D)
    y_p = _moe_call(
        _count_table(cnt_p), x1_p.reshape(b * s, D), h2_p.reshape(b * s, D),
        r3_p.reshape(b * s, LANES), mod_p, g_post, wg, wu, wd, TL, MOE_WINDOW // TL, s)
    y_s = _moe_call(
        _count_table(cnt_s[None]), x1_s, h2_s, r3_s, mod_s, g_post, wg, wu, wd, n_s, 1, 1)

    state_pool_prompt = plast[:, 1:][None]
    state_pool_sample = jnp.concatenate([state_pool[l][:, 1:], p_s[:, None, :]], axis=1)[None]
    chunk_v_sample = v_s.reshape(1, n_s, 1, D)
    return (y_p.reshape(b, s, D), y_s.reshape(n_s, 1, D), state_pool_prompt,
            state_pool_sample, chunk_v_sample)
```

```python
import functools

import jax
import jax.numpy as jnp
from jax import lax
from jax.experimental import pallas as pl
from jax.experimental.pallas import tpu as pltpu

D = 1024
CHUNK = 128
HEADS = 8
HEAD_DIM = 128
WINDOWS = (2, 4, 8, 16)
PW = 512
PG = 128
W_MAX = 16
N_GROUPS = 4
EPG = 8
N_EXPERTS = 32
D_EXPERT = 128
EPS = 1e-6
N_MOD = 6
LANES = 128
GROUP_LANE0 = 32
GIDX_LANE = 96

TL = 256
SUB = 2
MOE_WINDOW = 2048
ROW_ALIGN = 16
RB = 128
VMEM_LIMIT = 60 * 1024 * 1024

bf16 = jnp.bfloat16
f32 = jnp.float32


def _rms(x, g):
    ms = jnp.mean(x * x, axis=-1, keepdims=True)
    return x * lax.rsqrt(ms + EPS) * g


def _dot(a, b):
    return jnp.dot(a, b, preferred_element_type=f32)


_GELU_C = 2.0 * 0.7978845608028654


def _gelu(x):
    t = x * ((-_GELU_C) + (-_GELU_C * 0.044715) * (x * x))
    return x / (1.0 + jnp.exp(t))


def _expert_cast_specs(step_to_expert):
    e = step_to_expert
    col_block = pl.BlockSpec((D, D_EXPERT), lambda t: (e(t) // EPG, e(t) % EPG))
    in_specs = [
        pl.BlockSpec((1, D, D_EXPERT), lambda t: (e(t), 0, 0)),
        pl.BlockSpec((1, D, D_EXPERT), lambda t: (e(t), 0, 0)),
        pl.BlockSpec((1, D_EXPERT, D), lambda t: (e(t), 0, 0)),
    ]
    out_specs = [col_block, col_block, pl.BlockSpec((D_EXPERT, D), lambda t: (e(t), 0))]
    out_shapes = [
        jax.ShapeDtypeStruct((N_GROUPS * D, EPG * D_EXPERT), bf16),
        jax.ShapeDtypeStruct((N_GROUPS * D, EPG * D_EXPERT), bf16),
        jax.ShapeDtypeStruct((N_EXPERTS * D_EXPERT, D), bf16),
    ]
    return in_specs, out_specs, out_shapes


def _mod_kernel(cp_ref, cs_ref, w_ref, b_ref, w_in_ref, w_out_ref, op_ref, os_ref, w_in_o, w_out_o):
    w = w_ref[...].astype(bf16)
    for c_ref, o_ref in ((cp_ref, op_ref), (cs_ref, os_ref)):
        c = c_ref[...]
        o_ref[0] = _dot((c * jax.nn.sigmoid(c)).astype(bf16), w) + b_ref[0]
    w_in_o[...] = w_in_ref[...].astype(bf16)
    w_out_o[...] = w_out_ref[...].astype(bf16)


def _mod_call(c_p, c_s, w_ada, b_ada, w_in, w_out):
    nb, n_s = c_p.shape[0], c_s.shape[0]
    in_cols = w_in.shape[1] // N_MOD
    assert w_in.shape[1] == N_MOD * in_cols and in_cols % LANES == 0
    return pl.pallas_call(
        _mod_kernel,
        grid=(N_MOD,),
        in_specs=[
            pl.BlockSpec((nb, D), lambda j: (0, 0)),
            pl.BlockSpec((n_s, D), lambda j: (0, 0)),
            pl.BlockSpec((D, D), lambda j: (0, j)),
            pl.BlockSpec((1, 1, D), lambda j: (j, 0, 0)),
            pl.BlockSpec((D, in_cols), lambda j: (0, j)),
            pl.BlockSpec((D, D), lambda j: (0, 0)),
        ],
        out_specs=[pl.BlockSpec((1, nb, D), lambda j: (j, 0, 0)),
                   pl.BlockSpec((1, n_s, D), lambda j: (j, 0, 0)),
                   pl.BlockSpec((D, in_cols), lambda j: (0, j)),
                   pl.BlockSpec((D, D), lambda j: (0, 0))],
        out_shape=[jax.ShapeDtypeStruct((N_MOD, nb, D), f32),
                   jax.ShapeDtypeStruct((N_MOD, n_s, D), f32),
                   jax.ShapeDtypeStruct(w_in.shape, bf16),
                   jax.ShapeDtypeStruct(w_out.shape, bf16)],
        compiler_params=pltpu.CompilerParams(
            dimension_semantics=("arbitrary",), vmem_limit_bytes=VMEM_LIMIT),
    )(c_p, c_s, w_ada, b_ada.reshape(N_MOD, 1, D), w_in, w_out)


def _route(logits):
    t = logits.shape[0]
    lane = lax.broadcasted_iota(jnp.int32, (t, LANES), 1)
    lane_f = lane.astype(f32)
    neg = -jnp.inf
    big = 1e9
    gmask = (lane >= GROUP_LANE0) & (lane < GROUP_LANE0 + N_GROUPS)
    gl = jnp.where(gmask, logits, neg)
    gmax = jnp.max(gl, axis=-1, keepdims=True)
    g_idx = jnp.min(jnp.where(gl == gmax, lane_f - GROUP_LANE0, big), axis=-1, keepdims=True)
    sumexp = jnp.sum(jnp.where(gmask, jnp.exp(gl - gmax), 0.0), axis=-1, keepdims=True)
    p_g = 1.0 / sumexp
    lane_grp = (lane >> 3).astype(f32)
    emask = (lane < N_EXPERTS) & (lane_grp == g_idx)
    el = jnp.where(emask, logits, neg)
    m1 = jnp.max(el, axis=-1, keepdims=True)
    i1 = jnp.min(jnp.where(el == m1, lane_f, big), axis=-1, keepdims=True)
    el2 = jnp.where(lane_f == i1, neg, el)
    m2 = jnp.max(el2, axis=-1, keepdims=True)
    i2 = jnp.min(jnp.where(el2 == m2, lane_f, big), axis=-1, keepdims=True)
    e = jnp.exp(m2 - m1)
    w1 = p_g / (1.0 + e)
    w2 = w1 * e

    def split3(w):
        hi = w.astype(bf16).astype(f32)
        mid = (w - hi).astype(bf16).astype(f32)
        lo = w - hi - mid
        return hi, mid, lo

    r3 = jnp.where(lane == GIDX_LANE, g_idx, 0.0)
    for idx, w in ((i1, w1), (i2, w2)):
        for part, wp in enumerate(split3(w)):
            r3 = r3 + jnp.where(lane_f == idx + float(part * N_EXPERTS), wp, 0.0)
    counts = jnp.sum(jnp.where(lane_f == g_idx, 1.0, 0.0), axis=0, keepdims=True)
    return r3.astype(bf16), jnp.broadcast_to(counts, (8, LANES))


def _merge_project(u, ga, gb, mix, y_b, w_out_ref):
    y_a = u * mix
    merged = jax.nn.sigmoid(ga) * y_a + jax.nn.sigmoid(gb) * y_b
    return _dot(merged.astype(bf16), w_out_ref[...])


def _residual_route(x, y, mods, vec_ref, w_r_ref, b_r_ref):
    sh1, sc1, gt1, sh2, sc2, gt2 = mods
    x1 = x + gt1 * _rms(y, vec_ref[1:2])
    h2 = _rms(x1, vec_ref[2:3] * (1.0 + sc2)) + sh2
    h2_hi = h2.astype(bf16)
    h2_lo = (h2 - h2_hi.astype(f32)).astype(bf16)
    r = _dot(h2_hi, w_r_ref[...]) + _dot(h2_lo, w_r_ref[...])
    logits = r[:, :LANES] + r[:, LANES:] + b_r_ref[...]
    r3, counts = _route(logits)
    return x1, h2_hi, r3, counts


def _in_proj(x, mods, vec_ref, w_in_ref, after_first_dot=lambda: None):
    sh1, sc1 = mods[0], mods[1]
    h = _rms(x, vec_ref[0:1] * (1.0 + sc1)) + sh1
    hb = h.astype(bf16)
    zu = _dot(hb, w_in_ref[:, 0:D])
    after_first_dot()
    zv = _dot(hb, w_in_ref[:, D:2 * D])
    p = _dot(hb, w_in_ref[:, 2 * D:2 * D + PW])
    ga = _dot(hb, w_in_ref[:, 2 * D + PW:3 * D + PW])
    gb = _dot(hb, w_in_ref[:, 3 * D + PW:4 * D + PW])
    return zu, zv, p, ga, gb


def _activate(zu, zv, vec_ref):
    u = _gelu(zu)
    v = _gelu(zv)
    mu = jnp.mean(v, axis=-1, keepdims=True)
    vc = v - mu
    var = jnp.mean(vc * vc, axis=-1, keepdims=True)
    v = vc * lax.rsqrt(var + EPS) * vec_ref[4:5] + vec_ref[5:6]
    return u, v


def _pool_out(d_groups, vec_ref, w_pool_ref):
    parts = [_dot(d.astype(bf16), w_pool_ref[gi].astype(bf16)) for gi, d in enumerate(d_groups)]
    return jnp.concatenate(parts, axis=1) * vec_ref[6:7]


def _stage1_kernel(x_ref, xprev_ref, mod_ref, vec_ref, w_in_ref, w_sp_ref, bias_ref,
                   w_pool_ref, w_out_ref, w_r_ref, b_r_ref, eg_ref, eu_ref, ed_ref,
                   xs_ref, mods_ref, states_ref,
                   x1_ref, h2_ref, r3_ref, cnt_ref, plast_ref, wg_ref, wu_ref, wd_ref,
                   x1s_ref, h2s_ref, r3s_ref, cnts_ref, ps_ref, vs_ref,
                   pbuf, ybuf, *, tiles_per_seq):
    t = pl.program_id(0)
    last = pl.num_programs(0) - 1

    def mods_of(tile):
        b = tile // tiles_per_seq
        return [mod_ref[i, pl.ds(b, 1), :] for i in range(N_MOD)]

    def second_half(k):
        rows = slice(k * TL, (k + 1) * TL)
        tile = jnp.maximum((t - 1) * SUB + k, 0)
        x1, h2b, r3, counts = _residual_route(xprev_ref[rows, :], ybuf[k], mods_of(tile), vec_ref,
                                              w_r_ref, b_r_ref)
        x1_ref[rows, :] = x1
        h2_ref[rows, :] = h2b
        r3_ref[rows, :] = r3
        cnt_ref[k] = counts

    def cast_expert():
        wg_ref[...] = eg_ref[0].astype(bf16)
        wu_ref[...] = eu_ref[0].astype(bf16)
        wd_ref[...] = ed_ref[0].astype(bf16)

    def first_half(k, under_projection):
        tile = t * SUB + k
        s = tile % tiles_per_seq
        x = x_ref[k * TL:(k + 1) * TL, :]
        zu, zv, p, ga, gb = _in_proj(x, mods_of(tile), vec_ref, w_in_ref, under_projection)
        u, v = _activate(zu, zv, vec_ref)

        vb = v.astype(bf16)
        row = lax.broadcasted_iota(jnp.int32, (CHUNK, CHUNK), 0)
        col = lax.broadcasted_iota(jnp.int32, (CHUNK, CHUNK), 1)
        w_tril = [jnp.where(row >= col, w_sp_ref[hd], 0.0).astype(bf16) for hd in range(HEADS)]
        bias = bias_ref[...]
        chunks = []
        for c in range(TL // CHUNK):
            heads = [_dot(w_tril[hd],
                          vb[c * CHUNK:(c + 1) * CHUNK, hd * HEAD_DIM:(hd + 1) * HEAD_DIM])
                     for hd in range(HEADS)]
            chunks.append(jnp.concatenate(heads, axis=1) + bias)
        mix = jnp.concatenate(chunks, axis=0)

        carry = jnp.where(s == 0, 0.0, pbuf[...])
        ext = jnp.concatenate([carry, p], axis=0)
        pos = s * TL + lax.broadcasted_iota(jnp.int32, (TL, PG), 0)
        d_groups = []
        for gi, w in enumerate(WINDOWS):
            acc = ext[:, gi * PG:(gi + 1) * PG]
            shift = 1
            while shift < w:
                acc = acc + pltpu.roll(acc, shift, 0)
                shift *= 2
            cnt = jnp.minimum(pos + 1, w).astype(f32)
            d_groups.append(acc[W_MAX:] / cnt - p[:, gi * PG:(gi + 1) * PG])
        pbuf[...] = p[TL - W_MAX:]
        plast_ref[0] = p[TL - W_MAX:]
        y_b = _pool_out(d_groups, vec_ref, w_pool_ref)
        ybuf[k] = _merge_project(u, ga, gb, mix, y_b, w_out_ref)

    @pl.when(t == 0)
    def _():
        ybuf[...] = jnp.zeros_like(ybuf)
        pbuf[...] = jnp.zeros_like(pbuf)

    @pl.when(t < last)
    def _():
        for k in range(SUB):
            def under_projection(k=k):
                if k == 0:
                    cast_expert()
                second_half(k)
            first_half(k, under_projection)

    @pl.when(t == last)
    def _():
        for k in range(SUB):
            second_half(k)
        _sample_tokens(xs_ref, mods_ref, vec_ref, w_in_ref, states_ref, w_pool_ref, w_out_ref,
                       w_r_ref, b_r_ref, x1s_ref, h2s_ref, r3s_ref, cnts_ref, ps_ref, vs_ref)


def _sample_tokens(x_ref, mod_ref, vec_ref, w_in_ref, state_ref, w_pool_ref,
                   w_out_ref, w_r_ref, b_r_ref,
                   x1_ref, h2_ref, r3_ref, cnt_ref, p_ref, v_ref):
    x = x_ref[...]
    mods = [mod_ref[i] for i in range(N_MOD)]
    zu, zv, p, ga, gb = _in_proj(x, mods, vec_ref, w_in_ref)
    u, v = _activate(zu, zv, vec_ref)
    v_ref[...] = v
    p_ref[...] = p
    mix = v * vec_ref[8:9] + vec_ref[9:10]
    d_groups = []
    for gi, w in enumerate(WINDOWS):
        sl = slice(gi * PG, (gi + 1) * PG)
        acc = p[:, sl]
        for r in range(W_MAX - w, W_MAX - 1):
            acc = acc + state_ref[r][:, sl]
        d_groups.append(acc / float(w) - p[:, sl])
    y_b = _pool_out(d_groups, vec_ref, w_pool_ref)
    y = _merge_project(u, ga, gb, mix, y_b, w_out_ref)
    x1, h2b, r3, counts = _residual_route(x, y, mods, vec_ref, w_r_ref, b_r_ref)
    x1_ref[...] = x1
    h2_ref[...] = h2b
    r3_ref[...] = r3
    cnt_ref[...] = counts


def _const_spec(shape):
    nd = len(shape)
    return pl.BlockSpec(shape, lambda *_: (0,) * nd, pipeline_mode=pl.Buffered(1))


def _stage1_call(x, mod_p, vecs, w_in_b, w_sp, bias_full, w_pool_b, w_out_b, w_r, b_r,
                 w_gate, w_up, w_down, x_s, mod_s, state_t):
    b, s, _ = x.shape
    n_s = x_s.shape[0]
    const_out = lambda shape: pl.BlockSpec(shape, lambda t: (0,) * len(shape))
    ns = s // TL
    nt = b * ns
    steps = nt // SUB
    assert ns % SUB == 0
    assert steps >= N_EXPERTS
    x2 = x.reshape(b * s, D)
    rows = SUB * TL
    cur = lambda t: (jnp.minimum(t, steps - 1), 0)
    prev = lambda t: (jnp.maximum(t - 1, 0), 0)
    e_in, e_out, e_shapes = _expert_cast_specs(lambda t: jnp.minimum(t, N_EXPERTS - 1))
    return pl.pallas_call(
        functools.partial(_stage1_kernel, tiles_per_seq=ns),
        grid=(steps + 1,),
        in_specs=[
            pl.BlockSpec((rows, D), cur),
            pl.BlockSpec((rows, D), prev),
            _const_spec(mod_p.shape),
            _const_spec(vecs.shape),
            _const_spec(w_in_b.shape),
            _const_spec(w_sp.shape),
            _const_spec(bias_full.shape),
            _const_spec(w_pool_b.shape),
            _const_spec(w_out_b.shape),
            _const_spec(w_r.shape),
            _const_spec(b_r.shape),
        ] + e_in + [_const_spec(x_s.shape), _const_spec(mod_s.shape), _const_spec(state_t.shape)],
        out_specs=[
            pl.BlockSpec((rows, D), prev),
            pl.BlockSpec((rows, D), prev),
            pl.BlockSpec((rows, LANES), prev),
            pl.BlockSpec((SUB, 8, LANES), lambda t: (jnp.maximum(t - 1, 0), 0, 0)),
            pl.BlockSpec((1, W_MAX, PW),
                         lambda t: (jnp.minimum(t, steps - 1) * SUB // ns, 0, 0)),
        ] + e_out + [const_out((n_s, D)), const_out((n_s, D)), const_out((n_s, LANES)),
                     const_out((8, LANES)), const_out((n_s, PW)), const_out((n_s, D))],
        out_shape=[
            jax.ShapeDtypeStruct((b * s, D), f32),
            jax.ShapeDtypeStruct((b * s, D), bf16),
            jax.ShapeDtypeStruct((b * s, LANES), bf16),
            jax.ShapeDtypeStruct((nt, 8, LANES), f32),
            jax.ShapeDtypeStruct((b, W_MAX, PW), f32),
        ] + e_shapes + [
            jax.ShapeDtypeStruct((n_s, D), f32),
            jax.ShapeDtypeStruct((n_s, D), bf16),
            jax.ShapeDtypeStruct((n_s, LANES), bf16),
            jax.ShapeDtypeStruct((8, LANES), f32),
            jax.ShapeDtypeStruct((n_s, PW), f32),
            jax.ShapeDtypeStruct((n_s, D), f32),
        ],
        scratch_shapes=[pltpu.VMEM((W_MAX, PW), f32), pltpu.VMEM((SUB, TL, D), f32)],
        compiler_params=pltpu.CompilerParams(
            dimension_semantics=("arbitrary",), vmem_limit_bytes=VMEM_LIMIT),
    )(x2, x2, mod_p, vecs, w_in_b, w_sp, bias_full, w_pool_b, w_out_b, w_r, b_r,
      w_gate, w_up, w_down, x_s, mod_s, state_t)


def _moe_buffer_rows(ts, nt):
    worst = ts * nt + nt * N_GROUPS * (ROW_ALIGN - 1) + N_GROUPS * (RB - 1)
    return -(-worst // RB) * RB


def _moe_kernel(cnt_ref, h2_ref, r3_ref, x1_ref, mod_ref, g_ref, wg_ref, wu_ref, wd_ref, o_ref,
                hsbuf, hs_tile, ys_tile, pt_buf, tab,
                *, ts, nt, tpr, tps):
    w = pl.program_id(0)
    i = pl.program_id(1)
    rt = ts + LANES

    def copy_runs(tile_buf, run_lens, run_offs, ncols, to_window):
        starts = [0]
        for g in range(N_GROUPS):
            starts.append(starts[-1] + run_lens[g])

        def body(j, _):
            r = j * ROW_ALIGN
            shift = run_offs[0]
            for g in range(1, N_GROUPS):
                shift = jnp.where(r >= starts[g], run_offs[g] - starts[g], shift)
            t0 = pl.multiple_of(r, ROW_ALIGN)
            w0 = pl.multiple_of(r + shift, ROW_ALIGN)
            if to_window:
                hsbuf[pl.ds(w0, ROW_ALIGN), :] = tile_buf[pl.ds(t0, ROW_ALIGN), :ncols]
            else:
                tile_buf[pl.ds(t0, ROW_ALIGN), :] = hsbuf[pl.ds(w0, ROW_ALIGN), :ncols]
            return 0
        lax.fori_loop(0, starts[N_GROUPS] // ROW_ALIGN, body, 0)

    @pl.when((w == 0) & (i == 0))
    def _():
        hsbuf[...] = jnp.zeros_like(hsbuf)
        ys_tile[...] = jnp.zeros_like(ys_tile)

    @pl.when(i == 0)
    def _sort_and_run_experts():
        def run_len(tile, g):
            c = cnt_ref[(w * nt + tile) * N_GROUPS + g]
            return ((c + (ROW_ALIGN - 1)) // ROW_ALIGN) * ROW_ALIGN

        lens = [[run_len(t, g) for g in range(N_GROUPS)] for t in range(nt)]
        region = [sum(lens[t][g] for t in range(nt)) for g in range(N_GROUPS)]
        region = [((r + (RB - 1)) // RB) * RB for r in region]
        base = [sum(region[:g]) for g in range(N_GROUPS)]
        offs = list(base)
        woff = []
        for t in range(nt):
            woff.append(list(offs))
            for g in range(N_GROUPS):
                tab[t * 2 * N_GROUPS + g] = lens[t][g]
                tab[t * 2 * N_GROUPS + N_GROUPS + g] = offs[g]
                offs[g] = offs[g] + lens[t][g]

        lane = lax.broadcasted_iota(jnp.int32, (ts, LANES), 1)
        lane_f = lane.astype(f32)
        r_i = lax.broadcasted_iota(jnp.int32, (ts, ts), 0)
        c_i = lax.broadcasted_iota(jnp.int32, (ts, ts), 1)
        ltri = jnp.where(r_i > c_i, 1.0, 0.0).astype(bf16)
        rt_lane = lax.broadcasted_iota(jnp.int32, (ts, rt), 1).astype(f32)

        tiles = range(nt)
        r3s = [r3_ref[t * ts:(t + 1) * ts, :] for t in tiles]
        gids = [jnp.sum(jnp.where(lane == GIDX_LANE, r3s[t].astype(f32), 0.0), axis=-1,
                        keepdims=True) for t in tiles]
        onehots = [jnp.where(lane_f == gids[t], 1.0, 0.0) for t in tiles]
        ranks = [_dot(ltri, onehots[t].astype(bf16)) for t in tiles]
        poss = []
        for t in tiles:
            seg = jnp.zeros((1, LANES), f32)
            start = 0
            for g in range(N_GROUPS):
                seg = seg + jnp.where(lane[0:1] == g, jnp.asarray(start, jnp.int32).astype(f32), 0.0)
                start = start + lens[t][g]
            poss.append(jnp.sum(onehots[t] * (ranks[t] + seg), axis=-1, keepdims=True))
        for t in tiles:
            pt_buf[t] = jnp.where(rt_lane == poss[t], 1.0, 0.0).astype(bf16)
        def sort_rows(t, x):
            return lax.dot_general(pt_buf[t], x, (((0,), (0,)), ((), ())),
                                   preferred_element_type=f32).astype(bf16)

        per_pass = hs_tile.shape[0]
        for t0 in range(0, nt, per_pass):
            for k in range(per_pass):
                t = t0 + k
                hs_tile[k, :, :D] = sort_rows(t, h2_ref[t * ts:(t + 1) * ts, :])
                hs_tile[k, :, D:] = sort_rows(t, r3_ref[t * ts:(t + 1) * ts, :])
            for k in range(per_pass):
                t = t0 + k
                copy_runs(hs_tile.at[k], lens[t], woff[t], D + LANES, to_window=True)

        def expert_rows(g, r0, nrows):
            rows = hsbuf[pl.ds(r0, nrows), :D]
            gate = _dot(rows, wg_ref[g * D:(g + 1) * D, :])
            up = _dot(rows, wu_ref[g * D:(g + 1) * D, :])
            c3 = hsbuf[pl.ds(r0, nrows), D:].astype(f32)
            cw_lanes = c3 + pltpu.roll(c3, LANES - N_EXPERTS, 1) + pltpu.roll(c3, LANES - 2 * N_EXPERTS, 1)
            cw = jnp.concatenate(
                [jnp.broadcast_to(cw_lanes[:, g * EPG + j:g * EPG + j + 1], (nrows, D_EXPERT))
                 for j in range(EPG)], axis=1)
            act = gate * jax.nn.sigmoid(gate) * up * cw
            hsbuf[pl.ds(r0, nrows), :D] = _dot(
                act.astype(bf16), wd_ref[g * D:(g + 1) * D, :]).astype(bf16)

        for g in range(N_GROUPS):
            n_blocks = region[g] // RB

            def three_blocks(b, _, g=g):
                expert_rows(g, pl.multiple_of(base[g] + 3 * b * RB, RB), 3 * RB)
                return 0
            lax.fori_loop(0, n_blocks // 3, three_blocks, 0)
            done = (n_blocks // 3) * 3

            @pl.when(n_blocks - done == 2)
            def _(g=g, done=done):
                expert_rows(g, pl.multiple_of(base[g] + done * RB, RB), 2 * RB)

            @pl.when(n_blocks - done == 1)
            def _(g=g, done=done):
                expert_rows(g, pl.multiple_of(base[g] + done * RB, RB), RB)

    tiles = [i * tps + k for k in range(tps)]
    for k, tile in enumerate(tiles):
        run_lens = [tab[tile * 2 * N_GROUPS + g] for g in range(N_GROUPS)]
        run_offs = [tab[tile * 2 * N_GROUPS + N_GROUPS + g] for g in range(N_GROUPS)]
        copy_runs(ys_tile.at[k], run_lens, run_offs, D, to_window=False)
    fs =[_dot(pt_buf[tile], ys_tile[k]) for k, tile in enumerate(tiles)]
    for k, tile in enumerate(tiles):
        rows = slice(k * ts, (k + 1) * ts)
        tok0 = (w * nt + tile) * ts
        gt2_row = N_MOD - 1
        gt2 = (mod_ref[gt2_row, pl.ds(tok0, ts), :] if tpr == 1
               else mod_ref[gt2_row, pl.ds(tok0 // tpr, 1), :])
        o_ref[rows, :] = x1_ref[rows, :] + gt2 * _rms(fs[k], g_ref[...])


def _moe_call(cnt, x1, h2, r3, mod, g_post, wg, wu, wd, ts, nt, tpr):
    n = x1.shape[0]
    win = ts * nt
    rbuf = _moe_buffer_rows(ts, nt)
    rt = ts + LANES
    tps = 2 if nt % 2 == 0 else 1
    steps = nt // tps
    grid_spec = pltpu.PrefetchScalarGridSpec(
        num_scalar_prefetch=1,
        grid=(n // win, steps),
        in_specs=[
            pl.BlockSpec((win, D), lambda w, i, c: (w, 0)),
            pl.BlockSpec((win, LANES), lambda w, i, c: (w, 0)),
            pl.BlockSpec((tps * ts, D), lambda w, i, c: (w * steps + i, 0)),
            _const_spec(mod.shape),
            _const_spec(g_post.shape),
            _const_spec(wg.shape),
            _const_spec(wu.shape),
            _const_spec(wd.shape),
        ],
        out_specs=pl.BlockSpec((tps * ts, D), lambda w, i, c: (w * steps + i, 0)),
        scratch_shapes=[
            pltpu.VMEM((rbuf, D + LANES), bf16),
            pltpu.VMEM((4 if nt % 4 == 0 else tps, rt, D + LANES), bf16),
            pltpu.VMEM((tps, rt, D), bf16),
            pltpu.VMEM((nt, ts, rt), bf16),
            pltpu.SMEM((nt * 2 * N_GROUPS,), jnp.int32),
        ],
    )
    return pl.pallas_call(
        functools.partial(_moe_kernel, ts=ts, nt=nt, tpr=tpr, tps=tps),
        grid_spec=grid_spec,
        out_shape=jax.ShapeDtypeStruct((n, D), f32),
        compiler_params=pltpu.CompilerParams(
            dimension_semantics=("arbitrary", "arbitrary"), vmem_limit_bytes=VMEM_LIMIT),
    )(cnt, h2, r3, x1, mod, g_post, wg, wu, wd)


def _count_table(cnt):
    return cnt[:, 0, :N_GROUPS].astype(jnp.int32).reshape(-1)


def kernel(x_prompt, x_sample, c_prompt, c_sample, state_pool, w_ada, b_ada, g_pre_mix, g_post_mix, g_pre_ffn, g_post_ffn, w_in, ln_v_g, ln_v_b, w_spatial, b_spatial, w_pool, pool_scale, w_out, w_router_grp, b_router_grp, w_router_exp, b_router_exp, w_exp_gate, w_exp_up, w_exp_down):
    depth = w_in.shape[0]
    assert depth == 1
    b, s, _ = x_prompt.shape
    n_s = x_sample.shape[0]
    l = 0

    mod_p, mod_s, w_in_b, w_out_b = _mod_call(
        c_prompt, c_sample, w_ada[l], b_ada[l], w_in[l], w_out[l])

    ws, bs = w_spatial[l], b_spatial[l]
    zeros = jnp.zeros((D,), f32)
    vecs = jnp.stack([
        g_pre_mix[l], g_post_mix[l], g_pre_ffn[l], g_post_ffn[l], ln_v_g[l], ln_v_b[l],
        pool_scale[l], zeros,
        jnp.repeat(ws[:, 0, 0], HEAD_DIM), jnp.repeat(bs[:, 0], HEAD_DIM),
        zeros, zeros, zeros, zeros, zeros, zeros])
    bias_full = jnp.repeat(bs.T, HEAD_DIM, axis=1)
    pad = LANES - N_EXPERTS - N_GROUPS
    w_r = jnp.concatenate([w_router_exp[l], w_router_grp[l], jnp.zeros((D, pad), f32)], axis=1)
    w_r_hi = w_r.astype(bf16)
    w_r_lo = (w_r - w_r_hi.astype(f32)).astype(bf16)
    w_r2 = jnp.concatenate([w_r_hi, w_r_lo], axis=1)
    b_r = jnp.concatenate([b_router_exp[l], b_router_grp[l], jnp.zeros((pad,), f32)])[None]

    state_t = jnp.transpose(state_pool[l], (1, 0, 2))
    (x1_p, h2_p, r3_p, cnt_p, plast, wg, wu, wd,
     x1_s, h2_s, r3_s, cnt_s, p_s, v_s) = _stage1_call(
        x_prompt, mod_p, vecs, w_in_b, ws, bias_full, w_pool[l], w_out_b, w_r2, b_r,
        w_exp_gate[l], w_exp_up[l], w_exp_down[l], x_sample.reshape(n_s, D), mod_s, state_t)

    g_post = g_post_ffn[l].reshape(1, D)
    y_p = _moe_call(
        _count_table(cnt_p), x1_p.reshape(b * s, D), h2_p.reshape(b * s, D),
        r3_p.reshape(b * s, LANES), mod_p, g_post, wg, wu, wd, TL, MOE_WINDOW // TL, s)
    y_s = _moe_call(
        _count_table(cnt_s[None]), x1_s, h2_s, r3_s, mod_s, g_post, wg, wu, wd, n_s, 1, 1)

    state_pool_prompt = plast[:, 1:][None]
    state_pool_sample = jnp.concatenate([state_pool[l][:, 1:], p_s[:, None, :]], axis=1)[None]
    chunk_v_sample = v_s.reshape(1, n_s, 1, D)
    return (y_p.reshape(b, s, D), y_s.reshape(n_s, 1, D), state_pool_prompt,
            state_pool_sample, chunk_v_sample)
```

```python
import functools

import jax
import jax.numpy as jnp
from jax import lax
from jax.experimental import pallas as pl
from jax.experimental.pallas import tpu as pltpu

D = 1024
CHUNK = 128
HEADS = 8
HEAD_DIM = 128
WINDOWS = (2, 4, 8, 16)
PW = 512
PG = 128
W_MAX = 16
N_GROUPS = 4
EPG = 8
N_EXPERTS = 32
D_EXPERT = 128
EPS = 1e-6
N_MOD = 6
LANES = 128
GROUP_LANE0 = 32
GIDX_LANE = 96

TL = 256
SUB = 2
MOE_WINDOW = 2048
ROW_ALIGN = 16
RB = 128
VMEM_LIMIT = 60 * 1024 * 1024

bf16 = jnp.bfloat16
f32 = jnp.float32


def _rms(x, g):
    ms = jnp.mean(x * x, axis=-1, keepdims=True)
    return x * lax.rsqrt(ms + EPS) * g


def _dot(a, b):
    return jnp.dot(a, b, preferred_element_type=f32)


_GELU_C = 2.0 * 0.7978845608028654


def _gelu(x):
    t = x * ((-_GELU_C) + (-_GELU_C * 0.044715) * (x * x))
    return x / (1.0 + jnp.exp(t))


def _expert_cast_specs(step_to_expert):
    e = step_to_expert
    col_block = pl.BlockSpec((D, D_EXPERT), lambda t: (e(t) // EPG, e(t) % EPG))
    in_specs = [
        pl.BlockSpec((1, D, D_EXPERT), lambda t: (e(t), 0, 0)),
        pl.BlockSpec((1, D, D_EXPERT), lambda t: (e(t), 0, 0)),
        pl.BlockSpec((1, D_EXPERT, D), lambda t: (e(t), 0, 0)),
    ]
    out_specs = [col_block, col_block, pl.BlockSpec((D_EXPERT, D), lambda t: (e(t), 0))]
    out_shapes = [
        jax.ShapeDtypeStruct((N_GROUPS * D, EPG * D_EXPERT), bf16),
        jax.ShapeDtypeStruct((N_GROUPS * D, EPG * D_EXPERT), bf16),
        jax.ShapeDtypeStruct((N_EXPERTS * D_EXPERT, D), bf16),
    ]
    return in_specs, out_specs, out_shapes


def _mod_kernel(cp_ref, cs_ref, w_ref, b_ref, w_in_ref, w_out_ref, op_ref, os_ref, w_in_o, w_out_o):
    w = w_ref[...].astype(bf16)
    for c_ref, o_ref in ((cp_ref, op_ref), (cs_ref, os_ref)):
        c = c_ref[...]
        o_ref[0] = _dot((c * jax.nn.sigmoid(c)).astype(bf16), w) + b_ref[0]
    w_in_o[...] = w_in_ref[...].astype(bf16)
    w_out_o[...] = w_out_ref[...].astype(bf16)


def _mod_call(c_p, c_s, w_ada, b_ada, w_in, w_out):
    nb, n_s = c_p.shape[0], c_s.shape[0]
    in_cols = w_in.shape[1] // N_MOD
    assert w_in.shape[1] == N_MOD * in_cols and in_cols % LANES == 0
    return pl.pallas_call(
        _mod_kernel,
        grid=(N_MOD,),
        in_specs=[
            pl.BlockSpec((nb, D), lambda j: (0, 0)),
            pl.BlockSpec((n_s, D), lambda j: (0, 0)),
            pl.BlockSpec((D, D), lambda j: (0, j)),
            pl.BlockSpec((1, 1, D), lambda j: (j, 0, 0)),
            pl.BlockSpec((D, in_cols), lambda j: (0, j)),
            pl.BlockSpec((D, D), lambda j: (0, 0)),
        ],
        out_specs=[pl.BlockSpec((1, nb, D), lambda j: (j, 0, 0)),
                   pl.BlockSpec((1, n_s, D), lambda j: (j, 0, 0)),
                   pl.BlockSpec((D, in_cols), lambda j: (0, j)),
                   pl.BlockSpec((D, D), lambda j: (0, 0))],
        out_shape=[jax.ShapeDtypeStruct((N_MOD, nb, D), f32),
                   jax.ShapeDtypeStruct((N_MOD, n_s, D), f32),
                   jax.ShapeDtypeStruct(w_in.shape, bf16),
                   jax.ShapeDtypeStruct(w_out.shape, bf16)],
        compiler_params=pltpu.CompilerParams(
            dimension_semantics=("arbitrary",), vmem_limit_bytes=VMEM_LIMIT),
    )(c_p, c_s, w_ada, b_ada.reshape(N_MOD, 1, D), w_in, w_out)


def _route(logits):
    t = logits.shape[0]
    lane = lax.broadcasted_iota(jnp.int32, (t, LANES), 1)
    lane_f = lane.astype(f32)
    neg = -jnp.inf
    big = 1e9
    gmask = (lane >= GROUP_LANE0) & (lane < GROUP_LANE0 + N_GROUPS)
    gl = jnp.where(gmask, logits, neg)
    gmax = jnp.max(gl, axis=-1, keepdims=True)
    g_idx = jnp.min(jnp.where(gl == gmax, lane_f - GROUP_LANE0, big), axis=-1, keepdims=True)
    sumexp = jnp.sum(jnp.where(gmask, jnp.exp(gl - gmax), 0.0), axis=-1, keepdims=True)
    p_g = 1.0 / sumexp
    lane_grp = (lane >> 3).astype(f32)
    emask = (lane < N_EXPERTS) & (lane_grp == g_idx)
    el = jnp.where(emask, logits, neg)
    m1 = jnp.max(el, axis=-1, keepdims=True)
    i1 = jnp.min(jnp.where(el == m1, lane_f, big), axis=-1, keepdims=True)
    el2 = jnp.where(lane_f == i1, neg, el)
    m2 = jnp.max(el2, axis=-1, keepdims=True)
    i2 = jnp.min(jnp.where(el2 == m2, lane_f, big), axis=-1, keepdims=True)
    e = jnp.exp(m2 - m1)
    w1 = p_g / (1.0 + e)
    w2 = w1 * e

    def split3(w):
        hi = w.astype(bf16).astype(f32)
        mid = (w - hi).astype(bf16).astype(f32)
        lo = w - hi - mid
        return hi, mid, lo

    r3 = jnp.where(lane == GIDX_LANE, g_idx, 0.0)
    for idx, w in ((i1, w1), (i2, w2)):
        for part, wp in enumerate(split3(w)):
            r3 = r3 + jnp.where(lane_f == idx + float(part * N_EXPERTS), wp, 0.0)
    counts = jnp.sum(jnp.where(lane_f == g_idx, 1.0, 0.0), axis=0, keepdims=True)
    return r3.astype(bf16), jnp.broadcast_to(counts, (8, LANES))


def _merge_project(u, ga, gb, mix, y_b, w_out_ref):
    y_a = u * mix
    merged = jax.nn.sigmoid(ga) * y_a + jax.nn.sigmoid(gb) * y_b
    return _dot(merged.astype(bf16), w_out_ref[...])


def _residual(x, y, mods, vec_ref):
    sh1, sc1, gt1, sh2, sc2, gt2 = mods
    x1 = x + gt1 * _rms(y, vec_ref[1:2])
    h2 = _rms(x1, vec_ref[2:3] * (1.0 + sc2)) + sh2
    h2_hi = h2.astype(bf16)
    h2_lo = (h2 - h2_hi.astype(f32)).astype(bf16)
    return x1, h2_hi, h2_lo


def _router(h2_hi, h2_lo, w_r_ref, b_r_ref):
    r = _dot(h2_hi, w_r_ref[...]) + _dot(h2_lo, w_r_ref[...])
    logits = r[:, :LANES] + r[:, LANES:] + b_r_ref[...]
    return _route(logits)


def _residual_route(x, y, mods, vec_ref, w_r_ref, b_r_ref):
    x1, h2_hi, h2_lo = _residual(x, y, mods, vec_ref)
    r3, counts = _router(h2_hi, h2_lo, w_r_ref, b_r_ref)
    return x1, h2_hi, r3, counts


def _in_proj(x, mods, vec_ref, w_in_ref, after_first_dot=lambda: None,
             after_second_dot=lambda: None):
    sh1, sc1 = mods[0], mods[1]
    h = _rms(x, vec_ref[0:1] * (1.0 + sc1)) + sh1
    hb = h.astype(bf16)
    zu = _dot(hb, w_in_ref[:, 0:D])
    after_first_dot()
    zv = _dot(hb, w_in_ref[:, D:2 * D])
    after_second_dot()
    p = _dot(hb, w_in_ref[:, 2 * D:2 * D + PW])
    ga = _dot(hb, w_in_ref[:, 2 * D + PW:3 * D + PW])
    gb = _dot(hb, w_in_ref[:, 3 * D + PW:4 * D + PW])
    return zu, zv, p, ga, gb


def _activate(zu, zv, vec_ref):
    u = _gelu(zu)
    v = _gelu(zv)
    mu = jnp.mean(v, axis=-1, keepdims=True)
    vc = v - mu
    var = jnp.mean(vc * vc, axis=-1, keepdims=True)
    v = vc * lax.rsqrt(var + EPS) * vec_ref[4:5] + vec_ref[5:6]
    return u, v


def _pool_out(d_groups, vec_ref, w_pool_ref):
    parts = [_dot(d.astype(bf16), w_pool_ref[gi].astype(bf16)) for gi, d in enumerate(d_groups)]
    return jnp.concatenate(parts, axis=1) * vec_ref[6:7]


def _stage1_kernel(x_ref, xprev_ref, mod_ref, vec_ref, w_in_ref, w_sp_ref, bias_ref,
                   w_pool_ref, w_out_ref, w_r_ref, b_r_ref, eg_ref, eu_ref, ed_ref,
                   xs_ref, mods_ref, states_ref,
                   x1_ref, h2_ref, r3_ref, cnt_ref, plast_ref, wg_ref, wu_ref, wd_ref,
                   x1s_ref, h2s_ref, r3s_ref, cnts_ref, ps_ref, vs_ref,
                   pbuf, ybuf, *, tiles_per_seq):
    t = pl.program_id(0)
    last = pl.num_programs(0) - 1

    def mods_of(tile):
        b = tile // tiles_per_seq
        return [mod_ref[i, pl.ds(b, 1), :] for i in range(N_MOD)]

    split = {}

    def second_half_residual(k):
        rows = slice(k * TL, (k + 1) * TL)
        tile = jnp.maximum((t - 1) * SUB + k, 0)
        x1, h2_hi, h2_lo = _residual(xprev_ref[rows, :], ybuf[k], mods_of(tile), vec_ref)
        x1_ref[rows, :] = x1
        h2_ref[rows, :] = h2_hi
        split[k] = (h2_hi, h2_lo)

    def second_half_router(k):
        rows = slice(k * TL, (k + 1) * TL)
        r3, counts = _router(*split[k], w_r_ref, b_r_ref)
        r3_ref[rows, :] = r3
        cnt_ref[k] = counts

    def second_half(k):
        second_half_residual(k)
        second_half_router(k)

    def cast_expert():
        wg_ref[...] = eg_ref[0].astype(bf16)
        wu_ref[...] = eu_ref[0].astype(bf16)
        wd_ref[...] = ed_ref[0].astype(bf16)

    def first_half(k, under_first, under_second):
        tile = t * SUB + k
        s = tile % tiles_per_seq
        x = x_ref[k * TL:(k + 1) * TL, :]
        zu, zv, p, ga, gb = _in_proj(x, mods_of(tile), vec_ref, w_in_ref, under_first, under_second)
        u, v = _activate(zu, zv, vec_ref)

        vb = v.astype(bf16)
        row = lax.broadcasted_iota(jnp.int32, (CHUNK, CHUNK), 0)
        col = lax.broadcasted_iota(jnp.int32, (CHUNK, CHUNK), 1)
        w_tril = [jnp.where(row >= col, w_sp_ref[hd], 0.0).astype(bf16) for hd in range(HEADS)]
        bias = bias_ref[...]
        chunks = []
        for c in range(TL // CHUNK):
            heads = [_dot(w_tril[hd],
                          vb[c * CHUNK:(c + 1) * CHUNK, hd * HEAD_DIM:(hd + 1) * HEAD_DIM])
                     for hd in range(HEADS)]
            chunks.append(jnp.concatenate(heads, axis=1) + bias)
        mix = jnp.concatenate(chunks, axis=0)

        carry = jnp.where(s == 0, 0.0, pbuf[...])
        ext = jnp.concatenate([carry, p], axis=0)
        pos = s * TL + lax.broadcasted_iota(jnp.int32, (TL, PG), 0)
        d_groups = []
        for gi, w in enumerate(WINDOWS):
            acc = ext[:, gi * PG:(gi + 1) * PG]
            shift = 1
            while shift < w:
                acc = acc + pltpu.roll(acc, shift, 0)
                shift *= 2
            cnt = jnp.minimum(pos + 1, w).astype(f32)
            d_groups.append(acc[W_MAX:] / cnt - p[:, gi * PG:(gi + 1) * PG])
        pbuf[...] = p[TL - W_MAX:]
        plast_ref[0] = p[TL - W_MAX:]
        y_b = _pool_out(d_groups, vec_ref, w_pool_ref)
        ybuf[k] = _merge_project(u, ga, gb, mix, y_b, w_out_ref)

    @pl.when(t == 0)
    def _():
        ybuf[...] = jnp.zeros_like(ybuf)
        pbuf[...] = jnp.zeros_like(pbuf)

    @pl.when(t < last)
    def _():
        for k in range(SUB):
            def under_first(k=k):
                if k == 0:
                    cast_expert()
                second_half_residual(k)
            first_half(k, under_first, functools.partial(second_half_router, k))

    @pl.when(t == last)
    def _():
        for k in range(SUB):
            second_half(k)
        _sample_tokens(xs_ref, mods_ref, vec_ref, w_in_ref, states_ref, w_pool_ref, w_out_ref,
                       w_r_ref, b_r_ref, x1s_ref, h2s_ref, r3s_ref, cnts_ref, ps_ref, vs_ref)


def _sample_tokens(x_ref, mod_ref, vec_ref, w_in_ref, state_ref, w_pool_ref,
                   w_out_ref, w_r_ref, b_r_ref,
                   x1_ref, h2_ref, r3_ref, cnt_ref, p_ref, v_ref):
    x = x_ref[...]
    mods = [mod_ref[i] for i in range(N_MOD)]
    zu, zv, p, ga, gb = _in_proj(x, mods, vec_ref, w_in_ref)
    u, v = _activate(zu, zv, vec_ref)
    v_ref[...] = v
    p_ref[...] = p
    mix = v * vec_ref[8:9] + vec_ref[9:10]
    d_groups = []
    for gi, w in enumerate(WINDOWS):
        sl = slice(gi * PG, (gi + 1) * PG)
        acc = p[:, sl]
        for r in range(W_MAX - w, W_MAX - 1):
            acc = acc + state_ref[r][:, sl]
        d_groups.append(acc / float(w) - p[:, sl])
    y_b = _pool_out(d_groups, vec_ref, w_pool_ref)
    y = _merge_project(u, ga, gb, mix, y_b, w_out_ref)
    x1, h2b, r3, counts = _residual_route(x, y, mods, vec_ref, w_r_ref, b_r_ref)
    x1_ref[...] = x1
    h2_ref[...] = h2b
    r3_ref[...] = r3
    cnt_ref[...] = counts


def _const_spec(shape):
    nd = len(shape)
    return pl.BlockSpec(shape, lambda *_: (0,) * nd, pipeline_mode=pl.Buffered(1))


def _stage1_call(x, mod_p, vecs, w_in_b, w_sp, bias_full, w_pool_b, w_out_b, w_r, b_r,
                 w_gate, w_up, w_down, x_s, mod_s, state_t):
    b, s, _ = x.shape
    n_s = x_s.shape[0]
    const_out = lambda shape: pl.BlockSpec(shape, lambda t: (0,) * len(shape))
    ns = s // TL
    nt = b * ns
    steps = nt // SUB
    assert ns % SUB == 0
    assert steps >= N_EXPERTS
    x2 = x.reshape(b * s, D)
    rows = SUB * TL
    cur = lambda t: (jnp.minimum(t, steps - 1), 0)
    prev = lambda t: (jnp.maximum(t - 1, 0), 0)
    e_in, e_out, e_shapes = _expert_cast_specs(lambda t: jnp.minimum(t, N_EXPERTS - 1))
    return pl.pallas_call(
        functools.partial(_stage1_kernel, tiles_per_seq=ns),
        grid=(steps + 1,),
        in_specs=[
            pl.BlockSpec((rows, D), cur),
            pl.BlockSpec((rows, D), prev),
            _const_spec(mod_p.shape),
            _const_spec(vecs.shape),
            _const_spec(w_in_b.shape),
            _const_spec(w_sp.shape),
            _const_spec(bias_full.shape),
            _const_spec(w_pool_b.shape),
            _const_spec(w_out_b.shape),
            _const_spec(w_r.shape),
            _const_spec(b_r.shape),
        ] + e_in + [_const_spec(x_s.shape), _const_spec(mod_s.shape), _const_spec(state_t.shape)],
        out_specs=[
            pl.BlockSpec((rows, D), prev),
            pl.BlockSpec((rows, D), prev),
            pl.BlockSpec((rows, LANES), prev),
            pl.BlockSpec((SUB, 8, LANES), lambda t: (jnp.maximum(t - 1, 0), 0, 0)),
            pl.BlockSpec((1, W_MAX, PW),
                         lambda t: (jnp.minimum(t, steps - 1) * SUB // ns, 0, 0)),
        ] + e_out + [const_out((n_s, D)), const_out((n_s, D)), const_out((n_s, LANES)),
                     const_out((8, LANES)), const_out((n_s, PW)), const_out((n_s, D))],
        out_shape=[
            jax.ShapeDtypeStruct((b * s, D), f32),
            jax.ShapeDtypeStruct((b * s, D), bf16),
            jax.ShapeDtypeStruct((b * s, LANES), bf16),
            jax.ShapeDtypeStruct((nt, 8, LANES), f32),
            jax.ShapeDtypeStruct((b, W_MAX, PW), f32),
        ] + e_shapes + [
            jax.ShapeDtypeStruct((n_s, D), f32),
            jax.ShapeDtypeStruct((n_s, D), bf16),
            jax.ShapeDtypeStruct((n_s, LANES), bf16),
            jax.ShapeDtypeStruct((8, LANES), f32),
            jax.ShapeDtypeStruct((n_s, PW), f32),
            jax.ShapeDtypeStruct((n_s, D), f32),
        ],
        scratch_shapes=[pltpu.VMEM((W_MAX, PW), f32), pltpu.VMEM((SUB, TL, D), f32)],
        compiler_params=pltpu.CompilerParams(
            dimension_semantics=("arbitrary",), vmem_limit_bytes=VMEM_LIMIT),
    )(x2, x2, mod_p, vecs, w_in_b, w_sp, bias_full, w_pool_b, w_out_b, w_r, b_r,
      w_gate, w_up, w_down, x_s, mod_s, state_t)


def _moe_buffer_rows(ts, nt):
    worst = ts * nt + nt * N_GROUPS * (ROW_ALIGN - 1) + N_GROUPS * (RB - 1)
    return -(-worst // RB) * RB


def _moe_kernel(cnt_ref, h2_ref, r3_ref, x1_ref, mod_ref, g_ref, wg_ref, wu_ref, wd_ref, o_ref,
                hsbuf, hs_tile, ys_tile, pt_buf, tab,
                *, ts, nt, tpr, tps):
    w = pl.program_id(0)
    i = pl.program_id(1)
    rt = ts + LANES

    def copy_rows(src, src0, dst, dst0, nrows, ncols):
        def body(j, _):
            s0 = pl.multiple_of(src0 + j * ROW_ALIGN, ROW_ALIGN)
            d0 = pl.multiple_of(dst0 + j * ROW_ALIGN, ROW_ALIGN)
            dst[pl.ds(d0, ROW_ALIGN), :] = src[pl.ds(s0, ROW_ALIGN), :ncols]
            return 0
        lax.fori_loop(0, nrows // ROW_ALIGN, body, 0)

    @pl.when((w == 0) & (i == 0))
    def _():
        hsbuf[...] = jnp.zeros_like(hsbuf)
        ys_tile[...] = jnp.zeros_like(ys_tile)

    @pl.when(i == 0)
    def _sort_and_run_experts():
        def run_len(tile, g):
            c = cnt_ref[(w * nt + tile) * N_GROUPS + g]
            return ((c + (ROW_ALIGN - 1)) // ROW_ALIGN) * ROW_ALIGN

        lens = [[run_len(t, g) for g in range(N_GROUPS)] for t in range(nt)]
        region = [sum(lens[t][g] for t in range(nt)) for g in range(N_GROUPS)]
        region = [((r + (RB - 1)) // RB) * RB for r in region]
        base = [sum(region[:g]) for g in range(N_GROUPS)]
        offs = list(base)
        woff = []
        for t in range(nt):
            woff.append(list(offs))
            for g in range(N_GROUPS):
                tab[t * 2 * N_GROUPS + g] = lens[t][g]
                tab[t * 2 * N_GROUPS + N_GROUPS + g] = offs[g]
                offs[g] = offs[g] + lens[t][g]

        lane = lax.broadcasted_iota(jnp.int32, (ts, LANES), 1)
        lane_f = lane.astype(f32)
        r_i = lax.broadcasted_iota(jnp.int32, (ts, ts), 0)
        c_i = lax.broadcasted_iota(jnp.int32, (ts, ts), 1)
        ltri = jnp.where(r_i > c_i, 1.0, 0.0).astype(bf16)
        rt_lane = lax.broadcasted_iota(jnp.int32, (ts, rt), 1).astype(f32)

        tiles = range(nt)
        r3s = [r3_ref[t * ts:(t + 1) * ts, :] for t in tiles]
        gids = [jnp.sum(jnp.where(lane == GIDX_LANE, r3s[t].astype(f32), 0.0), axis=-1,
                        keepdims=True) for t in tiles]
        onehots = [jnp.where(lane_f == gids[t], 1.0, 0.0) for t in tiles]
        ranks = [_dot(ltri, onehots[t].astype(bf16)) for t in tiles]
        poss = []
        for t in tiles:
            seg = jnp.zeros((1, LANES), f32)
            start = 0
            for g in range(N_GROUPS):
                seg = seg + jnp.where(lane[0:1] == g, jnp.asarray(start, jnp.int32).astype(f32), 0.0)
                start = start + lens[t][g]
            poss.append(jnp.sum(onehots[t] * (ranks[t] + seg), axis=-1, keepdims=True))
        for t in tiles:
            pt_buf[t] = jnp.where(rt_lane == poss[t], 1.0, 0.0).astype(bf16)
        def sort_rows(t, x):
            return lax.dot_general(pt_buf[t], x, (((0,), (0,)), ((), ())),
                                   preferred_element_type=f32).astype(bf16)

        per_pass = hs_tile.shape[0]
        for t0 in range(0, nt, per_pass):
            for k in range(per_pass):
                t = t0 + k
                hs_tile[k, :, :D] = sort_rows(t, h2_ref[t * ts:(t + 1) * ts, :])
                hs_tile[k, :, D:] = sort_rows(t, r3_ref[t * ts:(t + 1) * ts, :])
            for k in range(per_pass):
                t = t0 + k
                start = 0
                for g in range(N_GROUPS):
                    copy_rows(hs_tile.at[k], start, hsbuf, woff[t][g], lens[t][g], D + LANES)
                    start = start + lens[t][g]

        def expert_rows(g, r0, nrows):
            rows = hsbuf[pl.ds(r0, nrows), :D]
            gate = _dot(rows, wg_ref[g * D:(g + 1) * D, :])
            up = _dot(rows, wu_ref[g * D:(g + 1) * D, :])
            c3 = hsbuf[pl.ds(r0, nrows), D:].astype(f32)
            cw_lanes = c3 + pltpu.roll(c3, LANES - N_EXPERTS, 1) + pltpu.roll(c3, LANES - 2 * N_EXPERTS, 1)
            cw = jnp.concatenate(
                [jnp.broadcast_to(cw_lanes[:, g * EPG + j:g * EPG + j + 1], (nrows, D_EXPERT))
                 for j in range(EPG)], axis=1)
            act = gate * jax.nn.sigmoid(gate) * up * cw
            hsbuf[pl.ds(r0, nrows), :D] = _dot(
                act.astype(bf16), wd_ref[g * D:(g + 1) * D, :]).astype(bf16)

        for g in range(N_GROUPS):
            n_blocks = region[g] // RB

            def three_blocks(b, _, g=g):
                expert_rows(g, pl.multiple_of(base[g] + 3 * b * RB, RB), 3 * RB)
                return 0
            lax.fori_loop(0, n_blocks // 3, three_blocks, 0)
            done = (n_blocks // 3) * 3

            @pl.when(n_blocks - done == 2)
            def _(g=g, done=done):
                expert_rows(g, pl.multiple_of(base[g] + done * RB, RB), 2 * RB)

            @pl.when(n_blocks - done == 1)
            def _(g=g, done=done):
                expert_rows(g, pl.multiple_of(base[g] + done * RB, RB), RB)

    tiles = [i * tps + k for k in range(tps)]
    for k, tile in enumerate(tiles):
        start = 0
        for g in range(N_GROUPS):
            ln = tab[tile * 2 * N_GROUPS + g]
            copy_rows(hsbuf, tab[tile * 2 * N_GROUPS + N_GROUPS + g], ys_tile.at[k], start, ln, D)
            start = start + ln
    fs = [_dot(pt_buf[tile], ys_tile[k]) for k, tile in enumerate(tiles)]
    for k, tile in enumerate(tiles):
        rows = slice(k * ts, (k + 1) * ts)
        tok0 = (w * nt + tile) * ts
        gt2_row = N_MOD - 1
        gt2 = (mod_ref[gt2_row, pl.ds(tok0, ts), :] if tpr == 1
               else mod_ref[gt2_row, pl.ds(tok0 // tpr, 1), :])
        o_ref[rows, :] = x1_ref[rows, :] + gt2 * _rms(fs[k], g_ref[...])


def _moe_call(cnt, x1, h2, r3, mod, g_post, wg, wu, wd, ts, nt, tpr):
    n = x1.shape[0]
    win = ts * nt
    rbuf = _moe_buffer_rows(ts, nt)
    rt = ts + LANES
    tps = 2 if nt % 2 == 0 else 1
    steps = nt // tps
    grid_spec = pltpu.PrefetchScalarGridSpec(
        num_scalar_prefetch=1,
        grid=(n // win, steps),
        in_specs=[
            pl.BlockSpec((win, D), lambda w, i, c: (w, 0)),
            pl.BlockSpec((win, LANES), lambda w, i, c: (w, 0)),
            pl.BlockSpec((tps * ts, D), lambda w, i, c: (w * steps + i, 0)),
            _const_spec(mod.shape),
            _const_spec(g_post.shape),
            _const_spec(wg.shape),
            _const_spec(wu.shape),
            _const_spec(wd.shape),
        ],
        out_specs=pl.BlockSpec((tps * ts, D), lambda w, i, c: (w * steps + i, 0)),
        scratch_shapes=[
            pltpu.VMEM((rbuf, D + LANES), bf16),
            pltpu.VMEM((4 if nt % 4 == 0 else tps, rt, D + LANES), bf16),
            pltpu.VMEM((tps, rt, D), bf16),
            pltpu.VMEM((nt, ts, rt), bf16),
            pltpu.SMEM((nt * 2 * N_GROUPS,), jnp.int32),
        ],
    )
    return pl.pallas_call(
        functools.partial(_moe_kernel, ts=ts, nt=nt, tpr=tpr, tps=tps),
        grid_spec=grid_spec,
        out_shape=jax.ShapeDtypeStruct((n, D), f32),
        compiler_params=pltpu.CompilerParams(
            dimension_semantics=("arbitrary", "arbitrary"), vmem_limit_bytes=VMEM_LIMIT),
    )(cnt, h2, r3, x1, mod, g_post, wg, wu, wd)


def _count_table(cnt):
    return cnt[:, 0, :N_GROUPS].astype(jnp.int32).reshape(-1)


def kernel(x_prompt, x_sample, c_prompt, c_sample, state_pool, w_ada, b_ada, g_pre_mix, g_post_mix, g_pre_ffn, g_post_ffn, w_in, ln_v_g, ln_v_b, w_spatial, b_spatial, w_pool, pool_scale, w_out, w_router_grp, b_router_grp, w_router_exp, b_router_exp, w_exp_gate, w_exp_up, w_exp_down):
    depth = w_in.shape[0]
    assert depth == 1
    b, s, _ = x_prompt.shape
    n_s = x_sample.shape[0]
    l = 0

    mod_p, mod_s, w_in_b, w_out_b = _mod_call(
        c_prompt, c_sample, w_ada[l], b_ada[l], w_in[l], w_out[l])

    ws, bs = w_spatial[l], b_spatial[l]
    zeros = jnp.zeros((D,), f32)
    vecs = jnp.stack([
        g_pre_mix[l], g_post_mix[l], g_pre_ffn[l], g_post_ffn[l], ln_v_g[l], ln_v_b[l],
        pool_scale[l], zeros,
        jnp.repeat(ws[:, 0, 0], HEAD_DIM), jnp.repeat(bs[:, 0], HEAD_DIM),
        zeros, zeros, zeros, zeros, zeros, zeros])
    bias_full = jnp.repeat(bs.T, HEAD_DIM, axis=1)
    pad = LANES - N_EXPERTS - N_GROUPS
    w_r = jnp.concatenate([w_router_exp[l], w_router_grp[l], jnp.zeros((D, pad), f32)], axis=1)
    w_r_hi = w_r.astype(bf16)
    w_r_lo = (w_r - w_r_hi.astype(f32)).astype(bf16)
    w_r2 = jnp.concatenate([w_r_hi, w_r_lo], axis=1)
    b_r = jnp.concatenate([b_router_exp[l], b_router_grp[l], jnp.zeros((pad,), f32)])[None]

    state_t = jnp.transpose(state_pool[l], (1, 0, 2))
    (x1_p, h2_p, r3_p, cnt_p, plast, wg, wu, wd,
     x1_s, h2_s, r3_s, cnt_s, p_s, v_s) = _stage1_call(
        x_prompt, mod_p, vecs, w_in_b, ws, bias_full, w_pool[l], w_out_b, w_r2, b_r,
        w_exp_gate[l], w_exp_up[l], w_exp_down[l], x_sample.reshape(n_s, D), mod_s, state_t)

    g_post = g_post_ffn[l].reshape(1, D)
    y_p = _moe_call(
        _count_table(cnt_p), x1_p.reshape(b * s, D), h2_p.reshape(b * s, D),
        r3_p.reshape(b * s, LANES), mod_p, g_post, wg, wu, wd, TL, MOE_WINDOW // TL, s)
    y_s = _moe_call(
        _count_table(cnt_s[None]), x1_s, h2_s, r3_s, mod_s, g_post, wg, wu, wd, n_s, 1, 1)

    state_pool_prompt = plast[:, 1:][None]
    state_pool_sample = jnp.concatenate([state_pool[l][:, 1:], p_s[:, None, :]], axis=1)[None]
    chunk_v_sample = v_s.reshape(1, n_s, 1, D)
    return (y_p.reshape(b, s, D), y_s.reshape(n_s, 1, D), state_pool_prompt,
            state_pool_sample, chunk_v_sample)
```

```python
import functools

import jax
import jax.numpy as jnp
from jax import lax
from jax.experimental import pallas as pl
from jax.experimental.pallas import tpu as pltpu

D = 1024
CHUNK = 128
HEADS = 8
HEAD_DIM = 128
WINDOWS = (2, 4, 8, 16)
PW = 512
PG = 128
W_MAX = 16
N_GROUPS = 4
EPG = 8
N_EXPERTS = 32
D_EXPERT = 128
EPS = 1e-6
N_MOD = 6
LANES = 128
GROUP_LANE0 = 32
GIDX_LANE = 96

TL = 256
SUB = 2
MOE_WINDOW = 2048
ROW_ALIGN = 16
RB = 128
VMEM_LIMIT = 60 * 1024 * 1024

bf16 = jnp.bfloat16
f32 = jnp.float32


def _rms(x, g):
    ms = jnp.mean(x * x, axis=-1, keepdims=True)
    return x * lax.rsqrt(ms + EPS) * g


def _dot(a, b):
    return jnp.dot(a, b, preferred_element_type=f32)


_GELU_C = 2.0 * 0.7978845608028654


def _gelu(x):
    t = x * ((-_GELU_C) + (-_GELU_C * 0.044715) * (x * x))
    return x / (1.0 + jnp.exp(t))


def _expert_cast_specs(step_to_expert):
    e = step_to_expert
    col_block = pl.BlockSpec((D, D_EXPERT), lambda t: (e(t) // EPG, e(t) % EPG))
    in_specs = [
        pl.BlockSpec((1, D, D_EXPERT), lambda t: (e(t), 0, 0)),
        pl.BlockSpec((1, D, D_EXPERT), lambda t: (e(t), 0, 0)),
        pl.BlockSpec((1, D_EXPERT, D), lambda t: (e(t), 0, 0)),
    ]
    out_specs = [col_block, col_block, pl.BlockSpec((D_EXPERT, D), lambda t: (e(t), 0))]
    out_shapes = [
        jax.ShapeDtypeStruct((N_GROUPS * D, EPG * D_EXPERT), bf16),
        jax.ShapeDtypeStruct((N_GROUPS * D, EPG * D_EXPERT), bf16),
        jax.ShapeDtypeStruct((N_EXPERTS * D_EXPERT, D), bf16),
    ]
    return in_specs, out_specs, out_shapes


def _mod_kernel(cp_ref, cs_ref, w_ref, b_ref, w_in_ref, w_out_ref, op_ref, os_ref, w_in_o, w_out_o):
    w = w_ref[...].astype(bf16)
    for c_ref, o_ref in ((cp_ref, op_ref), (cs_ref, os_ref)):
        c = c_ref[...]
        o_ref[0] = _dot((c * jax.nn.sigmoid(c)).astype(bf16), w) + b_ref[0]
    w_in_o[...] = w_in_ref[...].astype(bf16)
    w_out_o[...] = w_out_ref[...].astype(bf16)


def _mod_call(c_p, c_s, w_ada, b_ada, w_in, w_out):
    nb, n_s = c_p.shape[0], c_s.shape[0]
    in_cols = w_in.shape[1] // N_MOD
    assert w_in.shape[1] == N_MOD * in_cols and in_cols % LANES == 0
    return pl.pallas_call(
        _mod_kernel,
        grid=(N_MOD,),
        in_specs=[
            pl.BlockSpec((nb, D), lambda j: (0, 0)),
            pl.BlockSpec((n_s, D), lambda j: (0, 0)),
            pl.BlockSpec((D, D), lambda j: (0, j)),
            pl.BlockSpec((1, 1, D), lambda j: (j, 0, 0)),
            pl.BlockSpec((D, in_cols), lambda j: (0, j)),
            pl.BlockSpec((D, D), lambda j: (0, 0)),
        ],
        out_specs=[pl.BlockSpec((1, nb, D), lambda j: (j, 0, 0)),
                   pl.BlockSpec((1, n_s, D), lambda j: (j, 0, 0)),
                   pl.BlockSpec((D, in_cols), lambda j: (0, j)),
                   pl.BlockSpec((D, D), lambda j: (0, 0))],
        out_shape=[jax.ShapeDtypeStruct((N_MOD, nb, D), f32),
                   jax.ShapeDtypeStruct((N_MOD, n_s, D), f32),
                   jax.ShapeDtypeStruct(w_in.shape, bf16),
                   jax.ShapeDtypeStruct(w_out.shape, bf16)],
        compiler_params=pltpu.CompilerParams(
            dimension_semantics=("arbitrary",), vmem_limit_bytes=VMEM_LIMIT),
    )(c_p, c_s, w_ada, b_ada.reshape(N_MOD, 1, D), w_in, w_out)


def _route(logits):
    t = logits.shape[0]
    lane = lax.broadcasted_iota(jnp.int32, (t, LANES), 1)
    lane_f = lane.astype(f32)
    neg = -jnp.inf
    big = 1e9
    gmask = (lane >= GROUP_LANE0) & (lane < GROUP_LANE0 + N_GROUPS)
    gl = jnp.where(gmask, logits, neg)
    gmax = jnp.max(gl, axis=-1, keepdims=True)
    g_idx = jnp.min(jnp.where(gl == gmax, lane_f - GROUP_LANE0, big), axis=-1, keepdims=True)
    sumexp = jnp.sum(jnp.where(gmask, jnp.exp(gl - gmax), 0.0), axis=-1, keepdims=True)
    p_g = 1.0 / sumexp
    lane_grp = (lane >> 3).astype(f32)
    emask = (lane < N_EXPERTS) & (lane_grp == g_idx)
    el = jnp.where(emask, logits, neg)
    m1 = jnp.max(el, axis=-1, keepdims=True)
    i1 = jnp.min(jnp.where(el == m1, lane_f, big), axis=-1, keepdims=True)
    el2 = jnp.where(lane_f == i1, neg, el)
    m2 = jnp.max(el2, axis=-1, keepdims=True)
    i2 = jnp.min(jnp.where(el2 == m2, lane_f, big), axis=-1, keepdims=True)
    e = jnp.exp(m2 - m1)
    w1 = p_g / (1.0 + e)
    w2 = w1 * e

    def split3(w):
        hi = w.astype(bf16).astype(f32)
        mid = (w - hi).astype(bf16).astype(f32)
        lo = w - hi - mid
        return hi, mid, lo

    r3 = jnp.where(lane == GIDX_LANE, g_idx, 0.0)
    for idx, w in ((i1, w1), (i2, w2)):
        for part, wp in enumerate(split3(w)):
            r3 = r3 + jnp.where(lane_f == idx + float(part * N_EXPERTS), wp, 0.0)
    counts = jnp.sum(jnp.where(lane_f == g_idx, 1.0, 0.0), axis=0, keepdims=True)
    return r3.astype(bf16), jnp.broadcast_to(counts, (8, LANES))


def _merge_project(u, ga, gb, mix, y_b, w_out_ref):
    y_a = u * mix
    merged = jax.nn.sigmoid(ga) * y_a + jax.nn.sigmoid(gb) * y_b
    return _dot(merged.astype(bf16), w_out_ref[...])


def _residual(x, y, mods, vec_ref):
    sh1, sc1, gt1, sh2, sc2, gt2 = mods
    x1 = x + gt1 * _rms(y, vec_ref[1:2])
    h2 = _rms(x1, vec_ref[2:3] * (1.0 + sc2)) + sh2
    h2_hi = h2.astype(bf16)
    h2_lo = (h2 - h2_hi.astype(f32)).astype(bf16)
    return x1, h2_hi, h2_lo


def _router(h2_hi, h2_lo, w_r_ref, b_r_ref):
    r = _dot(h2_hi, w_r_ref[...]) + _dot(h2_lo, w_r_ref[...])
    logits = r[:, :LANES] + r[:, LANES:] + b_r_ref[...]
    return _route(logits)


def _residual_route(x, y, mods, vec_ref, w_r_ref, b_r_ref):
    x1, h2_hi, h2_lo = _residual(x, y, mods, vec_ref)
    r3, counts = _router(h2_hi, h2_lo, w_r_ref, b_r_ref)
    return x1, h2_hi, r3, counts


def _in_proj(x, mods, vec_ref, w_in_ref, after_first_dot=lambda: None,
             after_second_dot=lambda: None, after_third_dot=lambda: None):
    sh1, sc1 = mods[0], mods[1]
    h = _rms(x, vec_ref[0:1] * (1.0 + sc1)) + sh1
    hb = h.astype(bf16)
    zu = _dot(hb, w_in_ref[:, 0:D])
    after_first_dot()
    zv = _dot(hb, w_in_ref[:, D:2 * D])
    after_second_dot()
    p = _dot(hb, w_in_ref[:, 2 * D:2 * D + PW])
    after_third_dot()
    ga = _dot(hb, w_in_ref[:, 2 * D + PW:3 * D + PW])
    gb = _dot(hb, w_in_ref[:, 3 * D + PW:4 * D + PW])
    return zu, zv, p, ga, gb


def _activate(zu, zv, vec_ref):
    u = _gelu(zu)
    v = _gelu(zv)
    mu = jnp.mean(v, axis=-1, keepdims=True)
    vc = v - mu
    var = jnp.mean(vc * vc, axis=-1, keepdims=True)
    v = vc * lax.rsqrt(var + EPS) * vec_ref[4:5] + vec_ref[5:6]
    return u, v


def _pool_out(d_groups, vec_ref, w_pool_ref):
    parts = [_dot(d.astype(bf16), w_pool_ref[gi].astype(bf16)) for gi, d in enumerate(d_groups)]
    return jnp.concatenate(parts, axis=1) * vec_ref[6:7]


def _stage1_kernel(x_ref, xprev_ref, mod_ref, vec_ref, w_in_ref, w_sp_ref, bias_ref,
                   w_pool_ref, w_out_ref, w_r_ref, b_r_ref, eg_ref, eu_ref, ed_ref,
                   xs_ref, mods_ref, states_ref,
                   x1_ref, h2_ref, r3_ref, cnt_ref, plast_ref, wg_ref, wu_ref, wd_ref,
                   x1s_ref, h2s_ref, r3s_ref, cnts_ref, ps_ref, vs_ref,
                   pbuf, ybuf, *, tiles_per_seq):
    t = pl.program_id(0)
    last = pl.num_programs(0) - 1

    def mods_of(tile):
        b = tile // tiles_per_seq
        return [mod_ref[i, pl.ds(b, 1), :] for i in range(N_MOD)]

    split = {}

    def second_half_residual(k):
        rows = slice(k * TL, (k + 1) * TL)
        tile = jnp.maximum((t - 1) * SUB + k, 0)
        x1, h2_hi, h2_lo = _residual(xprev_ref[rows, :], ybuf[k], mods_of(tile), vec_ref)
        x1_ref[rows, :] = x1
        h2_ref[rows, :] = h2_hi
        split[k] = (h2_hi, h2_lo)

    def second_half_router(k):
        rows = slice(k * TL, (k + 1) * TL)
        r3, counts = _router(*split[k], w_r_ref, b_r_ref)
        r3_ref[rows, :] = r3
        cnt_ref[k] = counts

    def second_half(k):
        second_half_residual(k)
        second_half_router(k)

    def cast_expert():
        wg_ref[...] = eg_ref[0].astype(bf16)
        wu_ref[...] = eu_ref[0].astype(bf16)
        wd_ref[...] = ed_ref[0].astype(bf16)

    def first_half(k, *under_dots):
        tile = t * SUB + k
        s = tile % tiles_per_seq
        x = x_ref[k * TL:(k + 1) * TL, :]
        zu, zv, p, ga, gb = _in_proj(x, mods_of(tile), vec_ref, w_in_ref, *under_dots)
        u, v = _activate(zu, zv, vec_ref)

        vb = v.astype(bf16)
        row = lax.broadcasted_iota(jnp.int32, (CHUNK, CHUNK), 0)
        col = lax.broadcasted_iota(jnp.int32, (CHUNK, CHUNK), 1)
        w_tril = [jnp.where(row >= col, w_sp_ref[hd], 0.0).astype(bf16) for hd in range(HEADS)]
        bias = bias_ref[...]
        chunks = []
        for c in range(TL // CHUNK):
            heads = [_dot(w_tril[hd],
                          vb[c * CHUNK:(c + 1) * CHUNK, hd * HEAD_DIM:(hd + 1) * HEAD_DIM])
                     for hd in range(HEADS)]
            chunks.append(jnp.concatenate(heads, axis=1) + bias)
        mix = jnp.concatenate(chunks, axis=0)

        carry = jnp.where(s == 0, 0.0, pbuf[...])
        ext = jnp.concatenate([carry, p], axis=0)
        pos = s * TL + lax.broadcasted_iota(jnp.int32, (TL, PG), 0)
        d_groups = []
        for gi, w in enumerate(WINDOWS):
            acc = ext[:, gi * PG:(gi + 1) * PG]
            shift = 1
            while shift < w:
                acc = acc + pltpu.roll(acc, shift, 0)
                shift *= 2
            cnt = jnp.minimum(pos + 1, w).astype(f32)
            d_groups.append(acc[W_MAX:] / cnt - p[:, gi * PG:(gi + 1) * PG])
        pbuf[...] = p[TL - W_MAX:]
        plast_ref[0] = p[TL - W_MAX:]
        y_b = _pool_out(d_groups, vec_ref, w_pool_ref)
        ybuf[k] = _merge_project(u, ga, gb, mix, y_b, w_out_ref)

    @pl.when(t == 0)
    def _():
        ybuf[...] = jnp.zeros_like(ybuf)
        pbuf[...] = jnp.zeros_like(pbuf)

    @pl.when(t < last)
    def _():
        for k in range(SUB):
            first_half(k,
                       functools.partial(second_half_residual, k),
                       cast_expert if k == 0 else (lambda: None),
                       functools.partial(second_half_router, k))

    @pl.when(t == last)
    def _():
        for k in range(SUB):
            second_half(k)
        _sample_tokens(xs_ref, mods_ref, vec_ref, w_in_ref, states_ref, w_pool_ref, w_out_ref,
                       w_r_ref, b_r_ref, x1s_ref, h2s_ref, r3s_ref, cnts_ref, ps_ref, vs_ref)


def _sample_tokens(x_ref, mod_ref, vec_ref, w_in_ref, state_ref, w_pool_ref,
                   w_out_ref, w_r_ref, b_r_ref,
                   x1_ref, h2_ref, r3_ref, cnt_ref, p_ref, v_ref):
    x = x_ref[...]
    mods = [mod_ref[i] for i in range(N_MOD)]
    zu, zv, p, ga, gb = _in_proj(x, mods, vec_ref, w_in_ref)
    u, v = _activate(zu, zv, vec_ref)
    v_ref[...] = v
    p_ref[...] = p
    mix = v * vec_ref[8:9] + vec_ref[9:10]
    d_groups = []
    for gi, w in enumerate(WINDOWS):
        sl = slice(gi * PG, (gi + 1) * PG)
        acc = p[:, sl]
        for r in range(W_MAX - w, W_MAX - 1):
            acc = acc + state_ref[r][:, sl]
        d_groups.append(acc / float(w) - p[:, sl])
    y_b = _pool_out(d_groups, vec_ref, w_pool_ref)
    y = _merge_project(u, ga, gb, mix, y_b, w_out_ref)
    x1, h2b, r3, counts = _residual_route(x, y, mods, vec_ref, w_r_ref, b_r_ref)
    x1_ref[...] = x1
    h2_ref[...] = h2b
    r3_ref[...] = r3
    cnt_ref[...] = counts


def _const_spec(shape):
    nd = len(shape)
    return pl.BlockSpec(shape, lambda *_: (0,) * nd, pipeline_mode=pl.Buffered(1))


def _stage1_call(x, mod_p, vecs, w_in_b, w_sp, bias_full, w_pool_b, w_out_b, w_r, b_r,
                 w_gate, w_up, w_down, x_s, mod_s, state_t):
    b, s, _ = x.shape
    n_s = x_s.shape[0]
    const_out = lambda shape: pl.BlockSpec(shape, lambda t: (0,) * len(shape))
    ns = s // TL
    nt = b * ns
    steps = nt // SUB
    assert ns % SUB == 0
    assert steps >= N_EXPERTS
    x2 = x.reshape(b * s, D)
    rows = SUB * TL
    cur = lambda t: (jnp.minimum(t, steps - 1), 0)
    prev = lambda t: (jnp.maximum(t - 1, 0), 0)
    e_in, e_out, e_shapes = _expert_cast_specs(lambda t: jnp.minimum(t, N_EXPERTS - 1))
    return pl.pallas_call(
        functools.partial(_stage1_kernel, tiles_per_seq=ns),
        grid=(steps + 1,),
        in_specs=[
            pl.BlockSpec((rows, D), cur),
            pl.BlockSpec((rows, D), prev),
            _const_spec(mod_p.shape),
            _const_spec(vecs.shape),
            _const_spec(w_in_b.shape),
            _const_spec(w_sp.shape),
            _const_spec(bias_full.shape),
            _const_spec(w_pool_b.shape),
            _const_spec(w_out_b.shape),
            _const_spec(w_r.shape),
            _const_spec(b_r.shape),
        ] + e_in + [_const_spec(x_s.shape), _const_spec(mod_s.shape), _const_spec(state_t.shape)],
        out_specs=[
            pl.BlockSpec((rows, D), prev),
            pl.BlockSpec((rows, D), prev),
            pl.BlockSpec((rows, LANES), prev),
            pl.BlockSpec((SUB, 8, LANES), lambda t: (jnp.maximum(t - 1, 0), 0, 0)),
            pl.BlockSpec((1, W_MAX, PW),
                         lambda t: (jnp.minimum(t, steps - 1) * SUB // ns, 0, 0)),
        ] + e_out + [const_out((n_s, D)), const_out((n_s, D)), const_out((n_s, LANES)),
                     const_out((8, LANES)), const_out((n_s, PW)), const_out((n_s, D))],
        out_shape=[
            jax.ShapeDtypeStruct((b * s, D), f32),
            jax.ShapeDtypeStruct((b * s, D), bf16),
            jax.ShapeDtypeStruct((b * s, LANES), bf16),
            jax.ShapeDtypeStruct((nt, 8, LANES), f32),
            jax.ShapeDtypeStruct((b, W_MAX, PW), f32),
        ] + e_shapes + [
            jax.ShapeDtypeStruct((n_s, D), f32),
            jax.ShapeDtypeStruct((n_s, D), bf16),
            jax.ShapeDtypeStruct((n_s, LANES), bf16),
            jax.ShapeDtypeStruct((8, LANES), f32),
            jax.ShapeDtypeStruct((n_s, PW), f32),
            jax.ShapeDtypeStruct((n_s, D), f32),
        ],
        scratch_shapes=[pltpu.VMEM((W_MAX, PW), f32), pltpu.VMEM((SUB, TL, D), f32)],
        compiler_params=pltpu.CompilerParams(
            dimension_semantics=("arbitrary",), vmem_limit_bytes=VMEM_LIMIT),
    )(x2, x2, mod_p, vecs, w_in_b, w_sp, bias_full, w_pool_b, w_out_b, w_r, b_r,
      w_gate, w_up, w_down, x_s, mod_s, state_t)


def _moe_buffer_rows(ts, nt):
    worst = ts * nt + nt * N_GROUPS * (ROW_ALIGN - 1) + N_GROUPS * (RB - 1)
    return -(-worst // RB) * RB


def _moe_kernel(cnt_ref, h2_ref, r3_ref, x1_ref, mod_ref, g_ref, wg_ref, wu_ref, wd_ref, o_ref,
                hsbuf, hs_tile, ys_tile, pt_buf, tab,
                *, ts, nt, tpr, tps):
    w = pl.program_id(0)
    i = pl.program_id(1)
    rt = ts + LANES

    def copy_rows(src, src0, dst, dst0, nrows, ncols):
        def body(j, _):
            s0 = pl.multiple_of(src0 + j * ROW_ALIGN, ROW_ALIGN)
            d0 = pl.multiple_of(dst0 + j * ROW_ALIGN, ROW_ALIGN)
            dst[pl.ds(d0, ROW_ALIGN), :] = src[pl.ds(s0, ROW_ALIGN), :ncols]
            return 0
        lax.fori_loop(0, nrows // ROW_ALIGN, body, 0)

    @pl.when((w == 0) & (i == 0))
    def _():
        hsbuf[...] = jnp.zeros_like(hsbuf)
        ys_tile[...] = jnp.zeros_like(ys_tile)

    @pl.when(i == 0)
    def _sort_and_run_experts():
        def run_len(tile, g):
            c = cnt_ref[(w * nt + tile) * N_GROUPS + g]
            return ((c + (ROW_ALIGN - 1)) // ROW_ALIGN) * ROW_ALIGN

        lens = [[run_len(t, g) for g in range(N_GROUPS)] for t in range(nt)]
        region = [sum(lens[t][g] for t in range(nt)) for g in range(N_GROUPS)]
        region = [((r + (RB - 1)) // RB) * RB for r in region]
        base = [sum(region[:g]) for g in range(N_GROUPS)]
        offs = list(base)
        woff = []
        for t in range(nt):
            woff.append(list(offs))
            for g in range(N_GROUPS):
                tab[t * 2 * N_GROUPS + g] = lens[t][g]
                tab[t * 2 * N_GROUPS + N_GROUPS + g] = offs[g]
                offs[g] = offs[g] + lens[t][g]

        lane = lax.broadcasted_iota(jnp.int32, (ts, LANES), 1)
        lane_f = lane.astype(f32)
        r_i = lax.broadcasted_iota(jnp.int32, (ts, ts), 0)
        c_i = lax.broadcasted_iota(jnp.int32, (ts, ts), 1)
        ltri = jnp.where(r_i > c_i, 1.0, 0.0).astype(bf16)
        rt_lane = lax.broadcasted_iota(jnp.int32, (ts, rt), 1).astype(f32)

        tiles = range(nt)
        r3s = [r3_ref[t * ts:(t + 1) * ts, :] for t in tiles]
        gids = [jnp.sum(jnp.where(lane == GIDX_LANE, r3s[t].astype(f32), 0.0), axis=-1,
                        keepdims=True) for t in tiles]
        onehots = [jnp.where(lane_f == gids[t], 1.0, 0.0) for t in tiles]
        ranks = [_dot(ltri, onehots[t].astype(bf16)) for t in tiles]
        poss = []
        for t in tiles:
            seg = jnp.zeros((1, LANES), f32)
            start = 0
            for g in range(N_GROUPS):
                seg = seg + jnp.where(lane[0:1] == g, jnp.asarray(start, jnp.int32).astype(f32), 0.0)
                start = start + lens[t][g]
            poss.append(jnp.sum(onehots[t] * (ranks[t] + seg), axis=-1, keepdims=True))
        for t in tiles:
            pt_buf[t] = jnp.where(rt_lane == poss[t], 1.0, 0.0).astype(bf16)
        def sort_rows(t, x):
            return lax.dot_general(pt_buf[t], x, (((0,), (0,)), ((), ())),
                                   preferred_element_type=f32).astype(bf16)

        per_pass = hs_tile.shape[0]
        for t0 in range(0, nt, per_pass):
            for k in range(per_pass):
                t = t0 + k
                hs_tile[k, :, :D] = sort_rows(t, h2_ref[t * ts:(t + 1) * ts, :])
                hs_tile[k, :, D:] = sort_rows(t, r3_ref[t * ts:(t + 1) * ts, :])
            for k in range(per_pass):
                t = t0 + k
                start = 0
                for g in range(N_GROUPS):
                    copy_rows(hs_tile.at[k], start, hsbuf, woff[t][g], lens[t][g], D + LANES)
                    start = start + lens[t][g]

        def expert_rows(g, r0, nrows):
            rows = hsbuf[pl.ds(r0, nrows), :D]
            gate = _dot(rows, wg_ref[g * D:(g + 1) * D, :])
            up = _dot(rows, wu_ref[g * D:(g + 1) * D, :])
            c3 = hsbuf[pl.ds(r0, nrows), D:].astype(f32)
            cw_lanes = c3 + pltpu.roll(c3, LANES - N_EXPERTS, 1) + pltpu.roll(c3, LANES - 2 * N_EXPERTS, 1)
            cw = jnp.concatenate(
                [jnp.broadcast_to(cw_lanes[:, g * EPG + j:g * EPG + j + 1], (nrows, D_EXPERT))
                 for j in range(EPG)], axis=1)
            act = gate * jax.nn.sigmoid(gate) * up * cw
            hsbuf[pl.ds(r0, nrows), :D] = _dot(
                act.astype(bf16), wd_ref[g * D:(g + 1) * D, :]).astype(bf16)

        for g in range(N_GROUPS):
            n_blocks = region[g] // RB

            def three_blocks(b, _, g=g):
                expert_rows(g, pl.multiple_of(base[g] + 3 * b * RB, RB), 3 * RB)
                return 0
            lax.fori_loop(0, n_blocks // 3, three_blocks, 0)
            done = (n_blocks // 3) * 3

            @pl.when(n_blocks - done == 2)
            def _(g=g, done=done):
                expert_rows(g, pl.multiple_of(base[g] + done * RB, RB), 2 * RB)

            @pl.when(n_blocks - done == 1)
            def _(g=g, done=done):
                expert_rows(g, pl.multiple_of(base[g] + done * RB, RB), RB)

    tiles = [i * tps + k for k in range(tps)]
    for k, tile in enumerate(tiles):
        start = 0
        for g in range(N_GROUPS):
            ln = tab[tile * 2 * N_GROUPS + g]
            copy_rows(hsbuf, tab[tile * 2 * N_GROUPS + N_GROUPS + g], ys_tile.at[k], start, ln, D)
            start = start + ln
    fs = [_dot(pt_buf[tile], ys_tile[k]) for k, tile in enumerate(tiles)]
    for k, tile in enumerate(tiles):
        rows = slice(k * ts, (k + 1) * ts)
        tok0 = (w * nt + tile) * ts
        gt2_row = N_MOD - 1
        gt2 = (mod_ref[gt2_row, pl.ds(tok0, ts), :] if tpr == 1
               else mod_ref[gt2_row, pl.ds(tok0 // tpr, 1), :])
        o_ref[rows, :] = x1_ref[rows, :] + gt2 * _rms(fs[k], g_ref[...])


def _moe_call(cnt, x1, h2, r3, mod, g_post, wg, wu, wd, ts, nt, tpr):
    n = x1.shape[0]
    win = ts * nt
    rbuf = _moe_buffer_rows(ts, nt)
    rt = ts + LANES
    tps = 2 if nt % 2 == 0 else 1
    steps = nt // tps
    grid_spec = pltpu.PrefetchScalarGridSpec(
        num_scalar_prefetch=1,
        grid=(n // win, steps),
        in_specs=[
            pl.BlockSpec((win, D), lambda w, i, c: (w, 0)),
            pl.BlockSpec((win, LANES), lambda w, i, c: (w, 0)),
            pl.BlockSpec((tps * ts, D), lambda w, i, c: (w * steps + i, 0)),
            _const_spec(mod.shape),
            _const_spec(g_post.shape),
            _const_spec(wg.shape),
            _const_spec(wu.shape),
            _const_spec(wd.shape),
        ],
        out_specs=pl.BlockSpec((tps * ts, D), lambda w, i, c: (w * steps + i, 0)),
        scratch_shapes=[
            pltpu.VMEM((rbuf, D + LANES), bf16),
            pltpu.VMEM((4 if nt % 4 == 0 else tps, rt, D + LANES), bf16),
            pltpu.VMEM((tps, rt, D), bf16),
            pltpu.VMEM((nt, ts, rt), bf16),
            pltpu.SMEM((nt * 2 * N_GROUPS,), jnp.int32),
        ],
    )
    return pl.pallas_call(
        functools.partial(_moe_kernel, ts=ts, nt=nt, tpr=tpr, tps=tps),
        grid_spec=grid_spec,
        out_shape=jax.ShapeDtypeStruct((n, D), f32),
        compiler_params=pltpu.CompilerParams(
            dimension_semantics=("arbitrary", "arbitrary"), vmem_limit_bytes=VMEM_LIMIT),
    )(cnt, h2, r3, x1, mod, g_post, wg, wu, wd)


def _count_table(cnt):
    return cnt[:, 0, :N_GROUPS].astype(jnp.int32).reshape(-1)


def kernel(x_prompt, x_sample, c_prompt, c_sample, state_pool, w_ada, b_ada, g_pre_mix, g_post_mix, g_pre_ffn, g_post_ffn, w_in, ln_v_g, ln_v_b, w_spatial, b_spatial, w_pool, pool_scale, w_out, w_router_grp, b_router_grp, w_router_exp, b_router_exp, w_exp_gate, w_exp_up, w_exp_down):
    depth = w_in.shape[0]
    assert depth == 1
    b, s, _ = x_prompt.shape
    n_s = x_sample.shape[0]
    l = 0

    mod_p, mod_s, w_in_b, w_out_b = _mod_call(
        c_prompt, c_sample, w_ada[l], b_ada[l], w_in[l], w_out[l])

    ws, bs = w_spatial[l], b_spatial[l]
    zeros = jnp.zeros((D,), f32)
    vecs = jnp.stack([
        g_pre_mix[l], g_post_mix[l], g_pre_ffn[l], g_post_ffn[l], ln_v_g[l], ln_v_b[l],
        pool_scale[l], zeros,
        jnp.repeat(ws[:, 0, 0], HEAD_DIM), jnp.repeat(bs[:, 0], HEAD_DIM),
        zeros, zeros, zeros, zeros, zeros, zeros])
    bias_full = jnp.repeat(bs.T, HEAD_DIM, axis=1)
    pad = LANES - N_EXPERTS - N_GROUPS
    w_r = jnp.concatenate([w_router_exp[l], w_router_grp[l], jnp.zeros((D, pad), f32)], axis=1)
    w_r_hi = w_r.astype(bf16)
    w_r_lo = (w_r - w_r_hi.astype(f32)).astype(bf16)
    w_r2 = jnp.concatenate([w_r_hi, w_r_lo], axis=1)
    b_r = jnp.concatenate([b_router_exp[l], b_router_grp[l], jnp.zeros((pad,), f32)])[None]

    state_t = jnp.transpose(state_pool[l], (1, 0, 2))
    (x1_p, h2_p, r3_p, cnt_p, plast, wg, wu, wd,
     x1_s, h2_s, r3_s, cnt_s, p_s, v_s) = _stage1_call(
        x_prompt, mod_p, vecs, w_in_b, ws, bias_full, w_pool[l], w_out_b, w_r2, b_r,
        w_exp_gate[l], w_exp_up[l], w_exp_down[l], x_sample.reshape(n_s, D), mod_s, state_t)

    g_post = g_post_ffn[l].reshape(1, D)
    y_p = _moe_call(
        _count_table(cnt_p), x1_p.reshape(b * s, D), h2_p.reshape(b * s, D),
        r3_p.reshape(b * s, LANES), mod_p, g_post, wg, wu, wd, TL, MOE_WINDOW // TL, s)
    y_s = _moe_call(
        _count_table(cnt_s[None]), x1_s, h2_s, r3_s, mod_s, g_post, wg, wu, wd, n_s, 1, 1)

    state_pool_prompt = plast[:, 1:][None]
    state_pool_sample = jnp.concatenate([state_pool[l][:, 1:], p_s[:, None, :]], axis=1)[None]
    chunk_v_sample = v_s.reshape(1, n_s, 1, D)
    return (y_p.reshape(b, s, D), y_s.reshape(n_s, 1, D), state_pool_prompt,
            state_pool_sample, chunk_v_sample)
```

```python
import functools

import jax
import jax.numpy as jnp
from jax import lax
from jax.experimental import pallas as pl
from jax.experimental.pallas import tpu as pltpu

D = 1024
CHUNK = 128
HEADS = 8
HEAD_DIM = 128
WINDOWS = (2, 4, 8, 16)
PW = 512
PG = 128
W_MAX = 16
N_GROUPS = 4
EPG = 8
N_EXPERTS = 32
D_EXPERT = 128
EPS = 1e-6
N_MOD = 6
LANES = 128
GROUP_LANE0 = 32
GIDX_LANE = 96

TL = 256
SUB = 2
MOE_WINDOW = 2048
ROW_ALIGN = 16
RB = 128
VMEM_LIMIT = 60 * 1024 * 1024

bf16 = jnp.bfloat16
f32 = jnp.float32


def _rms(x, g):
    ms = jnp.mean(x * x, axis=-1, keepdims=True)
    return x * lax.rsqrt(ms + EPS) * g


def _dot(a, b):
    return jnp.dot(a, b, preferred_element_type=f32)


_GELU_C = 2.0 * 0.7978845608028654


def _gelu(x):
    t = x * ((-_GELU_C) + (-_GELU_C * 0.044715) * (x * x))
    return x / (1.0 + jnp.exp(t))


def _expert_cast_specs(step_to_expert):
    e = step_to_expert
    col_block = pl.BlockSpec((D, D_EXPERT), lambda t: (e(t) // EPG, e(t) % EPG))
    in_specs = [
        pl.BlockSpec((1, D, D_EXPERT), lambda t: (e(t), 0, 0)),
        pl.BlockSpec((1, D, D_EXPERT), lambda t: (e(t), 0, 0)),
        pl.BlockSpec((1, D_EXPERT, D), lambda t: (e(t), 0, 0)),
    ]
    out_specs = [col_block, col_block, pl.BlockSpec((D_EXPERT, D), lambda t: (e(t), 0))]
    out_shapes = [
        jax.ShapeDtypeStruct((N_GROUPS * D, EPG * D_EXPERT), bf16),
        jax.ShapeDtypeStruct((N_GROUPS * D, EPG * D_EXPERT), bf16),
        jax.ShapeDtypeStruct((N_EXPERTS * D_EXPERT, D), bf16),
    ]
    return in_specs, out_specs, out_shapes


def _mod_kernel(cp_ref, cs_ref, w_ref, b_ref, w_in_ref, w_out_ref, op_ref, os_ref, w_in_o, w_out_o):
    w = w_ref[...].astype(bf16)
    for c_ref, o_ref in ((cp_ref, op_ref), (cs_ref, os_ref)):
        c = c_ref[...]
        o_ref[0] = _dot((c * jax.nn.sigmoid(c)).astype(bf16), w) + b_ref[0]
    w_in_o[...] = w_in_ref[...].astype(bf16)
    w_out_o[...] = w_out_ref[...].astype(bf16)


def _mod_call(c_p, c_s, w_ada, b_ada, w_in, w_out):
    nb, n_s = c_p.shape[0], c_s.shape[0]
    in_cols = w_in.shape[1] // N_MOD
    assert w_in.shape[1] == N_MOD * in_cols and in_cols % LANES == 0
    return pl.pallas_call(
        _mod_kernel,
        grid=(N_MOD,),
        in_specs=[
            pl.BlockSpec((nb, D), lambda j: (0, 0)),
            pl.BlockSpec((n_s, D), lambda j: (0, 0)),
            pl.BlockSpec((D, D), lambda j: (0, j)),
            pl.BlockSpec((1, 1, D), lambda j: (j, 0, 0)),
            pl.BlockSpec((D, in_cols), lambda j: (0, j)),
            pl.BlockSpec((D, D), lambda j: (0, 0)),
        ],
        out_specs=[pl.BlockSpec((1, nb, D), lambda j: (j, 0, 0)),
                   pl.BlockSpec((1, n_s, D), lambda j: (j, 0, 0)),
                   pl.BlockSpec((D, in_cols), lambda j: (0, j)),
                   pl.BlockSpec((D, D), lambda j: (0, 0))],
        out_shape=[jax.ShapeDtypeStruct((N_MOD, nb, D), f32),
                   jax.ShapeDtypeStruct((N_MOD, n_s, D), f32),
                   jax.ShapeDtypeStruct(w_in.shape, bf16),
                   jax.ShapeDtypeStruct(w_out.shape, bf16)],
        compiler_params=pltpu.CompilerParams(
            dimension_semantics=("arbitrary",), vmem_limit_bytes=VMEM_LIMIT),
    )(c_p, c_s, w_ada, b_ada.reshape(N_MOD, 1, D), w_in, w_out)


def _route(logits):
    t = logits.shape[0]
    lane = lax.broadcasted_iota(jnp.int32, (t, LANES), 1)
    lane_f = lane.astype(f32)
    neg = -jnp.inf
    big = 1e9
    gmask = (lane >= GROUP_LANE0) & (lane < GROUP_LANE0 + N_GROUPS)
    gl = jnp.where(gmask, logits, neg)
    gmax = jnp.max(gl, axis=-1, keepdims=True)
    g_idx = jnp.min(jnp.where(gl == gmax, lane_f - GROUP_LANE0, big), axis=-1, keepdims=True)
    sumexp = jnp.sum(jnp.where(gmask, jnp.exp(gl - gmax), 0.0), axis=-1, keepdims=True)
    p_g = 1.0 / sumexp
    lane_grp = (lane >> 3).astype(f32)
    emask = (lane < N_EXPERTS) & (lane_grp == g_idx)
    el = jnp.where(emask, logits, neg)
    m1 = jnp.max(el, axis=-1, keepdims=True)
    i1 = jnp.min(jnp.where(el == m1, lane_f, big), axis=-1, keepdims=True)
    el2 = jnp.where(lane_f == i1, neg, el)
    m2 = jnp.max(el2, axis=-1, keepdims=True)
    i2 = jnp.min(jnp.where(el2 == m2, lane_f, big), axis=-1, keepdims=True)
    e = jnp.exp(m2 - m1)
    w1 = p_g / (1.0 + e)
    w2 = w1 * e

    def split3(w):
        hi = w.astype(bf16).astype(f32)
        mid = (w - hi).astype(bf16).astype(f32)
        lo = w - hi - mid
        return hi, mid, lo

    r3 = jnp.where(lane == GIDX_LANE, g_idx, 0.0)
    for idx, w in ((i1, w1), (i2, w2)):
        for part, wp in enumerate(split3(w)):
            r3 = r3 + jnp.where(lane_f == idx + float(part * N_EXPERTS), wp, 0.0)
    counts = jnp.sum(jnp.where(lane_f == g_idx, 1.0, 0.0), axis=0, keepdims=True)
    return r3.astype(bf16), jnp.broadcast_to(counts, (8, LANES))


def _merge_project(u, ga, gb, mix, y_b, w_out_ref):
    y_a = u * mix
    merged = jax.nn.sigmoid(ga) * y_a + jax.nn.sigmoid(gb) * y_b
    return _dot(merged.astype(bf16), w_out_ref[...])


def _residual(x, y, mods, vec_ref):
    sh1, sc1, gt1, sh2, sc2, gt2 = mods
    x1 = x + gt1 * _rms(y, vec_ref[1:2])
    h2 = _rms(x1, vec_ref[2:3] * (1.0 + sc2)) + sh2
    h2_hi = h2.astype(bf16)
    h2_lo = (h2 - h2_hi.astype(f32)).astype(bf16)
    return x1, h2_hi, h2_lo


def _router(h2_hi, h2_lo, w_r_ref, b_r_ref):
    r = _dot(h2_hi, w_r_ref[...]) + _dot(h2_lo, w_r_ref[...])
    logits = r[:, :LANES] + r[:, LANES:] + b_r_ref[...]
    return _route(logits)


def _residual_route(x, y, mods, vec_ref, w_r_ref, b_r_ref):
    x1, h2_hi, h2_lo = _residual(x, y, mods, vec_ref)
    r3, counts = _router(h2_hi, h2_lo, w_r_ref, b_r_ref)
    return x1, h2_hi, r3, counts


def _in_proj(x, mods, vec_ref, w_in_ref, after_first_dot=lambda: None,
             after_second_dot=lambda: None):
    sh1, sc1 = mods[0], mods[1]
    h = _rms(x, vec_ref[0:1] * (1.0 + sc1)) + sh1
    hb = h.astype(bf16)
    zu = _dot(hb, w_in_ref[:, 0:D])
    after_first_dot()
    zv = _dot(hb, w_in_ref[:, D:2 * D])
    after_second_dot()
    p = _dot(hb, w_in_ref[:, 2 * D:2 * D + PW])
    ga = _dot(hb, w_in_ref[:, 2 * D + PW:3 * D + PW])
    gb = _dot(hb, w_in_ref[:, 3 * D + PW:4 * D + PW])
    return zu, zv, p, ga, gb


def _activate(zu, zv, vec_ref):
    u = _gelu(zu)
    v = _gelu(zv)
    mu = jnp.mean(v, axis=-1, keepdims=True)
    vc = v - mu
    var = jnp.mean(vc * vc, axis=-1, keepdims=True)
    v = vc * lax.rsqrt(var + EPS) * vec_ref[4:5] + vec_ref[5:6]
    return u, v


def _pool_out(d_groups, vec_ref, w_pool_ref):
    parts = [_dot(d.astype(bf16), w_pool_ref[gi].astype(bf16)) for gi, d in enumerate(d_groups)]
    return jnp.concatenate(parts, axis=1) * vec_ref[6:7]


def _stage1_kernel(x_ref, xprev_ref, mod_ref, vec_ref, w_in_ref, w_sp_ref, bias_ref,
                   w_pool_ref, w_out_ref, w_r_ref, b_r_ref, eg_ref, eu_ref, ed_ref,
                   xs_ref, mods_ref, states_ref,
                   x1_ref, h2_ref, r3_ref, cnt_ref, plast_ref, wg_ref, wu_ref, wd_ref,
                   x1s_ref, h2s_ref, r3s_ref, cnts_ref, ps_ref, vs_ref,
                   pbuf, ybuf, *, tiles_per_seq):
    t = pl.program_id(0)
    last = pl.num_programs(0) - 1

    def mods_of(tile):
        b = tile // tiles_per_seq
        return [mod_ref[i, pl.ds(b, 1), :] for i in range(N_MOD)]

    split = {}

    def second_half_residual(k):
        rows = slice(k * TL, (k + 1) * TL)
        tile = jnp.maximum((t - 1) * SUB + k, 0)
        x1, h2_hi, h2_lo = _residual(xprev_ref[rows, :], ybuf[k], mods_of(tile), vec_ref)
        x1_ref[rows, :] = x1
        h2_ref[rows, :] = h2_hi
        split[k] = (h2_hi, h2_lo)

    def second_half_router(k):
        rows = slice(k * TL, (k + 1) * TL)
        r3, counts = _router(*split[k], w_r_ref, b_r_ref)
        r3_ref[rows, :] = r3
        cnt_ref[k] = counts

    def second_half(k):
        second_half_residual(k)
        second_half_router(k)

    def cast_expert():
        wg_ref[...] = eg_ref[0].astype(bf16)
        wu_ref[...] = eu_ref[0].astype(bf16)
        wd_ref[...] = ed_ref[0].astype(bf16)

    def first_half(k, under_first, under_second):
        tile = t * SUB + k
        s = tile % tiles_per_seq
        x = x_ref[k * TL:(k + 1) * TL, :]
        zu, zv, p, ga, gb = _in_proj(x, mods_of(tile), vec_ref, w_in_ref, under_first, under_second)
        u, v = _activate(zu, zv, vec_ref)

        vb = v.astype(bf16)
        row = lax.broadcasted_iota(jnp.int32, (CHUNK, CHUNK), 0)
        col = lax.broadcasted_iota(jnp.int32, (CHUNK, CHUNK), 1)
        w_tril = [jnp.where(row >= col, w_sp_ref[hd], 0.0).astype(bf16) for hd in range(HEADS)]
        bias = bias_ref[...]
        chunks = []
        for c in range(TL // CHUNK):
            heads = [_dot(w_tril[hd],
                          vb[c * CHUNK:(c + 1) * CHUNK, hd * HEAD_DIM:(hd + 1) * HEAD_DIM])
                     for hd in range(HEADS)]
            chunks.append(jnp.concatenate(heads, axis=1) + bias)
        mix = jnp.concatenate(chunks, axis=0)

        carry = jnp.where(s == 0, 0.0, pbuf[...])
        ext = jnp.concatenate([carry, p], axis=0)
        pos = s * TL + lax.broadcasted_iota(jnp.int32, (TL, PG), 0)
        d_groups = []
        for gi, w in enumerate(WINDOWS):
            acc = ext[:, gi * PG:(gi + 1) * PG]
            shift = 1
            while shift < w:
                acc = acc + pltpu.roll(acc, shift, 0)
                shift *= 2
            cnt = jnp.minimum(pos + 1, w).astype(f32)
            d_groups.append(acc[W_MAX:] / cnt - p[:, gi * PG:(gi + 1) * PG])
        pbuf[...] = p[TL - W_MAX:]
        plast_ref[0] = p[TL - W_MAX:]
        y_b = _pool_out(d_groups, vec_ref, w_pool_ref)
        ybuf[k] = _merge_project(u, ga, gb, mix, y_b, w_out_ref)

    @pl.when(t == 0)
    def _():
        ybuf[...] = jnp.zeros_like(ybuf)
        pbuf[...] = jnp.zeros_like(pbuf)

    @pl.when(t < last)
    def _():
        for k in range(SUB):
            def under_second(k=k):
                second_half_router(k)
                if k == SUB - 1:
                    cast_expert()
            first_half(k, functools.partial(second_half_residual, k), under_second)

    @pl.when(t == last)
    def _():
        for k in range(SUB):
            second_half(k)
        _sample_tokens(xs_ref, mods_ref, vec_ref, w_in_ref, states_ref, w_pool_ref, w_out_ref,
                       w_r_ref, b_r_ref, x1s_ref, h2s_ref, r3s_ref, cnts_ref, ps_ref, vs_ref)


def _sample_tokens(x_ref, mod_ref, vec_ref, w_in_ref, state_ref, w_pool_ref,
                   w_out_ref, w_r_ref, b_r_ref,
                   x1_ref, h2_ref, r3_ref, cnt_ref, p_ref, v_ref):
    x = x_ref[...]
    mods = [mod_ref[i] for i in range(N_MOD)]
    zu, zv, p, ga, gb = _in_proj(x, mods, vec_ref, w_in_ref)
    u, v = _activate(zu, zv, vec_ref)
    v_ref[...] = v
    p_ref[...] = p
    mix = v * vec_ref[8:9] + vec_ref[9:10]
    d_groups = []
    for gi, w in enumerate(WINDOWS):
        sl = slice(gi * PG, (gi + 1) * PG)
        acc = p[:, sl]
        for r in range(W_MAX - w, W_MAX - 1):
            acc = acc + state_ref[r][:, sl]
        d_groups.append(acc / float(w) - p[:, sl])
    y_b = _pool_out(d_groups, vec_ref, w_pool_ref)
    y = _merge_project(u, ga, gb, mix, y_b, w_out_ref)
    x1, h2b, r3, counts = _residual_route(x, y, mods, vec_ref, w_r_ref, b_r_ref)
    x1_ref[...] = x1
    h2_ref[...] = h2b
    r3_ref[...] = r3
    cnt_ref[...] = counts


def _const_spec(shape):
    nd = len(shape)
    return pl.BlockSpec(shape, lambda *_: (0,) * nd, pipeline_mode=pl.Buffered(1))


def _stage1_call(x, mod_p, vecs, w_in_b, w_sp, bias_full, w_pool_b, w_out_b, w_r, b_r,
                 w_gate, w_up, w_down, x_s, mod_s, state_t):
    b, s, _ = x.shape
    n_s = x_s.shape[0]
    const_out = lambda shape: pl.BlockSpec(shape, lambda t: (0,) * len(shape))
    ns = s // TL
    nt = b * ns
    steps = nt // SUB
    assert ns % SUB == 0
    assert steps >= N_EXPERTS
    x2 = x.reshape(b * s, D)
    rows = SUB * TL
    cur = lambda t: (jnp.minimum(t, steps - 1), 0)
    prev = lambda t: (jnp.maximum(t - 1, 0), 0)
    e_in, e_out, e_shapes = _expert_cast_specs(lambda t: jnp.minimum(t, N_EXPERTS - 1))
    return pl.pallas_call(
        functools.partial(_stage1_kernel, tiles_per_seq=ns),
        grid=(steps + 1,),
        in_specs=[
            pl.BlockSpec((rows, D), cur),
            pl.BlockSpec((rows, D), prev),
            _const_spec(mod_p.shape),
            _const_spec(vecs.shape),
            _const_spec(w_in_b.shape),
            _const_spec(w_sp.shape),
            _const_spec(bias_full.shape),
            _const_spec(w_pool_b.shape),
            _const_spec(w_out_b.shape),
            _const_spec(w_r.shape),
            _const_spec(b_r.shape),
        ] + e_in + [_const_spec(x_s.shape), _const_spec(mod_s.shape), _const_spec(state_t.shape)],
        out_specs=[
            pl.BlockSpec((rows, D), prev),
            pl.BlockSpec((rows, D), prev),
            pl.BlockSpec((rows, LANES), prev),
            pl.BlockSpec((SUB, 8, LANES), lambda t: (jnp.maximum(t - 1, 0), 0, 0)),
            pl.BlockSpec((1, W_MAX, PW),
                         lambda t: (jnp.minimum(t, steps - 1) * SUB // ns, 0, 0)),
        ] + e_out + [const_out((n_s, D)), const_out((n_s, D)), const_out((n_s, LANES)),
                     const_out((8, LANES)), const_out((n_s, PW)), const_out((n_s, D))],
        out_shape=[
            jax.ShapeDtypeStruct((b * s, D), f32),
            jax.ShapeDtypeStruct((b * s, D), bf16),
            jax.ShapeDtypeStruct((b * s, LANES), bf16),
            jax.ShapeDtypeStruct((nt, 8, LANES), f32),
            jax.ShapeDtypeStruct((b, W_MAX, PW), f32),
        ] + e_shapes + [
            jax.ShapeDtypeStruct((n_s, D), f32),
            jax.ShapeDtypeStruct((n_s, D), bf16),
            jax.ShapeDtypeStruct((n_s, LANES), bf16),
            jax.ShapeDtypeStruct((8, LANES), f32),
            jax.ShapeDtypeStruct((n_s, PW), f32),
            jax.ShapeDtypeStruct((n_s, D), f32),
        ],
        scratch_shapes=[pltpu.VMEM((W_MAX, PW), f32), pltpu.VMEM((SUB, TL, D), f32)],
        compiler_params=pltpu.CompilerParams(
            dimension_semantics=("arbitrary",), vmem_limit_bytes=VMEM_LIMIT),
    )(x2, x2, mod_p, vecs, w_in_b, w_sp, bias_full, w_pool_b, w_out_b, w_r, b_r,
      w_gate, w_up, w_down, x_s, mod_s, state_t)


def _moe_buffer_rows(ts, nt):
    worst = ts * nt + nt * N_GROUPS * (ROW_ALIGN - 1) + N_GROUPS * (RB - 1)
    return -(-worst // RB) * RB


def _moe_kernel(cnt_ref, h2_ref, r3_ref, x1_ref, mod_ref, g_ref, wg_ref, wu_ref, wd_ref, o_ref,
                hsbuf, hs_tile, ys_tile, pt_buf, tab,
                *, ts, nt, tpr, tps):
    w = pl.program_id(0)
    i = pl.program_id(1)
    rt = ts + LANES

    def copy_rows(src, src0, dst, dst0, nrows, ncols):
        def body(j, _):
            s0 = pl.multiple_of(src0 + j * ROW_ALIGN, ROW_ALIGN)
            d0 = pl.multiple_of(dst0 + j * ROW_ALIGN, ROW_ALIGN)
            dst[pl.ds(d0, ROW_ALIGN), :] = src[pl.ds(s0, ROW_ALIGN), :ncols]
            return 0
        lax.fori_loop(0, nrows // ROW_ALIGN, body, 0)

    @pl.when((w == 0) & (i == 0))
    def _():
        hsbuf[...] = jnp.zeros_like(hsbuf)
        ys_tile[...] = jnp.zeros_like(ys_tile)

    @pl.when(i == 0)
    def _sort_and_run_experts():
        def run_len(tile, g):
            c = cnt_ref[(w * nt + tile) * N_GROUPS + g]
            return ((c + (ROW_ALIGN - 1)) // ROW_ALIGN) * ROW_ALIGN

        lens = [[run_len(t, g) for g in range(N_GROUPS)] for t in range(nt)]
        region = [sum(lens[t][g] for t in range(nt)) for g in range(N_GROUPS)]
        region = [((r + (RB - 1)) // RB) * RB for r in region]
        base = [sum(region[:g]) for g in range(N_GROUPS)]
        offs = list(base)
        woff = []
        for t in range(nt):
            woff.append(list(offs))
            for g in range(N_GROUPS):
                tab[t * 2 * N_GROUPS + g] = lens[t][g]
                tab[t * 2 * N_GROUPS + N_GROUPS + g] = offs[g]
                offs[g] = offs[g] + lens[t][g]

        lane = lax.broadcasted_iota(jnp.int32, (ts, LANES), 1)
        lane_f = lane.astype(f32)
        r_i = lax.broadcasted_iota(jnp.int32, (ts, ts), 0)
        c_i = lax.broadcasted_iota(jnp.int32, (ts, ts), 1)
        ltri = jnp.where(r_i > c_i, 1.0, 0.0).astype(bf16)
        rt_lane = lax.broadcasted_iota(jnp.int32, (ts, rt), 1).astype(f32)

        tiles = range(nt)
        r3s = [r3_ref[t * ts:(t + 1) * ts, :] for t in tiles]
        gids = [jnp.sum(jnp.where(lane == GIDX_LANE, r3s[t].astype(f32), 0.0), axis=-1,
                        keepdims=True) for t in tiles]
        onehots = [jnp.where(lane_f == gids[t], 1.0, 0.0) for t in tiles]
        ranks = [_dot(ltri, onehots[t].astype(bf16)) for t in tiles]
        poss = []
        for t in tiles:
            seg = jnp.zeros((1, LANES), f32)
            start = 0
            for g in range(N_GROUPS):
                seg = seg + jnp.where(lane[0:1] == g, jnp.asarray(start, jnp.int32).astype(f32), 0.0)
                start = start + lens[t][g]
            poss.append(jnp.sum(onehots[t] * (ranks[t] + seg), axis=-1, keepdims=True))
        for t in tiles:
            pt_buf[t] = jnp.where(rt_lane == poss[t], 1.0, 0.0).astype(bf16)
        def sort_rows(t, x):
            return lax.dot_general(pt_buf[t], x, (((0,), (0,)), ((), ())),
                                   preferred_element_type=f32).astype(bf16)

        per_pass = hs_tile.shape[0]
        for t0 in range(0, nt, per_pass):
            for k in range(per_pass):
                t = t0 + k
                hs_tile[k, :, :D] = sort_rows(t, h2_ref[t * ts:(t + 1) * ts, :])
                hs_tile[k, :, D:] = sort_rows(t, r3_ref[t * ts:(t + 1) * ts, :])
            for k in range(per_pass):
                t = t0 + k
                start = 0
                for g in range(N_GROUPS):
                    copy_rows(hs_tile.at[k], start, hsbuf, woff[t][g], lens[t][g], D + LANES)
                    start = start + lens[t][g]

        def expert_rows(g, r0, nrows):
            rows = hsbuf[pl.ds(r0, nrows), :D]
            gate = _dot(rows, wg_ref[g * D:(g + 1) * D, :])
            up = _dot(rows, wu_ref[g * D:(g + 1) * D, :])
            c3 = hsbuf[pl.ds(r0, nrows), D:].astype(f32)
            cw_lanes = c3 + pltpu.roll(c3, LANES - N_EXPERTS, 1) + pltpu.roll(c3, LANES - 2 * N_EXPERTS, 1)
            cw = jnp.concatenate(
                [jnp.broadcast_to(cw_lanes[:, g * EPG + j:g * EPG + j + 1], (nrows, D_EXPERT))
                 for j in range(EPG)], axis=1)
            act = gate * jax.nn.sigmoid(gate) * up * cw
            hsbuf[pl.ds(r0, nrows), :D] = _dot(
                act.astype(bf16), wd_ref[g * D:(g + 1) * D, :]).astype(bf16)

        for g in range(N_GROUPS):
            n_blocks = region[g] // RB

            def three_blocks(b, _, g=g):
                expert_rows(g, pl.multiple_of(base[g] + 3 * b * RB, RB), 3 * RB)
                return 0
            lax.fori_loop(0, n_blocks // 3, three_blocks, 0)
            done = (n_blocks // 3) * 3

            @pl.when(n_blocks - done == 2)
            def _(g=g, done=done):
                expert_rows(g, pl.multiple_of(base[g] + done * RB, RB), 2 * RB)

            @pl.when(n_blocks - done == 1)
            def _(g=g, done=done):
                expert_rows(g, pl.multiple_of(base[g] + done * RB, RB), RB)

    tiles = [i * tps + k for k in range(tps)]
    for k, tile in enumerate(tiles):
        start = 0
        for g in range(N_GROUPS):
            ln = tab[tile * 2 * N_GROUPS + g]
            copy_rows(hsbuf, tab[tile * 2 * N_GROUPS + N_GROUPS + g], ys_tile.at[k], start, ln, D)
            start = start + ln
    fs = [_dot(pt_buf[tile], ys_tile[k]) for k, tile in enumerate(tiles)]
    for k, tile in enumerate(tiles):
        rows = slice(k * ts, (k + 1) * ts)
        tok0 = (w * nt + tile) * ts
        gt2_row = N_MOD - 1
        gt2 = (mod_ref[gt2_row, pl.ds(tok0, ts), :] if tpr == 1
               else mod_ref[gt2_row, pl.ds(tok0 // tpr, 1), :])
        o_ref[rows, :] = x1_ref[rows, :] + gt2 * _rms(fs[k], g_ref[...])


def _moe_call(cnt, x1, h2, r3, mod, g_post, wg, wu, wd, ts, nt, tpr):
    n = x1.shape[0]
    win = ts * nt
    rbuf = _moe_buffer_rows(ts, nt)
    rt = ts + LANES
    tps = 2 if nt % 2 == 0 else 1
    steps = nt // tps
    grid_spec = pltpu.PrefetchScalarGridSpec(
        num_scalar_prefetch=1,
        grid=(n // win, steps),
        in_specs=[
            pl.BlockSpec((win, D), lambda w, i, c: (w, 0)),
            pl.BlockSpec((win, LANES), lambda w, i, c: (w, 0)),
            pl.BlockSpec((tps * ts, D), lambda w, i, c: (w * steps + i, 0)),
            _const_spec(mod.shape),
            _const_spec(g_post.shape),
            _const_spec(wg.shape),
            _const_spec(wu.shape),
            _const_spec(wd.shape),
        ],
        out_specs=pl.BlockSpec((tps * ts, D), lambda w, i, c: (w * steps + i, 0)),
        scratch_shapes=[
            pltpu.VMEM((rbuf, D + LANES), bf16),
            pltpu.VMEM((4 if nt % 4 == 0 else tps, rt, D + LANES), bf16),
            pltpu.VMEM((tps, rt, D), bf16),
            pltpu.VMEM((nt, ts, rt), bf16),
            pltpu.SMEM((nt * 2 * N_GROUPS,), jnp.int32),
        ],
    )
    return pl.pallas_call(
        functools.partial(_moe_kernel, ts=ts, nt=nt, tpr=tpr, tps=tps),
        grid_spec=grid_spec,
        out_shape=jax.ShapeDtypeStruct((n, D), f32),
        compiler_params=pltpu.CompilerParams(
            dimension_semantics=("arbitrary", "arbitrary"), vmem_limit_bytes=VMEM_LIMIT),
    )(cnt, h2, r3, x1, mod, g_post, wg, wu, wd)


def _count_table(cnt):
    return cnt[:, 0, :N_GROUPS].astype(jnp.int32).reshape(-1)


def kernel(x_prompt, x_sample, c_prompt, c_sample, state_pool, w_ada, b_ada, g_pre_mix, g_post_mix, g_pre_ffn, g_post_ffn, w_in, ln_v_g, ln_v_b, w_spatial, b_spatial, w_pool, pool_scale, w_out, w_router_grp, b_router_grp, w_router_exp, b_router_exp, w_exp_gate, w_exp_up, w_exp_down):
    depth = w_in.shape[0]
    assert depth == 1
    b, s, _ = x_prompt.shape
    n_s = x_sample.shape[0]
    l = 0

    mod_p, mod_s, w_in_b, w_out_b = _mod_call(
        c_prompt, c_sample, w_ada[l], b_ada[l], w_in[l], w_out[l])

    ws, bs = w_spatial[l], b_spatial[l]
    zeros = jnp.zeros((D,), f32)
    vecs = jnp.stack([
        g_pre_mix[l], g_post_mix[l], g_pre_ffn[l], g_post_ffn[l], ln_v_g[l], ln_v_b[l],
        pool_scale[l], zeros,
        jnp.repeat(ws[:, 0, 0], HEAD_DIM), jnp.repeat(bs[:, 0], HEAD_DIM),
        zeros, zeros, zeros, zeros, zeros, zeros])
    bias_full = jnp.repeat(bs.T, HEAD_DIM, axis=1)
    pad = LANES - N_EXPERTS - N_GROUPS
    w_r = jnp.concatenate([w_router_exp[l], w_router_grp[l], jnp.zeros((D, pad), f32)], axis=1)
    w_r_hi = w_r.astype(bf16)
    w_r_lo = (w_r - w_r_hi.astype(f32)).astype(bf16)
    w_r2 = jnp.concatenate([w_r_hi, w_r_lo], axis=1)
    b_r = jnp.concatenate([b_router_exp[l], b_router_grp[l], jnp.zeros((pad,), f32)])[None]

    state_t = jnp.transpose(state_pool[l], (1, 0, 2))
    (x1_p, h2_p, r3_p, cnt_p, plast, wg, wu, wd,
     x1_s, h2_s, r3_s, cnt_s, p_s, v_s) = _stage1_call(
        x_prompt, mod_p, vecs, w_in_b, ws, bias_full, w_pool[l], w_out_b, w_r2, b_r,
        w_exp_gate[l], w_exp_up[l], w_exp_down[l], x_sample.reshape(n_s, D), mod_s, state_t)

    g_post = g_post_ffn[l].reshape(1, D)
    y_p = _moe_call(
        _count_table(cnt_p), x1_p.reshape(b * s, D), h2_p.reshape(b * s, D),
        r3_p.reshape(b * s, LANES), mod_p, g_post, wg, wu, wd, TL, MOE_WINDOW // TL, s)
    y_s = _moe_call(
        _count_table(cnt_s[None]), x1_s, h2_s, r3_s, mod_s, g_post, wg, wu, wd, n_s, 1, 1)

    state_pool_prompt = plast[:, 1:][None]
    state_pool_sample = jnp.concatenate([state_pool[l][:, 1:], p_s[:, None, :]], axis=1)[None]
    chunk_v_sample = v_s.reshape(1, n_s, 1, D)
    return (y_p.reshape(b, s, D), y_s.reshape(n_s, 1, D), state_pool_prompt,
            state_pool_sample, chunk_v_sample)
```

```python
import functools

import jax
import jax.numpy as jnp
from jax import lax
from jax.experimental import pallas as pl
from jax.experimental.pallas import tpu as pltpu

D = 1024
CHUNK = 128
HEADS = 8
HEAD_DIM = 128
WINDOWS = (2, 4, 8, 16)
PW = 512
PG = 128
W_MAX = 16
N_GROUPS = 4
EPG = 8
N_EXPERTS = 32
D_EXPERT = 128
EPS = 1e-6
N_MOD = 6
LANES = 128
GROUP_LANE0 = 32
GIDX_LANE = 96

TL = 256
SUB = 2
MOE_WINDOW = 2048
ROW_ALIGN = 16
RB = 128
VMEM_LIMIT = 60 * 1024 * 1024

bf16 = jnp.bfloat16
f32 = jnp.float32


def _rms(x, g):
    ms = jnp.mean(x * x, axis=-1, keepdims=True)
    return x * lax.rsqrt(ms + EPS) * g


def _dot(a, b):
    return jnp.dot(a, b, preferred_element_type=f32)


_GELU_C = 2.0 * 0.7978845608028654


def _gelu(x):
    t = x * ((-_GELU_C) + (-_GELU_C * 0.044715) * (x * x))
    return x / (1.0 + jnp.exp(t))


def _expert_cast_specs(step_to_expert):
    e = step_to_expert
    col_block = pl.BlockSpec((D, D_EXPERT), lambda t: (e(t) // EPG, e(t) % EPG))
    in_specs = [
        pl.BlockSpec((1, D, D_EXPERT), lambda t: (e(t), 0, 0)),
        pl.BlockSpec((1, D, D_EXPERT), lambda t: (e(t), 0, 0)),
        pl.BlockSpec((1, D_EXPERT, D), lambda t: (e(t), 0, 0)),
    ]
    out_specs = [col_block, col_block, pl.BlockSpec((D_EXPERT, D), lambda t: (e(t), 0))]
    out_shapes = [
        jax.ShapeDtypeStruct((N_GROUPS * D, EPG * D_EXPERT), bf16),
        jax.ShapeDtypeStruct((N_GROUPS * D, EPG * D_EXPERT), bf16),
        jax.ShapeDtypeStruct((N_EXPERTS * D_EXPERT, D), bf16),
    ]
    return in_specs, out_specs, out_shapes


def _mod_kernel(cp_ref, cs_ref, w_ref, b_ref, w_in_ref, w_out_ref, op_ref, os_ref, w_in_o, w_out_o):
    w = w_ref[...].astype(bf16)
    for c_ref, o_ref in ((cp_ref, op_ref), (cs_ref, os_ref)):
        c = c_ref[...]
        o_ref[0] = _dot((c * jax.nn.sigmoid(c)).astype(bf16), w) + b_ref[0]
    w_in_o[...] = w_in_ref[...].astype(bf16)
    w_out_o[...] = w_out_ref[...].astype(bf16)


def _mod_call(c_p, c_s, w_ada, b_ada, w_in, w_out):
    nb, n_s = c_p.shape[0], c_s.shape[0]
    in_cols = w_in.shape[1] // N_MOD
    assert w_in.shape[1] == N_MOD * in_cols and in_cols % LANES == 0
    return pl.pallas_call(
        _mod_kernel,
        grid=(N_MOD,),
        in_specs=[
            pl.BlockSpec((nb, D), lambda j: (0, 0)),
            pl.BlockSpec((n_s, D), lambda j: (0, 0)),
            pl.BlockSpec((D, D), lambda j: (0, j)),
            pl.BlockSpec((1, 1, D), lambda j: (j, 0, 0)),
            pl.BlockSpec((D, in_cols), lambda j: (0, j)),
            pl.BlockSpec((D, D), lambda j: (0, 0)),
        ],
        out_specs=[pl.BlockSpec((1, nb, D), lambda j: (j, 0, 0)),
                   pl.BlockSpec((1, n_s, D), lambda j: (j, 0, 0)),
                   pl.BlockSpec((D, in_cols), lambda j: (0, j)),
                   pl.BlockSpec((D, D), lambda j: (0, 0))],
        out_shape=[jax.ShapeDtypeStruct((N_MOD, nb, D), f32),
                   jax.ShapeDtypeStruct((N_MOD, n_s, D), f32),
                   jax.ShapeDtypeStruct(w_in.shape, bf16),
                   jax.ShapeDtypeStruct(w_out.shape, bf16)],
        compiler_params=pltpu.CompilerParams(
            dimension_semantics=("arbitrary",), vmem_limit_bytes=VMEM_LIMIT),
    )(c_p, c_s, w_ada, b_ada.reshape(N_MOD, 1, D), w_in, w_out)


def _route(logits):
    t = logits.shape[0]
    lane = lax.broadcasted_iota(jnp.int32, (t, LANES), 1)
    lane_f = lane.astype(f32)
    neg = -jnp.inf
    big = 1e9
    gmask = (lane >= GROUP_LANE0) & (lane < GROUP_LANE0 + N_GROUPS)
    gl = jnp.where(gmask, logits, neg)
    gmax = jnp.max(gl, axis=-1, keepdims=True)
    g_idx = jnp.min(jnp.where(gl == gmax, lane_f - GROUP_LANE0, big), axis=-1, keepdims=True)
    sumexp = jnp.sum(jnp.where(gmask, jnp.exp(gl - gmax), 0.0), axis=-1, keepdims=True)
    p_g = 1.0 / sumexp
    lane_grp = (lane >> 3).astype(f32)
    emask = (lane < N_EXPERTS) & (lane_grp == g_idx)
    el = jnp.where(emask, logits, neg)
    m1 = jnp.max(el, axis=-1, keepdims=True)
    i1 = jnp.min(jnp.where(el == m1, lane_f, big), axis=-1, keepdims=True)
    el2 = jnp.where(lane_f == i1, neg, el)
    m2 = jnp.max(el2, axis=-1, keepdims=True)
    i2 = jnp.min(jnp.where(el2 == m2, lane_f, big), axis=-1, keepdims=True)
    e = jnp.exp(m2 - m1)
    w1 = p_g / (1.0 + e)
    w2 = w1 * e

    def split3(w):
        hi = w.astype(bf16).astype(f32)
        mid = (w - hi).astype(bf16).astype(f32)
        lo = w - hi - mid
        return hi, mid, lo

    r3 = jnp.where(lane == GIDX_LANE, g_idx, 0.0)
    for idx, w in ((i1, w1), (i2, w2)):
        for part, wp in enumerate(split3(w)):
            r3 = r3 + jnp.where(lane_f == idx + float(part * N_EXPERTS), wp, 0.0)
    counts = jnp.sum(jnp.where(lane_f == g_idx, 1.0, 0.0), axis=0, keepdims=True)
    return r3.astype(bf16), jnp.broadcast_to(counts, (8, LANES))


def _merge_project(u, ga, gb, mix, y_b, w_out_ref):
    y_a = u * mix
    merged = jax.nn.sigmoid(ga) * y_a + jax.nn.sigmoid(gb) * y_b
    return _dot(merged.astype(bf16), w_out_ref[...])


def _residual(x, y, mods, vec_ref):
    sh1, sc1, gt1, sh2, sc2, gt2 = mods
    x1 = x + gt1 * _rms(y, vec_ref[1:2])
    h2 = _rms(x1, vec_ref[2:3] * (1.0 + sc2)) + sh2
    h2_hi = h2.astype(bf16)
    h2_lo = (h2 - h2_hi.astype(f32)).astype(bf16)
    return x1, h2_hi, h2_lo


def _router(h2_hi, h2_lo, w_r_ref, b_r_ref):
    r = _dot(h2_hi, w_r_ref[...]) + _dot(h2_lo, w_r_ref[...])
    logits = r[:, :LANES] + r[:, LANES:] + b_r_ref[...]
    return _route(logits)


def _residual_route(x, y, mods, vec_ref, w_r_ref, b_r_ref):
    x1, h2_hi, h2_lo = _residual(x, y, mods, vec_ref)
    r3, counts = _router(h2_hi, h2_lo, w_r_ref, b_r_ref)
    return x1, h2_hi, r3, counts


def _in_proj(x, mods, vec_ref, w_in_ref, after_first_dot=lambda: None,
             after_second_dot=lambda: None):
    sh1, sc1 = mods[0], mods[1]
    h = _rms(x, vec_ref[0:1] * (1.0 + sc1)) + sh1
    hb = h.astype(bf16)
    zu = _dot(hb, w_in_ref[:, 0:D])
    after_first_dot()
    zv = _dot(hb, w_in_ref[:, D:2 * D])
    after_second_dot()
    p = _dot(hb, w_in_ref[:, 2 * D:2 * D + PW])
    ga = _dot(hb, w_in_ref[:, 2 * D + PW:3 * D + PW])
    gb = _dot(hb, w_in_ref[:, 3 * D + PW:4 * D + PW])
    return zu, zv, p, ga, gb


def _activate(zu, zv, vec_ref):
    u = _gelu(zu)
    v = _gelu(zv)
    mu = jnp.mean(v, axis=-1, keepdims=True)
    vc = v - mu
    var = jnp.mean(vc * vc, axis=-1, keepdims=True)
    v = vc * lax.rsqrt(var + EPS) * vec_ref[4:5] + vec_ref[5:6]
    return u, v


def _pool_out(d_groups, vec_ref, w_pool_ref):
    parts = [_dot(d.astype(bf16), w_pool_ref[gi].astype(bf16)) for gi, d in enumerate(d_groups)]
    return jnp.concatenate(parts, axis=1) * vec_ref[6:7]


def _stage1_kernel(x_ref, xprev_ref, mod_ref, vec_ref, w_in_ref, w_sp_ref, bias_ref,
                   w_pool_ref, w_out_ref, w_r_ref, b_r_ref, eg_ref, eu_ref, ed_ref,
                   xs_ref, mods_ref, states_ref,
                   x1_ref, h2_ref, r3_ref, cnt_ref, plast_ref, wg_ref, wu_ref, wd_ref,
                   x1s_ref, h2s_ref, r3s_ref, cnts_ref, ps_ref, vs_ref,
                   pbuf, ybuf, *, tiles_per_seq):
    t = pl.program_id(0)
    last = pl.num_programs(0) - 1

    def mods_of(tile):
        b = tile // tiles_per_seq
        return [mod_ref[i, pl.ds(b, 1), :] for i in range(N_MOD)]

    split = {}

    def second_half_residual(k):
        rows = slice(k * TL, (k + 1) * TL)
        tile = jnp.maximum((t - 1) * SUB + k, 0)
        x1, h2_hi, h2_lo = _residual(xprev_ref[rows, :], ybuf[k], mods_of(tile), vec_ref)
        x1_ref[rows, :] = x1
        h2_ref[rows, :] = h2_hi
        split[k] = (h2_hi, h2_lo)

    def second_half_router(k):
        rows = slice(k * TL, (k + 1) * TL)
        r3, counts = _router(*split[k], w_r_ref, b_r_ref)
        r3_ref[rows, :] = r3
        cnt_ref[k] = counts

    def second_half(k):
        second_half_residual(k)
        second_half_router(k)

    def cast_expert():
        wg_ref[...] = eg_ref[0].astype(bf16)
        wu_ref[...] = eu_ref[0].astype(bf16)
        wd_ref[...] = ed_ref[0].astype(bf16)

    def first_half(k, under_first, under_second):
        tile = t * SUB + k
        s = tile % tiles_per_seq
        x = x_ref[k * TL:(k + 1) * TL, :]
        zu, zv, p, ga, gb = _in_proj(x, mods_of(tile), vec_ref, w_in_ref, under_first, under_second)
        u, v = _activate(zu, zv, vec_ref)

        vb = v.astype(bf16)
        row = lax.broadcasted_iota(jnp.int32, (CHUNK, CHUNK), 0)
        col = lax.broadcasted_iota(jnp.int32, (CHUNK, CHUNK), 1)
        w_tril = [jnp.where(row >= col, w_sp_ref[hd], 0.0).astype(bf16) for hd in range(HEADS)]
        bias = bias_ref[...]
        chunks = []
        for c in range(TL // CHUNK):
            heads = [_dot(w_tril[hd],
                          vb[c * CHUNK:(c + 1) * CHUNK, hd * HEAD_DIM:(hd + 1) * HEAD_DIM])
                     for hd in range(HEADS)]
            chunks.append(jnp.concatenate(heads, axis=1) + bias)
        mix = jnp.concatenate(chunks, axis=0)

        carry = jnp.where(s == 0, 0.0, pbuf[...])
        ext = jnp.concatenate([carry, p], axis=0)
        pos = s * TL + lax.broadcasted_iota(jnp.int32, (TL, PG), 0)
        d_groups = []
        for gi, w in enumerate(WINDOWS):
            acc = ext[:, gi * PG:(gi + 1) * PG]
            shift = 1
            while shift < w:
                acc = acc + pltpu.roll(acc, shift, 0)
                shift *= 2
            cnt = jnp.minimum(pos + 1, w).astype(f32)
            d_groups.append(acc[W_MAX:] / cnt - p[:, gi * PG:(gi + 1) * PG])
        pbuf[...] = p[TL - W_MAX:]
        plast_ref[0] = p[TL - W_MAX:]
        y_b = _pool_out(d_groups, vec_ref, w_pool_ref)
        ybuf[k] = _merge_project(u, ga, gb, mix, y_b, w_out_ref)

    @pl.when(t == 0)
    def _():
        ybuf[...] = jnp.zeros_like(ybuf)
        pbuf[...] = jnp.zeros_like(pbuf)

    @pl.when(t < last)
    def _():
        for k in range(SUB):
            def under_second(k=k):
                if k == 0:
                    cast_expert()
                second_half_router(k)
            first_half(k, functools.partial(second_half_residual, k), under_second)

    @pl.when(t == last)
    def _():
        for k in range(SUB):
            second_half(k)
        _sample_tokens(xs_ref, mods_ref, vec_ref, w_in_ref, states_ref, w_pool_ref, w_out_ref,
                       w_r_ref, b_r_ref, x1s_ref, h2s_ref, r3s_ref, cnts_ref, ps_ref, vs_ref)


def _sample_tokens(x_ref, mod_ref, vec_ref, w_in_ref, state_ref, w_pool_ref,
                   w_out_ref, w_r_ref, b_r_ref,
                   x1_ref, h2_ref, r3_ref, cnt_ref, p_ref, v_ref):
    x = x_ref[...]
    mods = [mod_ref[i] for i in range(N_MOD)]
    zu, zv, p, ga, gb = _in_proj(x, mods, vec_ref, w_in_ref)
    u, v = _activate(zu, zv, vec_ref)
    v_ref[...] = v
    p_ref[...] = p
    mix = v * vec_ref[8:9] + vec_ref[9:10]
    d_groups = []
    for gi, w in enumerate(WINDOWS):
        sl = slice(gi * PG, (gi + 1) * PG)
        acc = p[:, sl]
        for r in range(W_MAX - w, W_MAX - 1):
            acc = acc + state_ref[r][:, sl]
        d_groups.append(acc / float(w) - p[:, sl])
    y_b = _pool_out(d_groups, vec_ref, w_pool_ref)
    y = _merge_project(u, ga, gb, mix, y_b, w_out_ref)
    x1, h2b, r3, counts = _residual_route(x, y, mods, vec_ref, w_r_ref, b_r_ref)
    x1_ref[...] = x1
    h2_ref[...] = h2b
    r3_ref[...] = r3
    cnt_ref[...] = counts


def _const_spec(shape):
    nd = len(shape)
    return pl.BlockSpec(shape, lambda *_: (0,) * nd, pipeline_mode=pl.Buffered(1))


def _stage1_call(x, mod_p, vecs, w_in_b, w_sp, bias_full, w_pool_b, w_out_b, w_r, b_r,
                 w_gate, w_up, w_down, x_s, mod_s, state_t):
    b, s, _ = x.shape
    n_s = x_s.shape[0]
    const_out = lambda shape: pl.BlockSpec(shape, lambda t: (0,) * len(shape))
    ns = s // TL
    nt = b * ns
    steps = nt // SUB
    assert ns % SUB == 0
    assert steps >= N_EXPERTS
    x2 = x.reshape(b * s, D)
    rows = SUB * TL
    cur = lambda t: (jnp.minimum(t, steps - 1), 0)
    prev = lambda t: (jnp.maximum(t - 1, 0), 0)
    e_in, e_out, e_shapes = _expert_cast_specs(lambda t: jnp.minimum(t, N_EXPERTS - 1))
    return pl.pallas_call(
        functools.partial(_stage1_kernel, tiles_per_seq=ns),
        grid=(steps + 1,),
        in_specs=[
            pl.BlockSpec((rows, D), cur),
            pl.BlockSpec((rows, D), prev),
            _const_spec(mod_p.shape),
            _const_spec(vecs.shape),
            _const_spec(w_in_b.shape),
            _const_spec(w_sp.shape),
            _const_spec(bias_full.shape),
            _const_spec(w_pool_b.shape),
            _const_spec(w_out_b.shape),
            _const_spec(w_r.shape),
            _const_spec(b_r.shape),
        ] + e_in + [_const_spec(x_s.shape), _const_spec(mod_s.shape), _const_spec(state_t.shape)],
        out_specs=[
            pl.BlockSpec((rows, D), prev),
            pl.BlockSpec((rows, D), prev),
            pl.BlockSpec((rows, LANES), prev),
            pl.BlockSpec((SUB, 8, LANES), lambda t: (jnp.maximum(t - 1, 0), 0, 0)),
            pl.BlockSpec((1, W_MAX, PW),
                         lambda t: (jnp.minimum(t, steps - 1) * SUB // ns, 0, 0)),
        ] + e_out + [const_out((n_s, D)), const_out((n_s, D)), const_out((n_s, LANES)),
                     const_out((8, LANES)), const_out((n_s, PW)), const_out((n_s, D))],
        out_shape=[
            jax.ShapeDtypeStruct((b * s, D), f32),
            jax.ShapeDtypeStruct((b * s, D), bf16),
            jax.ShapeDtypeStruct((b * s, LANES), bf16),
            jax.ShapeDtypeStruct((nt, 8, LANES), f32),
            jax.ShapeDtypeStruct((b, W_MAX, PW), f32),
        ] + e_shapes + [
            jax.ShapeDtypeStruct((n_s, D), f32),
            jax.ShapeDtypeStruct((n_s, D), bf16),
            jax.ShapeDtypeStruct((n_s, LANES), bf16),
            jax.ShapeDtypeStruct((8, LANES), f32),
            jax.ShapeDtypeStruct((n_s, PW), f32),
            jax.ShapeDtypeStruct((n_s, D), f32),
        ],
        scratch_shapes=[pltpu.VMEM((W_MAX, PW), f32), pltpu.VMEM((SUB, TL, D), f32)],
        compiler_params=pltpu.CompilerParams(
            dimension_semantics=("arbitrary",), vmem_limit_bytes=VMEM_LIMIT),
    )(x2, x2, mod_p, vecs, w_in_b, w_sp, bias_full, w_pool_b, w_out_b, w_r, b_r,
      w_gate, w_up, w_down, x_s, mod_s, state_t)


def _moe_buffer_rows(ts, nt):
    worst = ts * nt + nt * N_GROUPS * (ROW_ALIGN - 1) + N_GROUPS * (RB - 1)
    return -(-worst // RB) * RB


def _moe_kernel(cnt_ref, h2_ref, r3_ref, x1_ref, mod_ref, g_ref, wg_ref, wu_ref, wd_ref, o_ref,
                hsbuf, hs_tile, ys_tile, pt_buf, tab,
                *, ts, nt, tpr, tps):
    w = pl.program_id(0)
    i = pl.program_id(1)
    rt = ts + LANES

    def copy_rows(src, src0, dst, dst0, nrows, ncols):
        def body(j, _):
            s0 = pl.multiple_of(src0 + j * ROW_ALIGN, ROW_ALIGN)
            d0 = pl.multiple_of(dst0 + j * ROW_ALIGN, ROW_ALIGN)
            dst[pl.ds(d0, ROW_ALIGN), :] = src[pl.ds(s0, ROW_ALIGN), :ncols]
            return 0
        lax.fori_loop(0, nrows // ROW_ALIGN, body, 0)

    @pl.when((w == 0) & (i == 0))
    def _():
        hsbuf[...] = jnp.zeros_like(hsbuf)
        ys_tile[...] = jnp.zeros_like(ys_tile)

    @pl.when(i == 0)
    def _sort_and_run_experts():
        def run_len(tile, g):
            c = cnt_ref[(w * nt + tile) * N_GROUPS + g]
            return ((c + (ROW_ALIGN - 1)) // ROW_ALIGN) * ROW_ALIGN

        lens = [[run_len(t, g) for g in range(N_GROUPS)] for t in range(nt)]
        region = [sum(lens[t][g] for t in range(nt)) for g in range(N_GROUPS)]
        region = [((r + (RB - 1)) // RB) * RB for r in region]
        base = [sum(region[:g]) for g in range(N_GROUPS)]
        offs = list(base)
        woff = []
        for t in range(nt):
            woff.append(list(offs))
            for g in range(N_GROUPS):
                tab[t * 2 * N_GROUPS + g] = lens[t][g]
                tab[t * 2 * N_GROUPS + N_GROUPS + g] = offs[g]
                offs[g] = offs[g] + lens[t][g]

        lane = lax.broadcasted_iota(jnp.int32, (ts, LANES), 1)
        lane_f = lane.astype(f32)
        r_i = lax.broadcasted_iota(jnp.int32, (ts, ts), 0)
        c_i = lax.broadcasted_iota(jnp.int32, (ts, ts), 1)
        ltri = jnp.where(r_i > c_i, 1.0, 0.0).astype(bf16)
        rt_lane = lax.broadcasted_iota(jnp.int32, (ts, rt), 1).astype(f32)

        tiles = range(nt)
        r3s = [r3_ref[t * ts:(t + 1) * ts, :] for t in tiles]
        gids = [jnp.sum(jnp.where(lane == GIDX_LANE, r3s[t].astype(f32), 0.0), axis=-1,
                        keepdims=True) for t in tiles]
        onehots = [jnp.where(lane_f == gids[t], 1.0, 0.0) for t in tiles]
        ranks = [_dot(ltri, onehots[t].astype(bf16)) for t in tiles]
        poss = []
        for t in tiles:
            seg = jnp.zeros((1, LANES), f32)
            start = 0
            for g in range(N_GROUPS):
                seg = seg + jnp.where(lane[0:1] == g, jnp.asarray(start, jnp.int32).astype(f32), 0.0)
                start = start + lens[t][g]
            poss.append(jnp.sum(onehots[t] * (ranks[t] + seg), axis=-1, keepdims=True))
        for t in tiles:
            pt_buf[t] = jnp.where(rt_lane == poss[t], 1.0, 0.0).astype(bf16)
        def sort_rows(t, x):
            return lax.dot_general(pt_buf[t], x, (((0,), (0,)), ((), ())),
                                   preferred_element_type=f32).astype(bf16)

        per_pass = hs_tile.shape[0]
        for t0 in range(0, nt, per_pass):
            for k in range(per_pass):
                t = t0 + k
                hs_tile[k, :, :D] = sort_rows(t, h2_ref[t * ts:(t + 1) * ts, :])
                hs_tile[k, :, D:] = sort_rows(t, r3_ref[t * ts:(t + 1) * ts, :])
            for k in range(per_pass):
                t = t0 + k
                start = 0
                for g in range(N_GROUPS):
                    copy_rows(hs_tile.at[k], start, hsbuf, woff[t][g], lens[t][g], D + LANES)
                    start = start + lens[t][g]

        def expert_rows(g, r0, nrows):
            rows = hsbuf[pl.ds(r0, nrows), :D]
            gate = _dot(rows, wg_ref[g * D:(g + 1) * D, :])
            up = _dot(rows, wu_ref[g * D:(g + 1) * D, :])
            c3 = hsbuf[pl.ds(r0, nrows), D:].astype(f32)
            cw_lanes = c3 + pltpu.roll(c3, LANES - N_EXPERTS, 1) + pltpu.roll(c3, LANES - 2 * N_EXPERTS, 1)
            cw = jnp.concatenate(
                [jnp.broadcast_to(cw_lanes[:, g * EPG + j:g * EPG + j + 1], (nrows, D_EXPERT))
                 for j in range(EPG)], axis=1)
            act = gate * jax.nn.sigmoid(gate) * up * cw
            hsbuf[pl.ds(r0, nrows), :D] = _dot(
                act.astype(bf16), wd_ref[g * D:(g + 1) * D, :]).astype(bf16)

        for g in range(N_GROUPS):
            n_blocks = region[g] // RB

            def three_blocks(b, _, g=g):
                expert_rows(g, pl.multiple_of(base[g] + 3 * b * RB, RB), 3 * RB)
                return 0
            lax.fori_loop(0, n_blocks // 3, three_blocks, 0)
            done = (n_blocks // 3) * 3

            @pl.when(n_blocks - done == 2)
            def _(g=g, done=done):
                expert_rows(g, pl.multiple_of(base[g] + done * RB, RB), 2 * RB)

            @pl.when(n_blocks - done == 1)
            def _(g=g, done=done):
                expert_rows(g, pl.multiple_of(base[g] + done * RB, RB), RB)

    tiles = [i * tps + k for k in range(tps)]
    for k, tile in enumerate(tiles):
        start = 0
        for g in range(N_GROUPS):
            ln = tab[tile * 2 * N_GROUPS + g]
            copy_rows(hsbuf, tab[tile * 2 * N_GROUPS + N_GROUPS + g], ys_tile.at[k], start, ln, D)
            start = start + ln
    fs = [_dot(pt_buf[tile], ys_tile[k]) for k, tile in enumerate(tiles)]
    for k, tile in enumerate(tiles):
        rows = slice(k * ts, (k + 1) * ts)
        tok0 = (w * nt + tile) * ts
        gt2_row = N_MOD - 1
        gt2 = (mod_ref[gt2_row, pl.ds(tok0, ts), :] if tpr == 1
               else mod_ref[gt2_row, pl.ds(tok0 // tpr, 1), :])
        o_ref[rows, :] = x1_ref[rows, :] + gt2 * _rms(fs[k], g_ref[...])


def _moe_call(cnt, x1, h2, r3, mod, g_post, wg, wu, wd, ts, nt, tpr):
    n = x1.shape[0]
    win = ts * nt
    rbuf = _moe_buffer_rows(ts, nt)
    rt = ts + LANES
    tps = 2 if nt % 2 == 0 else 1
    steps = nt // tps
    grid_spec = pltpu.PrefetchScalarGridSpec(
        num_scalar_prefetch=1,
        grid=(n // win, steps),
        in_specs=[
            pl.BlockSpec((win, D), lambda w, i, c: (w, 0)),
            pl.BlockSpec((win, LANES), lambda w, i, c: (w, 0)),
            pl.BlockSpec((tps * ts, D), lambda w, i, c: (w * steps + i, 0)),
            _const_spec(mod.shape),
            _const_spec(g_post.shape),
            _const_spec(wg.shape),
            _const_spec(wu.shape),
            _const_spec(wd.shape),
        ],
        out_specs=pl.BlockSpec((tps * ts, D), lambda w, i, c: (w * steps + i, 0)),
        scratch_shapes=[
            pltpu.VMEM((rbuf, D + LANES), bf16),
            pltpu.VMEM((4 if nt % 4 == 0 else tps, rt, D + LANES), bf16),
            pltpu.VMEM((tps, rt, D), bf16),
            pltpu.VMEM((nt, ts, rt), bf16),
            pltpu.SMEM((nt * 2 * N_GROUPS,), jnp.int32),
        ],
    )
    return pl.pallas_call(
        functools.partial(_moe_kernel, ts=ts, nt=nt, tpr=tpr, tps=tps),
        grid_spec=grid_spec,
        out_shape=jax.ShapeDtypeStruct((n, D), f32),
        compiler_params=pltpu.CompilerParams(
            dimension_semantics=("arbitrary", "arbitrary"), vmem_limit_bytes=VMEM_LIMIT),
    )(cnt, h2, r3, x1, mod, g_post, wg, wu, wd)


def _count_table(cnt):
    return cnt[:, 0, :N_GROUPS].astype(jnp.int32).reshape(-1)


def kernel(x_prompt, x_sample, c_prompt, c_sample, state_pool, w_ada, b_ada, g_pre_mix, g_post_mix, g_pre_ffn, g_post_ffn, w_in, ln_v_g, ln_v_b, w_spatial, b_spatial, w_pool, pool_scale, w_out, w_router_grp, b_router_grp, w_router_exp, b_router_exp, w_exp_gate, w_exp_up, w_exp_down):
    depth = w_in.shape[0]
    assert depth == 1
    b, s, _ = x_prompt.shape
    n_s = x_sample.shape[0]
    l = 0

    mod_p, mod_s, w_in_b, w_out_b = _mod_call(
        c_prompt, c_sample, w_ada[l], b_ada[l], w_in[l], w_out[l])

    ws, bs = w_spatial[l], b_spatial[l]
    zeros = jnp.zeros((D,), f32)
    vecs = jnp.stack([
        g_pre_mix[l], g_post_mix[l], g_pre_ffn[l], g_post_ffn[l], ln_v_g[l], ln_v_b[l],
        pool_scale[l], zeros,
        jnp.repeat(ws[:, 0, 0], HEAD_DIM), jnp.repeat(bs[:, 0], HEAD_DIM),
        zeros, zeros, zeros, zeros, zeros, zeros])
    bias_full = jnp.repeat(bs.T, HEAD_DIM, axis=1)
    pad = LANES - N_EXPERTS - N_GROUPS
    w_r = jnp.concatenate([w_router_exp[l], w_router_grp[l], jnp.zeros((D, pad), f32)], axis=1)
    w_r_hi = w_r.astype(bf16)
    w_r_lo = (w_r - w_r_hi.astype(f32)).astype(bf16)
    w_r2 = jnp.concatenate([w_r_hi, w_r_lo], axis=1)
    b_r = jnp.concatenate([b_router_exp[l], b_router_grp[l], jnp.zeros((pad,), f32)])[None]

    state_t = jnp.transpose(state_pool[l], (1, 0, 2))
    (x1_p, h2_p, r3_p, cnt_p, plast, wg, wu, wd,
     x1_s, h2_s, r3_s, cnt_s, p_s, v_s) = _stage1_call(
        x_prompt, mod_p, vecs, w_in_b, ws, bias_full, w_pool[l], w_out_b, w_r2, b_r,
        w_exp_gate[l], w_exp_up[l], w_exp_down[l], x_sample.reshape(n_s, D), mod_s, state_t)

    g_post = g_post_ffn[l].reshape(1, D)
    y_p = _moe_call(
        _count_table(cnt_p), x1_p.reshape(b * s, D), h2_p.reshape(b * s, D),
        r3_p.reshape(b * s, LANES), mod_p, g_post, wg, wu, wd, TL, MOE_WINDOW // TL, s)
    y_s = _moe_call(
        _count_table(cnt_s[None]), x1_s, h2_s, r3_s, mod_s, g_post, wg, wu, wd, n_s, 1, 1)

    state_pool_prompt = plast[:, 1:][None]
    state_pool_sample = jnp.concatenate([state_pool[l][:, 1:], p_s[:, None, :]], axis=1)[None]
    chunk_v_sample = v_s.reshape(1, n_s, 1, D)
    return (y_p.reshape(b, s, D), y_s.reshape(n_s, 1, D), state_pool_prompt,
            state_pool_sample, chunk_v_sample)
```
